```python
import jax, jax.numpy as jnp
from jax import lax
import numpy as np

D_MODEL = 2048
BATCH = 8
SEQ = 8192
DEPTH = 2

MEM_LEN = 256
N_MIXERS = 2
D_FF = 5632
CHUNK = 128
GMLP_WIDTH = 2048
GMLP_GROUPS = 8
GMLP_GROUP_DIM = GMLP_WIDTH // GMLP_GROUPS
CONV_WIDTH = 3
XATTN_HEADS = 4
XATTN_HEAD_DIM = D_MODEL // XATTN_HEADS
RMS_EPS = 1e-6
LN_EPS = 1e-5

kernel_name = "hybrid_gmlp_shortconv_macaron_memxattn"


def rmsnorm(x, g):
    xf = x.astype(jnp.float32)
    y = xf * lax.rsqrt(jnp.mean(xf * xf, axis=-1, keepdims=True) + RMS_EPS)
    return (y * g.astype(jnp.float32)).astype(x.dtype)


def layernorm(x, g, b):
    xf = x.astype(jnp.float32)
    mu = jnp.mean(xf, axis=-1, keepdims=True)
    xc = xf - mu
    var = jnp.mean(xc * xc, axis=-1, keepdims=True)
    y = xc * lax.rsqrt(var + LN_EPS) * g.astype(jnp.float32) + b.astype(jnp.float32)
    return y.astype(x.dtype)


def swiglu(h, w13, w2):
    gate, up = jnp.split(h @ w13, 2, axis=-1)
    return (jax.nn.silu(gate) * up) @ w2


def gmlp_mixer(h, w_in, ln_g, ln_b, w_s, b_s, w_out):
    bsz, seq, _ = h.shape
    z = jax.nn.gelu(h @ w_in, approximate=False)
    u, v = jnp.split(z, 2, axis=-1)
    v = layernorm(v, ln_g, ln_b)
    vc = v.reshape(bsz, seq // CHUNK, CHUNK, GMLP_GROUPS, GMLP_GROUP_DIM)
    causal = jnp.tril(jnp.ones((CHUNK, CHUNK), dtype=bool))
    w = jnp.where(causal[None], w_s, jnp.zeros_like(w_s)).astype(vc.dtype)
    f = jnp.einsum('gts,bcsge->bctge', w, vc) + b_s.T[:, :, None].astype(vc.dtype)
    return (u * f.reshape(bsz, seq, GMLP_WIDTH)) @ w_out


def short_conv_mixer(h, w_in, conv_w, w_out):
    d = h.shape[-1]
    gate_b, gate_c, val = jnp.split(h @ w_in, 3, axis=-1)
    z = gate_c * val
    kern = conv_w[:, None, :].astype(z.dtype)
    conv = lax.conv_general_dilated(
        z, kern, window_strides=(1,), padding=[(CONV_WIDTH - 1, 0)],
        dimension_numbers=('NWC', 'WIO', 'NWC'), feature_group_count=d)
    return (gate_b * conv) @ w_out


def mem_cross_attn(h, mem_n, wq, wkv, wo):
    bsz, seq, d = h.shape
    m = mem_n.shape[1]
    q = (h @ wq).reshape(bsz, seq, XATTN_HEADS, XATTN_HEAD_DIM)
    k, v = jnp.split(mem_n @ wkv, 2, axis=-1)
    k = k.reshape(bsz, m, XATTN_HEADS, XATTN_HEAD_DIM)
    v = v.reshape(bsz, m, XATTN_HEADS, XATTN_HEAD_DIM)
    s = jnp.einsum('bshd,bmhd->bhsm', q, k).astype(jnp.float32) * (XATTN_HEAD_DIM ** -0.5)
    p = jax.nn.softmax(s, axis=-1).astype(v.dtype)
    o = jnp.einsum('bhsm,bmhd->bshd', p, v).reshape(bsz, seq, d)
    return o @ wo


def _fwd_setup_inputs(seed: int = 0) -> dict:
    key = jax.random.key(seed)
    ks = iter(jax.random.split(key, 32))
    n_a = (DEPTH + 1) // 2
    n_b = DEPTH // 2
    D, F, E = D_MODEL, D_FF, GMLP_WIDTH

    def w(shape, fan_in):
        return jax.random.normal(next(ks), shape, jnp.float32) * (fan_in ** -0.5)

    def gain(shape):
        return 1.0 + 0.02 * jax.random.normal(next(ks), shape, jnp.float32)

    def bias(shape):
        return 0.02 * jax.random.normal(next(ks), shape, jnp.float32)

    return {
        "x": jax.random.normal(next(ks), (BATCH, SEQ, D), jnp.float32),
        "mem": jax.random.normal(next(ks), (BATCH, MEM_LEN, D), jnp.float32),
        "ffn1_norm": gain((DEPTH, D)),
        "ffn1_w13": w((DEPTH, D, 2 * F), D),
        "ffn1_w2": w((DEPTH, F, D), F),
        "mix_norm": gain((DEPTH, D)),
        "gmlp_w_in": w((n_a, D, 2 * E), D),
        "gmlp_ln_g": gain((n_a, E)),
        "gmlp_ln_b": bias((n_a, E)),
        "gmlp_w_s": w((n_a, GMLP_GROUPS, CHUNK, CHUNK), CHUNK),
        "gmlp_b_s": gain((n_a, GMLP_GROUPS, CHUNK)),
        "gmlp_w_out": w((n_a, E, D), E),
        "conv_w_in": w((n_b, D, 3 * D), D),
        "conv_w": w((n_b, CONV_WIDTH, D), CONV_WIDTH),
        "conv_w_out": w((n_b, D, D), D),
        "xattn_norm": gain((DEPTH, D)),
        "mem_norm": gain((DEPTH, D)),
        "xattn_wq": w((DEPTH, D, D), D),
        "xattn_wkv": w((DEPTH, D, 2 * D), D),
        "xattn_wo": w((DEPTH, D, D), D),
        "ffn2_norm": gain((DEPTH, D)),
        "ffn2_w13": w((DEPTH, D, 2 * F), D),
        "ffn2_w2": w((DEPTH, F, D), F),
        "final_norm": gain((D,)),
    }


def _fwd_reference(x, mem, ffn1_norm, ffn1_w13, ffn1_w2, mix_norm,
              gmlp_w_in, gmlp_ln_g, gmlp_ln_b, gmlp_w_s, gmlp_b_s, gmlp_w_out,
              conv_w_in, conv_w, conv_w_out,
              xattn_norm, mem_norm, xattn_wq, xattn_wkv, xattn_wo,
              ffn2_norm, ffn2_w13, ffn2_w2, final_norm):
    for i in range(DEPTH):
        x = x + 0.5 * swiglu(rmsnorm(x, ffn1_norm[i]), ffn1_w13[i], ffn1_w2[i])
        h = rmsnorm(x, mix_norm[i])
        j = i // N_MIXERS
        if i % N_MIXERS == 0:
            x = x + gmlp_mixer(h, gmlp_w_in[j], gmlp_ln_g[j], gmlp_ln_b[j],
                               gmlp_w_s[j], gmlp_b_s[j], gmlp_w_out[j])
        else:
            x = x + short_conv_mixer(h, conv_w_in[j], conv_w[j], conv_w_out[j])
        x = x + mem_cross_attn(rmsnorm(x, xattn_norm[i]), rmsnorm(mem, mem_norm[i]),
                               xattn_wq[i], xattn_wkv[i], xattn_wo[i])
        x = x + 0.5 * swiglu(rmsnorm(x, ffn2_norm[i]), ffn2_w13[i], ffn2_w2[i])
    return rmsnorm(x, final_norm)


import jax as _jax
import jax.numpy as _jnp

TWIN_FORMAT = 'train_step'
FWD_PARAMS = ['x', 'mem', 'ffn1_norm', 'ffn1_w13', 'ffn1_w2', 'mix_norm', 'gmlp_w_in', 'gmlp_ln_g', 'gmlp_ln_b', 'gmlp_w_s', 'gmlp_b_s', 'gmlp_w_out', 'conv_w_in', 'conv_w', 'conv_w_out', 'xattn_norm', 'mem_norm', 'xattn_wq', 'xattn_wkv', 'xattn_wo', 'ffn2_norm', 'ffn2_w13', 'ffn2_w2', 'final_norm']
TWIN_WEIGHTS = ['ffn1_norm', 'ffn1_w13', 'ffn1_w2', 'mix_norm', 'gmlp_w_in', 'gmlp_ln_g', 'gmlp_ln_b', 'gmlp_w_s', 'gmlp_b_s', 'gmlp_w_out', 'conv_w_in', 'conv_w', 'conv_w_out', 'xattn_norm', 'mem_norm', 'xattn_wq', 'xattn_wkv', 'xattn_wo', 'ffn2_norm', 'ffn2_w13', 'ffn2_w2', 'final_norm']
TWIN_DIFF_INPUT = 'x'
TWIN_INPUTS = ['x', 'mem', 'ffn1_norm', 'ffn1_w13', 'ffn1_w2', 'mix_norm', 'gmlp_w_in', 'gmlp_ln_g', 'gmlp_ln_b', 'gmlp_w_s', 'gmlp_b_s', 'gmlp_w_out', 'conv_w_in', 'conv_w', 'conv_w_out', 'xattn_norm', 'mem_norm', 'xattn_wq', 'xattn_wkv', 'xattn_wo', 'ffn2_norm', 'ffn2_w13', 'ffn2_w2', 'final_norm', 'loss_target', 'm_ffn1_norm', 'm_ffn1_w13', 'm_ffn1_w2', 'm_mix_norm', 'm_gmlp_w_in', 'm_gmlp_ln_g', 'm_gmlp_ln_b', 'm_gmlp_w_s', 'm_gmlp_b_s', 'm_gmlp_w_out', 'm_conv_w_in', 'm_conv_w', 'm_conv_w_out', 'm_xattn_norm', 'm_mem_norm', 'm_xattn_wq', 'm_xattn_wkv', 'm_xattn_wo', 'm_ffn2_norm', 'm_ffn2_w13', 'm_ffn2_w2', 'm_final_norm', 'v_ffn1_norm', 'v_ffn1_w13', 'v_ffn1_w2', 'v_mix_norm', 'v_gmlp_w_in', 'v_gmlp_ln_g', 'v_gmlp_ln_b', 'v_gmlp_w_s', 'v_gmlp_b_s', 'v_gmlp_w_out', 'v_conv_w_in', 'v_conv_w', 'v_conv_w_out', 'v_xattn_norm', 'v_mem_norm', 'v_xattn_wq', 'v_xattn_wkv', 'v_xattn_wo', 'v_ffn2_norm', 'v_ffn2_w13', 'v_ffn2_w2', 'v_final_norm']
TWIN_OUTPUTS = ['loss', 'grad_x', 'grad_ffn1_norm', 'grad_ffn1_w13', 'grad_ffn1_w2', 'grad_mix_norm', 'grad_gmlp_w_in', 'grad_gmlp_ln_g', 'grad_gmlp_ln_b', 'grad_gmlp_w_s', 'grad_gmlp_b_s', 'grad_gmlp_w_out', 'grad_conv_w_in', 'grad_conv_w', 'grad_conv_w_out', 'grad_xattn_norm', 'grad_mem_norm', 'grad_xattn_wq', 'grad_xattn_wkv', 'grad_xattn_wo', 'grad_ffn2_norm', 'grad_ffn2_w13', 'grad_ffn2_w2', 'grad_final_norm', 'delta_ffn1_norm', 'delta_ffn1_w13', 'delta_ffn1_w2', 'delta_mix_norm', 'delta_gmlp_w_in', 'delta_gmlp_ln_g', 'delta_gmlp_ln_b', 'delta_gmlp_w_s', 'delta_gmlp_b_s', 'delta_gmlp_w_out', 'delta_conv_w_in', 'delta_conv_w', 'delta_conv_w_out', 'delta_xattn_norm', 'delta_mem_norm', 'delta_xattn_wq', 'delta_xattn_wkv', 'delta_xattn_wo', 'delta_ffn2_norm', 'delta_ffn2_w13', 'delta_ffn2_w2', 'delta_final_norm', 'new_m_ffn1_norm', 'new_m_ffn1_w13', 'new_m_ffn1_w2', 'new_m_mix_norm', 'new_m_gmlp_w_in', 'new_m_gmlp_ln_g', 'new_m_gmlp_ln_b', 'new_m_gmlp_w_s', 'new_m_gmlp_b_s', 'new_m_gmlp_w_out', 'new_m_conv_w_in', 'new_m_conv_w', 'new_m_conv_w_out', 'new_m_xattn_norm', 'new_m_mem_norm', 'new_m_xattn_wq', 'new_m_xattn_wkv', 'new_m_xattn_wo', 'new_m_ffn2_norm', 'new_m_ffn2_w13', 'new_m_ffn2_w2', 'new_m_final_norm', 'new_v_ffn1_norm', 'new_v_ffn1_w13', 'new_v_ffn1_w2', 'new_v_mix_norm', 'new_v_gmlp_w_in', 'new_v_gmlp_ln_g', 'new_v_gmlp_ln_b', 'new_v_gmlp_w_s', 'new_v_gmlp_b_s', 'new_v_gmlp_w_out', 'new_v_conv_w_in', 'new_v_conv_w', 'new_v_conv_w_out', 'new_v_xattn_norm', 'new_v_mem_norm', 'new_v_xattn_wq', 'new_v_xattn_wkv', 'new_v_xattn_wo', 'new_v_ffn2_norm', 'new_v_ffn2_w13', 'new_v_ffn2_w2', 'new_v_final_norm']
TWIN_LEAF_KINDS = {'loss': 'loss', 'grad_x': 'grad_x', 'grad_ffn1_norm': 'grad_w', 'grad_ffn1_w13': 'grad_w', 'grad_ffn1_w2': 'grad_w', 'grad_mix_norm': 'grad_w', 'grad_gmlp_w_in': 'grad_w', 'grad_gmlp_ln_g': 'grad_w', 'grad_gmlp_ln_b': 'grad_w', 'grad_gmlp_w_s': 'grad_w', 'grad_gmlp_b_s': 'grad_w', 'grad_gmlp_w_out': 'grad_w', 'grad_conv_w_in': 'grad_w', 'grad_conv_w': 'grad_w', 'grad_conv_w_out': 'grad_w', 'grad_xattn_norm': 'grad_w', 'grad_mem_norm': 'grad_w', 'grad_xattn_wq': 'grad_w', 'grad_xattn_wkv': 'grad_w', 'grad_xattn_wo': 'grad_w', 'grad_ffn2_norm': 'grad_w', 'grad_ffn2_w13': 'grad_w', 'grad_ffn2_w2': 'grad_w', 'grad_final_norm': 'grad_w', 'delta_ffn1_norm': 'delta_w', 'delta_ffn1_w13': 'delta_w', 'delta_ffn1_w2': 'delta_w', 'delta_mix_norm': 'delta_w', 'delta_gmlp_w_in': 'delta_w', 'delta_gmlp_ln_g': 'delta_w', 'delta_gmlp_ln_b': 'delta_w', 'delta_gmlp_w_s': 'delta_w', 'delta_gmlp_b_s': 'delta_w', 'delta_gmlp_w_out': 'delta_w', 'delta_conv_w_in': 'delta_w', 'delta_conv_w': 'delta_w', 'delta_conv_w_out': 'delta_w', 'delta_xattn_norm': 'delta_w', 'delta_mem_norm': 'delta_w', 'delta_xattn_wq': 'delta_w', 'delta_xattn_wkv': 'delta_w', 'delta_xattn_wo': 'delta_w', 'delta_ffn2_norm': 'delta_w', 'delta_ffn2_w13': 'delta_w', 'delta_ffn2_w2': 'delta_w', 'delta_final_norm': 'delta_w', 'new_m_ffn1_norm': 'new_m', 'new_m_ffn1_w13': 'new_m', 'new_m_ffn1_w2': 'new_m', 'new_m_mix_norm': 'new_m', 'new_m_gmlp_w_in': 'new_m', 'new_m_gmlp_ln_g': 'new_m', 'new_m_gmlp_ln_b': 'new_m', 'new_m_gmlp_w_s': 'new_m', 'new_m_gmlp_b_s': 'new_m', 'new_m_gmlp_w_out': 'new_m', 'new_m_conv_w_in': 'new_m', 'new_m_conv_w': 'new_m', 'new_m_conv_w_out': 'new_m', 'new_m_xattn_norm': 'new_m', 'new_m_mem_norm': 'new_m', 'new_m_xattn_wq': 'new_m', 'new_m_xattn_wkv': 'new_m', 'new_m_xattn_wo': 'new_m', 'new_m_ffn2_norm': 'new_m', 'new_m_ffn2_w13': 'new_m', 'new_m_ffn2_w2': 'new_m', 'new_m_final_norm': 'new_m', 'new_v_ffn1_norm': 'new_v', 'new_v_ffn1_w13': 'new_v', 'new_v_ffn1_w2': 'new_v', 'new_v_mix_norm': 'new_v', 'new_v_gmlp_w_in': 'new_v', 'new_v_gmlp_ln_g': 'new_v', 'new_v_gmlp_ln_b': 'new_v', 'new_v_gmlp_w_s': 'new_v', 'new_v_gmlp_b_s': 'new_v', 'new_v_gmlp_w_out': 'new_v', 'new_v_conv_w_in': 'new_v', 'new_v_conv_w': 'new_v', 'new_v_conv_w_out': 'new_v', 'new_v_xattn_norm': 'new_v', 'new_v_mem_norm': 'new_v', 'new_v_xattn_wq': 'new_v', 'new_v_xattn_wkv': 'new_v', 'new_v_xattn_wo': 'new_v', 'new_v_ffn2_norm': 'new_v', 'new_v_ffn2_w13': 'new_v', 'new_v_ffn2_w2': 'new_v', 'new_v_final_norm': 'new_v'}


def _forward(args):
    return _fwd_reference(*[args[k] for k in FWD_PARAMS])


def _output_shape():
    def fwd():
        inp = _fwd_setup_inputs(0)
        return _fwd_reference(*[inp[k] for k in FWD_PARAMS])
    out = _jax.eval_shape(fwd)
    return out.shape, out.dtype

N_MICROBATCH = 1
ADAM_LR = 0.001
ADAM_B1 = 0.9
ADAM_B2 = 0.999
ADAM_EPS = 1e-08
ADAM_WD = 0.01
ADAM_STEP = 10
PER_EXAMPLE_BATCH_AXIS = {'x': 0, 'mem': 0, 'loss_target': 0}
SHARED_INPUTS = []
_WEIGHT_DTYPES = {'ffn1_norm': _jnp.float32, 'ffn1_w13': _jnp.float32, 'ffn1_w2': _jnp.float32, 'mix_norm': _jnp.float32, 'gmlp_w_in': _jnp.float32, 'gmlp_ln_g': _jnp.float32, 'gmlp_ln_b': _jnp.float32, 'gmlp_w_s': _jnp.float32, 'gmlp_b_s': _jnp.float32, 'gmlp_w_out': _jnp.float32, 'conv_w_in': _jnp.float32, 'conv_w': _jnp.float32, 'conv_w_out': _jnp.float32, 'xattn_norm': _jnp.float32, 'mem_norm': _jnp.float32, 'xattn_wq': _jnp.float32, 'xattn_wkv': _jnp.float32, 'xattn_wo': _jnp.float32, 'ffn2_norm': _jnp.float32, 'ffn2_w13': _jnp.float32, 'ffn2_w2': _jnp.float32, 'final_norm': _jnp.float32}
MOMENT_SCALE = {'ffn1_norm': 6.691011e-02, 'ffn1_w13': 2.808768e-02, 'ffn1_w2': 4.576656e-02, 'mix_norm': 1.319782e-01, 'gmlp_w_in': 9.073558e-02, 'gmlp_ln_g': 6.040240e-02, 'gmlp_ln_b': 6.065208e-02, 'gmlp_w_s': 8.531024e-02, 'gmlp_b_s': 1.224446e-01, 'gmlp_w_out': 1.056648e-01, 'conv_w_in': 7.446156e-02, 'conv_w': 7.398104e-02, 'conv_w_out': 7.482019e-02, 'xattn_norm': 1.096393e-02, 'mem_norm': 1.658684e-02, 'xattn_wq': 1.095568e-02, 'xattn_wkv': 1.111938e-02, 'xattn_wo': 1.123230e-02, 'ffn2_norm': 4.468534e-02, 'ffn2_w13': 1.872667e-02, 'ffn2_w2': 3.057911e-02, 'final_norm': 3.198794e+01}


def _to_microbatches(a, axis):
    t = _jnp.moveaxis(a, axis, 0)
    t = t.reshape((N_MICROBATCH, t.shape[0] // N_MICROBATCH) + t.shape[1:])
    return _jnp.moveaxis(t, 1, axis + 1)


def setup_inputs(seed: int = 0) -> dict:
    inp = _fwd_setup_inputs(seed)
    key = _jax.random.fold_in(_jax.random.key(seed), 7919)
    shape, _ = _output_shape()
    out = dict(inp)
    out["loss_target"] = _jax.random.normal(_jax.random.fold_in(key, 0), shape, _jnp.float32)
    for i, name in enumerate(TWIN_WEIGHTS):
        w = inp[name].astype(_jnp.float32)
        if MOMENT_SCALE is None:
            s = _jnp.sqrt(_jnp.mean(_jnp.square(w)) + 1e-30)
        else:
            s = MOMENT_SCALE[name]
        km, kv = _jax.random.split(_jax.random.fold_in(key, i + 1))
        out[name] = w
        out["m_" + name] = s * _jax.random.normal(km, w.shape, _jnp.float32)
        out["v_" + name] = (s * s) * _jax.random.uniform(kv, w.shape, _jnp.float32, 0.5, 1.5)
    if N_MICROBATCH > 1:
        for name, axis in PER_EXAMPLE_BATCH_AXIS.items():
            out[name] = _to_microbatches(out[name], axis)
    return {'x': out['x'], 'mem': out['mem'], 'ffn1_norm': out['ffn1_norm'], 'ffn1_w13': out['ffn1_w13'], 'ffn1_w2': out['ffn1_w2'], 'mix_norm': out['mix_norm'], 'gmlp_w_in': out['gmlp_w_in'], 'gmlp_ln_g': out['gmlp_ln_g'], 'gmlp_ln_b': out['gmlp_ln_b'], 'gmlp_w_s': out['gmlp_w_s'], 'gmlp_b_s': out['gmlp_b_s'], 'gmlp_w_out': out['gmlp_w_out'], 'conv_w_in': out['conv_w_in'], 'conv_w': out['conv_w'], 'conv_w_out': out['conv_w_out'], 'xattn_norm': out['xattn_norm'], 'mem_norm': out['mem_norm'], 'xattn_wq': out['xattn_wq'], 'xattn_wkv': out['xattn_wkv'], 'xattn_wo': out['xattn_wo'], 'ffn2_norm': out['ffn2_norm'], 'ffn2_w13': out['ffn2_w13'], 'ffn2_w2': out['ffn2_w2'], 'final_norm': out['final_norm'], 'loss_target': out['loss_target'], 'm_ffn1_norm': out['m_ffn1_norm'], 'm_ffn1_w13': out['m_ffn1_w13'], 'm_ffn1_w2': out['m_ffn1_w2'], 'm_mix_norm': out['m_mix_norm'], 'm_gmlp_w_in': out['m_gmlp_w_in'], 'm_gmlp_ln_g': out['m_gmlp_ln_g'], 'm_gmlp_ln_b': out['m_gmlp_ln_b'], 'm_gmlp_w_s': out['m_gmlp_w_s'], 'm_gmlp_b_s': out['m_gmlp_b_s'], 'm_gmlp_w_out': out['m_gmlp_w_out'], 'm_conv_w_in': out['m_conv_w_in'], 'm_conv_w': out['m_conv_w'], 'm_conv_w_out': out['m_conv_w_out'], 'm_xattn_norm': out['m_xattn_norm'], 'm_mem_norm': out['m_mem_norm'], 'm_xattn_wq': out['m_xattn_wq'], 'm_xattn_wkv': out['m_xattn_wkv'], 'm_xattn_wo': out['m_xattn_wo'], 'm_ffn2_norm': out['m_ffn2_norm'], 'm_ffn2_w13': out['m_ffn2_w13'], 'm_ffn2_w2': out['m_ffn2_w2'], 'm_final_norm': out['m_final_norm'], 'v_ffn1_norm': out['v_ffn1_norm'], 'v_ffn1_w13': out['v_ffn1_w13'], 'v_ffn1_w2': out['v_ffn1_w2'], 'v_mix_norm': out['v_mix_norm'], 'v_gmlp_w_in': out['v_gmlp_w_in'], 'v_gmlp_ln_g': out['v_gmlp_ln_g'], 'v_gmlp_ln_b': out['v_gmlp_ln_b'], 'v_gmlp_w_s': out['v_gmlp_w_s'], 'v_gmlp_b_s': out['v_gmlp_b_s'], 'v_gmlp_w_out': out['v_gmlp_w_out'], 'v_conv_w_in': out['v_conv_w_in'], 'v_conv_w': out['v_conv_w'], 'v_conv_w_out': out['v_conv_w_out'], 'v_xattn_norm': out['v_xattn_norm'], 'v_mem_norm': out['v_mem_norm'], 'v_xattn_wq': out['v_xattn_wq'], 'v_xattn_wkv': out['v_xattn_wkv'], 'v_xattn_wo': out['v_xattn_wo'], 'v_ffn2_norm': out['v_ffn2_norm'], 'v_ffn2_w13': out['v_ffn2_w13'], 'v_ffn2_w2': out['v_ffn2_w2'], 'v_final_norm': out['v_final_norm']}


def _loss(weights, diff, rest, loss_target):
    with _jax.named_scope("forward"):
        args = {**rest, TWIN_DIFF_INPUT: diff, **{k: w.astype(_WEIGHT_DTYPES[k]) for k, w in weights.items()}}
        y = _forward(args)
    with _jax.named_scope("loss_head"):
        err = _jnp.square(y.astype(_jnp.float32) - loss_target)
        return 0.5 * _jnp.sum(_jnp.mean(err, axis=-1)) if err.ndim else 0.5 * err


def _adamw(w, g, m, v):
    m = ADAM_B1 * m + (1.0 - ADAM_B1) * g
    v = ADAM_B2 * v + (1.0 - ADAM_B2) * _jnp.square(g)
    m_hat = m / (1.0 - ADAM_B1 ** ADAM_STEP)
    v_hat = v / (1.0 - ADAM_B2 ** ADAM_STEP)
    delta = -ADAM_LR * (m_hat / (_jnp.sqrt(v_hat) + ADAM_EPS) + ADAM_WD * w)
    return delta, m, v


def reference(x, mem, ffn1_norm, ffn1_w13, ffn1_w2, mix_norm, gmlp_w_in, gmlp_ln_g, gmlp_ln_b, gmlp_w_s, gmlp_b_s, gmlp_w_out, conv_w_in, conv_w, conv_w_out, xattn_norm, mem_norm, xattn_wq, xattn_wkv, xattn_wo, ffn2_norm, ffn2_w13, ffn2_w2, final_norm, loss_target, m_ffn1_norm, m_ffn1_w13, m_ffn1_w2, m_mix_norm, m_gmlp_w_in, m_gmlp_ln_g, m_gmlp_ln_b, m_gmlp_w_s, m_gmlp_b_s, m_gmlp_w_out, m_conv_w_in, m_conv_w, m_conv_w_out, m_xattn_norm, m_mem_norm, m_xattn_wq, m_xattn_wkv, m_xattn_wo, m_ffn2_norm, m_ffn2_w13, m_ffn2_w2, m_final_norm, v_ffn1_norm, v_ffn1_w13, v_ffn1_w2, v_mix_norm, v_gmlp_w_in, v_gmlp_ln_g, v_gmlp_ln_b, v_gmlp_w_s, v_gmlp_b_s, v_gmlp_w_out, v_conv_w_in, v_conv_w, v_conv_w_out, v_xattn_norm, v_mem_norm, v_xattn_wq, v_xattn_wkv, v_xattn_wo, v_ffn2_norm, v_ffn2_w13, v_ffn2_w2, v_final_norm):
    given = dict(x=x, mem=mem, ffn1_norm=ffn1_norm, ffn1_w13=ffn1_w13, ffn1_w2=ffn1_w2, mix_norm=mix_norm, gmlp_w_in=gmlp_w_in, gmlp_ln_g=gmlp_ln_g, gmlp_ln_b=gmlp_ln_b, gmlp_w_s=gmlp_w_s, gmlp_b_s=gmlp_b_s, gmlp_w_out=gmlp_w_out, conv_w_in=conv_w_in, conv_w=conv_w, conv_w_out=conv_w_out, xattn_norm=xattn_norm, mem_norm=mem_norm, xattn_wq=xattn_wq, xattn_wkv=xattn_wkv, xattn_wo=xattn_wo, ffn2_norm=ffn2_norm, ffn2_w13=ffn2_w13, ffn2_w2=ffn2_w2, final_norm=final_norm, loss_target=loss_target, m_ffn1_norm=m_ffn1_norm, m_ffn1_w13=m_ffn1_w13, m_ffn1_w2=m_ffn1_w2, m_mix_norm=m_mix_norm, m_gmlp_w_in=m_gmlp_w_in, m_gmlp_ln_g=m_gmlp_ln_g, m_gmlp_ln_b=m_gmlp_ln_b, m_gmlp_w_s=m_gmlp_w_s, m_gmlp_b_s=m_gmlp_b_s, m_gmlp_w_out=m_gmlp_w_out, m_conv_w_in=m_conv_w_in, m_conv_w=m_conv_w, m_conv_w_out=m_conv_w_out, m_xattn_norm=m_xattn_norm, m_mem_norm=m_mem_norm, m_xattn_wq=m_xattn_wq, m_xattn_wkv=m_xattn_wkv, m_xattn_wo=m_xattn_wo, m_ffn2_norm=m_ffn2_norm, m_ffn2_w13=m_ffn2_w13, m_ffn2_w2=m_ffn2_w2, m_final_norm=m_final_norm, v_ffn1_norm=v_ffn1_norm, v_ffn1_w13=v_ffn1_w13, v_ffn1_w2=v_ffn1_w2, v_mix_norm=v_mix_norm, v_gmlp_w_in=v_gmlp_w_in, v_gmlp_ln_g=v_gmlp_ln_g, v_gmlp_ln_b=v_gmlp_ln_b, v_gmlp_w_s=v_gmlp_w_s, v_gmlp_b_s=v_gmlp_b_s, v_gmlp_w_out=v_gmlp_w_out, v_conv_w_in=v_conv_w_in, v_conv_w=v_conv_w, v_conv_w_out=v_conv_w_out, v_xattn_norm=v_xattn_norm, v_mem_norm=v_mem_norm, v_xattn_wq=v_xattn_wq, v_xattn_wkv=v_xattn_wkv, v_xattn_wo=v_xattn_wo, v_ffn2_norm=v_ffn2_norm, v_ffn2_w13=v_ffn2_w13, v_ffn2_w2=v_ffn2_w2, v_final_norm=v_final_norm)
    weights = {n: given[n] for n in TWIN_WEIGHTS}
    shared = {n: given[n] for n in SHARED_INPUTS}
    per_example = {n: given[n] for n in ['x', 'mem']}
    grad_fn = _jax.value_and_grad(_loss, argnums=(0, 1))

    def one_microbatch(ex, loss_target):
        ex = dict(ex)
        diff = ex.pop(TWIN_DIFF_INPUT)
        return grad_fn(weights, diff, {**shared, **ex}, loss_target)

    if N_MICROBATCH == 1:
        loss, (grad_w, grad_x) = one_microbatch(per_example, given["loss_target"])
    else:
        def body(carry, xs):
            loss_sum, grad_sum = carry
            l_k, (gw_k, gx_k) = one_microbatch(xs[0], xs[1])
            with _jax.named_scope("update"):
                return (loss_sum + l_k, _jax.tree.map(_jnp.add, grad_sum, gw_k)), gx_k

        init = (_jnp.zeros((), _jnp.float32), _jax.tree.map(_jnp.zeros_like, weights))
        (loss, grad_w), grad_x = _jax.lax.scan(body, init, (per_example, given["loss_target"]))
    with _jax.named_scope("update"):
        delta_w, new_m, new_v = {}, {}, {}
        for n in TWIN_WEIGHTS:
            delta_w[n], new_m[n], new_v[n] = _adamw(weights[n], grad_w[n], given["m_" + n], given["v_" + n])
    return (loss, grad_x, *[grad_w[n] for n in TWIN_WEIGHTS], *[delta_w[n] for n in TWIN_WEIGHTS],
            *[new_m[n] for n in TWIN_WEIGHTS], *[new_v[n] for n in TWIN_WEIGHTS])
```

```python
import functools
import math

import jax
import jax.numpy as jnp
from jax import lax
from jax.experimental import pallas as pl
from jax.experimental.pallas import tpu as pltpu

F32 = jnp.float32
BF16 = jnp.bfloat16

N_DEV = 8
N_PEERS = N_DEV - 1
CHUNK = 128
GROUPS = 8
HEADS = 4
CONV_WIDTH = 3
RMS_EPS = 1e-6
LN_EPS = 1e-5
ADAM_LR = 0.001
ADAM_B1 = 0.9
ADAM_B2 = 0.999
ADAM_EPS = 1e-08
ADAM_WD = 0.01
ADAM_STEP = 10
LANES = 128
BF16_SUBLANES = 16
VMEM_LIMIT_BYTES = 56 * 1024 * 1024

_NT = (((1,), (1,)), ((), ()))
_TN = (((0,), (0,)), ((), ()))
_SQRT_HALF = 0.7071067811865476
_INV_SQRT_2PI = 0.3989422804014327


def _div(n, pref, align):
    best = None
    for t in range(align, min(n, pref) + 1, align):
        if n % t == 0:
            best = t
    return n if best is None else best


def _call(body, name, grid, ins, outs, scratch=()):
    return pl.pallas_call(
        body,
        name=name,
        grid=grid,
        in_specs=[pl.BlockSpec(bs, im) for _, bs, im in ins],
        out_specs=[pl.BlockSpec(bs, im) for _, _, bs, im in outs],
        out_shape=[jax.ShapeDtypeStruct(s, d) for s, d, _, _ in outs],
        scratch_shapes=list(scratch),
        compiler_params=pltpu.CompilerParams(
            dimension_semantics=("arbitrary",) * len(grid), vmem_limit_bytes=VMEM_LIMIT_BYTES),
    )(*[a for a, _, _ in ins])


def _dot(a, b, dims=None):
    if dims is None:
        return jnp.dot(a, b, preferred_element_type=F32)
    return lax.dot_general(a, b, dims, preferred_element_type=F32)


def _sigmoid(v):
    return 1.0 / (1.0 + jnp.exp(-v))


def _gelu(v):
    return 0.5 * v * (1.0 + lax.erf(v * _SQRT_HALF))


def _gelu_grad(v):
    return 0.5 * (1.0 + lax.erf(v * _SQRT_HALF)) + v * (_INV_SQRT_2PI * jnp.exp(-0.5 * v * v))


def _accumulate(ref, part, first):
    @pl.when(first)
    def _():
        ref[...] = part

    @pl.when(jnp.logical_not(first))
    def _():
        ref[...] += part


def rms_fwd(x, g):
    T, D = x.shape
    tt = _div(T, 512, BF16_SUBLANES)

    def body(x_ref, g_ref, o_ref):
        xv = x_ref[...]
        r = lax.rsqrt(jnp.mean(xv * xv, axis=-1, keepdims=True) + RMS_EPS)
        o_ref[...] = ((xv * r) * g_ref[...]).astype(BF16)

    return _call(body, "rms_fwd", (T // tt,),
                 [(x, (tt, D), lambda i: (i, 0)), (g.reshape(1, D), (1, D), lambda i: (0, 0))],
                 [((T, D), BF16, (tt, D), lambda i: (i, 0))])[0]


def rms_bwd(x, g, dxn, d, scale):
    T, D = x.shape
    tt = _div(T, 256, BF16_SUBLANES)

    def body(x_ref, g_ref, dn_ref, d_ref, dx_ref, dxs_ref, dg_ref):
        xv = x_ref[...]
        r = lax.rsqrt(jnp.mean(xv * xv, axis=-1, keepdims=True) + RMS_EPS)
        xh = xv * r
        dn = dn_ref[...]
        dxh = dn * g_ref[...]
        dx = r * (dxh - xh * jnp.mean(dxh * xh, axis=-1, keepdims=True)) + d_ref[...]
        dx_ref[...] = dx
        dxs_ref[...] = (scale * dx).astype(BF16)
        _accumulate(dg_ref, jnp.sum(dn * xh, axis=0, keepdims=True), pl.program_id(0) == 0)

    row = lambda i: (i, 0)
    fix = lambda i: (0, 0)
    return _call(body, "rms_bwd", (T // tt,),
                 [(x, (tt, D), row), (g.reshape(1, D), (1, D), fix), (dxn, (tt, D), row), (d, (tt, D), row)],
                 [((T, D), F32, (tt, D), row), ((T, D), BF16, (tt, D), row), ((1, D), F32, (1, D), fix)])


def rms_gain_grad(x, dxn):
    T, D = x.shape
    tt = _div(T, 256, 8)

    def body(x_ref, dn_ref, dg_ref):
        xv = x_ref[...]
        r = lax.rsqrt(jnp.mean(xv * xv, axis=-1, keepdims=True) + RMS_EPS)
        _accumulate(dg_ref, jnp.sum(dn_ref[...] * (xv * r), axis=0, keepdims=True), pl.program_id(0) == 0)

    row = lambda i: (i, 0)
    return _call(body, "rms_gain_grad", (T // tt,), [(x, (tt, D), row), (dxn, (tt, D), row)],
                 [((1, D), F32, (1, D), lambda i: (0, 0))])[0]


def loss_head(x, g, target, scale):
    T, D = x.shape
    tt = _div(T, 256, BF16_SUBLANES)

    def body(x_ref, g_ref, t_ref, loss_ref, dx_ref, dxs_ref, dg_ref):
        first = pl.program_id(0) == 0
        xv = x_ref[...]
        gv = g_ref[...]
        r = lax.rsqrt(jnp.mean(xv * xv, axis=-1, keepdims=True) + RMS_EPS)
        xh = xv * r
        err = xh * gv - t_ref[...]
        part = 0.5 * jnp.sum(jnp.mean(err * err, axis=-1, keepdims=True), axis=0, keepdims=True)
        _accumulate(loss_ref, jnp.broadcast_to(part, (1, LANES)), first)
        dy = err * (1.0 / D)
        dxh = dy * gv
        dx = r * (dxh - xh * jnp.mean(dxh * xh, axis=-1, keepdims=True))
        dx_ref[...] = dx
        dxs_ref[...] = (scale * dx).astype(BF16)
        _accumulate(dg_ref, jnp.sum(dy * xh, axis=0, keepdims=True), first)

    row = lambda i: (i, 0)
    fix = lambda i: (0, 0)
    return _call(body, "loss_head", (T // tt,),
                 [(x, (tt, D), row), (g.reshape(1, D), (1, D), fix), (target, (tt, D), row)],
                 [((1, LANES), F32, (1, LANES), fix), ((T, D), F32, (tt, D), row),
                  ((T, D), BF16, (tt, D), row), ((1, D), F32, (1, D), fix)])


def mm_nn(a, w3, name, out_dtype=BF16, res=None, res_scale=1.0, tm_pref=1024, tn_pref=1024):
    M, K = a.shape
    nb, _, ns = w3.shape
    tn = _div(ns, tn_pref, LANES)
    per = ns // tn
    tm = _div(M, tm_pref, BF16_SUBLANES)
    ins = [(a, (tm, K), lambda j, m: (m, 0)), (w3, (None, K, tn), lambda j, m: (j // per, 0, j % per))]
    if res is not None:
        ins.append((res, (tm, tn), lambda j, m: (m, j)))

    def body(*refs):
        a_ref, w_ref = refs[0], refs[1]
        o_ref = refs[-1]
        acc = _dot(a_ref[...], w_ref[...])
        if res is not None:
            acc = refs[2][...] + res_scale * acc
        o_ref[...] = acc.astype(o_ref.dtype)

    return _call(body, name, (nb * per, M // tm), ins,
                 [((M, nb * ns), out_dtype, (tm, tn), lambda j, m: (m, j))])[0]


def mm_nt(a_in, w3, name, out_dtype, M, tm_pref=1024, to_pref=2048):
    nb, Ko, ns = w3.shape
    to = _div(Ko, to_pref, LANES)
    tm = _div(M, tm_pref, BF16_SUBLANES)
    if isinstance(a_in, tuple):
        a, a_bs, a_im = a_in
        a_bs = tuple(tm if s == "tm" else s for s in a_bs)
    else:
        a, a_bs, a_im = a_in, (tm, ns), lambda m, o, b: (m, b)
    if nb > 1:
        assert out_dtype == F32

    def body(a_ref, w_ref, o_ref):
        p = _dot(a_ref[...], w_ref[...], _NT)
        if nb == 1:
            o_ref[...] = p.astype(o_ref.dtype)
        else:
            _accumulate(o_ref, p, pl.program_id(2) == 0)

    return _call(body, name, (M // tm, Ko // to, nb),
                 [(a, a_bs, a_im), (w3, (None, to, ns), lambda m, o, b: (b, o, 0))],
                 [((M, Ko), out_dtype, (tm, to), lambda m, o, b: (m, o))])[0]


def mm_tn(a, b_in, name, nbo, ns, tka_pref=2048, tt_pref=512, tn_pref=2048):
    T, Ka = a.shape
    tt = _div(T, tt_pref, BF16_SUBLANES)
    tka = _div(Ka, tka_pref, LANES)
    if isinstance(b_in, tuple):
        b, b_bs, b_im = b_in
        b_bs = tuple(tt if s == "tt" else s for s in b_bs)
        tn, per = ns, 1
    else:
        tn = _div(ns, tn_pref, LANES)
        per = ns // tn
        b, b_bs, b_im = b_in, (tt, tn), lambda i, j, t: (t, j)
    nt = T // tt

    def body(a_ref, b_ref, o_ref, acc_ref):
        t = pl.program_id(2)
        _accumulate(acc_ref, _dot(a_ref[...], b_ref[...], _TN), t == 0)

        @pl.when(t == nt - 1)
        def _():
            o_ref[...] = acc_ref[...].astype(BF16)

    return _call(body, name, (Ka // tka, nbo * per, nt),
                 [(a, (tt, tka), lambda i, j, t: (t, i)), (b, b_bs, b_im)],
                 [((nbo, Ka, ns), BF16, (None, tka, tn), lambda i, j, t: (j // per, i, j % per))],
                 scratch=[pltpu.VMEM((tka, tn), F32)])[0]


def ffn_up(xn, w13):
    T, D = xn.shape
    nb, _, ns = w13.shape
    half = nb // 2
    F = half * ns
    tm = _div(T, 512, BF16_SUBLANES)

    def body(x_ref, wg_ref, wu_ref, gu_ref, act_ref):
        xv = x_ref[...]
        gate = _dot(xv, wg_ref[...])
        up = _dot(xv, wu_ref[...])
        gu_ref[0] = gate.astype(BF16)
        gu_ref[1] = up.astype(BF16)
        act_ref[...] = (gate * _sigmoid(gate) * up).astype(BF16)

    return _call(body, "ffn_up", (half, T // tm),
                 [(xn, (tm, D), lambda j, m: (m, 0)),
                  (w13, (None, D, ns), lambda j, m: (j, 0, 0)),
                  (w13, (None, D, ns), lambda j, m: (j + half, 0, 0))],
                 [((2, T, F), BF16, (2, tm, ns), lambda j, m: (0, m, j)),
                  ((T, F), BF16, (tm, ns), lambda j, m: (m, j))])


def ffn_dact(dy, w2, gu):
    T, D = dy.shape
    F = w2.shape[0]
    tm = _div(T, 1024, BF16_SUBLANES)
    tn = _div(F, 512, LANES)

    def body(dy_ref, w_ref, gu_ref, dh_ref):
        da = _dot(dy_ref[...], w_ref[...], _NT)
        gate = gu_ref[0].astype(F32)
        up = gu_ref[1].astype(F32)
        s = _sigmoid(gate)
        dh_ref[0] = (da * up * (s * (1.0 + gate * (1.0 - s)))).astype(BF16)
        dh_ref[1] = (da * (gate * s)).astype(BF16)

    return _call(body, "ffn_dact", (F // tn, T // tm),
                 [(dy, (tm, D), lambda j, m: (m, 0)), (w2, (tn, D), lambda j, m: (j, 0)),
                  (gu, (2, tm, tn), lambda j, m: (0, m, j))],
                 [((2, T, F), BF16, (2, tm, tn), lambda j, m: (0, m, j))])[0]


def ffn_fwd(x, norm_g, w13, w2):
    xn = rms_fwd(x, norm_g)
    gu, act = ffn_up(xn, w13)
    F = act.shape[1]
    y = mm_nn(act, w2.reshape(1, F, -1), "ffn_down", F32, res=x, res_scale=0.5, tm_pref=512)
    return y, (x, xn, gu, act)


def ffn_bwd(d, dys, saved, norm_g, w13, w2, scale_out):
    x, xn, gu, act = saved
    T, D = x.shape
    nb, _, ns = w13.shape
    half = nb // 2
    F = half * ns
    dh = ffn_dact(dys, w2.reshape(F, D), gu)
    dw2 = mm_tn(act, dys, "ffn_dw2", 1, D, tka_pref=ns)
    dw13 = mm_tn(xn, (dh, (None, "tt", ns), lambda i, j, t: (j // half, t, j % half)), "ffn_dw13", nb, ns)
    dxn = mm_nt((dh, (None, "tm", ns), lambda m, o, b: (b // half, m, b % half)), w13, "ffn_dxn", F32, T)
    dx, dxs, dg = rms_bwd(x, norm_g, dxn, d, scale_out)
    return dx, dxs, dg, dw13, dw2.reshape(N_DEV, F // N_DEV, D)


def _gmlp_parts(p_ref, lng_ref, lnb_ref):
    E = lng_ref.shape[-1]
    z = _gelu(p_ref[...].astype(F32))
    u = z[:, :E]
    vp = z[:, E:]
    mu = jnp.mean(vp, axis=-1, keepdims=True)
    xc = vp - mu
    rstd = lax.rsqrt(jnp.mean(xc * xc, axis=-1, keepdims=True) + LN_EPS)
    vh = xc * rstd
    v = vh * lng_ref[...] + lnb_ref[...]
    return u, vh, rstd, v


def _causal_ws(ws_ref, g):
    keep = lax.broadcasted_iota(jnp.int32, (CHUNK, CHUNK), 0) >= lax.broadcasted_iota(jnp.int32, (CHUNK, CHUNK), 1)
    return jnp.where(keep, ws_ref[g], 0.0).astype(BF16), keep


def gmlp_mid_fwd(p, ln_g, ln_b, w_s, bias_full):
    T, E2 = p.shape
    E = E2 // 2
    gd = E // GROUPS
    tm = _div(T, 256, CHUNK)
    fix2 = lambda i: (0, 0)

    def body(p_ref, lng_ref, lnb_ref, ws_ref, bias_ref, o_ref):
        u, _, _, v = _gmlp_parts(p_ref, lng_ref, lnb_ref)
        vb = v.astype(BF16)
        for g in range(GROUPS):
            wm, _ = _causal_ws(ws_ref, g)
            cols = slice(g * gd, (g + 1) * gd)
            for c in range(tm // CHUNK):
                rows = slice(c * CHUNK, (c + 1) * CHUNK)
                f = _dot(wm, vb[rows, cols]) + bias_ref[:, cols]
                o_ref[rows, cols] = (u[rows, cols] * f).astype(BF16)

    return _call(body, "gmlp_mid_fwd", (T // tm,),
                 [(p, (tm, E2), lambda i: (i, 0)), (ln_g, (1, E), fix2), (ln_b, (1, E), fix2),
                  (w_s, (GROUPS, CHUNK, CHUNK), lambda i: (0, 0, 0)), (bias_full, (CHUNK, E), fix2)],
                 [((T, E), BF16, (tm, E), lambda i: (i, 0))])[0]


def gmlp_mid_bwd(p, dgated, ln_g, ln_b, w_s, bias_full):
    T, E2 = p.shape
    E = E2 // 2
    gd = E // GROUPS
    tm = _div(T, 256, CHUNK)
    nsteps = T // tm
    fix2 = lambda i: (0, 0)
    fix3 = lambda i: (0, 0, 0)

    def body(p_ref, dg_ref, lng_ref, lnb_ref, ws_ref, bias_ref,
             dp_ref, dws_ref, dbs_ref, dlng_ref, dlnb_ref, f_sc, dv_sc, db_sc):
        i = pl.program_id(0)
        first = i == 0
        u, vh, rstd, v = _gmlp_parts(p_ref, lng_ref, lnb_ref)
        vb = v.astype(BF16)
        dgt = dg_ref[...].astype(F32)
        df = dgt * u
        dfb = df.astype(BF16)
        for g in range(GROUPS):
            wm, keep = _causal_ws(ws_ref, g)
            cols = slice(g * gd, (g + 1) * gd)
            dw = None
            dbg = None
            for c in range(tm // CHUNK):
                rows = slice(c * CHUNK, (c + 1) * CHUNK)
                f_sc[rows, cols] = _dot(wm, vb[rows, cols]) + bias_ref[:, cols]
                dv_sc[rows, cols] = _dot(wm, dfb[rows, cols], _TN)
                part = _dot(dfb[rows, cols], vb[rows, cols], _NT)
                dw = part if dw is None else dw + part
                dbg = df[rows, cols] if dbg is None else dbg + df[rows, cols]
            dw = jnp.where(keep, dw, 0.0)

            @pl.when(first)
            def _():
                dws_ref[g] = dw
                db_sc[:, cols] = dbg

            @pl.when(jnp.logical_not(first))
            def _():
                dws_ref[g] += dw
                db_sc[:, cols] += dbg

        du = dgt * f_sc[...]
        dv = dv_sc[...]
        _accumulate(dlng_ref, jnp.sum(dv * vh, axis=0, keepdims=True), first)
        _accumulate(dlnb_ref, jnp.sum(dv, axis=0, keepdims=True), first)
        dvh = dv * lng_ref[...]
        dvp = rstd * (dvh - jnp.mean(dvh, axis=-1, keepdims=True)
                      - vh * jnp.mean(dvh * vh, axis=-1, keepdims=True))
        gp = _gelu_grad(p_ref[...].astype(F32))
        dp_ref[:, :E] = (du * gp[:, :E]).astype(BF16)
        dp_ref[:, E:] = (dvp * gp[:, E:]).astype(BF16)

        @pl.when(i == nsteps - 1)
        def _():
            for g in range(GROUPS):
                tot = jnp.sum(db_sc[:, g * gd:(g + 1) * gd], axis=-1, keepdims=True)
                dbs_ref[g] = jnp.broadcast_to(tot, (CHUNK, LANES))

    return _call(body, "gmlp_mid_bwd", (nsteps,),
                 [(p, (tm, E2), lambda i: (i, 0)), (dgated, (tm, E), lambda i: (i, 0)),
                  (ln_g, (1, E), fix2), (ln_b, (1, E), fix2),
                  (w_s, (GROUPS, CHUNK, CHUNK), fix3), (bias_full, (CHUNK, E), fix2)],
                 [((T, E2), BF16, (tm, E2), lambda i: (i, 0)),
                  ((GROUPS, CHUNK, CHUNK), F32, (GROUPS, CHUNK, CHUNK), fix3),
                  ((GROUPS, CHUNK, LANES), F32, (GROUPS, CHUNK, LANES), fix3),
                  ((1, E), F32, (1, E), fix2), ((1, E), F32, (1, E), fix2)],
                 scratch=[pltpu.VMEM((tm, E), F32), pltpu.VMEM((tm, E), F32), pltpu.VMEM((CHUNK, E), F32)])


HALO = 16


def _row_of(block, r):
    rows = lax.broadcasted_iota(jnp.int32, block.shape, 0)
    return jnp.sum(jnp.where(rows == r, block, 0.0), axis=0, keepdims=True)


def _shift_down(z, k, fill):
    out = pltpu.roll(z, k, 0)
    rows = lax.broadcasted_iota(jnp.int32, z.shape, 0)
    for t in range(k):
        out = jnp.where(rows == t, fill[t], out)
    return out


def _shift_up(z, k, fill):
    n = z.shape[0]
    out = pltpu.roll(z, n - k, 0)
    rows = lax.broadcasted_iota(jnp.int32, z.shape, 0)
    for j in range(k):
        out = jnp.where(rows == n - k + j, fill[j], out)
    return out


def _conv_parts(p_ref, prev_ref, cw_ref, is_first):
    D = cw_ref.shape[-1]
    pv = p_ref[...].astype(F32)
    bg, cg, val = pv[:, :D], pv[:, D:2 * D], pv[:, 2 * D:]
    z = cg * val
    pp = prev_ref[...].astype(F32)
    zp = jnp.where(is_first, 0.0, pp[:, D:2 * D] * pp[:, 2 * D:])
    zl1 = _row_of(zp, HALO - 1)
    zl2 = _row_of(zp, HALO - 2)
    z1 = _shift_down(z, 1, [zl1])
    z2 = _shift_down(z, 2, [zl2, zl1])
    conv = z2 * cw_ref[0:1, :] + z1 * cw_ref[1:2, :] + z * cw_ref[2:3, :]
    return bg, cg, val, z, z1, z2, conv


def conv_mid_fwd(p, cw):
    T, D3 = p.shape
    D = D3 // 3
    tm = _div(T, 256, HALO)
    per = tm // HALO

    def body(p_ref, prev_ref, cw_ref, o_ref):
        bg, _, _, _, _, _, conv = _conv_parts(p_ref, prev_ref, cw_ref, pl.program_id(0) == 0)
        o_ref[...] = (bg * conv).astype(BF16)

    return _call(body, "conv_mid_fwd", (T // tm,),
                 [(p, (tm, D3), lambda i: (i, 0)),
                  (p, (HALO, D3), lambda i: (jnp.maximum(i * per - 1, 0), 0)),
                  (cw, (CONV_WIDTH, D), lambda i: (0, 0))],
                 [((T, D), BF16, (tm, D), lambda i: (i, 0))])[0]


def conv_mid_bwd(p, dgated, cw):
    T, D3 = p.shape
    D = D3 // 3
    tm = _div(T, 256, HALO)
    per = tm // HALO
    nsteps = T // tm
    last_halo = T // HALO - 1
    nxt = lambda i: (jnp.minimum((i + 1) * per, last_halo), 0)

    def body(p_ref, prev_ref, next_ref, dg_ref, dgn_ref, cw_ref, dp_ref, dcw_ref):
        i = pl.program_id(0)
        bg, cg, val, z, z1, z2, conv = _conv_parts(p_ref, prev_ref, cw_ref, i == 0)
        dgt = dg_ref[...].astype(F32)
        dconv = dgt * bg
        dcn = jnp.where(i == nsteps - 1, 0.0, dgn_ref[...].astype(F32) * next_ref[:, :D].astype(F32))
        n0 = _row_of(dcn, 0)
        n1 = _row_of(dcn, 1)
        up1 = _shift_up(dconv, 1, [n0])
        up2 = _shift_up(dconv, 2, [n0, n1])
        dz = dconv * cw_ref[2:3, :] + up1 * cw_ref[1:2, :] + up2 * cw_ref[0:1, :]
        dp_ref[:, :D] = (dgt * conv).astype(BF16)
        dp_ref[:, D:2 * D] = (dz * val).astype(BF16)
        dp_ref[:, 2 * D:] = (dz * cg).astype(BF16)
        first = i == 0
        parts = (jnp.sum(dconv * z2, axis=0, keepdims=True), jnp.sum(dconv * z1, axis=0, keepdims=True),
                 jnp.sum(dconv * z, axis=0, keepdims=True))

        @pl.when(first)
        def _():
            for k in range(CONV_WIDTH):
                dcw_ref[k:k + 1, :] = parts[k]

        @pl.when(jnp.logical_not(first))
        def _():
            for k in range(CONV_WIDTH):
                dcw_ref[k:k + 1, :] += parts[k]

    return _call(body, "conv_mid_bwd", (nsteps,),
                 [(p, (tm, D3), lambda i: (i, 0)),
                  (p, (HALO, D3), lambda i: (jnp.maximum(i * per - 1, 0), 0)),
                  (p, (HALO, D3), nxt),
                  (dgated, (tm, D), lambda i: (i, 0)),
                  (dgated, (HALO, D), nxt),
                  (cw, (CONV_WIDTH, D), lambda i: (0, 0))],
                 [((T, D3), BF16, (tm, D3), lambda i: (i, 0)),
                  ((CONV_WIDTH, D), F32, (CONV_WIDTH, D), lambda i: (0, 0))])


def mixer_bwd_common(d, dys, saved, norm_g, w_in, w_out, mid_bwd, scale_out):
    x, hn, p, gated = saved
    T, D = x.shape
    E = gated.shape[1]
    w_out3 = w_out.reshape(1, E, D)
    dgated = mm_nt(dys, w_out3, "mix_dgated", BF16, T)
    dw_out = mm_tn(gated, dys, "mix_dwout", 1, D)
    dp, extra = mid_bwd(p, dgated)
    nb, _, ns = w_in.shape
    dw_in = mm_tn(hn, dp, "mix_dwin", nb, ns)
    dhn = mm_nt(dp, w_in, "mix_dhn", F32, T)
    dx, dxs, dg = rms_bwd(x, norm_g, dhn, d, scale_out)
    return dx, dxs, dg, dw_in, dw_out.reshape(N_DEV, E // N_DEV, D), extra


def _softmax_rows(s):
    e = jnp.exp(s - jnp.max(s, axis=-1, keepdims=True))
    return e / jnp.sum(e, axis=-1, keepdims=True)


def attn_fwd(q, kv):
    T, D = q.shape
    M = kv.shape[0]
    hd = D // HEADS
    scale = hd ** -0.5
    tm = _div(T, 512, BF16_SUBLANES)

    def body(q_ref, kv_ref, o_ref):
        for h in range(HEADS):
            cols = slice(h * hd, (h + 1) * hd)
            s = _dot(q_ref[:, cols], kv_ref[:, cols], _NT) * scale
            pr = _softmax_rows(s).astype(BF16)
            o_ref[:, cols] = _dot(pr, kv_ref[:, D + h * hd:D + (h + 1) * hd]).astype(BF16)

    return _call(body, "attn_fwd", (T // tm,),
                 [(q, (tm, D), lambda i: (i, 0)), (kv, (M, 2 * D), lambda i: (0, 0))],
                 [((T, D), BF16, (tm, D), lambda i: (i, 0))])[0]


def attn_bwd(q, do, kv):
    T, D = q.shape
    M = kv.shape[0]
    hd = D // HEADS
    scale = hd ** -0.5
    tm = _div(T, 512, BF16_SUBLANES)
    nsteps = T // tm

    def body(q_ref, do_ref, kv_ref, dq_ref, dkv_ref, acc_ref):
        i = pl.program_id(0)
        for h in range(HEADS):
            cols = slice(h * hd, (h + 1) * hd)
            vcols = slice(D + h * hd, D + (h + 1) * hd)
            qh = q_ref[:, cols]
            kh = kv_ref[:, cols]
            doh = do_ref[:, cols]
            pr = _softmax_rows(_dot(qh, kh, _NT) * scale)
            dpr = _dot(doh, kv_ref[:, vcols], _NT)
            ds = (pr * (dpr - jnp.sum(dpr * pr, axis=-1, keepdims=True)) * scale).astype(BF16)
            dq_ref[:, cols] = _dot(ds, kh).astype(BF16)
            dk = _dot(ds, qh, _TN)
            dv = _dot(pr.astype(BF16), doh, _TN)

            @pl.when(i == 0)
            def _():
                acc_ref[:, cols] = dk
                acc_ref[:, vcols] = dv

            @pl.when(i > 0)
            def _():
                acc_ref[:, cols] += dk
                acc_ref[:, vcols] += dv

        @pl.when(i == nsteps - 1)
        def _():
            dkv_ref[...] = acc_ref[...].astype(BF16)

    return _call(body, "attn_bwd", (nsteps,),
                 [(q, (tm, D), lambda i: (i, 0)), (do, (tm, D), lambda i: (i, 0)),
                  (kv, (M, 2 * D), lambda i: (0, 0))],
                 [((T, D), BF16, (tm, D), lambda i: (i, 0)), ((M, 2 * D), BF16, (M, 2 * D), lambda i: (0, 0))],
                 scratch=[pltpu.VMEM((M, 2 * D), F32)])


def xattn_fwd(x, mem, xnorm_g, mnorm_g, wq, wkv, wo):
    D = x.shape[1]
    hq = rms_fwd(x, xnorm_g)
    mn = rms_fwd(mem, mnorm_g)
    q = mm_nn(hq, wq.reshape(1, D, D), "xattn_q")
    kv = mm_nn(mn, wkv, "xattn_kv")
    o = attn_fwd(q, kv)
    y = mm_nn(o, wo.reshape(1, D, D), "xattn_out", F32, res=x)
    return y, (x, hq, mn, q, kv, o)


def xattn_bwd(d, dys, saved, mem, xnorm_g, wq, wkv, wo, scale_out):
    x, hq, mn, q, kv, o = saved
    T, D = x.shape
    M = mem.shape[0]
    do = mm_nt(dys, wo.reshape(1, D, D), "xattn_do", BF16, T)
    dwo = mm_tn(o, dys, "xattn_dwo", 1, D)
    dq, dkv = attn_bwd(q, do, kv)
    dwq = mm_tn(hq, dq, "xattn_dwq", 1, D)
    dhq = mm_nt(dq, wq.reshape(1, D, D), "xattn_dhq", F32, T)
    dx, dxs, dgx = rms_bwd(x, xnorm_g, dhq, d, scale_out)
    nb, _, ns = wkv.shape
    dwkv = mm_tn(mn, dkv, "xattn_dwkv", nb, ns)
    dmn = mm_nt(dkv, wkv, "xattn_dmn", F32, M)
    dgm = rms_gain_grad(mem, dmn)
    rows = D // N_DEV
    return dx, dxs, dgx, dgm, dwq.reshape(N_DEV, rows, D), dwkv, dwo.reshape(N_DEV, rows, D)


def _mesh_places():
    x, y, c = lax.axis_index("x"), lax.axis_index("y"), lax.axis_index("c")
    chips = [(1 - x, y), (x, 1 - y), (1 - x, 1 - y)]
    return (x, y, c), (x, y, 1 - c), chips


def _slot(place):
    return 4 * place[0] + 2 * place[1] + place[2]


_ANY = pl.BlockSpec(memory_space=pl.ANY)


def all_gather_stage(shards):
    n = len(shards)
    arrays = [a for a, _ in shards]
    shapes = [a.shape if l is None else a.shape[1:] for a, l in shards]

    def body(*refs):
        ins = [r if l is None else r.at[l] for r, (_, l) in zip(refs[:n], shards)]
        outs = refs[n:2 * n]
        send_sems, recv_sems, local_sems = refs[2 * n:]
        me, sibling, chips = _mesh_places()
        c = me[2]

        def copy(a, k, block, to, src=None):
            dst = outs[a].at[_slot(block)]
            return pltpu.make_async_remote_copy(
                src_ref=dst if src is None else src, dst_ref=dst,
                send_sem=send_sems.at[a * N_PEERS + k], recv_sem=recv_sems.at[a * N_PEERS + k],
                device_id=to, device_id_type=pl.DeviceIdType.MESH)

        mine = [pltpu.make_async_copy(ins[a], outs[a].at[_slot(me)], local_sems.at[a]) for a in range(n)]
        for cp in mine:
            cp.start()
        first = []
        for a in range(n):
            first.append(copy(a, 0, me, sibling, src=ins[a]))
            first += [copy(a, 1 + j, me, (*chip, c), src=ins[a]) for j, chip in enumerate(chips)]
        for cp in first:
            cp.start()
        passed = []
        for j, chip in enumerate(chips):
            for a in range(n):
                copy(a, 1 + j, (*chip, c), me).wait_recv()
                cp = copy(a, 4 + j, (*chip, c), sibling)
                cp.start()
                passed.append(cp)
        for a in range(n):
            copy(a, 0, sibling, me).wait_recv()
        for j, chip in enumerate(chips):
            for a in range(n):
                copy(a, 4 + j, (*chip, 1 - c), me).wait_recv()
        for cp in first + passed:
            cp.wait_send()
        for cp in mine:
            cp.wait()

    return pl.pallas_call(
        body, name="all_gather_stage",
        out_shape=[jax.ShapeDtypeStruct((N_DEV,) + tuple(s), BF16) for s in shapes],
        in_specs=[_ANY] * n, out_specs=[_ANY] * n,
        scratch_shapes=[pltpu.SemaphoreType.DMA((n * N_PEERS,)), pltpu.SemaphoreType.DMA((n * N_PEERS,)),
                        pltpu.SemaphoreType.DMA((n,))],
    )(*arrays)


def _all_peers(me, chips):
    c = me[2]
    return [(me[0], me[1], 1 - c)] + [(*chip, c) for chip in chips] + [(*chip, 1 - c) for chip in chips]


def scatter_partials_stage(grads):
    n = len(grads)

    def body(*refs):
        ins = refs[:n]
        outs = refs[n:2 * n]
        send_sems, recv_sems, local_sems = refs[2 * n:]
        me, _, chips = _mesh_places()
        peers = _all_peers(me, chips)
        mine = [pltpu.make_async_copy(ins[a].at[_slot(me)], outs[a].at[_slot(me)], local_sems.at[a])
                for a in range(n)]
        for cp in mine:
            cp.start()
        sends, recvs = [], []
        for a in range(n):
            for k, peer in enumerate(peers):
                sem = dict(send_sem=send_sems.at[a * N_PEERS + k], recv_sem=recv_sems.at[a * N_PEERS + k],
                           device_id=peer, device_id_type=pl.DeviceIdType.MESH)
                sends.append(pltpu.make_async_remote_copy(
                    src_ref=ins[a].at[_slot(peer)], dst_ref=outs[a].at[_slot(me)], **sem))
                recvs.append(pltpu.make_async_remote_copy(
                    src_ref=ins[a].at[_slot(peer)], dst_ref=outs[a].at[_slot(peer)], **sem))
        for cp in sends:
            cp.start()
        for cp in recvs:
            cp.wait_recv()
        for cp in sends:
            cp.wait_send()
        for cp in mine:
            cp.wait()

    return pl.pallas_call(
        body, name="scatter_partials_stage",
        out_shape=[jax.ShapeDtypeStruct(g.shape, g.dtype) for g in grads],
        in_specs=[_ANY] * n, out_specs=[_ANY] * n,
        scratch_shapes=[pltpu.SemaphoreType.DMA((n * N_PEERS,)), pltpu.SemaphoreType.DMA((n * N_PEERS,)),
                        pltpu.SemaphoreType.DMA((n,))],
    )(*grads)


def small_all_reduce(vec):
    R = vec.shape[0]

    def body(v_ref, o_ref, all_ref, send_sems, recv_sems):
        me, _, chips = _mesh_places()
        peers = _all_peers(me, chips)
        all_ref[_slot(me)] = v_ref[...]
        sends, recvs = [], []
        for k, peer in enumerate(peers):
            sem = dict(send_sem=send_sems.at[k], recv_sem=recv_sems.at[k],
                       device_id=peer, device_id_type=pl.DeviceIdType.MESH)
            sends.append(pltpu.make_async_remote_copy(src_ref=v_ref, dst_ref=all_ref.at[_slot(me)], **sem))
            recvs.append(pltpu.make_async_remote_copy(src_ref=v_ref, dst_ref=all_ref.at[_slot(peer)], **sem))
        for cp in sends:
            cp.start()
        for cp in recvs:
            cp.wait_recv()
        for cp in sends:
            cp.wait_send()
        acc = all_ref[0]
        for s in range(1, N_DEV):
            acc = acc + all_ref[s]
        o_ref[...] = acc

    return pl.pallas_call(
        body, name="small_all_reduce",
        out_shape=jax.ShapeDtypeStruct(vec.shape, F32),
        in_specs=[pl.BlockSpec(memory_space=pltpu.VMEM)], out_specs=pl.BlockSpec(memory_space=pltpu.VMEM),
        scratch_shapes=[pltpu.VMEM((N_DEV, R, LANES), F32), pltpu.SemaphoreType.DMA((N_PEERS,)),
                        pltpu.SemaphoreType.DMA((N_PEERS,))],
    )(vec)


def _adamw_math(w, g, m, v):
    m2 = ADAM_B1 * m + (1.0 - ADAM_B1) * g
    v2 = ADAM_B2 * v + (1.0 - ADAM_B2) * (g * g)
    m_hat = m2 / (1.0 - ADAM_B1 ** ADAM_STEP)
    v_hat = v2 / (1.0 - ADAM_B2 ** ADAM_STEP)
    delta = -ADAM_LR * (m_hat / (jnp.sqrt(v_hat) + ADAM_EPS) + ADAM_WD * w)
    return delta, m2, v2


def adamw_sharded(partials, w, m, v):
    L, r, c = w.shape
    tr = _div(r, max(BF16_SUBLANES, (1 << 18) // c), BF16_SUBLANES)
    nt = r // tr

    def part_map(l0):
        return lambda l, t: (0, jnp.where(l == l0, t, jnp.where(l < l0, 0, nt - 1)), 0)

    def body(*refs):
        parts = refs[:L]
        w_ref, m_ref, v_ref, g_out, d_out, m_out, v_out = refs[L:]
        layer = pl.program_id(0)
        for l0 in range(L):
            @pl.when(layer == l0)
            def _():
                g = parts[l0][0].astype(F32)
                for s in range(1, N_DEV):
                    g = g + parts[l0][s].astype(F32)
                delta, m2, v2 = _adamw_math(w_ref[...], g, m_ref[...], v_ref[...])
                g_out[...] = g
                d_out[...] = delta
                m_out[...] = m2
                v_out[...] = v2

    own = lambda l, t: (l, t, 0)
    return _call(body, "adamw_sharded", (L, nt),
                 [(p, (N_DEV, tr, c), part_map(l0)) for l0, p in enumerate(partials)]
                 + [(w, (None, tr, c), own), (m, (None, tr, c), own), (v, (None, tr, c), own)],
                 [((L, r, c), F32, (None, tr, c), own)] * 4)


def adamw_flat(g, w, m, v):
    shape = g.shape

    def body(g_ref, w_ref, m_ref, v_ref, d_out, m_out, v_out):
        delta, m2, v2 = _adamw_math(w_ref[...], g_ref[...], m_ref[...], v_ref[...])
        d_out[...] = delta
        m_out[...] = m2
        v_out[...] = v2

    whole = lambda: (0, 0)
    return _call(body, "adamw_flat", (), [(a, shape, whole) for a in (g, w, m, v)],
                 [(shape, F32, shape, whole)] * 3)


def _pack(parts):
    flat = jnp.concatenate([p.reshape(-1).astype(F32) for p in parts])
    rows = -(-flat.shape[0] // (8 * LANES)) * 8
    return jnp.pad(flat, (0, rows * LANES - flat.shape[0])).reshape(rows, LANES)


def _unpack(packed, shapes):
    flat = packed.reshape(-1)
    out, off = [], 0
    for s in shapes:
        size = math.prod(s)
        out.append(flat[off:off + size].reshape(s))
        off += size
    return out


def kernel(x, mem, ffn1_norm, ffn1_w13, ffn1_w2, mix_norm, gmlp_w_in, gmlp_ln_g, gmlp_ln_b, gmlp_w_s, gmlp_b_s, gmlp_w_out, conv_w_in, conv_w, conv_w_out, xattn_norm, mem_norm, xattn_wq, xattn_wkv, xattn_wo, ffn2_norm, ffn2_w13, ffn2_w2, final_norm, loss_target, m_ffn1_norm, m_ffn1_w13, m_ffn1_w2, m_mix_norm, m_gmlp_w_in, m_gmlp_ln_g, m_gmlp_ln_b, m_gmlp_w_s, m_gmlp_b_s, m_gmlp_w_out, m_conv_w_in, m_conv_w, m_conv_w_out, m_xattn_norm, m_mem_norm, m_xattn_wq, m_xattn_wkv, m_xattn_wo, m_ffn2_norm, m_ffn2_w13, m_ffn2_w2, m_final_norm, v_ffn1_norm, v_ffn1_w13, v_ffn1_w2, v_mix_norm, v_gmlp_w_in, v_gmlp_ln_g, v_gmlp_ln_b, v_gmlp_w_s, v_gmlp_b_s, v_gmlp_w_out, v_conv_w_in, v_conv_w, v_conv_w_out, v_xattn_norm, v_mem_norm, v_xattn_wq, v_xattn_wkv, v_xattn_wo, v_ffn2_norm, v_ffn2_w13, v_ffn2_w2, v_final_norm):
    given = dict(locals())
    T, D = x.shape[1], x.shape[2]
    depth = ffn1_norm.shape[0]
    xs = x.reshape(T, D)
    mems = mem.reshape(mem.shape[1], D)
    target = loss_target.reshape(T, D)
    me = 4 * lax.axis_index("x") + 2 * lax.axis_index("y") + lax.axis_index("c")
    E = gmlp_ln_g.shape[1]
    gd = E // GROUPS
    cshard = conv_w.shape[2]

    bf = {k: given[k].astype(BF16) for k in
          ("ffn1_w13", "ffn1_w2", "gmlp_w_in", "gmlp_w_out", "conv_w_in", "conv_w_out",
           "xattn_wq", "xattn_wkv", "xattn_wo", "ffn2_w13", "ffn2_w2")}

    def gather(names_layers):
        got = all_gather_stage([(bf[k], l) for k, l in names_layers])
        return {kl: g for kl, g in zip(names_layers, got)}

    cw_place = lax.dynamic_update_slice(jnp.zeros((CONV_WIDTH, D), F32), conv_w[0], (jnp.int32(0), me * cshard))
    cw_full = small_all_reduce(_pack([cw_place])).reshape(-1)[:CONV_WIDTH * D].reshape(CONV_WIDTH, D)
    bias_full = jnp.repeat(gmlp_b_s[0].T, gd, axis=1)

    W = {}
    saved = []
    h = xs
    for i in range(depth):
        j = i // 2
        is_gmlp = i % 2 == 0
        W.update(gather([("ffn1_w13", i), ("ffn1_w2", i)]))
        h, sv1 = ffn_fwd(h, ffn1_norm[i], W["ffn1_w13", i], W["ffn1_w2", i])
        mix = ("gmlp_w_in", "gmlp_w_out") if is_gmlp else ("conv_w_in", "conv_w_out")
        W.update(gather([(mix[0], j), (mix[1], j), ("xattn_wq", i), ("xattn_wkv", i), ("xattn_wo", i)]))
        hn = rms_fwd(h, mix_norm[i])
        p = mm_nn(hn, W[mix[0], j], "mix_in", tn_pref=768)
        if is_gmlp:
            gated = gmlp_mid_fwd(p, gmlp_ln_g[j:j + 1], gmlp_ln_b[j:j + 1], gmlp_w_s[j], bias_full)
        else:
            gated = conv_mid_fwd(p, cw_full)
        h_mix = mm_nn(gated, W[mix[1], j].reshape(1, gated.shape[1], D), "mix_out", F32, res=h)
        sv2 = (h, hn, p, gated)
        h, sv3 = xattn_fwd(h_mix, mems, xattn_norm[i], mem_norm[i],
                           W["xattn_wq", i], W["xattn_wkv", i], W["xattn_wo", i])
        W.update(gather([("ffn2_w13", i), ("ffn2_w2", i)]))
        h, sv4 = ffn_fwd(h, ffn2_norm[i], W["ffn2_w13", i], W["ffn2_w2", i])
        saved.append((sv1, sv2, sv3, sv4))

    loss_part, d, dys, d_final_norm = loss_head(h, final_norm, target, 0.5)

    small = {k: [None] * depth for k in ("ffn1_norm", "mix_norm", "xattn_norm", "mem_norm", "ffn2_norm")}
    partial = {}
    for i in reversed(range(depth)):
        j = i // 2
        is_gmlp = i % 2 == 0
        sv1, sv2, sv3, sv4 = saved[i]
        d, dys, small["ffn2_norm"][i], dw13, dw2 = ffn_bwd(
            d, dys, sv4, ffn2_norm[i], W["ffn2_w13", i], W["ffn2_w2", i], 1.0)
        partial["ffn2_w13", i], partial["ffn2_w2", i] = scatter_partials_stage([dw13, dw2])
        d, dys, small["xattn_norm"][i], small["mem_norm"][i], dwq, dwkv, dwo = xattn_bwd(
            d, dys, sv3, mems, xattn_norm[i], W["xattn_wq", i], W["xattn_wkv", i], W["xattn_wo", i], 1.0)
        if is_gmlp:
            mix = ("gmlp_w_in", "gmlp_w_out")
            mid = lambda p, dg: (lambda r: (r[0], r[1:]))(gmlp_mid_bwd(
                p, dg, gmlp_ln_g[j:j + 1], gmlp_ln_b[j:j + 1], gmlp_w_s[j], bias_full))
        else:
            mix = ("conv_w_in", "conv_w_out")
            mid = lambda p, dg: (lambda r: (r[0], r[1:]))(conv_mid_bwd(p, dg, cw_full))
        d, dys, small["mix_norm"][i], dw_in, dw_out, extra = mixer_bwd_common(
            d, dys, sv2, mix_norm[i], W[mix[0], j], W[mix[1], j], mid, 0.5)
        if is_gmlp:
            d_ws, d_bs_wide, d_lng, d_lnb = extra
        else:
            (d_cw,) = extra
        got = scatter_partials_stage([dw_in, dw_out, dwq, dwkv, dwo])
        for k, g in zip(((mix[0], j), (mix[1], j), ("xattn_wq", i), ("xattn_wkv", i), ("xattn_wo", i)), got):
            partial[k] = g
        d, dys, small["ffn1_norm"][i], dw13, dw2 = ffn_bwd(
            d, dys, sv1, ffn1_norm[i], W["ffn1_w13", i], W["ffn1_w2", i], 0.5)
        partial["ffn1_w13", i], partial["ffn1_w2", i] = scatter_partials_stage([dw13, dw2])
    grad_x = d.reshape(x.shape)

    small_grads = {k: jnp.concatenate(v, axis=0) for k, v in small.items()}
    small_grads["gmlp_ln_g"] = d_lng
    small_grads["gmlp_ln_b"] = d_lnb
    small_grads["gmlp_w_s"] = d_ws[None]
    small_grads["gmlp_b_s"] = d_bs_wide[None, :, :, 0]
    small_grads["final_norm"] = d_final_norm.reshape(-1)
    small_names = ["ffn1_norm", "mix_norm", "gmlp_ln_g", "gmlp_ln_b", "gmlp_w_s", "gmlp_b_s",
                   "xattn_norm", "mem_norm", "ffn2_norm", "final_norm"]
    summed = small_all_reduce(_pack([small_grads[k] for k in small_names] + [d_cw, loss_part]))
    parts = _unpack(summed, [given[k].shape for k in small_names] + [(CONV_WIDTH, D), (1, LANES)])
    grads = dict(zip(small_names, parts[:len(small_names)]))
    grads["conv_w"] = lax.dynamic_slice(parts[-2], (jnp.int32(0), me * cshard), (CONV_WIDTH, cshard))[None]
    loss = parts[-1][0, 0]
    flat_names = small_names + ["conv_w"]
    flat = adamw_flat(*[_pack([src[k] for k in flat_names]) for src in
                        (grads, given, {k: given["m_" + k] for k in flat_names},
                         {k: given["v_" + k] for k in flat_names})])
    delta, new_m, new_v = [dict(zip(flat_names, _unpack(f, [given[k].shape for k in flat_names]))) for f in flat]

    for k in bf:
        w = given[k]
        L = w.shape[0]
        shard = w.shape[1:]
        view = lambda a: a.reshape((L,) + shard)
        g, dl, m2, v2 = adamw_sharded([partial[k, l] for l in range(L)], w, given["m_" + k], given["v_" + k])
        grads[k], delta[k], new_m[k], new_v[k] = view(g), view(dl), view(m2), view(v2)

    order = ["ffn1_norm", "ffn1_w13", "ffn1_w2", "mix_norm", "gmlp_w_in", "gmlp_ln_g", "gmlp_ln_b", "gmlp_w_s",
             "gmlp_b_s", "gmlp_w_out", "conv_w_in", "conv_w", "conv_w_out", "xattn_norm", "mem_norm", "xattn_wq",
             "xattn_wkv", "xattn_wo", "ffn2_norm", "ffn2_w13", "ffn2_w2", "final_norm"]
    return (loss, grad_x, *[grads[k] for k in order], *[delta[k] for k in order],
            *[new_m[k] for k in order], *[new_v[k] for k in order])
```

```python
import functools
import math

import jax
import jax.numpy as jnp
from jax import lax
from jax.experimental import pallas as pl
from jax.experimental.pallas import tpu as pltpu

F32 = jnp.float32
BF16 = jnp.bfloat16

N_DEV = 8
N_PEERS = N_DEV - 1
CHUNK = 128
GROUPS = 8
HEADS = 4
CONV_WIDTH = 3
RMS_EPS = 1e-6
LN_EPS = 1e-5
ADAM_LR = 0.001
ADAM_B1 = 0.9
ADAM_B2 = 0.999
ADAM_EPS = 1e-08
ADAM_WD = 0.01
ADAM_STEP = 10
LANES = 128
BF16_SUBLANES = 16
VMEM_LIMIT_BYTES = 56 * 1024 * 1024

_NT = (((1,), (1,)), ((), ()))
_TN = (((0,), (0,)), ((), ()))
_SQRT_HALF = 0.7071067811865476
_INV_SQRT_2PI = 0.3989422804014327


def _div(n, pref, align):
    best = None
    for t in range(align, min(n, pref) + 1, align):
        if n % t == 0:
            best = t
    return n if best is None else best


_ANY = pl.BlockSpec(memory_space=pl.ANY)


class Exchange:
    def __init__(self, arrays, out_shapes, sems, start, finish, forward=None):
        self.arrays, self.out_shapes, self.sems = list(arrays), list(out_shapes), list(sems)
        self.start, self.finish, self.forward = start, finish, forward
        self.results = None


def _call(body, name, grid, ins, outs, scratch=(), exchange=None):
    in_specs = [pl.BlockSpec(bs, im) for _, bs, im in ins]
    out_specs = [pl.BlockSpec(bs, im) for _, _, bs, im in outs]
    out_shape = [jax.ShapeDtypeStruct(s, d) for s, d, _, _ in outs]
    arrays = [a for a, _, _ in ins]
    scratch = list(scratch)
    kernel_fn = body
    if exchange is not None:
        n_in, n_out, n_scr = len(ins), len(outs), len(scratch)
        n_xin, n_xout = len(exchange.arrays), len(exchange.out_shapes)
        steps = math.prod(grid)
        forward_step = min(steps - 1, (3 * steps) // 4)

        def kernel_fn(*refs):
            refs = list(refs)
            own_in, x_in = refs[:n_in], refs[n_in:n_in + n_xin]
            refs = refs[n_in + n_xin:]
            own_out, x_out = refs[:n_out], refs[n_out:n_out + n_xout]
            refs = refs[n_out + n_xout:]
            own_scr, x_sems = refs[:n_scr], refs[n_scr:]
            step = 0
            for axis, size in enumerate(grid):
                step = step * size + pl.program_id(axis)

            @pl.when(step == 0)
            def _():
                exchange.start(x_in, x_out, x_sems)

            if exchange.forward is not None:
                @pl.when(step == forward_step)
                def _():
                    exchange.forward(x_in, x_out, x_sems)

            body(*own_in, *own_out, *own_scr)

            @pl.when(step == steps - 1)
            def _():
                exchange.finish(x_in, x_out, x_sems)

        in_specs += [_ANY] * n_xin
        out_specs += [_ANY] * n_xout
        out_shape += exchange.out_shapes
        arrays += exchange.arrays
        scratch += exchange.sems
    res = pl.pallas_call(
        kernel_fn,
        name=name,
        grid=grid,
        in_specs=in_specs,
        out_specs=out_specs,
        out_shape=out_shape,
        scratch_shapes=scratch,
        compiler_params=pltpu.CompilerParams(
            dimension_semantics=("arbitrary",) * len(grid), vmem_limit_bytes=VMEM_LIMIT_BYTES),
    )(*arrays)
    if exchange is not None:
        exchange.results = list(res[len(outs):])
        res = res[:len(outs)]
    return res


def _dot(a, b, dims=None):
    if dims is None:
        return jnp.dot(a, b, preferred_element_type=F32)
    return lax.dot_general(a, b, dims, preferred_element_type=F32)


def _sigmoid(v):
    return 1.0 / (1.0 + jnp.exp(-v))


def _gelu(v):
    return 0.5 * v * (1.0 + lax.erf(v * _SQRT_HALF))


def _gelu_grad(v):
    return 0.5 * (1.0 + lax.erf(v * _SQRT_HALF)) + v * (_INV_SQRT_2PI * jnp.exp(-0.5 * v * v))


def _accumulate(ref, part, first):
    @pl.when(first)
    def _():
        ref[...] = part

    @pl.when(jnp.logical_not(first))
    def _():
        ref[...] += part


def rms_fwd(x, g):
    T, D = x.shape
    tt = _div(T, 512, BF16_SUBLANES)

    def body(x_ref, g_ref, o_ref):
        xv = x_ref[...]
        r = lax.rsqrt(jnp.mean(xv * xv, axis=-1, keepdims=True) + RMS_EPS)
        o_ref[...] = ((xv * r) * g_ref[...]).astype(BF16)

    return _call(body, "rms_fwd", (T // tt,),
                 [(x, (tt, D), lambda i: (i, 0)), (g.reshape(1, D), (1, D), lambda i: (0, 0))],
                 [((T, D), BF16, (tt, D), lambda i: (i, 0))])[0]


def rms_bwd(x, g, dxn, d, scale):
    T, D = x.shape
    tt = _div(T, 256, BF16_SUBLANES)

    def body(x_ref, g_ref, dn_ref, d_ref, dx_ref, dxs_ref, dg_ref):
        xv = x_ref[...]
        r = lax.rsqrt(jnp.mean(xv * xv, axis=-1, keepdims=True) + RMS_EPS)
        xh = xv * r
        dn = dn_ref[...]
        dxh = dn * g_ref[...]
        dx = r * (dxh - xh * jnp.mean(dxh * xh, axis=-1, keepdims=True)) + d_ref[...]
        dx_ref[...] = dx
        dxs_ref[...] = (scale * dx).astype(BF16)
        _accumulate(dg_ref, jnp.sum(dn * xh, axis=0, keepdims=True), pl.program_id(0) == 0)

    row = lambda i: (i, 0)
    fix = lambda i: (0, 0)
    return _call(body, "rms_bwd", (T // tt,),
                 [(x, (tt, D), row), (g.reshape(1, D), (1, D), fix), (dxn, (tt, D), row), (d, (tt, D), row)],
                 [((T, D), F32, (tt, D), row), ((T, D), BF16, (tt, D), row), ((1, D), F32, (1, D), fix)])


def rms_gain_grad(x, dxn):
    T, D = x.shape
    tt = _div(T, 256, 8)

    def body(x_ref, dn_ref, dg_ref):
        xv = x_ref[...]
        r = lax.rsqrt(jnp.mean(xv * xv, axis=-1, keepdims=True) + RMS_EPS)
        _accumulate(dg_ref, jnp.sum(dn_ref[...] * (xv * r), axis=0, keepdims=True), pl.program_id(0) == 0)

    row = lambda i: (i, 0)
    return _call(body, "rms_gain_grad", (T // tt,), [(x, (tt, D), row), (dxn, (tt, D), row)],
                 [((1, D), F32, (1, D), lambda i: (0, 0))])[0]


def loss_head(x, g, target, scale):
    T, D = x.shape
    tt = _div(T, 256, BF16_SUBLANES)

    def body(x_ref, g_ref, t_ref, loss_ref, dx_ref, dxs_ref, dg_ref):
        first = pl.program_id(0) == 0
        xv = x_ref[...]
        gv = g_ref[...]
        r = lax.rsqrt(jnp.mean(xv * xv, axis=-1, keepdims=True) + RMS_EPS)
        xh = xv * r
        err = xh * gv - t_ref[...]
        part = 0.5 * jnp.sum(jnp.mean(err * err, axis=-1, keepdims=True), axis=0, keepdims=True)
        _accumulate(loss_ref, jnp.broadcast_to(part, (1, LANES)), first)
        dy = err * (1.0 / D)
        dxh = dy * gv
        dx = r * (dxh - xh * jnp.mean(dxh * xh, axis=-1, keepdims=True))
        dx_ref[...] = dx
        dxs_ref[...] = (scale * dx).astype(BF16)
        _accumulate(dg_ref, jnp.sum(dy * xh, axis=0, keepdims=True), first)

    row = lambda i: (i, 0)
    fix = lambda i: (0, 0)
    return _call(body, "loss_head", (T // tt,),
                 [(x, (tt, D), row), (g.reshape(1, D), (1, D), fix), (target, (tt, D), row)],
                 [((1, LANES), F32, (1, LANES), fix), ((T, D), F32, (tt, D), row),
                  ((T, D), BF16, (tt, D), row), ((1, D), F32, (1, D), fix)])


def mm_nn(a, w3, name, out_dtype=BF16, res=None, res_scale=1.0, tm_pref=1024, tn_pref=1024, exchange=None):
    M, K = a.shape
    nb, _, ns = w3.shape
    tn = _div(ns, tn_pref, LANES)
    per = ns // tn
    tm = _div(M, tm_pref, BF16_SUBLANES)
    ins = [(a, (tm, K), lambda j, m: (m, 0)), (w3, (None, K, tn), lambda j, m: (j // per, 0, j % per))]
    if res is not None:
        ins.append((res, (tm, tn), lambda j, m: (m, j)))

    def body(*refs):
        a_ref, w_ref = refs[0], refs[1]
        o_ref = refs[-1]
        acc = _dot(a_ref[...], w_ref[...])
        if res is not None:
            acc = refs[2][...] + res_scale * acc
        o_ref[...] = acc.astype(o_ref.dtype)

    return _call(body, name, (nb * per, M // tm), ins,
                 [((M, nb * ns), out_dtype, (tm, tn), lambda j, m: (m, j))], exchange=exchange)[0]


def mm_nt(a_in, w3, name, out_dtype, M, tm_pref=1024, to_pref=2048, exchange=None):
    nb, Ko, ns = w3.shape
    to = _div(Ko, to_pref, LANES)
    tm = _div(M, tm_pref, BF16_SUBLANES)
    if isinstance(a_in, tuple):
        a, a_bs, a_im = a_in
        a_bs = tuple(tm if s == "tm" else s for s in a_bs)
    else:
        a, a_bs, a_im = a_in, (tm, ns), lambda m, o, b: (m, b)
    if nb > 1:
        assert out_dtype == F32

    def body(a_ref, w_ref, o_ref):
        p = _dot(a_ref[...], w_ref[...], _NT)
        if nb == 1:
            o_ref[...] = p.astype(o_ref.dtype)
        else:
            _accumulate(o_ref, p, pl.program_id(2) == 0)

    return _call(body, name, (M // tm, Ko // to, nb),
                 [(a, a_bs, a_im), (w3, (None, to, ns), lambda m, o, b: (b, o, 0))],
                 [((M, Ko), out_dtype, (tm, to), lambda m, o, b: (m, o))], exchange=exchange)[0]


def mm_tn(a, b_in, name, nbo, ns, tka_pref=1024, tt_pref=2048, tn_pref=2048, exchange=None):
    T, Ka = a.shape
    tt = _div(T, tt_pref, BF16_SUBLANES)
    tka = _div(Ka, tka_pref, LANES)
    if isinstance(b_in, tuple):
        b, b_bs, b_im = b_in
        b_bs = tuple(tt if s == "tt" else s for s in b_bs)
        tn, per = ns, 1
    else:
        tn = _div(ns, tn_pref, LANES)
        per = ns // tn
        b, b_bs, b_im = b_in, (tt, tn), lambda i, j, t: (t, j)
    nt = T // tt

    def body(a_ref, b_ref, o_ref, acc_ref):
        t = pl.program_id(2)
        _accumulate(acc_ref, _dot(a_ref[...], b_ref[...], _TN), t == 0)

        @pl.when(t == nt - 1)
        def _():
            o_ref[...] = acc_ref[...].astype(BF16)

    return _call(body, name, (Ka // tka, nbo * per, nt),
                 [(a, (tt, tka), lambda i, j, t: (t, i)), (b, b_bs, b_im)],
                 [((nbo, Ka, ns), BF16, (None, tka, tn), lambda i, j, t: (j // per, i, j % per))],
                 scratch=[pltpu.VMEM((tka, tn), F32)], exchange=exchange)[0]


def ffn_up(xn, w13, exchange=None):
    T, D = xn.shape
    nb, _, ns = w13.shape
    half = nb // 2
    F = half * ns
    tm = _div(T, 512, BF16_SUBLANES)

    def body(x_ref, wg_ref, wu_ref, gu_ref, act_ref):
        xv = x_ref[...]
        gate = _dot(xv, wg_ref[...])
        up = _dot(xv, wu_ref[...])
        gu_ref[0] = gate.astype(BF16)
        gu_ref[1] = up.astype(BF16)
        act_ref[...] = (gate * _sigmoid(gate) * up).astype(BF16)

    return _call(body, "ffn_up", (half, T // tm),
                 [(xn, (tm, D), lambda j, m: (m, 0)),
                  (w13, (None, D, ns), lambda j, m: (j, 0, 0)),
                  (w13, (None, D, ns), lambda j, m: (j + half, 0, 0))],
                 [((2, T, F), BF16, (2, tm, ns), lambda j, m: (0, m, j)),
                  ((T, F), BF16, (tm, ns), lambda j, m: (m, j))], exchange=exchange)


def ffn_dact(dy, w2, gu, exchange=None):
    T, D = dy.shape
    F = w2.shape[0]
    tm = _div(T, 1024, BF16_SUBLANES)
    tn = _div(F, 512, LANES)

    def body(dy_ref, w_ref, gu_ref, dh_ref):
        da = _dot(dy_ref[...], w_ref[...], _NT)
        gate = gu_ref[0].astype(F32)
        up = gu_ref[1].astype(F32)
        s = _sigmoid(gate)
        dh_ref[0] = (da * up * (s * (1.0 + gate * (1.0 - s)))).astype(BF16)
        dh_ref[1] = (da * (gate * s)).astype(BF16)

    return _call(body, "ffn_dact", (F // tn, T // tm),
                 [(dy, (tm, D), lambda j, m: (m, 0)), (w2, (tn, D), lambda j, m: (j, 0)),
                  (gu, (2, tm, tn), lambda j, m: (0, m, j))],
                 [((2, T, F), BF16, (2, tm, tn), lambda j, m: (0, m, j))], exchange=exchange)[0]


def ffn_fwd(x, norm_g, w13, get_w2, up_exchange=None, down_exchange=None):
    xn = rms_fwd(x, norm_g)
    gu, act = ffn_up(xn, w13, exchange=up_exchange)
    F = act.shape[1]
    y = mm_nn(act, get_w2().reshape(1, F, -1), "ffn_down", F32, res=x, res_scale=0.5, tm_pref=512,
              exchange=down_exchange)
    return y, (x, xn, gu, act)


def ffn_bwd(d, dys, saved, norm_g, w13, w2, scale_out, dact_exchange=None):
    x, xn, gu, act = saved
    T, D = x.shape
    nb, _, ns = w13.shape
    half = nb // 2
    F = half * ns
    dh = ffn_dact(dys, w2.reshape(F, D), gu, exchange=dact_exchange)
    dw2 = mm_tn(act, dys, "ffn_dw2", 1, D, tka_pref=ns, tn_pref=1024)
    send_w2 = scatter_exchange([dw2.reshape(N_DEV, F // N_DEV, D)])
    dw13 = mm_tn(xn, (dh, (None, "tt", ns), lambda i, j, t: (j // half, t, j % half)), "ffn_dw13", nb, ns,
                 exchange=send_w2)
    send_w13 = scatter_exchange([dw13])
    dxn = mm_nt((dh, (None, "tm", ns), lambda m, o, b: (b // half, m, b % half)), w13, "ffn_dxn", F32, T,
                exchange=send_w13)
    dx, dxs, dg = rms_bwd(x, norm_g, dxn, d, scale_out)
    return dx, dxs, dg, send_w13.results[0], send_w2.results[0]


def _gmlp_parts(p_ref, lng_ref, lnb_ref):
    E = lng_ref.shape[-1]
    z = _gelu(p_ref[...].astype(F32))
    u = z[:, :E]
    vp = z[:, E:]
    mu = jnp.mean(vp, axis=-1, keepdims=True)
    xc = vp - mu
    rstd = lax.rsqrt(jnp.mean(xc * xc, axis=-1, keepdims=True) + LN_EPS)
    vh = xc * rstd
    v = vh * lng_ref[...] + lnb_ref[...]
    return u, vh, rstd, v


def _causal_ws(ws_ref, g):
    keep = lax.broadcasted_iota(jnp.int32, (CHUNK, CHUNK), 0) >= lax.broadcasted_iota(jnp.int32, (CHUNK, CHUNK), 1)
    return jnp.where(keep, ws_ref[g], 0.0).astype(BF16), keep


def gmlp_mid_fwd(p, ln_g, ln_b, w_s, bias_full):
    T, E2 = p.shape
    E = E2 // 2
    gd = E // GROUPS
    tm = _div(T, 256, CHUNK)
    fix2 = lambda i: (0, 0)

    def body(p_ref, lng_ref, lnb_ref, ws_ref, bias_ref, o_ref):
        u, _, _, v = _gmlp_parts(p_ref, lng_ref, lnb_ref)
        vb = v.astype(BF16)
        for g in range(GROUPS):
            wm, _ = _causal_ws(ws_ref, g)
            cols = slice(g * gd, (g + 1) * gd)
            for c in range(tm // CHUNK):
                rows = slice(c * CHUNK, (c + 1) * CHUNK)
                f = _dot(wm, vb[rows, cols]) + bias_ref[:, cols]
                o_ref[rows, cols] = (u[rows, cols] * f).astype(BF16)

    return _call(body, "gmlp_mid_fwd", (T // tm,),
                 [(p, (tm, E2), lambda i: (i, 0)), (ln_g, (1, E), fix2), (ln_b, (1, E), fix2),
                  (w_s, (GROUPS, CHUNK, CHUNK), lambda i: (0, 0, 0)), (bias_full, (CHUNK, E), fix2)],
                 [((T, E), BF16, (tm, E), lambda i: (i, 0))])[0]


def gmlp_mid_bwd(p, dgated, ln_g, ln_b, w_s, bias_full):
    T, E2 = p.shape
    E = E2 // 2
    gd = E // GROUPS
    tm = _div(T, 256, CHUNK)
    nsteps = T // tm
    fix2 = lambda i: (0, 0)
    fix3 = lambda i: (0, 0, 0)

    def body(p_ref, dg_ref, lng_ref, lnb_ref, ws_ref, bias_ref,
             dp_ref, dws_ref, dbs_ref, dlng_ref, dlnb_ref, f_sc, dv_sc, db_sc):
        i = pl.program_id(0)
        first = i == 0
        u, vh, rstd, v = _gmlp_parts(p_ref, lng_ref, lnb_ref)
        vb = v.astype(BF16)
        dgt = dg_ref[...].astype(F32)
        df = dgt * u
        dfb = df.astype(BF16)
        for g in range(GROUPS):
            wm, keep = _causal_ws(ws_ref, g)
            cols = slice(g * gd, (g + 1) * gd)
            dw = None
            dbg = None
            for c in range(tm // CHUNK):
                rows = slice(c * CHUNK, (c + 1) * CHUNK)
                f_sc[rows, cols] = _dot(wm, vb[rows, cols]) + bias_ref[:, cols]
                dv_sc[rows, cols] = _dot(wm, dfb[rows, cols], _TN)
                part = _dot(dfb[rows, cols], vb[rows, cols], _NT)
                dw = part if dw is None else dw + part
                dbg = df[rows, cols] if dbg is None else dbg + df[rows, cols]
            dw = jnp.where(keep, dw, 0.0)

            @pl.when(first)
            def _():
                dws_ref[g] = dw
                db_sc[:, cols] = dbg

            @pl.when(jnp.logical_not(first))
            def _():
                dws_ref[g] += dw
                db_sc[:, cols] += dbg

        du = dgt * f_sc[...]
        dv = dv_sc[...]
        _accumulate(dlng_ref, jnp.sum(dv * vh, axis=0, keepdims=True), first)
        _accumulate(dlnb_ref, jnp.sum(dv, axis=0, keepdims=True), first)
        dvh = dv * lng_ref[...]
        dvp = rstd * (dvh - jnp.mean(dvh, axis=-1, keepdims=True)
                      - vh * jnp.mean(dvh * vh, axis=-1, keepdims=True))
        gp = _gelu_grad(p_ref[...].astype(F32))
        dp_ref[:, :E] = (du * gp[:, :E]).astype(BF16)
        dp_ref[:, E:] = (dvp * gp[:, E:]).astype(BF16)

        @pl.when(i == nsteps - 1)
        def _():
            for g in range(GROUPS):
                tot = jnp.sum(db_sc[:, g * gd:(g + 1) * gd], axis=-1, keepdims=True)
                dbs_ref[g] = jnp.broadcast_to(tot, (CHUNK, LANES))

    return _call(body, "gmlp_mid_bwd", (nsteps,),
                 [(p, (tm, E2), lambda i: (i, 0)), (dgated, (tm, E), lambda i: (i, 0)),
                  (ln_g, (1, E), fix2), (ln_b, (1, E), fix2),
                  (w_s, (GROUPS, CHUNK, CHUNK), fix3), (bias_full, (CHUNK, E), fix2)],
                 [((T, E2), BF16, (tm, E2), lambda i: (i, 0)),
                  ((GROUPS, CHUNK, CHUNK), F32, (GROUPS, CHUNK, CHUNK), fix3),
                  ((GROUPS, CHUNK, LANES), F32, (GROUPS, CHUNK, LANES), fix3),
                  ((1, E), F32, (1, E), fix2), ((1, E), F32, (1, E), fix2)],
                 scratch=[pltpu.VMEM((tm, E), F32), pltpu.VMEM((tm, E), F32), pltpu.VMEM((CHUNK, E), F32)])


HALO = 16


def _row_of(block, r):
    rows = lax.broadcasted_iota(jnp.int32, block.shape, 0)
    return jnp.sum(jnp.where(rows == r, block, 0.0), axis=0, keepdims=True)


def _shift_down(z, k, fill):
    out = pltpu.roll(z, k, 0)
    rows = lax.broadcasted_iota(jnp.int32, z.shape, 0)
    for t in range(k):
        out = jnp.where(rows == t, fill[t], out)
    return out


def _shift_up(z, k, fill):
    n = z.shape[0]
    out = pltpu.roll(z, n - k, 0)
    rows = lax.broadcasted_iota(jnp.int32, z.shape, 0)
    for j in range(k):
        out = jnp.where(rows == n - k + j, fill[j], out)
    return out


def _conv_parts(p_ref, prev_ref, cw_ref, is_first):
    D = cw_ref.shape[-1]
    pv = p_ref[...].astype(F32)
    bg, cg, val = pv[:, :D], pv[:, D:2 * D], pv[:, 2 * D:]
    z = cg * val
    pp = prev_ref[...].astype(F32)
    zp = jnp.where(is_first, 0.0, pp[:, D:2 * D] * pp[:, 2 * D:])
    zl1 = _row_of(zp, HALO - 1)
    zl2 = _row_of(zp, HALO - 2)
    z1 = _shift_down(z, 1, [zl1])
    z2 = _shift_down(z, 2, [zl2, zl1])
    conv = z2 * cw_ref[0:1, :] + z1 * cw_ref[1:2, :] + z * cw_ref[2:3, :]
    return bg, cg, val, z, z1, z2, conv


def conv_mid_fwd(p, cw):
    T, D3 = p.shape
    D = D3 // 3
    tm = _div(T, 256, HALO)
    per = tm // HALO

    def body(p_ref, prev_ref, cw_ref, o_ref):
        bg, _, _, _, _, _, conv = _conv_parts(p_ref, prev_ref, cw_ref, pl.program_id(0) == 0)
        o_ref[...] = (bg * conv).astype(BF16)

    return _call(body, "conv_mid_fwd", (T // tm,),
                 [(p, (tm, D3), lambda i: (i, 0)),
                  (p, (HALO, D3), lambda i: (jnp.maximum(i * per - 1, 0), 0)),
                  (cw, (CONV_WIDTH, D), lambda i: (0, 0))],
                 [((T, D), BF16, (tm, D), lambda i: (i, 0))])[0]


def conv_mid_bwd(p, dgated, cw):
    T, D3 = p.shape
    D = D3 // 3
    tm = _div(T, 256, HALO)
    per = tm // HALO
    nsteps = T // tm
    last_halo = T // HALO - 1
    nxt = lambda i: (jnp.minimum((i + 1) * per, last_halo), 0)

    def body(p_ref, prev_ref, next_ref, dg_ref, dgn_ref, cw_ref, dp_ref, dcw_ref):
        i = pl.program_id(0)
        bg, cg, val, z, z1, z2, conv = _conv_parts(p_ref, prev_ref, cw_ref, i == 0)
        dgt = dg_ref[...].astype(F32)
        dconv = dgt * bg
        dcn = jnp.where(i == nsteps - 1, 0.0, dgn_ref[...].astype(F32) * next_ref[:, :D].astype(F32))
        n0 = _row_of(dcn, 0)
        n1 = _row_of(dcn, 1)
        up1 = _shift_up(dconv, 1, [n0])
        up2 = _shift_up(dconv, 2, [n0, n1])
        dz = dconv * cw_ref[2:3, :] + up1 * cw_ref[1:2, :] + up2 * cw_ref[0:1, :]
        dp_ref[:, :D] = (dgt * conv).astype(BF16)
        dp_ref[:, D:2 * D] = (dz * val).astype(BF16)
        dp_ref[:, 2 * D:] = (dz * cg).astype(BF16)
        first = i == 0
        parts = (jnp.sum(dconv * z2, axis=0, keepdims=True), jnp.sum(dconv * z1, axis=0, keepdims=True),
                 jnp.sum(dconv * z, axis=0, keepdims=True))

        @pl.when(first)
        def _():
            for k in range(CONV_WIDTH):
                dcw_ref[k:k + 1, :] = parts[k]

        @pl.when(jnp.logical_not(first))
        def _():
            for k in range(CONV_WIDTH):
                dcw_ref[k:k + 1, :] += parts[k]

    return _call(body, "conv_mid_bwd", (nsteps,),
                 [(p, (tm, D3), lambda i: (i, 0)),
                  (p, (HALO, D3), lambda i: (jnp.maximum(i * per - 1, 0), 0)),
                  (p, (HALO, D3), nxt),
                  (dgated, (tm, D), lambda i: (i, 0)),
                  (dgated, (HALO, D), nxt),
                  (cw, (CONV_WIDTH, D), lambda i: (0, 0))],
                 [((T, D3), BF16, (tm, D3), lambda i: (i, 0)),
                  ((CONV_WIDTH, D), F32, (CONV_WIDTH, D), lambda i: (0, 0))])


def mixer_bwd_common(d, dys, saved, norm_g, w_in, w_out, mid_bwd, scale_out, dwin_exchange=None):
    x, hn, p, gated = saved
    T, D = x.shape
    E = gated.shape[1]
    w_out3 = w_out.reshape(1, E, D)
    dgated = mm_nt(dys, w_out3, "mix_dgated", BF16, T)
    dw_out = mm_tn(gated, dys, "mix_dwout", 1, D)
    dp, extra = mid_bwd(p, dgated)
    nb, _, ns = w_in.shape
    dw_in = mm_tn(hn, dp, "mix_dwin", nb, ns, exchange=dwin_exchange)
    dhn = mm_nt(dp, w_in, "mix_dhn", F32, T)
    dx, dxs, dg = rms_bwd(x, norm_g, dhn, d, scale_out)
    return dx, dxs, dg, dw_in, dw_out.reshape(N_DEV, E // N_DEV, D), extra


def _softmax_rows(s):
    e = jnp.exp(s - jnp.max(s, axis=-1, keepdims=True))
    return e / jnp.sum(e, axis=-1, keepdims=True)


def attn_fwd(q, kv):
    T, D = q.shape
    M = kv.shape[0]
    hd = D // HEADS
    scale = hd ** -0.5
    tm = _div(T, 512, BF16_SUBLANES)

    def body(q_ref, kv_ref, o_ref):
        for h in range(HEADS):
            cols = slice(h * hd, (h + 1) * hd)
            s = _dot(q_ref[:, cols], kv_ref[:, cols], _NT) * scale
            pr = _softmax_rows(s).astype(BF16)
            o_ref[:, cols] = _dot(pr, kv_ref[:, D + h * hd:D + (h + 1) * hd]).astype(BF16)

    return _call(body, "attn_fwd", (T // tm,),
                 [(q, (tm, D), lambda i: (i, 0)), (kv, (M, 2 * D), lambda i: (0, 0))],
                 [((T, D), BF16, (tm, D), lambda i: (i, 0))])[0]


def attn_bwd(q, do, kv):
    T, D = q.shape
    M = kv.shape[0]
    hd = D // HEADS
    scale = hd ** -0.5
    tm = _div(T, 512, BF16_SUBLANES)
    nsteps = T // tm

    def body(q_ref, do_ref, kv_ref, dq_ref, dkv_ref, acc_ref):
        i = pl.program_id(0)
        for h in range(HEADS):
            cols = slice(h * hd, (h + 1) * hd)
            vcols = slice(D + h * hd, D + (h + 1) * hd)
            qh = q_ref[:, cols]
            kh = kv_ref[:, cols]
            doh = do_ref[:, cols]
            pr = _softmax_rows(_dot(qh, kh, _NT) * scale)
            dpr = _dot(doh, kv_ref[:, vcols], _NT)
            ds = (pr * (dpr - jnp.sum(dpr * pr, axis=-1, keepdims=True)) * scale).astype(BF16)
            dq_ref[:, cols] = _dot(ds, kh).astype(BF16)
            dk = _dot(ds, qh, _TN)
            dv = _dot(pr.astype(BF16), doh, _TN)

            @pl.when(i == 0)
            def _():
                acc_ref[:, cols] = dk
                acc_ref[:, vcols] = dv

            @pl.when(i > 0)
            def _():
                acc_ref[:, cols] += dk
                acc_ref[:, vcols] += dv

        @pl.when(i == nsteps - 1)
        def _():
            dkv_ref[...] = acc_ref[...].astype(BF16)

    return _call(body, "attn_bwd", (nsteps,),
                 [(q, (tm, D), lambda i: (i, 0)), (do, (tm, D), lambda i: (i, 0)),
                  (kv, (M, 2 * D), lambda i: (0, 0))],
                 [((T, D), BF16, (tm, D), lambda i: (i, 0)), ((M, 2 * D), BF16, (M, 2 * D), lambda i: (0, 0))],
                 scratch=[pltpu.VMEM((M, 2 * D), F32)])


def xattn_fwd(x, mem, xnorm_g, mnorm_g, wq, wkv, wo):
    D = x.shape[1]
    hq = rms_fwd(x, xnorm_g)
    mn = rms_fwd(mem, mnorm_g)
    q = mm_nn(hq, wq.reshape(1, D, D), "xattn_q")
    kv = mm_nn(mn, wkv, "xattn_kv")
    o = attn_fwd(q, kv)
    y = mm_nn(o, wo.reshape(1, D, D), "xattn_out", F32, res=x)
    return y, (x, hq, mn, q, kv, o)


def xattn_bwd(d, dys, saved, mem, xnorm_g, wq, wkv, wo, scale_out):
    x, hq, mn, q, kv, o = saved
    T, D = x.shape
    M = mem.shape[0]
    do = mm_nt(dys, wo.reshape(1, D, D), "xattn_do", BF16, T)
    dwo = mm_tn(o, dys, "xattn_dwo", 1, D)
    dq, dkv = attn_bwd(q, do, kv)
    dwq = mm_tn(hq, dq, "xattn_dwq", 1, D)
    dhq = mm_nt(dq, wq.reshape(1, D, D), "xattn_dhq", F32, T)
    dx, dxs, dgx = rms_bwd(x, xnorm_g, dhq, d, scale_out)
    nb, _, ns = wkv.shape
    dwkv = mm_tn(mn, dkv, "xattn_dwkv", nb, ns)
    dmn = mm_nt(dkv, wkv, "xattn_dmn", F32, M)
    dgm = rms_gain_grad(mem, dmn)
    rows = D // N_DEV
    return dx, dxs, dgx, dgm, dwq.reshape(N_DEV, rows, D), dwkv, dwo.reshape(N_DEV, rows, D)


def _mesh_places():
    x, y, c = lax.axis_index("x"), lax.axis_index("y"), lax.axis_index("c")
    chips = [(1 - x, y), (x, 1 - y), (1 - x, 1 - y)]
    return (x, y, c), (x, y, 1 - c), chips


def _slot(place):
    return 4 * place[0] + 2 * place[1] + place[2]


def _exchange_sems(n):
    return [pltpu.SemaphoreType.DMA((n * N_PEERS,)), pltpu.SemaphoreType.DMA((n * N_PEERS,)),
            pltpu.SemaphoreType.DMA((n,))]


def gather_exchange(shards):
    n = len(shards)
    shapes = [a.shape if l is None else a.shape[1:] for a, l in shards]

    def parts(x_in, x_out, sems):
        ins = [r if l is None else r.at[l] for r, (_, l) in zip(x_in, shards)]
        send_sems, recv_sems, local_sems = sems
        me, sibling, chips = _mesh_places()

        def copy(a, k, block, to, src=None):
            dst = x_out[a].at[_slot(block)]
            return pltpu.make_async_remote_copy(
                src_ref=dst if src is None else src, dst_ref=dst,
                send_sem=send_sems.at[a * N_PEERS + k], recv_sem=recv_sems.at[a * N_PEERS + k],
                device_id=to, device_id_type=pl.DeviceIdType.MESH)

        mine = [pltpu.make_async_copy(ins[a], x_out[a].at[_slot(me)], local_sems.at[a]) for a in range(n)]
        first = []
        for a in range(n):
            first.append(copy(a, 0, me, sibling, src=ins[a]))
            first += [copy(a, 1 + j, me, (*chip, me[2]), src=ins[a]) for j, chip in enumerate(chips)]
        return me, sibling, chips, copy, mine, first

    def start(x_in, x_out, sems):
        _, _, _, _, mine, first = parts(x_in, x_out, sems)
        for cp in mine + first:
            cp.start()

    def forward(x_in, x_out, sems):
        me, sibling, chips, copy, _, _ = parts(x_in, x_out, sems)
        for j, chip in enumerate(chips):
            for a in range(n):
                copy(a, 1 + j, (*chip, me[2]), me).wait_recv()
                copy(a, 4 + j, (*chip, me[2]), sibling).start()

    def finish(x_in, x_out, sems):
        me, sibling, chips, copy, mine, first = parts(x_in, x_out, sems)
        for a in range(n):
            copy(a, 0, sibling, me).wait_recv()
        for j, chip in enumerate(chips):
            for a in range(n):
                copy(a, 4 + j, (*chip, 1 - me[2]), me).wait_recv()
        for cp in first:
            cp.wait_send()
        for j, chip in enumerate(chips):
            for a in range(n):
                copy(a, 4 + j, (*chip, me[2]), sibling).wait_send()
        for cp in mine:
            cp.wait()

    return Exchange([a for a, _ in shards],
                    [jax.ShapeDtypeStruct((N_DEV,) + tuple(s), BF16) for s in shapes],
                    _exchange_sems(n), start, finish, forward)


def _all_peers(me, chips):
    c = me[2]
    return [(me[0], me[1], 1 - c)] + [(*chip, c) for chip in chips] + [(*chip, 1 - c) for chip in chips]


def scatter_exchange(grads):
    n = len(grads)

    def parts(x_in, x_out, sems):
        send_sems, recv_sems, local_sems = sems
        me, _, chips = _mesh_places()
        mine = [pltpu.make_async_copy(x_in[a].at[_slot(me)], x_out[a].at[_slot(me)], local_sems.at[a])
                for a in range(n)]
        sends, recvs = [], []
        for a in range(n):
            for k, peer in enumerate(_all_peers(me, chips)):
                sem = dict(send_sem=send_sems.at[a * N_PEERS + k], recv_sem=recv_sems.at[a * N_PEERS + k],
                           device_id=peer, device_id_type=pl.DeviceIdType.MESH)
                sends.append(pltpu.make_async_remote_copy(
                    src_ref=x_in[a].at[_slot(peer)], dst_ref=x_out[a].at[_slot(me)], **sem))
                recvs.append(pltpu.make_async_remote_copy(
                    src_ref=x_in[a].at[_slot(peer)], dst_ref=x_out[a].at[_slot(peer)], **sem))
        return mine, sends, recvs

    def start(x_in, x_out, sems):
        mine, sends, _ = parts(x_in, x_out, sems)
        for cp in mine + sends:
            cp.start()

    def finish(x_in, x_out, sems):
        mine, sends, recvs = parts(x_in, x_out, sems)
        for cp in recvs:
            cp.wait_recv()
        for cp in sends:
            cp.wait_send()
        for cp in mine:
            cp.wait()

    return Exchange(grads, [jax.ShapeDtypeStruct(g.shape, g.dtype) for g in grads],
                    _exchange_sems(n), start, finish)


def run_exchange(exchange, name):
    n_in, n_out = len(exchange.arrays), len(exchange.out_shapes)

    def body(*refs):
        x_in, x_out, sems = refs[:n_in], refs[n_in:n_in + n_out], refs[n_in + n_out:]
        exchange.start(x_in, x_out, sems)
        if exchange.forward is not None:
            exchange.forward(x_in, x_out, sems)
        exchange.finish(x_in, x_out, sems)

    exchange.results = list(pl.pallas_call(
        body, name=name, out_shape=exchange.out_shapes, in_specs=[_ANY] * n_in, out_specs=[_ANY] * n_out,
        scratch_shapes=exchange.sems)(*exchange.arrays))
    return exchange.results


def small_all_reduce(vec):
    R = vec.shape[0]

    def body(v_ref, o_ref, all_ref, send_sems, recv_sems):
        me, _, chips = _mesh_places()
        peers = _all_peers(me, chips)
        all_ref[_slot(me)] = v_ref[...]
        sends, recvs = [], []
        for k, peer in enumerate(peers):
            sem = dict(send_sem=send_sems.at[k], recv_sem=recv_sems.at[k],
                       device_id=peer, device_id_type=pl.DeviceIdType.MESH)
            sends.append(pltpu.make_async_remote_copy(src_ref=v_ref, dst_ref=all_ref.at[_slot(me)], **sem))
            recvs.append(pltpu.make_async_remote_copy(src_ref=v_ref, dst_ref=all_ref.at[_slot(peer)], **sem))
        for cp in sends:
            cp.start()
        for cp in recvs:
            cp.wait_recv()
        for cp in sends:
            cp.wait_send()
        acc = all_ref[0]
        for s in range(1, N_DEV):
            acc = acc + all_ref[s]
        o_ref[...] = acc

    return pl.pallas_call(
        body, name="small_all_reduce",
        out_shape=jax.ShapeDtypeStruct(vec.shape, F32),
        in_specs=[pl.BlockSpec(memory_space=pltpu.VMEM)], out_specs=pl.BlockSpec(memory_space=pltpu.VMEM),
        scratch_shapes=[pltpu.VMEM((N_DEV, R, LANES), F32), pltpu.SemaphoreType.DMA((N_PEERS,)),
                        pltpu.SemaphoreType.DMA((N_PEERS,))],
    )(vec)


def _adamw_math(w, g, m, v):
    m2 = ADAM_B1 * m + (1.0 - ADAM_B1) * g
    v2 = ADAM_B2 * v + (1.0 - ADAM_B2) * (g * g)
    m_hat = m2 / (1.0 - ADAM_B1 ** ADAM_STEP)
    v_hat = v2 / (1.0 - ADAM_B2 ** ADAM_STEP)
    delta = -ADAM_LR * (m_hat / (jnp.sqrt(v_hat) + ADAM_EPS) + ADAM_WD * w)
    return delta, m2, v2


def adamw_sharded(partials, w, m, v):
    L, r, c = w.shape
    tr = _div(r, max(BF16_SUBLANES, (1 << 18) // c), BF16_SUBLANES)
    nt = r // tr

    def part_map(l0):
        return lambda l, t: (0, jnp.where(l == l0, t, jnp.where(l < l0, 0, nt - 1)), 0)

    def body(*refs):
        parts = refs[:L]
        w_ref, m_ref, v_ref, g_out, d_out, m_out, v_out = refs[L:]
        layer = pl.program_id(0)
        for l0 in range(L):
            @pl.when(layer == l0)
            def _():
                g = parts[l0][0].astype(F32)
                for s in range(1, N_DEV):
                    g = g + parts[l0][s].astype(F32)
                delta, m2, v2 = _adamw_math(w_ref[...], g, m_ref[...], v_ref[...])
                g_out[...] = g
                d_out[...] = delta
                m_out[...] = m2
                v_out[...] = v2

    own = lambda l, t: (l, t, 0)
    return _call(body, "adamw_sharded", (L, nt),
                 [(p, (N_DEV, tr, c), part_map(l0)) for l0, p in enumerate(partials)]
                 + [(w, (None, tr, c), own), (m, (None, tr, c), own), (v, (None, tr, c), own)],
                 [((L, r, c), F32, (None, tr, c), own)] * 4)


def adamw_flat(g, w, m, v):
    shape = g.shape

    def body(g_ref, w_ref, m_ref, v_ref, d_out, m_out, v_out):
        delta, m2, v2 = _adamw_math(w_ref[...], g_ref[...], m_ref[...], v_ref[...])
        d_out[...] = delta
        m_out[...] = m2
        v_out[...] = v2

    whole = lambda: (0, 0)
    return _call(body, "adamw_flat", (), [(a, shape, whole) for a in (g, w, m, v)],
                 [(shape, F32, shape, whole)] * 3)


def _pack(parts):
    flat = jnp.concatenate([p.reshape(-1).astype(F32) for p in parts])
    rows = -(-flat.shape[0] // (8 * LANES)) * 8
    return jnp.pad(flat, (0, rows * LANES - flat.shape[0])).reshape(rows, LANES)


def _unpack(packed, shapes):
    flat = packed.reshape(-1)
    out, off = [], 0
    for s in shapes:
        size = math.prod(s)
        out.append(flat[off:off + size].reshape(s))
        off += size
    return out


def kernel(x, mem, ffn1_norm, ffn1_w13, ffn1_w2, mix_norm, gmlp_w_in, gmlp_ln_g, gmlp_ln_b, gmlp_w_s, gmlp_b_s, gmlp_w_out, conv_w_in, conv_w, conv_w_out, xattn_norm, mem_norm, xattn_wq, xattn_wkv, xattn_wo, ffn2_norm, ffn2_w13, ffn2_w2, final_norm, loss_target, m_ffn1_norm, m_ffn1_w13, m_ffn1_w2, m_mix_norm, m_gmlp_w_in, m_gmlp_ln_g, m_gmlp_ln_b, m_gmlp_w_s, m_gmlp_b_s, m_gmlp_w_out, m_conv_w_in, m_conv_w, m_conv_w_out, m_xattn_norm, m_mem_norm, m_xattn_wq, m_xattn_wkv, m_xattn_wo, m_ffn2_norm, m_ffn2_w13, m_ffn2_w2, m_final_norm, v_ffn1_norm, v_ffn1_w13, v_ffn1_w2, v_mix_norm, v_gmlp_w_in, v_gmlp_ln_g, v_gmlp_ln_b, v_gmlp_w_s, v_gmlp_b_s, v_gmlp_w_out, v_conv_w_in, v_conv_w, v_conv_w_out, v_xattn_norm, v_mem_norm, v_xattn_wq, v_xattn_wkv, v_xattn_wo, v_ffn2_norm, v_ffn2_w13, v_ffn2_w2, v_final_norm):
    given = dict(locals())
    T, D = x.shape[1], x.shape[2]
    depth = ffn1_norm.shape[0]
    xs = x.reshape(T, D)
    mems = mem.reshape(mem.shape[1], D)
    target = loss_target.reshape(T, D)
    me = 4 * lax.axis_index("x") + 2 * lax.axis_index("y") + lax.axis_index("c")
    E = gmlp_ln_g.shape[1]
    gd = E // GROUPS
    cshard = conv_w.shape[2]

    bf = {k: given[k].astype(BF16) for k in
          ("ffn1_w13", "ffn1_w2", "gmlp_w_in", "gmlp_w_out", "conv_w_in", "conv_w_out",
           "xattn_wq", "xattn_wkv", "xattn_wo", "ffn2_w13", "ffn2_w2")}

    W = {}

    def gather(names_layers):
        return names_layers, gather_exchange([(bf[k], l) for k, l in names_layers])

    def landed(tagged):
        names_layers, exchange = tagged
        W.update(zip(names_layers, exchange.results))

    cw_place = lax.dynamic_update_slice(jnp.zeros((CONV_WIDTH, D), F32), conv_w[0], (jnp.int32(0), me * cshard))
    cw_full = small_all_reduce(_pack([cw_place])).reshape(-1)[:CONV_WIDTH * D].reshape(CONV_WIDTH, D)
    bias_full = jnp.repeat(gmlp_b_s[0].T, gd, axis=1)

    first = gather([("ffn1_w13", 0)])
    run_exchange(first[1], "gather_first")
    landed(first)
    saved = []
    h = xs
    for i in range(depth):
        j = i // 2
        is_gmlp = i % 2 == 0
        mix = ("gmlp_w_in", "gmlp_w_out") if is_gmlp else ("conv_w_in", "conv_w_out")
        on_up1 = gather(([("ffn1_w2", i)] if i == 0 else [])
                        + [(mix[0], j), (mix[1], j), ("xattn_wq", i), ("xattn_wkv", i), ("xattn_wo", i)])
        on_down1 = gather([("ffn2_w13", i)])

        def w2_after_up(on_up=on_up1, i=i):
            landed(on_up)
            return W["ffn1_w2", i]

        h, sv1 = ffn_fwd(h, ffn1_norm[i], W["ffn1_w13", i], w2_after_up, on_up1[1], on_down1[1])
        landed(on_down1)
        hn = rms_fwd(h, mix_norm[i])
        on_mix_in = gather([("ffn2_w2", i)])
        p = mm_nn(hn, W[mix[0], j], "mix_in", tn_pref=768, exchange=on_mix_in[1])
        landed(on_mix_in)
        if is_gmlp:
            gated = gmlp_mid_fwd(p, gmlp_ln_g[j:j + 1], gmlp_ln_b[j:j + 1], gmlp_w_s[j], bias_full)
        else:
            gated = conv_mid_fwd(p, cw_full)
        h_mix = mm_nn(gated, W[mix[1], j].reshape(1, gated.shape[1], D), "mix_out", F32, res=h)
        sv2 = (h, hn, p, gated)
        h, sv3 = xattn_fwd(h_mix, mems, xattn_norm[i], mem_norm[i],
                           W["xattn_wq", i], W["xattn_wkv", i], W["xattn_wo", i])
        on_up2 = gather([("ffn1_w13", i + 1), ("ffn1_w2", i + 1)]) if i + 1 < depth else None
        h, sv4 = ffn_fwd(h, ffn2_norm[i], W["ffn2_w13", i], lambda i=i: W["ffn2_w2", i],
                         None if on_up2 is None else on_up2[1])
        if on_up2 is not None:
            landed(on_up2)
        saved.append((sv1, sv2, sv3, sv4))

    loss_part, d, dys, d_final_norm = loss_head(h, final_norm, target, 0.5)

    small = {k: [None] * depth for k in ("ffn1_norm", "mix_norm", "xattn_norm", "mem_norm", "ffn2_norm")}
    partial = {}
    for i in reversed(range(depth)):
        j = i // 2
        is_gmlp = i % 2 == 0
        sv1, sv2, sv3, sv4 = saved[i]
        d, dys, small["ffn2_norm"][i], partial["ffn2_w13", i], partial["ffn2_w2", i] = ffn_bwd(
            d, dys, sv4, ffn2_norm[i], W["ffn2_w13", i], W["ffn2_w2", i], 1.0)
        d, dys, small["xattn_norm"][i], small["mem_norm"][i], dwq, dwkv, dwo = xattn_bwd(
            d, dys, sv3, mems, xattn_norm[i], W["xattn_wq", i], W["xattn_wkv", i], W["xattn_wo", i], 1.0)
        if is_gmlp:
            mix = ("gmlp_w_in", "gmlp_w_out")
            mid = lambda p, dg: (lambda r: (r[0], r[1:]))(gmlp_mid_bwd(
                p, dg, gmlp_ln_g[j:j + 1], gmlp_ln_b[j:j + 1], gmlp_w_s[j], bias_full))
        else:
            mix = ("conv_w_in", "conv_w_out")
            mid = lambda p, dg: (lambda r: (r[0], r[1:]))(conv_mid_bwd(p, dg, cw_full))
        send_attn = scatter_exchange([dwq, dwkv, dwo])
        d, dys, small["mix_norm"][i], dw_in, dw_out, extra = mixer_bwd_common(
            d, dys, sv2, mix_norm[i], W[mix[0], j], W[mix[1], j], mid, 0.5, dwin_exchange=send_attn)
        partial["xattn_wq", i], partial["xattn_wkv", i], partial["xattn_wo", i] = send_attn.results
        if is_gmlp:
            d_ws, d_bs_wide, d_lng, d_lnb = extra
        else:
            (d_cw,) = extra
        send_mix = scatter_exchange([dw_in, dw_out])
        d, dys, small["ffn1_norm"][i], partial["ffn1_w13", i], partial["ffn1_w2", i] = ffn_bwd(
            d, dys, sv1, ffn1_norm[i], W["ffn1_w13", i], W["ffn1_w2", i], 0.5, dact_exchange=send_mix)
        partial[mix[0], j], partial[mix[1], j] = send_mix.results
    grad_x = d.reshape(x.shape)

    small_grads = {k: jnp.concatenate(v, axis=0) for k, v in small.items()}
    small_grads["gmlp_ln_g"] = d_lng
    small_grads["gmlp_ln_b"] = d_lnb
    small_grads["gmlp_w_s"] = d_ws[None]
    small_grads["gmlp_b_s"] = d_bs_wide[None, :, :, 0]
    small_grads["final_norm"] = d_final_norm.reshape(-1)
    small_names = ["ffn1_norm", "mix_norm", "gmlp_ln_g", "gmlp_ln_b", "gmlp_w_s", "gmlp_b_s",
                   "xattn_norm", "mem_norm", "ffn2_norm", "final_norm"]
    summed = small_all_reduce(_pack([small_grads[k] for k in small_names] + [d_cw, loss_part]))
    parts = _unpack(summed, [given[k].shape for k in small_names] + [(CONV_WIDTH, D), (1, LANES)])
    grads = dict(zip(small_names, parts[:len(small_names)]))
    grads["conv_w"] = lax.dynamic_slice(parts[-2], (jnp.int32(0), me * cshard), (CONV_WIDTH, cshard))[None]
    loss = parts[-1][0, 0]
    flat_names = small_names + ["conv_w"]
    flat = adamw_flat(*[_pack([src[k] for k in flat_names]) for src in
                        (grads, given, {k: given["m_" + k] for k in flat_names},
                         {k: given["v_" + k] for k in flat_names})])
    delta, new_m, new_v = [dict(zip(flat_names, _unpack(f, [given[k].shape for k in flat_names]))) for f in flat]

    for k in bf:
        w = given[k]
        L = w.shape[0]
        shard = w.shape[1:]
        view = lambda a: a.reshape((L,) + shard)
        g, dl, m2, v2 = adamw_sharded([partial[k, l] for l in range(L)], w, given["m_" + k], given["v_" + k])
        grads[k], delta[k], new_m[k], new_v[k] = view(g), view(dl), view(m2), view(v2)

    order = ["ffn1_norm", "ffn1_w13", "ffn1_w2", "mix_norm", "gmlp_w_in", "gmlp_ln_g", "gmlp_ln_b", "gmlp_w_s",
             "gmlp_b_s", "gmlp_w_out", "conv_w_in", "conv_w", "conv_w_out", "xattn_norm", "mem_norm", "xattn_wq",
             "xattn_wkv", "xattn_wo", "ffn2_norm", "ffn2_w13", "ffn2_w2", "final_norm"]
    return (loss, grad_x, *[grads[k] for k in order], *[delta[k] for k in order],
            *[new_m[k] for k in order], *[new_v[k] for k in order])
```

```python
import functools
import math

import jax
import jax.numpy as jnp
from jax import lax
from jax.experimental import pallas as pl
from jax.experimental.pallas import tpu as pltpu

F32 = jnp.float32
BF16 = jnp.bfloat16

N_DEV = 8
N_PEERS = N_DEV - 1
CHUNK = 128
GROUPS = 8
HEADS = 4
CONV_WIDTH = 3
RMS_EPS = 1e-6
LN_EPS = 1e-5
ADAM_LR = 0.001
ADAM_B1 = 0.9
ADAM_B2 = 0.999
ADAM_EPS = 1e-08
ADAM_WD = 0.01
ADAM_STEP = 10
LANES = 128
BF16_SUBLANES = 16
MXU_WIDTH = 256
VMEM_LIMIT_BYTES = 56 * 1024 * 1024

_NT = (((1,), (1,)), ((), ()))
_TN = (((0,), (0,)), ((), ()))
_SQRT_HALF = 0.7071067811865476
_INV_SQRT_2PI = 0.3989422804014327


def _div(n, pref, align):
    best = None
    for t in range(align, min(n, pref) + 1, align):
        if n % t == 0:
            best = t
    return n if best is None else best


def _chunks(n, width):
    return [(c0, min(width, n - c0)) for c0 in range(0, n, width)]


_ANY = pl.BlockSpec(memory_space=pl.ANY)


class Exchange:
    def __init__(self, arrays, out_shapes, sems, start, finish, forward=None):
        self.arrays, self.out_shapes, self.sems = list(arrays), list(out_shapes), list(sems)
        self.start, self.finish, self.forward = start, finish, forward
        self.results = None


def _call(body, name, grid, ins, outs, scratch=(), exchange=None):
    in_specs = [pl.BlockSpec(*spec[1:3], **({"pipeline_mode": pl.Buffered(1)} if len(spec) > 3 else {}))
                for spec in ins]
    ins = [spec[:3] for spec in ins]
    out_specs = [pl.BlockSpec(bs, im) for _, _, bs, im in outs]
    out_shape = [jax.ShapeDtypeStruct(s, d) for s, d, _, _ in outs]
    arrays = [a for a, _, _ in ins]
    scratch = list(scratch)
    kernel_fn = body
    if exchange is not None:
        n_in, n_out, n_scr = len(ins), len(outs), len(scratch)
        n_xin, n_xout = len(exchange.arrays), len(exchange.out_shapes)
        steps = math.prod(grid)
        forward_step = min(steps - 1, (3 * steps) // 4)

        def kernel_fn(*refs):
            refs = list(refs)
            own_in, x_in = refs[:n_in], refs[n_in:n_in + n_xin]
            refs = refs[n_in + n_xin:]
            own_out, x_out = refs[:n_out], refs[n_out:n_out + n_xout]
            refs = refs[n_out + n_xout:]
            own_scr, x_sems = refs[:n_scr], refs[n_scr:]
            step = 0
            for axis, size in enumerate(grid):
                step = step * size + pl.program_id(axis)

            @pl.when(step == 0)
            def _():
                exchange.start(x_in, x_out, x_sems)

            if exchange.forward is not None:
                @pl.when(step == forward_step)
                def _():
                    exchange.forward(x_in, x_out, x_sems)

            body(*own_in, *own_out, *own_scr)

            @pl.when(step == steps - 1)
            def _():
                exchange.finish(x_in, x_out, x_sems)

        in_specs += [_ANY] * n_xin
        out_specs += [_ANY] * n_xout
        out_shape += exchange.out_shapes
        arrays += exchange.arrays
        scratch += exchange.sems
    res = pl.pallas_call(
        kernel_fn,
        name=name,
        grid=grid,
        in_specs=in_specs,
        out_specs=out_specs,
        out_shape=out_shape,
        scratch_shapes=scratch,
        compiler_params=pltpu.CompilerParams(
            dimension_semantics=("arbitrary",) * len(grid), vmem_limit_bytes=VMEM_LIMIT_BYTES),
    )(*arrays)
    if exchange is not None:
        exchange.results = list(res[len(outs):])
        res = res[:len(outs)]
    return res


def _dot(a, b, dims=None):
    if dims is None:
        return jnp.dot(a, b, preferred_element_type=F32)
    return lax.dot_general(a, b, dims, preferred_element_type=F32)


def _sigmoid(v):
    return 1.0 / (1.0 + jnp.exp(-v))


def _gelu(v):
    return 0.5 * v * (1.0 + lax.erf(v * _SQRT_HALF))


def _gelu_grad(v):
    return 0.5 * (1.0 + lax.erf(v * _SQRT_HALF)) + v * (_INV_SQRT_2PI * jnp.exp(-0.5 * v * v))


def _accumulate(ref, part, first):
    @pl.when(first)
    def _():
        ref[...] = part

    @pl.when(jnp.logical_not(first))
    def _():
        ref[...] += part


def rms_fwd(x, g):
    T, D = x.shape
    tt = _div(T, 512, BF16_SUBLANES)

    def body(x_ref, g_ref, o_ref):
        xv = x_ref[...]
        r = lax.rsqrt(jnp.mean(xv * xv, axis=-1, keepdims=True) + RMS_EPS)
        o_ref[...] = ((xv * r) * g_ref[...]).astype(BF16)

    return _call(body, "rms_fwd", (T // tt,),
                 [(x, (tt, D), lambda i: (i, 0)), (g.reshape(1, D), (1, D), lambda i: (0, 0))],
                 [((T, D), BF16, (tt, D), lambda i: (i, 0))])[0]


def rms_gain_grad(x, dxn):
    T, D = x.shape
    tt = _div(T, 256, 8)

    def body(x_ref, dn_ref, dg_ref):
        xv = x_ref[...]
        r = lax.rsqrt(jnp.mean(xv * xv, axis=-1, keepdims=True) + RMS_EPS)
        _accumulate(dg_ref, jnp.sum(dn_ref[...] * (xv * r), axis=0, keepdims=True), pl.program_id(0) == 0)

    row = lambda i: (i, 0)
    return _call(body, "rms_gain_grad", (T // tt,), [(x, (tt, D), row), (dxn, (tt, D), row)],
                 [((1, D), F32, (1, D), lambda i: (0, 0))])[0]


def loss_head(x, g, target, scale):
    T, D = x.shape
    tt = _div(T, 256, BF16_SUBLANES)

    def body(x_ref, g_ref, t_ref, loss_ref, dx_ref, dxs_ref, dg_ref):
        first = pl.program_id(0) == 0
        xv = x_ref[...]
        gv = g_ref[...]
        r = lax.rsqrt(jnp.mean(xv * xv, axis=-1, keepdims=True) + RMS_EPS)
        xh = xv * r
        err = xh * gv - t_ref[...]
        part = 0.5 * jnp.sum(jnp.mean(err * err, axis=-1, keepdims=True), axis=0, keepdims=True)
        _accumulate(loss_ref, jnp.broadcast_to(part, (1, LANES)), first)
        dy = err * (1.0 / D)
        dxh = dy * gv
        dx = r * (dxh - xh * jnp.mean(dxh * xh, axis=-1, keepdims=True))
        dx_ref[...] = dx
        dxs_ref[...] = (scale * dx).astype(BF16)
        _accumulate(dg_ref, jnp.sum(dy * xh, axis=0, keepdims=True), first)

    row = lambda i: (i, 0)
    fix = lambda i: (0, 0)
    return _call(body, "loss_head", (T // tt,),
                 [(x, (tt, D), row), (g.reshape(1, D), (1, D), fix), (target, (tt, D), row)],
                 [((1, LANES), F32, (1, LANES), fix), ((T, D), F32, (tt, D), row),
                  ((T, D), BF16, (tt, D), row), ((1, D), F32, (1, D), fix)])


def mm_nn(a, w3, name, out_dtype=BF16, res=None, res_scale=1.0, tm_pref=1024, tn_pref=1024, exchange=None):
    M, K = a.shape
    nb, _, ns = w3.shape
    tn = _div(ns, tn_pref, LANES)
    per = ns // tn
    tm = _div(M, tm_pref, BF16_SUBLANES)
    ins = [(a, (tm, K), lambda j, m: (m, 0)), (w3, (None, K, tn), lambda j, m: (j // per, 0, j % per))]
    if res is not None:
        ins.append((res, (tm, tn), lambda j, m: (m, j)))

    def body(*refs):
        a_ref, w_ref = refs[0], refs[1]
        o_ref = refs[-1]
        acc = _dot(a_ref[...], w_ref[...])
        if res is not None:
            acc = refs[2][...] + res_scale * acc
        o_ref[...] = acc.astype(o_ref.dtype)

    return _call(body, name, (nb * per, M // tm), ins,
                 [((M, nb * ns), out_dtype, (tm, tn), lambda j, m: (m, j))], exchange=exchange)[0]


def mm_nt(a_in, w3, name, out_dtype, M, tm_pref=1024, to_pref=2048, exchange=None):
    nb, Ko, ns = w3.shape
    to = _div(Ko, to_pref, LANES)
    tm = _div(M, tm_pref, BF16_SUBLANES)
    if isinstance(a_in, tuple):
        a, a_bs, a_im = a_in
        a_bs = tuple(tm if s == "tm" else s for s in a_bs)
    else:
        a, a_bs, a_im = a_in, (tm, ns), lambda m, o, b: (m, b)
    if nb > 1:
        assert out_dtype == F32

    def body(a_ref, w_ref, o_ref):
        p = _dot(a_ref[...], w_ref[...], _NT)
        if nb == 1:
            o_ref[...] = p.astype(o_ref.dtype)
        else:
            _accumulate(o_ref, p, pl.program_id(2) == 0)

    return _call(body, name, (M // tm, Ko // to, nb),
                 [(a, a_bs, a_im), (w3, (None, to, ns), lambda m, o, b: (b, o, 0))],
                 [((M, Ko), out_dtype, (tm, to), lambda m, o, b: (m, o))], exchange=exchange)[0]


def mm_nt_rms_bwd(a_in, w3, name, x, g, d, scale, exchange=None):
    nb, D, ns = w3.shape
    M = x.shape[0]
    tm = _div(M, 512, BF16_SUBLANES)
    if isinstance(a_in, tuple):
        a, a_bs, a_im = a_in
        a_bs = tuple(tm if s == "tm" else s for s in a_bs)
    else:
        a, a_bs, a_im = a_in, (tm, ns), lambda m, b: (m, b)

    rc = _div(tm, 64, BF16_SUBLANES)

    def body(a_ref, w_ref, x_ref, g_ref, d_ref, dx_ref, dxs_ref, dg_ref, acc_ref):
        m, b = pl.program_id(0), pl.program_id(1)
        _accumulate(acc_ref, _dot(a_ref[...], w_ref[...], _NT), b == 0)

        @pl.when(b == nb - 1)
        def _():
            gv = g_ref[...]

            def piece(c, dg):
                rows = pl.ds(pl.multiple_of(c * rc, rc), rc)
                dn = acc_ref[rows, :]
                xv = x_ref[rows, :]
                r = lax.rsqrt(jnp.mean(xv * xv, axis=-1, keepdims=True) + RMS_EPS)
                xh = xv * r
                dxh = dn * gv
                dx = r * (dxh - xh * jnp.mean(dxh * xh, axis=-1, keepdims=True)) + d_ref[rows, :]
                dx_ref[rows, :] = dx
                dxs_ref[rows, :] = (scale * dx).astype(BF16)
                return dg + jnp.sum(dn * xh, axis=0, keepdims=True)

            _accumulate(dg_ref, lax.fori_loop(0, tm // rc, piece, jnp.zeros((1, D), F32)), m == 0)

    row = lambda m, b: (m, 0)
    fix = lambda m, b: (0, 0)
    w_spec = (w3, (None, D, ns), lambda m, b: (b, 0, 0)) + (("single",) if nb == 1 else ())
    return _call(body, name, (M // tm, nb),
                 [(a, a_bs, a_im), w_spec,
                  (x, (tm, D), row), (g.reshape(1, D), (1, D), fix), (d, (tm, D), row)],
                 [((M, D), F32, (tm, D), row), ((M, D), BF16, (tm, D), row), ((1, D), F32, (1, D), fix)],
                 scratch=[pltpu.VMEM((tm, D), F32)], exchange=exchange)


def mm_tn(a, b_in, name, nbo, ns, tka_pref=1024, tt_pref=2048, tn_pref=2048, exchange=None):
    T, Ka = a.shape
    tt = _div(T, tt_pref, BF16_SUBLANES)
    tka = _div(Ka, tka_pref, LANES)
    if isinstance(b_in, tuple):
        b, b_bs, b_im = b_in
        b_bs = tuple(tt if s == "tt" else s for s in b_bs)
        tn, per = ns, 1
    else:
        tn = _div(ns, tn_pref, LANES)
        per = ns // tn
        b, b_bs, b_im = b_in, (tt, tn), lambda i, j, t: (t, j)
    nt = T // tt

    def body(a_ref, b_ref, o_ref, acc_ref):
        t = pl.program_id(2)
        _accumulate(acc_ref, _dot(a_ref[...], b_ref[...], _TN), t == 0)

        @pl.when(t == nt - 1)
        def _():
            o_ref[...] = acc_ref[...].astype(BF16)

    return _call(body, name, (Ka // tka, nbo * per, nt),
                 [(a, (tt, tka), lambda i, j, t: (t, i)), (b, b_bs, b_im)],
                 [((nbo, Ka, ns), BF16, (None, tka, tn), lambda i, j, t: (j // per, i, j % per))],
                 scratch=[pltpu.VMEM((tka, tn), F32)], exchange=exchange)[0]


def ffn_up(xn, w13, exchange=None):
    T, D = xn.shape
    nb, _, ns = w13.shape
    half = nb // 2
    F = half * ns
    tm = _div(T, 1024, BF16_SUBLANES)

    def body(x_ref, wg_ref, wu_ref, gu_ref, act_ref):
        xv = x_ref[...]
        for c0, cw in _chunks(ns, MXU_WIDTH):
            cols = slice(c0, c0 + cw)
            gate = _dot(xv, wg_ref[:, cols])
            up = _dot(xv, wu_ref[:, cols])
            gu_ref[0, :, cols] = gate.astype(BF16)
            gu_ref[1, :, cols] = up.astype(BF16)
            act_ref[:, cols] = (gate * _sigmoid(gate) * up).astype(BF16)

    return _call(body, "ffn_up", (half, T // tm),
                 [(xn, (tm, D), lambda j, m: (m, 0)),
                  (w13, (None, D, ns), lambda j, m: (j, 0, 0), "single"),
                  (w13, (None, D, ns), lambda j, m: (j + half, 0, 0), "single")],
                 [((2, T, F), BF16, (2, tm, ns), lambda j, m: (0, m, j)),
                  ((T, F), BF16, (tm, ns), lambda j, m: (m, j))], exchange=exchange)


def ffn_dact(dy, w2, gu, exchange=None):
    T, D = dy.shape
    F = w2.shape[0]
    tm = _div(T, 512, BF16_SUBLANES)
    tn = _div(F, F // 2, MXU_WIDTH)

    def body(dy_ref, w_ref, gu_ref, dh_ref):
        dyv = dy_ref[...]
        for c0, cw in _chunks(tn, MXU_WIDTH):
            cols = slice(c0, c0 + cw)
            da = _dot(dyv, w_ref[cols, :], _NT)
            gate = gu_ref[0, :, cols].astype(F32)
            up = gu_ref[1, :, cols].astype(F32)
            s = _sigmoid(gate)
            dh_ref[0, :, cols] = (da * up * (s * (1.0 + gate * (1.0 - s)))).astype(BF16)
            dh_ref[1, :, cols] = (da * (gate * s)).astype(BF16)

    return _call(body, "ffn_dact", (F // tn, T // tm),
                 [(dy, (tm, D), lambda j, m: (m, 0)), (w2, (tn, D), lambda j, m: (j, 0), "single"),
                  (gu, (2, tm, tn), lambda j, m: (0, m, j))],
                 [((2, T, F), BF16, (2, tm, tn), lambda j, m: (0, m, j))], exchange=exchange)[0]


def ffn_fwd(x, norm_g, w13, get_w2, up_exchange=None, down_exchange=None):
    xn = rms_fwd(x, norm_g)
    gu, act = ffn_up(xn, w13, exchange=up_exchange)
    F = act.shape[1]
    y = mm_nn(act, get_w2().reshape(1, F, -1), "ffn_down", F32, res=x, res_scale=0.5, tm_pref=512,
              exchange=down_exchange)
    return y, (x, xn, gu, act)


def ffn_bwd(d, dys, saved, norm_g, w13, w2, scale_out, dact_exchange=None):
    x, xn, gu, act = saved
    T, D = x.shape
    nb, _, ns = w13.shape
    half = nb // 2
    F = half * ns
    dh = ffn_dact(dys, w2.reshape(F, D), gu, exchange=dact_exchange)
    dw2 = mm_tn(act, dys, "ffn_dw2", 1, D, tka_pref=ns, tn_pref=1024)
    send_w2 = scatter_exchange([dw2.reshape(N_DEV, F // N_DEV, D)])
    dw13 = mm_tn(xn, (dh, (None, "tt", ns), lambda i, j, t: (j // half, t, j % half)), "ffn_dw13", nb, ns,
                 exchange=send_w2)
    send_w13 = scatter_exchange([dw13])
    dx, dxs, dg = mm_nt_rms_bwd((dh, (None, "tm", ns), lambda m, b: (b // half, m, b % half)), w13, "ffn_dxn",
                                x, norm_g, d, scale_out, exchange=send_w13)
    return dx, dxs, dg, send_w13.results[0], send_w2.results[0]


def _gmlp_parts(p_ref, lng_ref, lnb_ref):
    E = lng_ref.shape[-1]
    z = _gelu(p_ref[...].astype(F32))
    u = z[:, :E]
    vp = z[:, E:]
    mu = jnp.mean(vp, axis=-1, keepdims=True)
    xc = vp - mu
    rstd = lax.rsqrt(jnp.mean(xc * xc, axis=-1, keepdims=True) + LN_EPS)
    vh = xc * rstd
    v = vh * lng_ref[...] + lnb_ref[...]
    return u, vh, rstd, v


def _causal_ws(ws_ref, g):
    keep = lax.broadcasted_iota(jnp.int32, (CHUNK, CHUNK), 0) >= lax.broadcasted_iota(jnp.int32, (CHUNK, CHUNK), 1)
    return jnp.where(keep, ws_ref[g], 0.0).astype(BF16), keep


def gmlp_mid_fwd(p, ln_g, ln_b, w_s, bias_full):
    T, E2 = p.shape
    E = E2 // 2
    gd = E // GROUPS
    tm = _div(T, 256, CHUNK)
    fix2 = lambda i: (0, 0)

    def body(p_ref, lng_ref, lnb_ref, ws_ref, bias_ref, o_ref):
        u, _, _, v = _gmlp_parts(p_ref, lng_ref, lnb_ref)
        vb = v.astype(BF16)
        for g in range(GROUPS):
            wm, _ = _causal_ws(ws_ref, g)
            cols = slice(g * gd, (g + 1) * gd)
            for c in range(tm // CHUNK):
                rows = slice(c * CHUNK, (c + 1) * CHUNK)
                f = _dot(wm, vb[rows, cols]) + bias_ref[:, cols]
                o_ref[rows, cols] = (u[rows, cols] * f).astype(BF16)

    return _call(body, "gmlp_mid_fwd", (T // tm,),
                 [(p, (tm, E2), lambda i: (i, 0)), (ln_g, (1, E), fix2), (ln_b, (1, E), fix2),
                  (w_s, (GROUPS, CHUNK, CHUNK), lambda i: (0, 0, 0)), (bias_full, (CHUNK, E), fix2)],
                 [((T, E), BF16, (tm, E), lambda i: (i, 0))])[0]


def gmlp_mid_bwd(p, dgated, ln_g, ln_b, w_s, bias_full):
    T, E2 = p.shape
    E = E2 // 2
    gd = E // GROUPS
    tm = _div(T, 256, CHUNK)
    nsteps = T // tm
    fix2 = lambda i: (0, 0)
    fix3 = lambda i: (0, 0, 0)

    def body(p_ref, dg_ref, lng_ref, lnb_ref, ws_ref, bias_ref,
             dp_ref, dws_ref, dbs_ref, dlng_ref, dlnb_ref, f_sc, dv_sc, db_sc):
        i = pl.program_id(0)
        first = i == 0
        u, vh, rstd, v = _gmlp_parts(p_ref, lng_ref, lnb_ref)
        vb = v.astype(BF16)
        dgt = dg_ref[...].astype(F32)
        df = dgt * u
        dfb = df.astype(BF16)
        for g in range(GROUPS):
            wm, keep = _causal_ws(ws_ref, g)
            cols = slice(g * gd, (g + 1) * gd)
            dw = None
            dbg = None
            for c in range(tm // CHUNK):
                rows = slice(c * CHUNK, (c + 1) * CHUNK)
                f_sc[rows, cols] = _dot(wm, vb[rows, cols]) + bias_ref[:, cols]
                dv_sc[rows, cols] = _dot(wm, dfb[rows, cols], _TN)
                part = _dot(dfb[rows, cols], vb[rows, cols], _NT)
                dw = part if dw is None else dw + part
                dbg = df[rows, cols] if dbg is None else dbg + df[rows, cols]
            dw = jnp.where(keep, dw, 0.0)

            @pl.when(first)
            def _():
                dws_ref[g] = dw
                db_sc[:, cols] = dbg

            @pl.when(jnp.logical_not(first))
            def _():
                dws_ref[g] += dw
                db_sc[:, cols] += dbg

        du = dgt * f_sc[...]
        dv = dv_sc[...]
        _accumulate(dlng_ref, jnp.sum(dv * vh, axis=0, keepdims=True), first)
        _accumulate(dlnb_ref, jnp.sum(dv, axis=0, keepdims=True), first)
        dvh = dv * lng_ref[...]
        dvp = rstd * (dvh - jnp.mean(dvh, axis=-1, keepdims=True)
                      - vh * jnp.mean(dvh * vh, axis=-1, keepdims=True))
        gp = _gelu_grad(p_ref[...].astype(F32))
        dp_ref[:, :E] = (du * gp[:, :E]).astype(BF16)
        dp_ref[:, E:] = (dvp * gp[:, E:]).astype(BF16)

        @pl.when(i == nsteps - 1)
        def _():
            for g in range(GROUPS):
                tot = jnp.sum(db_sc[:, g * gd:(g + 1) * gd], axis=-1, keepdims=True)
                dbs_ref[g] = jnp.broadcast_to(tot, (CHUNK, LANES))

    return _call(body, "gmlp_mid_bwd", (nsteps,),
                 [(p, (tm, E2), lambda i: (i, 0)), (dgated, (tm, E), lambda i: (i, 0)),
                  (ln_g, (1, E), fix2), (ln_b, (1, E), fix2),
                  (w_s, (GROUPS, CHUNK, CHUNK), fix3), (bias_full, (CHUNK, E), fix2)],
                 [((T, E2), BF16, (tm, E2), lambda i: (i, 0)),
                  ((GROUPS, CHUNK, CHUNK), F32, (GROUPS, CHUNK, CHUNK), fix3),
                  ((GROUPS, CHUNK, LANES), F32, (GROUPS, CHUNK, LANES), fix3),
                  ((1, E), F32, (1, E), fix2), ((1, E), F32, (1, E), fix2)],
                 scratch=[pltpu.VMEM((tm, E), F32), pltpu.VMEM((tm, E), F32), pltpu.VMEM((CHUNK, E), F32)])


HALO = 16


def _row_of(block, r):
    rows = lax.broadcasted_iota(jnp.int32, block.shape, 0)
    return jnp.sum(jnp.where(rows == r, block, 0.0), axis=0, keepdims=True)


def _shift_down(z, k, fill):
    out = pltpu.roll(z, k, 0)
    rows = lax.broadcasted_iota(jnp.int32, z.shape, 0)
    for t in range(k):
        out = jnp.where(rows == t, fill[t], out)
    return out


def _shift_up(z, k, fill):
    n = z.shape[0]
    out = pltpu.roll(z, n - k, 0)
    rows = lax.broadcasted_iota(jnp.int32, z.shape, 0)
    for j in range(k):
        out = jnp.where(rows == n - k + j, fill[j], out)
    return out


def _conv_parts(p_ref, prev_ref, cw_ref, is_first):
    D = cw_ref.shape[-1]
    pv = p_ref[...].astype(F32)
    bg, cg, val = pv[:, :D], pv[:, D:2 * D], pv[:, 2 * D:]
    z = cg * val
    pp = prev_ref[...].astype(F32)
    zp = jnp.where(is_first, 0.0, pp[:, D:2 * D] * pp[:, 2 * D:])
    zl1 = _row_of(zp, HALO - 1)
    zl2 = _row_of(zp, HALO - 2)
    z1 = _shift_down(z, 1, [zl1])
    z2 = _shift_down(z, 2, [zl2, zl1])
    conv = z2 * cw_ref[0:1, :] + z1 * cw_ref[1:2, :] + z * cw_ref[2:3, :]
    return bg, cg, val, z, z1, z2, conv


def conv_mid_fwd(p, cw):
    T, D3 = p.shape
    D = D3 // 3
    tm = _div(T, 256, HALO)
    per = tm // HALO

    def body(p_ref, prev_ref, cw_ref, o_ref):
        bg, _, _, _, _, _, conv = _conv_parts(p_ref, prev_ref, cw_ref, pl.program_id(0) == 0)
        o_ref[...] = (bg * conv).astype(BF16)

    return _call(body, "conv_mid_fwd", (T // tm,),
                 [(p, (tm, D3), lambda i: (i, 0)),
                  (p, (HALO, D3), lambda i: (jnp.maximum(i * per - 1, 0), 0)),
                  (cw, (CONV_WIDTH, D), lambda i: (0, 0))],
                 [((T, D), BF16, (tm, D), lambda i: (i, 0))])[0]


def conv_mid_bwd(p, dgated, cw):
    T, D3 = p.shape
    D = D3 // 3
    tm = _div(T, 256, HALO)
    per = tm // HALO
    nsteps = T // tm
    last_halo = T // HALO - 1
    nxt = lambda i: (jnp.minimum((i + 1) * per, last_halo), 0)

    def body(p_ref, prev_ref, next_ref, dg_ref, dgn_ref, cw_ref, dp_ref, dcw_ref):
        i = pl.program_id(0)
        bg, cg, val, z, z1, z2, conv = _conv_parts(p_ref, prev_ref, cw_ref, i == 0)
        dgt = dg_ref[...].astype(F32)
        dconv = dgt * bg
        dcn = jnp.where(i == nsteps - 1, 0.0, dgn_ref[...].astype(F32) * next_ref[:, :D].astype(F32))
        n0 = _row_of(dcn, 0)
        n1 = _row_of(dcn, 1)
        up1 = _shift_up(dconv, 1, [n0])
        up2 = _shift_up(dconv, 2, [n0, n1])
        dz = dconv * cw_ref[2:3, :] + up1 * cw_ref[1:2, :] + up2 * cw_ref[0:1, :]
        dp_ref[:, :D] = (dgt * conv).astype(BF16)
        dp_ref[:, D:2 * D] = (dz * val).astype(BF16)
        dp_ref[:, 2 * D:] = (dz * cg).astype(BF16)
        first = i == 0
        parts = (jnp.sum(dconv * z2, axis=0, keepdims=True), jnp.sum(dconv * z1, axis=0, keepdims=True),
                 jnp.sum(dconv * z, axis=0, keepdims=True))

        @pl.when(first)
        def _():
            for k in range(CONV_WIDTH):
                dcw_ref[k:k + 1, :] = parts[k]

        @pl.when(jnp.logical_not(first))
        def _():
            for k in range(CONV_WIDTH):
                dcw_ref[k:k + 1, :] += parts[k]

    return _call(body, "conv_mid_bwd", (nsteps,),
                 [(p, (tm, D3), lambda i: (i, 0)),
                  (p, (HALO, D3), lambda i: (jnp.maximum(i * per - 1, 0), 0)),
                  (p, (HALO, D3), nxt),
                  (dgated, (tm, D), lambda i: (i, 0)),
                  (dgated, (HALO, D), nxt),
                  (cw, (CONV_WIDTH, D), lambda i: (0, 0))],
                 [((T, D3), BF16, (tm, D3), lambda i: (i, 0)),
                  ((CONV_WIDTH, D), F32, (CONV_WIDTH, D), lambda i: (0, 0))])


def mixer_bwd_common(d, dys, saved, norm_g, w_in, w_out, mid_bwd, scale_out, dwin_exchange=None):
    x, hn, p, gated = saved
    T, D = x.shape
    E = gated.shape[1]
    w_out3 = w_out.reshape(1, E, D)
    dgated = mm_nt(dys, w_out3, "mix_dgated", BF16, T)
    dw_out = mm_tn(gated, dys, "mix_dwout", 1, D)
    dp, extra = mid_bwd(p, dgated)
    nb, _, ns = w_in.shape
    dw_in = mm_tn(hn, dp, "mix_dwin", nb, ns, exchange=dwin_exchange)
    dx, dxs, dg = mm_nt_rms_bwd(dp, w_in, "mix_dhn", x, norm_g, d, scale_out)
    return dx, dxs, dg, dw_in, dw_out.reshape(N_DEV, E // N_DEV, D), extra


def _softmax_rows(s):
    e = jnp.exp(s - jnp.max(s, axis=-1, keepdims=True))
    return e / jnp.sum(e, axis=-1, keepdims=True)


def attn_fwd(q, kv):
    T, D = q.shape
    M = kv.shape[0]
    hd = D // HEADS
    scale = hd ** -0.5
    tm = _div(T, 512, BF16_SUBLANES)

    def body(q_ref, kv_ref, o_ref):
        for h in range(HEADS):
            cols = slice(h * hd, (h + 1) * hd)
            s = _dot(q_ref[:, cols], kv_ref[:, cols], _NT) * scale
            pr = _softmax_rows(s).astype(BF16)
            o_ref[:, cols] = _dot(pr, kv_ref[:, D + h * hd:D + (h + 1) * hd]).astype(BF16)

    return _call(body, "attn_fwd", (T // tm,),
                 [(q, (tm, D), lambda i: (i, 0)), (kv, (M, 2 * D), lambda i: (0, 0))],
                 [((T, D), BF16, (tm, D), lambda i: (i, 0))])[0]


def attn_bwd(q, do, kv):
    T, D = q.shape
    M = kv.shape[0]
    hd = D // HEADS
    scale = hd ** -0.5
    tm = _div(T, 512, BF16_SUBLANES)
    nsteps = T // tm

    def body(q_ref, do_ref, kv_ref, dq_ref, dkv_ref, acc_ref):
        i = pl.program_id(0)
        for h in range(HEADS):
            cols = slice(h * hd, (h + 1) * hd)
            vcols = slice(D + h * hd, D + (h + 1) * hd)
            qh = q_ref[:, cols]
            kh = kv_ref[:, cols]
            doh = do_ref[:, cols]
            pr = _softmax_rows(_dot(qh, kh, _NT) * scale)
            dpr = _dot(doh, kv_ref[:, vcols], _NT)
            ds = (pr * (dpr - jnp.sum(dpr * pr, axis=-1, keepdims=True)) * scale).astype(BF16)
            dq_ref[:, cols] = _dot(ds, kh).astype(BF16)
            dk = _dot(ds, qh, _TN)
            dv = _dot(pr.astype(BF16), doh, _TN)

            @pl.when(i == 0)
            def _():
                acc_ref[:, cols] = dk
                acc_ref[:, vcols] = dv

            @pl.when(i > 0)
            def _():
                acc_ref[:, cols] += dk
                acc_ref[:, vcols] += dv

        @pl.when(i == nsteps - 1)
        def _():
            dkv_ref[...] = acc_ref[...].astype(BF16)

    return _call(body, "attn_bwd", (nsteps,),
                 [(q, (tm, D), lambda i: (i, 0)), (do, (tm, D), lambda i: (i, 0)),
                  (kv, (M, 2 * D), lambda i: (0, 0))],
                 [((T, D), BF16, (tm, D), lambda i: (i, 0)), ((M, 2 * D), BF16, (M, 2 * D), lambda i: (0, 0))],
                 scratch=[pltpu.VMEM((M, 2 * D), F32)])


def xattn_fwd(x, mem, xnorm_g, mnorm_g, wq, wkv, wo):
    D = x.shape[1]
    hq = rms_fwd(x, xnorm_g)
    mn = rms_fwd(mem, mnorm_g)
    q = mm_nn(hq, wq.reshape(1, D, D), "xattn_q")
    kv = mm_nn(mn, wkv, "xattn_kv")
    o = attn_fwd(q, kv)
    y = mm_nn(o, wo.reshape(1, D, D), "xattn_out", F32, res=x)
    return y, (x, hq, mn, q, kv, o)


def xattn_bwd(d, dys, saved, mem, xnorm_g, wq, wkv, wo, scale_out):
    x, hq, mn, q, kv, o = saved
    T, D = x.shape
    M = mem.shape[0]
    do = mm_nt(dys, wo.reshape(1, D, D), "xattn_do", BF16, T)
    dwo = mm_tn(o, dys, "xattn_dwo", 1, D)
    dq, dkv = attn_bwd(q, do, kv)
    dwq = mm_tn(hq, dq, "xattn_dwq", 1, D)
    dx, dxs, dgx = mm_nt_rms_bwd(dq, wq.reshape(1, D, D), "xattn_dhq", x, xnorm_g, d, scale_out)
    nb, _, ns = wkv.shape
    dwkv = mm_tn(mn, dkv, "xattn_dwkv", nb, ns)
    dmn = mm_nt(dkv, wkv, "xattn_dmn", F32, M)
    dgm = rms_gain_grad(mem, dmn)
    rows = D // N_DEV
    return dx, dxs, dgx, dgm, dwq.reshape(N_DEV, rows, D), dwkv, dwo.reshape(N_DEV, rows, D)


def _mesh_places():
    x, y, c = lax.axis_index("x"), lax.axis_index("y"), lax.axis_index("c")
    chips = [(1 - x, y), (x, 1 - y), (1 - x, 1 - y)]
    return (x, y, c), (x, y, 1 - c), chips


def _slot(place):
    return 4 * place[0] + 2 * place[1] + place[2]


def _exchange_sems(n):
    return [pltpu.SemaphoreType.DMA((n * N_PEERS,)), pltpu.SemaphoreType.DMA((n * N_PEERS,)),
            pltpu.SemaphoreType.DMA((n,))]


def gather_exchange(shards):
    n = len(shards)
    shapes = [a.shape if l is None else a.shape[1:] for a, l in shards]

    def parts(x_in, x_out, sems):
        ins = [r if l is None else r.at[l] for r, (_, l) in zip(x_in, shards)]
        send_sems, recv_sems, local_sems = sems
        me, sibling, chips = _mesh_places()

        def copy(a, k, block, to, src=None):
            dst = x_out[a].at[_slot(block)]
            return pltpu.make_async_remote_copy(
                src_ref=dst if src is None else src, dst_ref=dst,
                send_sem=send_sems.at[a * N_PEERS + k], recv_sem=recv_sems.at[a * N_PEERS + k],
                device_id=to, device_id_type=pl.DeviceIdType.MESH)

        mine = [pltpu.make_async_copy(ins[a], x_out[a].at[_slot(me)], local_sems.at[a]) for a in range(n)]
        first = []
        for a in range(n):
            first.append(copy(a, 0, me, sibling, src=ins[a]))
            first += [copy(a, 1 + j, me, (*chip, me[2]), src=ins[a]) for j, chip in enumerate(chips)]
        return me, sibling, chips, copy, mine, first

    def start(x_in, x_out, sems):
        _, _, _, _, mine, first = parts(x_in, x_out, sems)
        for cp in mine + first:
            cp.start()

    def forward(x_in, x_out, sems):
        me, sibling, chips, copy, _, _ = parts(x_in, x_out, sems)
        for j, chip in enumerate(chips):
            for a in range(n):
                copy(a, 1 + j, (*chip, me[2]), me).wait_recv()
                copy(a, 4 + j, (*chip, me[2]), sibling).start()

    def finish(x_in, x_out, sems):
        me, sibling, chips, copy, mine, first = parts(x_in, x_out, sems)
        for a in range(n):
            copy(a, 0, sibling, me).wait_recv()
        for j, chip in enumerate(chips):
            for a in range(n):
                copy(a, 4 + j, (*chip, 1 - me[2]), me).wait_recv()
        for cp in first:
            cp.wait_send()
        for j, chip in enumerate(chips):
            for a in range(n):
                copy(a, 4 + j, (*chip, me[2]), sibling).wait_send()
        for cp in mine:
            cp.wait()

    return Exchange([a for a, _ in shards],
                    [jax.ShapeDtypeStruct((N_DEV,) + tuple(s), BF16) for s in shapes],
                    _exchange_sems(n), start, finish, forward)


def _all_peers(me, chips):
    c = me[2]
    return [(me[0], me[1], 1 - c)] + [(*chip, c) for chip in chips] + [(*chip, 1 - c) for chip in chips]


def scatter_exchange(grads):
    n = len(grads)

    def parts(x_in, x_out, sems):
        send_sems, recv_sems, local_sems = sems
        me, _, chips = _mesh_places()
        mine = [pltpu.make_async_copy(x_in[a].at[_slot(me)], x_out[a].at[_slot(me)], local_sems.at[a])
                for a in range(n)]
        sends, recvs = [], []
        for a in range(n):
            for k, peer in enumerate(_all_peers(me, chips)):
                sem = dict(send_sem=send_sems.at[a * N_PEERS + k], recv_sem=recv_sems.at[a * N_PEERS + k],
                           device_id=peer, device_id_type=pl.DeviceIdType.MESH)
                sends.append(pltpu.make_async_remote_copy(
                    src_ref=x_in[a].at[_slot(peer)], dst_ref=x_out[a].at[_slot(me)], **sem))
                recvs.append(pltpu.make_async_remote_copy(
                    src_ref=x_in[a].at[_slot(peer)], dst_ref=x_out[a].at[_slot(peer)], **sem))
        return mine, sends, recvs

    def start(x_in, x_out, sems):
        mine, sends, _ = parts(x_in, x_out, sems)
        for cp in mine + sends:
            cp.start()

    def finish(x_in, x_out, sems):
        mine, sends, recvs = parts(x_in, x_out, sems)
        for cp in recvs:
            cp.wait_recv()
        for cp in sends:
            cp.wait_send()
        for cp in mine:
            cp.wait()

    return Exchange(grads, [jax.ShapeDtypeStruct(g.shape, g.dtype) for g in grads],
                    _exchange_sems(n), start, finish)


def run_exchange(exchange, name):
    n_in, n_out = len(exchange.arrays), len(exchange.out_shapes)

    def body(*refs):
        x_in, x_out, sems = refs[:n_in], refs[n_in:n_in + n_out], refs[n_in + n_out:]
        exchange.start(x_in, x_out, sems)
        if exchange.forward is not None:
            exchange.forward(x_in, x_out, sems)
        exchange.finish(x_in, x_out, sems)

    exchange.results = list(pl.pallas_call(
        body, name=name, out_shape=exchange.out_shapes, in_specs=[_ANY] * n_in, out_specs=[_ANY] * n_out,
        scratch_shapes=exchange.sems)(*exchange.arrays))
    return exchange.results


def small_all_reduce(vec):
    R = vec.shape[0]

    def body(v_ref, o_ref, all_ref, send_sems, recv_sems):
        me, _, chips = _mesh_places()
        peers = _all_peers(me, chips)
        all_ref[_slot(me)] = v_ref[...]
        sends, recvs = [], []
        for k, peer in enumerate(peers):
            sem = dict(send_sem=send_sems.at[k], recv_sem=recv_sems.at[k],
                       device_id=peer, device_id_type=pl.DeviceIdType.MESH)
            sends.append(pltpu.make_async_remote_copy(src_ref=v_ref, dst_ref=all_ref.at[_slot(me)], **sem))
            recvs.append(pltpu.make_async_remote_copy(src_ref=v_ref, dst_ref=all_ref.at[_slot(peer)], **sem))
        for cp in sends:
            cp.start()
        for cp in recvs:
            cp.wait_recv()
        for cp in sends:
            cp.wait_send()
        acc = all_ref[0]
        for s in range(1, N_DEV):
            acc = acc + all_ref[s]
        o_ref[...] = acc

    return pl.pallas_call(
        body, name="small_all_reduce",
        out_shape=jax.ShapeDtypeStruct(vec.shape, F32),
        in_specs=[pl.BlockSpec(memory_space=pltpu.VMEM)], out_specs=pl.BlockSpec(memory_space=pltpu.VMEM),
        scratch_shapes=[pltpu.VMEM((N_DEV, R, LANES), F32), pltpu.SemaphoreType.DMA((N_PEERS,)),
                        pltpu.SemaphoreType.DMA((N_PEERS,))],
    )(vec)


def _adamw_math(w, g, m, v):
    m2 = ADAM_B1 * m + (1.0 - ADAM_B1) * g
    v2 = ADAM_B2 * v + (1.0 - ADAM_B2) * (g * g)
    m_hat = m2 / (1.0 - ADAM_B1 ** ADAM_STEP)
    v_hat = v2 / (1.0 - ADAM_B2 ** ADAM_STEP)
    delta = -ADAM_LR * (m_hat / (jnp.sqrt(v_hat) + ADAM_EPS) + ADAM_WD * w)
    return delta, m2, v2


def adamw_sharded(partials, w, m, v):
    L, r, c = w.shape
    tr = _div(r, max(BF16_SUBLANES, (1 << 18) // c), BF16_SUBLANES)
    nt = r // tr

    def part_map(l0):
        return lambda l, t: (0, jnp.where(l == l0, t, jnp.where(l < l0, 0, nt - 1)), 0)

    def body(*refs):
        parts = refs[:L]
        w_ref, m_ref, v_ref, g_out, d_out, m_out, v_out = refs[L:]
        layer = pl.program_id(0)
        for l0 in range(L):
            @pl.when(layer == l0)
            def _():
                g = parts[l0][0].astype(F32)
                for s in range(1, N_DEV):
                    g = g + parts[l0][s].astype(F32)
                delta, m2, v2 = _adamw_math(w_ref[...], g, m_ref[...], v_ref[...])
                g_out[...] = g
                d_out[...] = delta
                m_out[...] = m2
                v_out[...] = v2

    own = lambda l, t: (l, t, 0)
    return _call(body, "adamw_sharded", (L, nt),
                 [(p, (N_DEV, tr, c), part_map(l0)) for l0, p in enumerate(partials)]
                 + [(w, (None, tr, c), own), (m, (None, tr, c), own), (v, (None, tr, c), own)],
                 [((L, r, c), F32, (None, tr, c), own)] * 4)


def adamw_flat(g, w, m, v):
    shape = g.shape

    def body(g_ref, w_ref, m_ref, v_ref, d_out, m_out, v_out):
        delta, m2, v2 = _adamw_math(w_ref[...], g_ref[...], m_ref[...], v_ref[...])
        d_out[...] = delta
        m_out[...] = m2
        v_out[...] = v2

    whole = lambda: (0, 0)
    return _call(body, "adamw_flat", (), [(a, shape, whole) for a in (g, w, m, v)],
                 [(shape, F32, shape, whole)] * 3)


def _pack(parts):
    flat = jnp.concatenate([p.reshape(-1).astype(F32) for p in parts])
    rows = -(-flat.shape[0] // (8 * LANES)) * 8
    return jnp.pad(flat, (0, rows * LANES - flat.shape[0])).reshape(rows, LANES)


def _unpack(packed, shapes):
    flat = packed.reshape(-1)
    out, off = [], 0
    for s in shapes:
        size = math.prod(s)
        out.append(flat[off:off + size].reshape(s))
        off += size
    return out


def kernel(x, mem, ffn1_norm, ffn1_w13, ffn1_w2, mix_norm, gmlp_w_in, gmlp_ln_g, gmlp_ln_b, gmlp_w_s, gmlp_b_s, gmlp_w_out, conv_w_in, conv_w, conv_w_out, xattn_norm, mem_norm, xattn_wq, xattn_wkv, xattn_wo, ffn2_norm, ffn2_w13, ffn2_w2, final_norm, loss_target, m_ffn1_norm, m_ffn1_w13, m_ffn1_w2, m_mix_norm, m_gmlp_w_in, m_gmlp_ln_g, m_gmlp_ln_b, m_gmlp_w_s, m_gmlp_b_s, m_gmlp_w_out, m_conv_w_in, m_conv_w, m_conv_w_out, m_xattn_norm, m_mem_norm, m_xattn_wq, m_xattn_wkv, m_xattn_wo, m_ffn2_norm, m_ffn2_w13, m_ffn2_w2, m_final_norm, v_ffn1_norm, v_ffn1_w13, v_ffn1_w2, v_mix_norm, v_gmlp_w_in, v_gmlp_ln_g, v_gmlp_ln_b, v_gmlp_w_s, v_gmlp_b_s, v_gmlp_w_out, v_conv_w_in, v_conv_w, v_conv_w_out, v_xattn_norm, v_mem_norm, v_xattn_wq, v_xattn_wkv, v_xattn_wo, v_ffn2_norm, v_ffn2_w13, v_ffn2_w2, v_final_norm):
    given = dict(locals())
    T, D = x.shape[1], x.shape[2]
    depth = ffn1_norm.shape[0]
    xs = x.reshape(T, D)
    mems = mem.reshape(mem.shape[1], D)
    target = loss_target.reshape(T, D)
    me = 4 * lax.axis_index("x") + 2 * lax.axis_index("y") + lax.axis_index("c")
    E = gmlp_ln_g.shape[1]
    gd = E // GROUPS
    cshard = conv_w.shape[2]

    bf = {k: given[k].astype(BF16) for k in
          ("ffn1_w13", "ffn1_w2", "gmlp_w_in", "gmlp_w_out", "conv_w_in", "conv_w_out",
           "xattn_wq", "xattn_wkv", "xattn_wo", "ffn2_w13", "ffn2_w2")}

    W = {}

    def gather(names_layers):
        return names_layers, gather_exchange([(bf[k], l) for k, l in names_layers])

    def landed(tagged):
        names_layers, exchange = tagged
        W.update(zip(names_layers, exchange.results))

    cw_place = lax.dynamic_update_slice(jnp.zeros((CONV_WIDTH, D), F32), conv_w[0], (jnp.int32(0), me * cshard))
    cw_full = small_all_reduce(_pack([cw_place])).reshape(-1)[:CONV_WIDTH * D].reshape(CONV_WIDTH, D)
    bias_full = jnp.repeat(gmlp_b_s[0].T, gd, axis=1)

    first = gather([("ffn1_w13", 0)])
    run_exchange(first[1], "gather_first")
    landed(first)
    saved = []
    h = xs
    for i in range(depth):
        j = i // 2
        is_gmlp = i % 2 == 0
        mix = ("gmlp_w_in", "gmlp_w_out") if is_gmlp else ("conv_w_in", "conv_w_out")
        on_up1 = gather(([("ffn1_w2", i)] if i == 0 else [])
                        + [(mix[0], j), (mix[1], j), ("xattn_wq", i), ("xattn_wkv", i), ("xattn_wo", i)])
        on_down1 = gather([("ffn2_w13", i)])

        def w2_after_up(on_up=on_up1, i=i):
            landed(on_up)
            return W["ffn1_w2", i]

        h, sv1 = ffn_fwd(h, ffn1_norm[i], W["ffn1_w13", i], w2_after_up, on_up1[1], on_down1[1])
        landed(on_down1)
        hn = rms_fwd(h, mix_norm[i])
        on_mix_in = gather([("ffn2_w2", i)])
        p = mm_nn(hn, W[mix[0], j], "mix_in", tn_pref=768, exchange=on_mix_in[1])
        landed(on_mix_in)
        if is_gmlp:
            gated = gmlp_mid_fwd(p, gmlp_ln_g[j:j + 1], gmlp_ln_b[j:j + 1], gmlp_w_s[j], bias_full)
        else:
            gated = conv_mid_fwd(p, cw_full)
        h_mix = mm_nn(gated, W[mix[1], j].reshape(1, gated.shape[1], D), "mix_out", F32, res=h)
        sv2 = (h, hn, p, gated)
        h, sv3 = xattn_fwd(h_mix, mems, xattn_norm[i], mem_norm[i],
                           W["xattn_wq", i], W["xattn_wkv", i], W["xattn_wo", i])
        on_up2 = gather([("ffn1_w13", i + 1), ("ffn1_w2", i + 1)]) if i + 1 < depth else None
        h, sv4 = ffn_fwd(h, ffn2_norm[i], W["ffn2_w13", i], lambda i=i: W["ffn2_w2", i],
                         None if on_up2 is None else on_up2[1])
        if on_up2 is not None:
            landed(on_up2)
        saved.append((sv1, sv2, sv3, sv4))

    loss_part, d, dys, d_final_norm = loss_head(h, final_norm, target, 0.5)

    small = {k: [None] * depth for k in ("ffn1_norm", "mix_norm", "xattn_norm", "mem_norm", "ffn2_norm")}
    partial = {}
    for i in reversed(range(depth)):
        j = i // 2
        is_gmlp = i % 2 == 0
        sv1, sv2, sv3, sv4 = saved[i]
        d, dys, small["ffn2_norm"][i], partial["ffn2_w13", i], partial["ffn2_w2", i] = ffn_bwd(
            d, dys, sv4, ffn2_norm[i], W["ffn2_w13", i], W["ffn2_w2", i], 1.0)
        d, dys, small["xattn_norm"][i], small["mem_norm"][i], dwq, dwkv, dwo = xattn_bwd(
            d, dys, sv3, mems, xattn_norm[i], W["xattn_wq", i], W["xattn_wkv", i], W["xattn_wo", i], 1.0)
        if is_gmlp:
            mix = ("gmlp_w_in", "gmlp_w_out")
            mid = lambda p, dg: (lambda r: (r[0], r[1:]))(gmlp_mid_bwd(
                p, dg, gmlp_ln_g[j:j + 1], gmlp_ln_b[j:j + 1], gmlp_w_s[j], bias_full))
        else:
            mix = ("conv_w_in", "conv_w_out")
            mid = lambda p, dg: (lambda r: (r[0], r[1:]))(conv_mid_bwd(p, dg, cw_full))
        send_attn = scatter_exchange([dwq, dwkv, dwo])
        d, dys, small["mix_norm"][i], dw_in, dw_out, extra = mixer_bwd_common(
            d, dys, sv2, mix_norm[i], W[mix[0], j], W[mix[1], j], mid, 0.5, dwin_exchange=send_attn)
        partial["xattn_wq", i], partial["xattn_wkv", i], partial["xattn_wo", i] = send_attn.results
        if is_gmlp:
            d_ws, d_bs_wide, d_lng, d_lnb = extra
        else:
            (d_cw,) = extra
        send_mix = scatter_exchange([dw_in, dw_out])
        d, dys, small["ffn1_norm"][i], partial["ffn1_w13", i], partial["ffn1_w2", i] = ffn_bwd(
            d, dys, sv1, ffn1_norm[i], W["ffn1_w13", i], W["ffn1_w2", i], 0.5, dact_exchange=send_mix)
        partial[mix[0], j], partial[mix[1], j] = send_mix.results
    grad_x = d.reshape(x.shape)

    small_grads = {k: jnp.concatenate(v, axis=0) for k, v in small.items()}
    small_grads["gmlp_ln_g"] = d_lng
    small_grads["gmlp_ln_b"] = d_lnb
    small_grads["gmlp_w_s"] = d_ws[None]
    small_grads["gmlp_b_s"] = d_bs_wide[None, :, :, 0]
    small_grads["final_norm"] = d_final_norm.reshape(-1)
    small_names = ["ffn1_norm", "mix_norm", "gmlp_ln_g", "gmlp_ln_b", "gmlp_w_s", "gmlp_b_s",
                   "xattn_norm", "mem_norm", "ffn2_norm", "final_norm"]
    summed = small_all_reduce(_pack([small_grads[k] for k in small_names] + [d_cw, loss_part]))
    parts = _unpack(summed, [given[k].shape for k in small_names] + [(CONV_WIDTH, D), (1, LANES)])
    grads = dict(zip(small_names, parts[:len(small_names)]))
    grads["conv_w"] = lax.dynamic_slice(parts[-2], (jnp.int32(0), me * cshard), (CONV_WIDTH, cshard))[None]
    loss = parts[-1][0, 0]
    flat_names = small_names + ["conv_w"]
    flat = adamw_flat(*[_pack([src[k] for k in flat_names]) for src in
                        (grads, given, {k: given["m_" + k] for k in flat_names},
                         {k: given["v_" + k] for k in flat_names})])
    delta, new_m, new_v = [dict(zip(flat_names, _unpack(f, [given[k].shape for k in flat_names]))) for f in flat]

    for k in bf:
        w = given[k]
        L = w.shape[0]
        shard = w.shape[1:]
        view = lambda a: a.reshape((L,) + shard)
        g, dl, m2, v2 = adamw_sharded([partial[k, l] for l in range(L)], w, given["m_" + k], given["v_" + k])
        grads[k], delta[k], new_m[k], new_v[k] = view(g), view(dl), view(m2), view(v2)

    order = ["ffn1_norm", "ffn1_w13", "ffn1_w2", "mix_norm", "gmlp_w_in", "gmlp_ln_g", "gmlp_ln_b", "gmlp_w_s",
             "gmlp_b_s", "gmlp_w_out", "conv_w_in", "conv_w", "conv_w_out", "xattn_norm", "mem_norm", "xattn_wq",
             "xattn_wkv", "xattn_wo", "ffn2_norm", "ffn2_w13", "ffn2_w2", "final_norm"]
    return (loss, grad_x, *[grads[k] for k in order], *[delta[k] for k in order],
            *[new_m[k] for k in order], *[new_v[k] for k in order])
```

```python
import functools
import math

import jax
import jax.numpy as jnp
from jax import lax
from jax.experimental import pallas as pl
from jax.experimental.pallas import tpu as pltpu

F32 = jnp.float32
BF16 = jnp.bfloat16

N_DEV = 8
N_PEERS = N_DEV - 1
CHUNK = 128
GROUPS = 8
HEADS = 4
CONV_WIDTH = 3
RMS_EPS = 1e-6
LN_EPS = 1e-5
ADAM_LR = 0.001
ADAM_B1 = 0.9
ADAM_B2 = 0.999
ADAM_EPS = 1e-08
ADAM_WD = 0.01
ADAM_STEP = 10
LANES = 128
BF16_SUBLANES = 16
MXU_WIDTH = 256
VMEM_LIMIT_BYTES = 56 * 1024 * 1024

_NT = (((1,), (1,)), ((), ()))
_TN = (((0,), (0,)), ((), ()))
_SQRT_HALF = 0.7071067811865476
_INV_SQRT_2PI = 0.3989422804014327


def _div(n, pref, align):
    best = None
    for t in range(align, min(n, pref) + 1, align):
        if n % t == 0:
            best = t
    return n if best is None else best


def _chunks(n, width):
    return [(c0, min(width, n - c0)) for c0 in range(0, n, width)]


_ANY = pl.BlockSpec(memory_space=pl.ANY)


class Exchange:
    def __init__(self, arrays, out_shapes, sems, start, finish, forward=None):
        self.arrays, self.out_shapes, self.sems = list(arrays), list(out_shapes), list(sems)
        self.start, self.finish, self.forward = start, finish, forward
        self.results = None


def _call(body, name, grid, ins, outs, scratch=(), exchange=None):
    in_specs = [pl.BlockSpec(*spec[1:3], **({"pipeline_mode": pl.Buffered(1)} if len(spec) > 3 else {}))
                for spec in ins]
    ins = [spec[:3] for spec in ins]
    out_specs = [pl.BlockSpec(bs, im) for _, _, bs, im in outs]
    out_shape = [jax.ShapeDtypeStruct(s, d) for s, d, _, _ in outs]
    arrays = [a for a, _, _ in ins]
    scratch = list(scratch)
    kernel_fn = body
    if exchange is not None:
        n_in, n_out, n_scr = len(ins), len(outs), len(scratch)
        n_xin, n_xout = len(exchange.arrays), len(exchange.out_shapes)
        steps = math.prod(grid)
        forward_step = min(steps - 1, (3 * steps) // 4)

        def kernel_fn(*refs):
            refs = list(refs)
            own_in, x_in = refs[:n_in], refs[n_in:n_in + n_xin]
            refs = refs[n_in + n_xin:]
            own_out, x_out = refs[:n_out], refs[n_out:n_out + n_xout]
            refs = refs[n_out + n_xout:]
            own_scr, x_sems = refs[:n_scr], refs[n_scr:]
            step = 0
            for axis, size in enumerate(grid):
                step = step * size + pl.program_id(axis)

            @pl.when(step == 0)
            def _():
                exchange.start(x_in, x_out, x_sems)

            if exchange.forward is not None:
                @pl.when(step == forward_step)
                def _():
                    exchange.forward(x_in, x_out, x_sems)

            body(*own_in, *own_out, *own_scr)

            @pl.when(step == steps - 1)
            def _():
                exchange.finish(x_in, x_out, x_sems)

        in_specs += [_ANY] * n_xin
        out_specs += [_ANY] * n_xout
        out_shape += exchange.out_shapes
        arrays += exchange.arrays
        scratch += exchange.sems
    res = pl.pallas_call(
        kernel_fn,
        name=name,
        grid=grid,
        in_specs=in_specs,
        out_specs=out_specs,
        out_shape=out_shape,
        scratch_shapes=scratch,
        compiler_params=pltpu.CompilerParams(
            dimension_semantics=("arbitrary",) * len(grid), vmem_limit_bytes=VMEM_LIMIT_BYTES),
    )(*arrays)
    if exchange is not None:
        exchange.results = list(res[len(outs):])
        res = res[:len(outs)]
    return res


def _dot(a, b, dims=None):
    if dims is None:
        return jnp.dot(a, b, preferred_element_type=F32)
    return lax.dot_general(a, b, dims, preferred_element_type=F32)


def _sigmoid(v):
    return 1.0 / (1.0 + jnp.exp(-v))


def _gelu(v):
    return 0.5 * v * (1.0 + lax.erf(v * _SQRT_HALF))


def _gelu_grad(v):
    return 0.5 * (1.0 + lax.erf(v * _SQRT_HALF)) + v * (_INV_SQRT_2PI * jnp.exp(-0.5 * v * v))


def _accumulate(ref, part, first):
    @pl.when(first)
    def _():
        ref[...] = part

    @pl.when(jnp.logical_not(first))
    def _():
        ref[...] += part


def rms_fwd(x, g):
    T, D = x.shape
    tt = _div(T, 512, BF16_SUBLANES)

    def body(x_ref, g_ref, o_ref):
        xv = x_ref[...]
        r = lax.rsqrt(jnp.mean(xv * xv, axis=-1, keepdims=True) + RMS_EPS)
        o_ref[...] = ((xv * r) * g_ref[...]).astype(BF16)

    return _call(body, "rms_fwd", (T // tt,),
                 [(x, (tt, D), lambda i: (i, 0)), (g.reshape(1, D), (1, D), lambda i: (0, 0))],
                 [((T, D), BF16, (tt, D), lambda i: (i, 0))])[0]


def rms_gain_grad(x, dxn):
    T, D = x.shape
    tt = _div(T, 256, 8)

    def body(x_ref, dn_ref, dg_ref):
        xv = x_ref[...]
        r = lax.rsqrt(jnp.mean(xv * xv, axis=-1, keepdims=True) + RMS_EPS)
        _accumulate(dg_ref, jnp.sum(dn_ref[...] * (xv * r), axis=0, keepdims=True), pl.program_id(0) == 0)

    row = lambda i: (i, 0)
    return _call(body, "rms_gain_grad", (T // tt,), [(x, (tt, D), row), (dxn, (tt, D), row)],
                 [((1, D), F32, (1, D), lambda i: (0, 0))])[0]


def loss_head(x, g, target, scale):
    T, D = x.shape
    tt = _div(T, 256, BF16_SUBLANES)

    def body(x_ref, g_ref, t_ref, loss_ref, dx_ref, dxs_ref, dg_ref):
        first = pl.program_id(0) == 0
        xv = x_ref[...]
        gv = g_ref[...]
        r = lax.rsqrt(jnp.mean(xv * xv, axis=-1, keepdims=True) + RMS_EPS)
        xh = xv * r
        err = xh * gv - t_ref[...]
        part = 0.5 * jnp.sum(jnp.mean(err * err, axis=-1, keepdims=True), axis=0, keepdims=True)
        _accumulate(loss_ref, jnp.broadcast_to(part, (1, LANES)), first)
        dy = err * (1.0 / D)
        dxh = dy * gv
        dx = r * (dxh - xh * jnp.mean(dxh * xh, axis=-1, keepdims=True))
        dx_ref[...] = dx
        dxs_ref[...] = (scale * dx).astype(BF16)
        _accumulate(dg_ref, jnp.sum(dy * xh, axis=0, keepdims=True), first)

    row = lambda i: (i, 0)
    fix = lambda i: (0, 0)
    return _call(body, "loss_head", (T // tt,),
                 [(x, (tt, D), row), (g.reshape(1, D), (1, D), fix), (target, (tt, D), row)],
                 [((1, LANES), F32, (1, LANES), fix), ((T, D), F32, (tt, D), row),
                  ((T, D), BF16, (tt, D), row), ((1, D), F32, (1, D), fix)])


def mm_nn(a, w3, name, out_dtype=BF16, res=None, res_scale=1.0, tm_pref=1024, tn_pref=1024, exchange=None):
    M, K = a.shape
    nb, _, ns = w3.shape
    tn = _div(ns, tn_pref, LANES)
    per = ns // tn
    tm = _div(M, tm_pref, BF16_SUBLANES)
    ins = [(a, (tm, K), lambda j, m: (m, 0)), (w3, (None, K, tn), lambda j, m: (j // per, 0, j % per))]
    if res is not None:
        ins.append((res, (tm, tn), lambda j, m: (m, j)))

    def body(*refs):
        a_ref, w_ref = refs[0], refs[1]
        o_ref = refs[-1]
        acc = _dot(a_ref[...], w_ref[...])
        if res is not None:
            acc = refs[2][...] + res_scale * acc
        o_ref[...] = acc.astype(o_ref.dtype)

    return _call(body, name, (nb * per, M // tm), ins,
                 [((M, nb * ns), out_dtype, (tm, tn), lambda j, m: (m, j))], exchange=exchange)[0]


def mm_nt(a_in, w3, name, out_dtype, M, tm_pref=1024, to_pref=2048, exchange=None):
    nb, Ko, ns = w3.shape
    to = _div(Ko, to_pref, LANES)
    tm = _div(M, tm_pref, BF16_SUBLANES)
    if isinstance(a_in, tuple):
        a, a_bs, a_im = a_in
        a_bs = tuple(tm if s == "tm" else s for s in a_bs)
    else:
        a, a_bs, a_im = a_in, (tm, ns), lambda m, o, b: (m, b)
    if nb > 1:
        assert out_dtype == F32

    def body(a_ref, w_ref, o_ref):
        p = _dot(a_ref[...], w_ref[...], _NT)
        if nb == 1:
            o_ref[...] = p.astype(o_ref.dtype)
        else:
            _accumulate(o_ref, p, pl.program_id(2) == 0)

    return _call(body, name, (M // tm, Ko // to, nb),
                 [(a, a_bs, a_im), (w3, (None, to, ns), lambda m, o, b: (b, o, 0))],
                 [((M, Ko), out_dtype, (tm, to), lambda m, o, b: (m, o))], exchange=exchange)[0]


def mm_nt_rms_bwd(a_in, w3, name, x, g, d, scale, exchange=None):
    nb, D, ns = w3.shape
    M = x.shape[0]
    tm = _div(M, 512, BF16_SUBLANES)
    if isinstance(a_in, tuple):
        a, a_bs, a_im = a_in
        a_bs = tuple(tm if s == "tm" else s for s in a_bs)
    else:
        a, a_bs, a_im = a_in, (tm, ns), lambda m, b: (m, b)

    rc = _div(tm, 64, BF16_SUBLANES)

    def body(a_ref, w_ref, x_ref, g_ref, d_ref, dx_ref, dxs_ref, dg_ref, acc_ref):
        m, b = pl.program_id(0), pl.program_id(1)
        _accumulate(acc_ref, _dot(a_ref[...], w_ref[...], _NT), b == 0)

        @pl.when(b == nb - 1)
        def _():
            gv = g_ref[...]

            def piece(c, dg):
                rows = pl.ds(pl.multiple_of(c * rc, rc), rc)
                dn = acc_ref[rows, :]
                xv = x_ref[rows, :]
                r = lax.rsqrt(jnp.mean(xv * xv, axis=-1, keepdims=True) + RMS_EPS)
                xh = xv * r
                dxh = dn * gv
                dx = r * (dxh - xh * jnp.mean(dxh * xh, axis=-1, keepdims=True)) + d_ref[rows, :]
                dx_ref[rows, :] = dx
                dxs_ref[rows, :] = (scale * dx).astype(BF16)
                return dg + jnp.sum(dn * xh, axis=0, keepdims=True)

            _accumulate(dg_ref, lax.fori_loop(0, tm // rc, piece, jnp.zeros((1, D), F32)), m == 0)

    row = lambda m, b: (m, 0)
    fix = lambda m, b: (0, 0)
    w_spec = (w3, (None, D, ns), lambda m, b: (b, 0, 0)) + (("single",) if nb == 1 else ())
    return _call(body, name, (M // tm, nb),
                 [(a, a_bs, a_im), w_spec,
                  (x, (tm, D), row), (g.reshape(1, D), (1, D), fix), (d, (tm, D), row)],
                 [((M, D), F32, (tm, D), row), ((M, D), BF16, (tm, D), row), ((1, D), F32, (1, D), fix)],
                 scratch=[pltpu.VMEM((tm, D), F32)], exchange=exchange)


def mm_tn(a, b_in, name, nbo, ns, tka_pref=1024, tt_pref=2048, tn_pref=2048, exchange=None):
    T, Ka = a.shape
    tt = _div(T, tt_pref, BF16_SUBLANES)
    tka = _div(Ka, tka_pref, LANES)
    if isinstance(b_in, tuple):
        b, b_bs, b_im = b_in
        b_bs = tuple(tt if s == "tt" else s for s in b_bs)
        tn, per = ns, 1
    else:
        tn = _div(ns, tn_pref, LANES)
        per = ns // tn
        b, b_bs, b_im = b_in, (tt, tn), lambda i, j, t: (t, j)
    nt = T // tt

    def body(a_ref, b_ref, o_ref, acc_ref):
        t = pl.program_id(2)
        _accumulate(acc_ref, _dot(a_ref[...], b_ref[...], _TN), t == 0)

        @pl.when(t == nt - 1)
        def _():
            o_ref[...] = acc_ref[...].astype(BF16)

    return _call(body, name, (Ka // tka, nbo * per, nt),
                 [(a, (tt, tka), lambda i, j, t: (t, i)), (b, b_bs, b_im)],
                 [((nbo, Ka, ns), BF16, (None, tka, tn), lambda i, j, t: (j // per, i, j % per))],
                 scratch=[pltpu.VMEM((tka, tn), F32)], exchange=exchange)[0]


def ffn_up(xn, w13, exchange=None):
    T, D = xn.shape
    nb, _, ns = w13.shape
    half = nb // 2
    F = half * ns
    tm = _div(T, 512, BF16_SUBLANES)
    pair = 2 if half % 2 == 0 else 1

    def columns(w_ref, c0, cw):
        k, off = divmod(c0, ns)
        if off + cw <= ns:
            return w_ref[k, :, off:off + cw]
        return jnp.concatenate([w_ref[k, :, off:ns], w_ref[k + 1, :, 0:off + cw - ns]], axis=1)

    def body(x_ref, wg_ref, wu_ref, fac_ref, act_ref):
        xv = x_ref[...]
        for c0, cw in _chunks(pair * ns, MXU_WIDTH):
            cols = slice(c0, c0 + cw)
            gate = _dot(xv, columns(wg_ref, c0, cw))
            up = _dot(xv, columns(wu_ref, c0, cw))
            s = _sigmoid(gate)
            silu = gate * s
            fac_ref[0, :, cols] = (up * (s * (1.0 + gate * (1.0 - s)))).astype(BF16)
            fac_ref[1, :, cols] = silu.astype(BF16)
            act_ref[:, cols] = (silu * up).astype(BF16)

    tn = pair * ns
    return _call(body, "ffn_up", (half // pair, T // tm),
                 [(xn, (tm, D), lambda j, m: (m, 0)),
                  (w13, (pair, D, ns), lambda j, m: (j, 0, 0), "single"),
                  (w13, (pair, D, ns), lambda j, m: (j + half // pair, 0, 0), "single")],
                 [((2, T, F), BF16, (2, tm, tn), lambda j, m: (0, m, j)),
                  ((T, F), BF16, (tm, tn), lambda j, m: (m, j))], exchange=exchange)


def ffn_dact(dy, w2, fac, exchange=None):
    T, D = dy.shape
    F = w2.shape[0]
    tm = _div(T, 512, BF16_SUBLANES)
    tn = _div(F, F // 2, MXU_WIDTH)

    def body(dy_ref, w_ref, fac_ref, dh_ref):
        dyv = dy_ref[...]
        for c0, cw in _chunks(tn, MXU_WIDTH):
            cols = slice(c0, c0 + cw)
            da = _dot(dyv, w_ref[cols, :], _NT)
            dh_ref[0, :, cols] = (da * fac_ref[0, :, cols].astype(F32)).astype(BF16)
            dh_ref[1, :, cols] = (da * fac_ref[1, :, cols].astype(F32)).astype(BF16)

    return _call(body, "ffn_dact", (F // tn, T // tm),
                 [(dy, (tm, D), lambda j, m: (m, 0)), (w2, (tn, D), lambda j, m: (j, 0), "single"),
                  (fac, (2, tm, tn), lambda j, m: (0, m, j))],
                 [((2, T, F), BF16, (2, tm, tn), lambda j, m: (0, m, j))], exchange=exchange)[0]


def ffn_fwd(x, norm_g, w13, get_w2, up_exchange=None, down_exchange=None):
    xn = rms_fwd(x, norm_g)
    fac, act = ffn_up(xn, w13, exchange=up_exchange)
    F = act.shape[1]
    y = mm_nn(act, get_w2().reshape(1, F, -1), "ffn_down", F32, res=x, res_scale=0.5, tm_pref=512,
              exchange=down_exchange)
    return y, (x, xn, fac, act)


def ffn_bwd(d, dys, saved, norm_g, w13, w2, scale_out, dact_exchange=None):
    x, xn, fac, act = saved
    T, D = x.shape
    nb, _, ns = w13.shape
    half = nb // 2
    F = half * ns
    dh = ffn_dact(dys, w2.reshape(F, D), fac, exchange=dact_exchange)
    dw2 = mm_tn(act, dys, "ffn_dw2", 1, D, tka_pref=ns, tn_pref=1024)
    send_w2 = scatter_exchange([dw2.reshape(N_DEV, F // N_DEV, D)])
    dw13 = mm_tn(xn, (dh, (None, "tt", ns), lambda i, j, t: (j // half, t, j % half)), "ffn_dw13", nb, ns,
                 exchange=send_w2)
    send_w13 = scatter_exchange([dw13])
    dx, dxs, dg = mm_nt_rms_bwd((dh, (None, "tm", ns), lambda m, b: (b // half, m, b % half)), w13, "ffn_dxn",
                                x, norm_g, d, scale_out, exchange=send_w13)
    return dx, dxs, dg, send_w13.results[0], send_w2.results[0]


def _gmlp_parts(p_ref, lng_ref, lnb_ref):
    E = lng_ref.shape[-1]
    z = _gelu(p_ref[...].astype(F32))
    u = z[:, :E]
    vp = z[:, E:]
    mu = jnp.mean(vp, axis=-1, keepdims=True)
    xc = vp - mu
    rstd = lax.rsqrt(jnp.mean(xc * xc, axis=-1, keepdims=True) + LN_EPS)
    vh = xc * rstd
    v = vh * lng_ref[...] + lnb_ref[...]
    return u, vh, rstd, v


def _causal_ws(ws_ref, g):
    keep = lax.broadcasted_iota(jnp.int32, (CHUNK, CHUNK), 0) >= lax.broadcasted_iota(jnp.int32, (CHUNK, CHUNK), 1)
    return jnp.where(keep, ws_ref[g], 0.0).astype(BF16), keep


def gmlp_mid_fwd(p, ln_g, ln_b, w_s, bias_full):
    T, E2 = p.shape
    E = E2 // 2
    gd = E // GROUPS
    tm = _div(T, 256, CHUNK)
    fix2 = lambda i: (0, 0)

    def body(p_ref, lng_ref, lnb_ref, ws_ref, bias_ref, o_ref):
        u, _, _, v = _gmlp_parts(p_ref, lng_ref, lnb_ref)
        vb = v.astype(BF16)
        for g in range(GROUPS):
            wm, _ = _causal_ws(ws_ref, g)
            cols = slice(g * gd, (g + 1) * gd)
            for c in range(tm // CHUNK):
                rows = slice(c * CHUNK, (c + 1) * CHUNK)
                f = _dot(wm, vb[rows, cols]) + bias_ref[:, cols]
                o_ref[rows, cols] = (u[rows, cols] * f).astype(BF16)

    return _call(body, "gmlp_mid_fwd", (T // tm,),
                 [(p, (tm, E2), lambda i: (i, 0)), (ln_g, (1, E), fix2), (ln_b, (1, E), fix2),
                  (w_s, (GROUPS, CHUNK, CHUNK), lambda i: (0, 0, 0)), (bias_full, (CHUNK, E), fix2)],
                 [((T, E), BF16, (tm, E), lambda i: (i, 0))])[0]


def gmlp_mid_bwd(p, dgated, ln_g, ln_b, w_s, bias_full):
    T, E2 = p.shape
    E = E2 // 2
    gd = E // GROUPS
    tm = _div(T, 256, CHUNK)
    nsteps = T // tm
    fix2 = lambda i: (0, 0)
    fix3 = lambda i: (0, 0, 0)

    def body(p_ref, dg_ref, lng_ref, lnb_ref, ws_ref, bias_ref,
             dp_ref, dws_ref, dbs_ref, dlng_ref, dlnb_ref, f_sc, dv_sc, db_sc):
        i = pl.program_id(0)
        first = i == 0
        u, vh, rstd, v = _gmlp_parts(p_ref, lng_ref, lnb_ref)
        vb = v.astype(BF16)
        dgt = dg_ref[...].astype(F32)
        df = dgt * u
        dfb = df.astype(BF16)
        for g in range(GROUPS):
            wm, keep = _causal_ws(ws_ref, g)
            cols = slice(g * gd, (g + 1) * gd)
            dw = None
            dbg = None
            for c in range(tm // CHUNK):
                rows = slice(c * CHUNK, (c + 1) * CHUNK)
                f_sc[rows, cols] = _dot(wm, vb[rows, cols]) + bias_ref[:, cols]
                dv_sc[rows, cols] = _dot(wm, dfb[rows, cols], _TN)
                part = _dot(dfb[rows, cols], vb[rows, cols], _NT)
                dw = part if dw is None else dw + part
                dbg = df[rows, cols] if dbg is None else dbg + df[rows, cols]
            dw = jnp.where(keep, dw, 0.0)

            @pl.when(first)
            def _():
                dws_ref[g] = dw
                db_sc[:, cols] = dbg

            @pl.when(jnp.logical_not(first))
            def _():
                dws_ref[g] += dw
                db_sc[:, cols] += dbg

        du = dgt * f_sc[...]
        dv = dv_sc[...]
        _accumulate(dlng_ref, jnp.sum(dv * vh, axis=0, keepdims=True), first)
        _accumulate(dlnb_ref, jnp.sum(dv, axis=0, keepdims=True), first)
        dvh = dv * lng_ref[...]
        dvp = rstd * (dvh - jnp.mean(dvh, axis=-1, keepdims=True)
                      - vh * jnp.mean(dvh * vh, axis=-1, keepdims=True))
        gp = _gelu_grad(p_ref[...].astype(F32))
        dp_ref[:, :E] = (du * gp[:, :E]).astype(BF16)
        dp_ref[:, E:] = (dvp * gp[:, E:]).astype(BF16)

        @pl.when(i == nsteps - 1)
        def _():
            for g in range(GROUPS):
                tot = jnp.sum(db_sc[:, g * gd:(g + 1) * gd], axis=-1, keepdims=True)
                dbs_ref[g] = jnp.broadcast_to(tot, (CHUNK, LANES))

    return _call(body, "gmlp_mid_bwd", (nsteps,),
                 [(p, (tm, E2), lambda i: (i, 0)), (dgated, (tm, E), lambda i: (i, 0)),
                  (ln_g, (1, E), fix2), (ln_b, (1, E), fix2),
                  (w_s, (GROUPS, CHUNK, CHUNK), fix3), (bias_full, (CHUNK, E), fix2)],
                 [((T, E2), BF16, (tm, E2), lambda i: (i, 0)),
                  ((GROUPS, CHUNK, CHUNK), F32, (GROUPS, CHUNK, CHUNK), fix3),
                  ((GROUPS, CHUNK, LANES), F32, (GROUPS, CHUNK, LANES), fix3),
                  ((1, E), F32, (1, E), fix2), ((1, E), F32, (1, E), fix2)],
                 scratch=[pltpu.VMEM((tm, E), F32), pltpu.VMEM((tm, E), F32), pltpu.VMEM((CHUNK, E), F32)])


HALO = 16


def _row_of(block, r):
    rows = lax.broadcasted_iota(jnp.int32, block.shape, 0)
    return jnp.sum(jnp.where(rows == r, block, 0.0), axis=0, keepdims=True)


def _shift_down(z, k, fill):
    out = pltpu.roll(z, k, 0)
    rows = lax.broadcasted_iota(jnp.int32, z.shape, 0)
    for t in range(k):
        out = jnp.where(rows == t, fill[t], out)
    return out


def _shift_up(z, k, fill):
    n = z.shape[0]
    out = pltpu.roll(z, n - k, 0)
    rows = lax.broadcasted_iota(jnp.int32, z.shape, 0)
    for j in range(k):
        out = jnp.where(rows == n - k + j, fill[j], out)
    return out


def _conv_parts(p_ref, prev_ref, cw_ref, is_first):
    D = cw_ref.shape[-1]
    pv = p_ref[...].astype(F32)
    bg, cg, val = pv[:, :D], pv[:, D:2 * D], pv[:, 2 * D:]
    z = cg * val
    pp = prev_ref[...].astype(F32)
    zp = jnp.where(is_first, 0.0, pp[:, D:2 * D] * pp[:, 2 * D:])
    zl1 = _row_of(zp, HALO - 1)
    zl2 = _row_of(zp, HALO - 2)
    z1 = _shift_down(z, 1, [zl1])
    z2 = _shift_down(z, 2, [zl2, zl1])
    conv = z2 * cw_ref[0:1, :] + z1 * cw_ref[1:2, :] + z * cw_ref[2:3, :]
    return bg, cg, val, z, z1, z2, conv


def conv_mid_fwd(p, cw):
    T, D3 = p.shape
    D = D3 // 3
    tm = _div(T, 256, HALO)
    per = tm // HALO

    def body(p_ref, prev_ref, cw_ref, o_ref):
        bg, _, _, _, _, _, conv = _conv_parts(p_ref, prev_ref, cw_ref, pl.program_id(0) == 0)
        o_ref[...] = (bg * conv).astype(BF16)

    return _call(body, "conv_mid_fwd", (T // tm,),
                 [(p, (tm, D3), lambda i: (i, 0)),
                  (p, (HALO, D3), lambda i: (jnp.maximum(i * per - 1, 0), 0)),
                  (cw, (CONV_WIDTH, D), lambda i: (0, 0))],
                 [((T, D), BF16, (tm, D), lambda i: (i, 0))])[0]


def conv_mid_bwd(p, dgated, cw):
    T, D3 = p.shape
    D = D3 // 3
    tm = _div(T, 256, HALO)
    per = tm // HALO
    nsteps = T // tm
    last_halo = T // HALO - 1
    nxt = lambda i: (jnp.minimum((i + 1) * per, last_halo), 0)

    def body(p_ref, prev_ref, next_ref, dg_ref, dgn_ref, cw_ref, dp_ref, dcw_ref):
        i = pl.program_id(0)
        bg, cg, val, z, z1, z2, conv = _conv_parts(p_ref, prev_ref, cw_ref, i == 0)
        dgt = dg_ref[...].astype(F32)
        dconv = dgt * bg
        dcn = jnp.where(i == nsteps - 1, 0.0, dgn_ref[...].astype(F32) * next_ref[:, :D].astype(F32))
        n0 = _row_of(dcn, 0)
        n1 = _row_of(dcn, 1)
        up1 = _shift_up(dconv, 1, [n0])
        up2 = _shift_up(dconv, 2, [n0, n1])
        dz = dconv * cw_ref[2:3, :] + up1 * cw_ref[1:2, :] + up2 * cw_ref[0:1, :]
        dp_ref[:, :D] = (dgt * conv).astype(BF16)
        dp_ref[:, D:2 * D] = (dz * val).astype(BF16)
        dp_ref[:, 2 * D:] = (dz * cg).astype(BF16)
        first = i == 0
        parts = (jnp.sum(dconv * z2, axis=0, keepdims=True), jnp.sum(dconv * z1, axis=0, keepdims=True),
                 jnp.sum(dconv * z, axis=0, keepdims=True))

        @pl.when(first)
        def _():
            for k in range(CONV_WIDTH):
                dcw_ref[k:k + 1, :] = parts[k]

        @pl.when(jnp.logical_not(first))
        def _():
            for k in range(CONV_WIDTH):
                dcw_ref[k:k + 1, :] += parts[k]

    return _call(body, "conv_mid_bwd", (nsteps,),
                 [(p, (tm, D3), lambda i: (i, 0)),
                  (p, (HALO, D3), lambda i: (jnp.maximum(i * per - 1, 0), 0)),
                  (p, (HALO, D3), nxt),
                  (dgated, (tm, D), lambda i: (i, 0)),
                  (dgated, (HALO, D), nxt),
                  (cw, (CONV_WIDTH, D), lambda i: (0, 0))],
                 [((T, D3), BF16, (tm, D3), lambda i: (i, 0)),
                  ((CONV_WIDTH, D), F32, (CONV_WIDTH, D), lambda i: (0, 0))])


def mixer_bwd_common(d, dys, saved, norm_g, w_in, w_out, mid_bwd, scale_out, dwin_exchange=None):
    x, hn, p, gated = saved
    T, D = x.shape
    E = gated.shape[1]
    w_out3 = w_out.reshape(1, E, D)
    dgated = mm_nt(dys, w_out3, "mix_dgated", BF16, T)
    dw_out = mm_tn(gated, dys, "mix_dwout", 1, D)
    dp, extra = mid_bwd(p, dgated)
    nb, _, ns = w_in.shape
    dw_in = mm_tn(hn, dp, "mix_dwin", nb, ns, exchange=dwin_exchange)
    dx, dxs, dg = mm_nt_rms_bwd(dp, w_in, "mix_dhn", x, norm_g, d, scale_out)
    return dx, dxs, dg, dw_in, dw_out.reshape(N_DEV, E // N_DEV, D), extra


def _softmax_rows(s):
    e = jnp.exp(s - jnp.max(s, axis=-1, keepdims=True))
    return e / jnp.sum(e, axis=-1, keepdims=True)


def attn_fwd(q, kv):
    T, D = q.shape
    M = kv.shape[0]
    hd = D // HEADS
    scale = hd ** -0.5
    tm = _div(T, 512, BF16_SUBLANES)

    def body(q_ref, kv_ref, o_ref):
        for h in range(HEADS):
            cols = slice(h * hd, (h + 1) * hd)
            s = _dot(q_ref[:, cols], kv_ref[:, cols], _NT) * scale
            pr = _softmax_rows(s).astype(BF16)
            o_ref[:, cols] = _dot(pr, kv_ref[:, D + h * hd:D + (h + 1) * hd]).astype(BF16)

    return _call(body, "attn_fwd", (T // tm,),
                 [(q, (tm, D), lambda i: (i, 0)), (kv, (M, 2 * D), lambda i: (0, 0))],
                 [((T, D), BF16, (tm, D), lambda i: (i, 0))])[0]


def attn_bwd(q, do, kv, exchange=None):
    T, D = q.shape
    M = kv.shape[0]
    hd = D // HEADS
    scale = hd ** -0.5
    tm = _div(T, 512, BF16_SUBLANES)
    nsteps = T // tm

    def body(q_ref, do_ref, kv_ref, dq_ref, dkv_ref, acc_ref):
        i = pl.program_id(0)
        for h in range(HEADS):
            cols = slice(h * hd, (h + 1) * hd)
            vcols = slice(D + h * hd, D + (h + 1) * hd)
            qh = q_ref[:, cols]
            kh = kv_ref[:, cols]
            doh = do_ref[:, cols]
            pr = _softmax_rows(_dot(qh, kh, _NT) * scale)
            dpr = _dot(doh, kv_ref[:, vcols], _NT)
            ds = (pr * (dpr - jnp.sum(dpr * pr, axis=-1, keepdims=True)) * scale).astype(BF16)
            dq_ref[:, cols] = _dot(ds, kh).astype(BF16)
            dk = _dot(ds, qh, _TN)
            dv = _dot(pr.astype(BF16), doh, _TN)

            @pl.when(i == 0)
            def _():
                acc_ref[:, cols] = dk
                acc_ref[:, vcols] = dv

            @pl.when(i > 0)
            def _():
                acc_ref[:, cols] += dk
                acc_ref[:, vcols] += dv

        @pl.when(i == nsteps - 1)
        def _():
            dkv_ref[...] = acc_ref[...].astype(BF16)

    return _call(body, "attn_bwd", (nsteps,),
                 [(q, (tm, D), lambda i: (i, 0)), (do, (tm, D), lambda i: (i, 0)),
                  (kv, (M, 2 * D), lambda i: (0, 0))],
                 [((T, D), BF16, (tm, D), lambda i: (i, 0)), ((M, 2 * D), BF16, (M, 2 * D), lambda i: (0, 0))],
                 scratch=[pltpu.VMEM((M, 2 * D), F32)], exchange=exchange)


def xattn_fwd(x, mem, xnorm_g, mnorm_g, wq, wkv, wo):
    D = x.shape[1]
    hq = rms_fwd(x, xnorm_g)
    mn = rms_fwd(mem, mnorm_g)
    q = mm_nn(hq, wq.reshape(1, D, D), "xattn_q")
    kv = mm_nn(mn, wkv, "xattn_kv")
    o = attn_fwd(q, kv)
    y = mm_nn(o, wo.reshape(1, D, D), "xattn_out", F32, res=x)
    return y, (x, hq, mn, q, kv, o)


def xattn_bwd(d, dys, saved, mem, xnorm_g, wq, wkv, wo, scale_out):
    x, hq, mn, q, kv, o = saved
    T, D = x.shape
    M = mem.shape[0]
    do = mm_nt(dys, wo.reshape(1, D, D), "xattn_do", BF16, T)
    rows = D // N_DEV
    dwo = mm_tn(o, dys, "xattn_dwo", 1, D)
    send_wo = scatter_exchange([dwo.reshape(N_DEV, rows, D)])
    dq, dkv = attn_bwd(q, do, kv, exchange=send_wo)
    dwq = mm_tn(hq, dq, "xattn_dwq", 1, D)
    dx, dxs, dgx = mm_nt_rms_bwd(dq, wq.reshape(1, D, D), "xattn_dhq", x, xnorm_g, d, scale_out)
    nb, _, ns = wkv.shape
    dwkv = mm_tn(mn, dkv, "xattn_dwkv", nb, ns)
    dmn = mm_nt(dkv, wkv, "xattn_dmn", F32, M)
    dgm = rms_gain_grad(mem, dmn)
    return dx, dxs, dgx, dgm, dwq.reshape(N_DEV, rows, D), dwkv, send_wo.results[0]


def _mesh_places():
    x, y, c = lax.axis_index("x"), lax.axis_index("y"), lax.axis_index("c")
    chips = [(1 - x, y), (x, 1 - y), (1 - x, 1 - y)]
    return (x, y, c), (x, y, 1 - c), chips


def _slot(place):
    return 4 * place[0] + 2 * place[1] + place[2]


def _exchange_sems(n):
    return [pltpu.SemaphoreType.DMA((n * N_PEERS,)), pltpu.SemaphoreType.DMA((n * N_PEERS,)),
            pltpu.SemaphoreType.DMA((n,))]


def gather_exchange(shards):
    n = len(shards)
    shapes = [a.shape if l is None else a.shape[1:] for a, l in shards]

    def parts(x_in, x_out, sems):
        ins = [r if l is None else r.at[l] for r, (_, l) in zip(x_in, shards)]
        send_sems, recv_sems, local_sems = sems
        me, sibling, chips = _mesh_places()

        def copy(a, k, block, to, src=None):
            dst = x_out[a].at[_slot(block)]
            return pltpu.make_async_remote_copy(
                src_ref=dst if src is None else src, dst_ref=dst,
                send_sem=send_sems.at[a * N_PEERS + k], recv_sem=recv_sems.at[a * N_PEERS + k],
                device_id=to, device_id_type=pl.DeviceIdType.MESH)

        mine = [pltpu.make_async_copy(ins[a], x_out[a].at[_slot(me)], local_sems.at[a]) for a in range(n)]
        first = []
        for a in range(n):
            first.append(copy(a, 0, me, sibling, src=ins[a]))
            first += [copy(a, 1 + j, me, (*chip, me[2]), src=ins[a]) for j, chip in enumerate(chips)]
        return me, sibling, chips, copy, mine, first

    def start(x_in, x_out, sems):
        _, _, _, _, mine, first = parts(x_in, x_out, sems)
        for cp in mine + first:
            cp.start()

    def forward(x_in, x_out, sems):
        me, sibling, chips, copy, _, _ = parts(x_in, x_out, sems)
        for j, chip in enumerate(chips):
            for a in range(n):
                copy(a, 1 + j, (*chip, me[2]), me).wait_recv()
                copy(a, 4 + j, (*chip, me[2]), sibling).start()

    def finish(x_in, x_out, sems):
        me, sibling, chips, copy, mine, first = parts(x_in, x_out, sems)
        for a in range(n):
            copy(a, 0, sibling, me).wait_recv()
        for j, chip in enumerate(chips):
            for a in range(n):
                copy(a, 4 + j, (*chip, 1 - me[2]), me).wait_recv()
        for cp in first:
            cp.wait_send()
        for j, chip in enumerate(chips):
            for a in range(n):
                copy(a, 4 + j, (*chip, me[2]), sibling).wait_send()
        for cp in mine:
            cp.wait()

    return Exchange([a for a, _ in shards],
                    [jax.ShapeDtypeStruct((N_DEV,) + tuple(s), BF16) for s in shapes],
                    _exchange_sems(n), start, finish, forward)


def _all_peers(me, chips):
    c = me[2]
    return [(me[0], me[1], 1 - c)] + [(*chip, c) for chip in chips] + [(*chip, 1 - c) for chip in chips]


def scatter_exchange(grads):
    n = len(grads)

    def parts(x_in, x_out, sems):
        send_sems, recv_sems, local_sems = sems
        me, _, chips = _mesh_places()
        mine = [pltpu.make_async_copy(x_in[a].at[_slot(me)], x_out[a].at[_slot(me)], local_sems.at[a])
                for a in range(n)]
        sends, recvs = [], []
        for a in range(n):
            for k, peer in enumerate(_all_peers(me, chips)):
                sem = dict(send_sem=send_sems.at[a * N_PEERS + k], recv_sem=recv_sems.at[a * N_PEERS + k],
                           device_id=peer, device_id_type=pl.DeviceIdType.MESH)
                sends.append(pltpu.make_async_remote_copy(
                    src_ref=x_in[a].at[_slot(peer)], dst_ref=x_out[a].at[_slot(me)], **sem))
                recvs.append(pltpu.make_async_remote_copy(
                    src_ref=x_in[a].at[_slot(peer)], dst_ref=x_out[a].at[_slot(peer)], **sem))
        return mine, sends, recvs

    def start(x_in, x_out, sems):
        mine, sends, _ = parts(x_in, x_out, sems)
        for cp in mine + sends:
            cp.start()

    def finish(x_in, x_out, sems):
        mine, sends, recvs = parts(x_in, x_out, sems)
        for cp in recvs:
            cp.wait_recv()
        for cp in sends:
            cp.wait_send()
        for cp in mine:
            cp.wait()

    return Exchange(grads, [jax.ShapeDtypeStruct(g.shape, g.dtype) for g in grads],
                    _exchange_sems(n), start, finish)


def run_exchange(exchange, name):
    n_in, n_out = len(exchange.arrays), len(exchange.out_shapes)

    def body(*refs):
        x_in, x_out, sems = refs[:n_in], refs[n_in:n_in + n_out], refs[n_in + n_out:]
        exchange.start(x_in, x_out, sems)
        if exchange.forward is not None:
            exchange.forward(x_in, x_out, sems)
        exchange.finish(x_in, x_out, sems)

    exchange.results = list(pl.pallas_call(
        body, name=name, out_shape=exchange.out_shapes, in_specs=[_ANY] * n_in, out_specs=[_ANY] * n_out,
        scratch_shapes=exchange.sems)(*exchange.arrays))
    return exchange.results


def small_all_reduce(vec):
    R = vec.shape[0]

    def body(v_ref, o_ref, all_ref, send_sems, recv_sems):
        me, _, chips = _mesh_places()
        peers = _all_peers(me, chips)
        all_ref[_slot(me)] = v_ref[...]
        sends, recvs = [], []
        for k, peer in enumerate(peers):
            sem = dict(send_sem=send_sems.at[k], recv_sem=recv_sems.at[k],
                       device_id=peer, device_id_type=pl.DeviceIdType.MESH)
            sends.append(pltpu.make_async_remote_copy(src_ref=v_ref, dst_ref=all_ref.at[_slot(me)], **sem))
            recvs.append(pltpu.make_async_remote_copy(src_ref=v_ref, dst_ref=all_ref.at[_slot(peer)], **sem))
        for cp in sends:
            cp.start()
        for cp in recvs:
            cp.wait_recv()
        for cp in sends:
            cp.wait_send()
        acc = all_ref[0]
        for s in range(1, N_DEV):
            acc = acc + all_ref[s]
        o_ref[...] = acc

    return pl.pallas_call(
        body, name="small_all_reduce",
        out_shape=jax.ShapeDtypeStruct(vec.shape, F32),
        in_specs=[pl.BlockSpec(memory_space=pltpu.VMEM)], out_specs=pl.BlockSpec(memory_space=pltpu.VMEM),
        scratch_shapes=[pltpu.VMEM((N_DEV, R, LANES), F32), pltpu.SemaphoreType.DMA((N_PEERS,)),
                        pltpu.SemaphoreType.DMA((N_PEERS,))],
    )(vec)


def _adamw_math(w, g, m, v):
    m2 = ADAM_B1 * m + (1.0 - ADAM_B1) * g
    v2 = ADAM_B2 * v + (1.0 - ADAM_B2) * (g * g)
    m_hat = m2 / (1.0 - ADAM_B1 ** ADAM_STEP)
    v_hat = v2 / (1.0 - ADAM_B2 ** ADAM_STEP)
    delta = -ADAM_LR * (m_hat / (jnp.sqrt(v_hat) + ADAM_EPS) + ADAM_WD * w)
    return delta, m2, v2


def adamw_sharded(partials, w, m, v):
    L, r, c = w.shape
    tr = _div(r, max(BF16_SUBLANES, (1 << 18) // c), BF16_SUBLANES)
    nt = r // tr

    def part_map(l0):
        return lambda l, t: (0, jnp.where(l == l0, t, jnp.where(l < l0, 0, nt - 1)), 0)

    def body(*refs):
        parts = refs[:L]
        w_ref, m_ref, v_ref, g_out, d_out, m_out, v_out = refs[L:]
        layer = pl.program_id(0)
        for l0 in range(L):
            @pl.when(layer == l0)
            def _():
                g = parts[l0][0].astype(F32)
                for s in range(1, N_DEV):
                    g = g + parts[l0][s].astype(F32)
                delta, m2, v2 = _adamw_math(w_ref[...], g, m_ref[...], v_ref[...])
                g_out[...] = g
                d_out[...] = delta
                m_out[...] = m2
                v_out[...] = v2

    own = lambda l, t: (l, t, 0)
    return _call(body, "adamw_sharded", (L, nt),
                 [(p, (N_DEV, tr, c), part_map(l0)) for l0, p in enumerate(partials)]
                 + [(w, (None, tr, c), own), (m, (None, tr, c), own), (v, (None, tr, c), own)],
                 [((L, r, c), F32, (None, tr, c), own)] * 4)


def adamw_flat(g, w, m, v):
    shape = g.shape

    def body(g_ref, w_ref, m_ref, v_ref, d_out, m_out, v_out):
        delta, m2, v2 = _adamw_math(w_ref[...], g_ref[...], m_ref[...], v_ref[...])
        d_out[...] = delta
        m_out[...] = m2
        v_out[...] = v2

    whole = lambda: (0, 0)
    return _call(body, "adamw_flat", (), [(a, shape, whole) for a in (g, w, m, v)],
                 [(shape, F32, shape, whole)] * 3)


def _pack(parts):
    flat = jnp.concatenate([p.reshape(-1).astype(F32) for p in parts])
    rows = -(-flat.shape[0] // (8 * LANES)) * 8
    return jnp.pad(flat, (0, rows * LANES - flat.shape[0])).reshape(rows, LANES)


def _unpack(packed, shapes):
    flat = packed.reshape(-1)
    out, off = [], 0
    for s in shapes:
        size = math.prod(s)
        out.append(flat[off:off + size].reshape(s))
        off += size
    return out


def kernel(x, mem, ffn1_norm, ffn1_w13, ffn1_w2, mix_norm, gmlp_w_in, gmlp_ln_g, gmlp_ln_b, gmlp_w_s, gmlp_b_s, gmlp_w_out, conv_w_in, conv_w, conv_w_out, xattn_norm, mem_norm, xattn_wq, xattn_wkv, xattn_wo, ffn2_norm, ffn2_w13, ffn2_w2, final_norm, loss_target, m_ffn1_norm, m_ffn1_w13, m_ffn1_w2, m_mix_norm, m_gmlp_w_in, m_gmlp_ln_g, m_gmlp_ln_b, m_gmlp_w_s, m_gmlp_b_s, m_gmlp_w_out, m_conv_w_in, m_conv_w, m_conv_w_out, m_xattn_norm, m_mem_norm, m_xattn_wq, m_xattn_wkv, m_xattn_wo, m_ffn2_norm, m_ffn2_w13, m_ffn2_w2, m_final_norm, v_ffn1_norm, v_ffn1_w13, v_ffn1_w2, v_mix_norm, v_gmlp_w_in, v_gmlp_ln_g, v_gmlp_ln_b, v_gmlp_w_s, v_gmlp_b_s, v_gmlp_w_out, v_conv_w_in, v_conv_w, v_conv_w_out, v_xattn_norm, v_mem_norm, v_xattn_wq, v_xattn_wkv, v_xattn_wo, v_ffn2_norm, v_ffn2_w13, v_ffn2_w2, v_final_norm):
    given = dict(locals())
    T, D = x.shape[1], x.shape[2]
    depth = ffn1_norm.shape[0]
    xs = x.reshape(T, D)
    mems = mem.reshape(mem.shape[1], D)
    target = loss_target.reshape(T, D)
    me = 4 * lax.axis_index("x") + 2 * lax.axis_index("y") + lax.axis_index("c")
    E = gmlp_ln_g.shape[1]
    gd = E // GROUPS
    cshard = conv_w.shape[2]

    bf = {k: given[k].astype(BF16) for k in
          ("ffn1_w13", "ffn1_w2", "gmlp_w_in", "gmlp_w_out", "conv_w_in", "conv_w_out",
           "xattn_wq", "xattn_wkv", "xattn_wo", "ffn2_w13", "ffn2_w2")}

    W = {}

    def gather(names_layers):
        return names_layers, gather_exchange([(bf[k], l) for k, l in names_layers])

    def landed(tagged):
        names_layers, exchange = tagged
        W.update(zip(names_layers, exchange.results))

    cw_place = lax.dynamic_update_slice(jnp.zeros((CONV_WIDTH, D), F32), conv_w[0], (jnp.int32(0), me * cshard))
    cw_full = small_all_reduce(_pack([cw_place])).reshape(-1)[:CONV_WIDTH * D].reshape(CONV_WIDTH, D)
    bias_full = jnp.repeat(gmlp_b_s[0].T, gd, axis=1)

    first = gather([("ffn1_w13", 0)])
    run_exchange(first[1], "gather_first")
    landed(first)
    saved = []
    h = xs
    for i in range(depth):
        j = i // 2
        is_gmlp = i % 2 == 0
        mix = ("gmlp_w_in", "gmlp_w_out") if is_gmlp else ("conv_w_in", "conv_w_out")
        on_up1 = gather(([("ffn1_w2", i)] if i == 0 else [])
                        + [(mix[0], j), (mix[1], j), ("xattn_wq", i), ("xattn_wkv", i)])
        on_down1 = gather([("xattn_wo", i), ("ffn2_w2", i)])

        def w2_after_up(on_up=on_up1, i=i):
            landed(on_up)
            return W["ffn1_w2", i]

        h, sv1 = ffn_fwd(h, ffn1_norm[i], W["ffn1_w13", i], w2_after_up, on_up1[1], on_down1[1])
        landed(on_down1)
        hn = rms_fwd(h, mix_norm[i])
        on_mix_in = gather([("ffn2_w13", i)])
        p = mm_nn(hn, W[mix[0], j], "mix_in", tn_pref=768, exchange=on_mix_in[1])
        landed(on_mix_in)
        if is_gmlp:
            gated = gmlp_mid_fwd(p, gmlp_ln_g[j:j + 1], gmlp_ln_b[j:j + 1], gmlp_w_s[j], bias_full)
        else:
            gated = conv_mid_fwd(p, cw_full)
        h_mix = mm_nn(gated, W[mix[1], j].reshape(1, gated.shape[1], D), "mix_out", F32, res=h)
        sv2 = (h, hn, p, gated)
        h, sv3 = xattn_fwd(h_mix, mems, xattn_norm[i], mem_norm[i],
                           W["xattn_wq", i], W["xattn_wkv", i], W["xattn_wo", i])
        on_up2 = gather([("ffn1_w13", i + 1), ("ffn1_w2", i + 1)]) if i + 1 < depth else None
        h, sv4 = ffn_fwd(h, ffn2_norm[i], W["ffn2_w13", i], lambda i=i: W["ffn2_w2", i],
                         None if on_up2 is None else on_up2[1])
        if on_up2 is not None:
            landed(on_up2)
        saved.append((sv1, sv2, sv3, sv4))

    loss_part, d, dys, d_final_norm = loss_head(h, final_norm, target, 0.5)

    small = {k: [None] * depth for k in ("ffn1_norm", "mix_norm", "xattn_norm", "mem_norm", "ffn2_norm")}
    partial = {}
    for i in reversed(range(depth)):
        j = i // 2
        is_gmlp = i % 2 == 0
        sv1, sv2, sv3, sv4 = saved[i]
        d, dys, small["ffn2_norm"][i], partial["ffn2_w13", i], partial["ffn2_w2", i] = ffn_bwd(
            d, dys, sv4, ffn2_norm[i], W["ffn2_w13", i], W["ffn2_w2", i], 1.0)
        d, dys, small["xattn_norm"][i], small["mem_norm"][i], dwq, dwkv, partial["xattn_wo", i] = xattn_bwd(
            d, dys, sv3, mems, xattn_norm[i], W["xattn_wq", i], W["xattn_wkv", i], W["xattn_wo", i], 1.0)
        if is_gmlp:
            mix = ("gmlp_w_in", "gmlp_w_out")
            mid = lambda p, dg: (lambda r: (r[0], r[1:]))(gmlp_mid_bwd(
                p, dg, gmlp_ln_g[j:j + 1], gmlp_ln_b[j:j + 1], gmlp_w_s[j], bias_full))
        else:
            mix = ("conv_w_in", "conv_w_out")
            mid = lambda p, dg: (lambda r: (r[0], r[1:]))(conv_mid_bwd(p, dg, cw_full))
        send_attn = scatter_exchange([dwq, dwkv])
        d, dys, small["mix_norm"][i], dw_in, dw_out, extra = mixer_bwd_common(
            d, dys, sv2, mix_norm[i], W[mix[0], j], W[mix[1], j], mid, 0.5, dwin_exchange=send_attn)
        partial["xattn_wq", i], partial["xattn_wkv", i] = send_attn.results
        if is_gmlp:
            d_ws, d_bs_wide, d_lng, d_lnb = extra
        else:
            (d_cw,) = extra
        send_mix = scatter_exchange([dw_in, dw_out])
        d, dys, small["ffn1_norm"][i], partial["ffn1_w13", i], partial["ffn1_w2", i] = ffn_bwd(
            d, dys, sv1, ffn1_norm[i], W["ffn1_w13", i], W["ffn1_w2", i], 0.5, dact_exchange=send_mix)
        partial[mix[0], j], partial[mix[1], j] = send_mix.results
    grad_x = d.reshape(x.shape)

    small_grads = {k: jnp.concatenate(v, axis=0) for k, v in small.items()}
    small_grads["gmlp_ln_g"] = d_lng
    small_grads["gmlp_ln_b"] = d_lnb
    small_grads["gmlp_w_s"] = d_ws[None]
    small_grads["gmlp_b_s"] = d_bs_wide[None, :, :, 0]
    small_grads["final_norm"] = d_final_norm.reshape(-1)
    small_names = ["ffn1_norm", "mix_norm", "gmlp_ln_g", "gmlp_ln_b", "gmlp_w_s", "gmlp_b_s",
                   "xattn_norm", "mem_norm", "ffn2_norm", "final_norm"]
    summed = small_all_reduce(_pack([small_grads[k] for k in small_names] + [d_cw, loss_part]))
    parts = _unpack(summed, [given[k].shape for k in small_names] + [(CONV_WIDTH, D), (1, LANES)])
    grads = dict(zip(small_names, parts[:len(small_names)]))
    grads["conv_w"] = lax.dynamic_slice(parts[-2], (jnp.int32(0), me * cshard), (CONV_WIDTH, cshard))[None]
    loss = parts[-1][0, 0]
    flat_names = small_names + ["conv_w"]
    flat = adamw_flat(*[_pack([src[k] for k in flat_names]) for src in
                        (grads, given, {k: given["m_" + k] for k in flat_names},
                         {k: given["v_" + k] for k in flat_names})])
    delta, new_m, new_v = [dict(zip(flat_names, _unpack(f, [given[k].shape for k in flat_names]))) for f in flat]

    for k in bf:
        w = given[k]
        L = w.shape[0]
        shard = w.shape[1:]
        view = lambda a: a.reshape((L,) + shard)
        g, dl, m2, v2 = adamw_sharded([partial[k, l] for l in range(L)], w, given["m_" + k], given["v_" + k])
        grads[k], delta[k], new_m[k], new_v[k] = view(g), view(dl), view(m2), view(v2)

    order = ["ffn1_norm", "ffn1_w13", "ffn1_w2", "mix_norm", "gmlp_w_in", "gmlp_ln_g", "gmlp_ln_b", "gmlp_w_s",
             "gmlp_b_s", "gmlp_w_out", "conv_w_in", "conv_w", "conv_w_out", "xattn_norm", "mem_norm", "xattn_wq",
             "xattn_wkv", "xattn_wo", "ffn2_norm", "ffn2_w13", "ffn2_w2", "final_norm"]
    return (loss, grad_x, *[grads[k] for k in order], *[delta[k] for k in order],
            *[new_m[k] for k in order], *[new_v[k] for k in order])
```

```python
import functools
import math

import jax
import jax.numpy as jnp
from jax import lax
from jax.experimental import pallas as pl
from jax.experimental.pallas import tpu as pltpu

F32 = jnp.float32
BF16 = jnp.bfloat16

N_DEV = 8
N_PEERS = N_DEV - 1
CHUNK = 128
GROUPS = 8
HEADS = 4
CONV_WIDTH = 3
RMS_EPS = 1e-6
LN_EPS = 1e-5
ADAM_LR = 0.001
ADAM_B1 = 0.9
ADAM_B2 = 0.999
ADAM_EPS = 1e-08
ADAM_WD = 0.01
ADAM_STEP = 10
LANES = 128
BF16_SUBLANES = 16
MXU_WIDTH = 256
VMEM_LIMIT_BYTES = 56 * 1024 * 1024

_NT = (((1,), (1,)), ((), ()))
_TN = (((0,), (0,)), ((), ()))
_SQRT_HALF = 0.7071067811865476
_INV_SQRT_2PI = 0.3989422804014327


def _div(n, pref, align):
    best = None
    for t in range(align, min(n, pref) + 1, align):
        if n % t == 0:
            best = t
    return n if best is None else best


def _chunks(n, width):
    return [(c0, min(width, n - c0)) for c0 in range(0, n, width)]


_ANY = pl.BlockSpec(memory_space=pl.ANY)


class Exchange:
    def __init__(self, arrays, out_shapes, sems, start, finish, forward=None):
        self.arrays, self.out_shapes, self.sems = list(arrays), list(out_shapes), list(sems)
        self.start, self.finish, self.forward = start, finish, forward
        self.results = None


def _call(body, name, grid, ins, outs, scratch=(), exchange=None):
    in_specs = [pl.BlockSpec(*spec[1:3], **({"pipeline_mode": pl.Buffered(1)} if len(spec) > 3 else {}))
                for spec in ins]
    ins = [spec[:3] for spec in ins]
    out_specs = [pl.BlockSpec(bs, im) for _, _, bs, im in outs]
    out_shape = [jax.ShapeDtypeStruct(s, d) for s, d, _, _ in outs]
    arrays = [a for a, _, _ in ins]
    scratch = list(scratch)
    kernel_fn = body
    if exchange is not None:
        n_in, n_out, n_scr = len(ins), len(outs), len(scratch)
        n_xin, n_xout = len(exchange.arrays), len(exchange.out_shapes)
        steps = math.prod(grid)
        forward_step = min(steps - 1, (15 * steps) // 16)

        def kernel_fn(*refs):
            refs = list(refs)
            own_in, x_in = refs[:n_in], refs[n_in:n_in + n_xin]
            refs = refs[n_in + n_xin:]
            own_out, x_out = refs[:n_out], refs[n_out:n_out + n_xout]
            refs = refs[n_out + n_xout:]
            own_scr, x_sems = refs[:n_scr], refs[n_scr:]
            step = 0
            for axis, size in enumerate(grid):
                step = step * size + pl.program_id(axis)

            @pl.when(step == 0)
            def _():
                exchange.start(x_in, x_out, x_sems)

            if exchange.forward is not None:
                @pl.when(step == forward_step)
                def _():
                    exchange.forward(x_in, x_out, x_sems)

            body(*own_in, *own_out, *own_scr)

            @pl.when(step == steps - 1)
            def _():
                exchange.finish(x_in, x_out, x_sems)

        in_specs += [_ANY] * n_xin
        out_specs += [_ANY] * n_xout
        out_shape += exchange.out_shapes
        arrays += exchange.arrays
        scratch += exchange.sems
    res = pl.pallas_call(
        kernel_fn,
        name=name,
        grid=grid,
        in_specs=in_specs,
        out_specs=out_specs,
        out_shape=out_shape,
        scratch_shapes=scratch,
        compiler_params=pltpu.CompilerParams(
            dimension_semantics=("arbitrary",) * len(grid), vmem_limit_bytes=VMEM_LIMIT_BYTES),
    )(*arrays)
    if exchange is not None:
        exchange.results = list(res[len(outs):])
        res = res[:len(outs)]
    return res


def _dot(a, b, dims=None):
    if dims is None:
        return jnp.dot(a, b, preferred_element_type=F32)
    return lax.dot_general(a, b, dims, preferred_element_type=F32)


def _sigmoid(v):
    return 1.0 / (1.0 + jnp.exp(-v))


def _gelu(v):
    return 0.5 * v * (1.0 + lax.erf(v * _SQRT_HALF))


def _gelu_grad(v):
    return 0.5 * (1.0 + lax.erf(v * _SQRT_HALF)) + v * (_INV_SQRT_2PI * jnp.exp(-0.5 * v * v))


def _accumulate(ref, part, first):
    @pl.when(first)
    def _():
        ref[...] = part

    @pl.when(jnp.logical_not(first))
    def _():
        ref[...] += part


def rms_fwd(x, g):
    T, D = x.shape
    tt = _div(T, 512, BF16_SUBLANES)

    def body(x_ref, g_ref, o_ref):
        xv = x_ref[...]
        r = lax.rsqrt(jnp.mean(xv * xv, axis=-1, keepdims=True) + RMS_EPS)
        o_ref[...] = ((xv * r) * g_ref[...]).astype(BF16)

    return _call(body, "rms_fwd", (T // tt,),
                 [(x, (tt, D), lambda i: (i, 0)), (g.reshape(1, D), (1, D), lambda i: (0, 0))],
                 [((T, D), BF16, (tt, D), lambda i: (i, 0))])[0]


def rms_gain_grad(x, dxn):
    T, D = x.shape
    tt = _div(T, 256, 8)

    def body(x_ref, dn_ref, dg_ref):
        xv = x_ref[...]
        r = lax.rsqrt(jnp.mean(xv * xv, axis=-1, keepdims=True) + RMS_EPS)
        _accumulate(dg_ref, jnp.sum(dn_ref[...] * (xv * r), axis=0, keepdims=True), pl.program_id(0) == 0)

    row = lambda i: (i, 0)
    return _call(body, "rms_gain_grad", (T // tt,), [(x, (tt, D), row), (dxn, (tt, D), row)],
                 [((1, D), F32, (1, D), lambda i: (0, 0))])[0]


def loss_head(x, g, target, scale):
    T, D = x.shape
    tt = _div(T, 256, BF16_SUBLANES)

    def body(x_ref, g_ref, t_ref, loss_ref, dx_ref, dxs_ref, dg_ref):
        first = pl.program_id(0) == 0
        xv = x_ref[...]
        gv = g_ref[...]
        r = lax.rsqrt(jnp.mean(xv * xv, axis=-1, keepdims=True) + RMS_EPS)
        xh = xv * r
        err = xh * gv - t_ref[...]
        part = 0.5 * jnp.sum(jnp.mean(err * err, axis=-1, keepdims=True), axis=0, keepdims=True)
        _accumulate(loss_ref, jnp.broadcast_to(part, (1, LANES)), first)
        dy = err * (1.0 / D)
        dxh = dy * gv
        dx = r * (dxh - xh * jnp.mean(dxh * xh, axis=-1, keepdims=True))
        dx_ref[...] = dx
        dxs_ref[...] = (scale * dx).astype(BF16)
        _accumulate(dg_ref, jnp.sum(dy * xh, axis=0, keepdims=True), first)

    row = lambda i: (i, 0)
    fix = lambda i: (0, 0)
    return _call(body, "loss_head", (T // tt,),
                 [(x, (tt, D), row), (g.reshape(1, D), (1, D), fix), (target, (tt, D), row)],
                 [((1, LANES), F32, (1, LANES), fix), ((T, D), F32, (tt, D), row),
                  ((T, D), BF16, (tt, D), row), ((1, D), F32, (1, D), fix)])


def mm_nn(a, w3, name, out_dtype=BF16, res=None, res_scale=1.0, tm_pref=1024, tn_pref=1024, exchange=None):
    M, K = a.shape
    nb, _, ns = w3.shape
    tn = _div(ns, tn_pref, LANES)
    per = ns // tn
    tm = _div(M, tm_pref, BF16_SUBLANES)
    ins = [(a, (tm, K), lambda j, m: (m, 0)), (w3, (None, K, tn), lambda j, m: (j // per, 0, j % per))]
    if res is not None:
        ins.append((res, (tm, tn), lambda j, m: (m, j)))

    def body(*refs):
        a_ref, w_ref = refs[0], refs[1]
        o_ref = refs[-1]
        acc = _dot(a_ref[...], w_ref[...])
        if res is not None:
            acc = refs[2][...] + res_scale * acc
        o_ref[...] = acc.astype(o_ref.dtype)

    return _call(body, name, (nb * per, M // tm), ins,
                 [((M, nb * ns), out_dtype, (tm, tn), lambda j, m: (m, j))], exchange=exchange)[0]


def mm_nt(a_in, w3, name, out_dtype, M, tm_pref=1024, to_pref=2048, exchange=None):
    nb, Ko, ns = w3.shape
    to = _div(Ko, to_pref, LANES)
    tm = _div(M, tm_pref, BF16_SUBLANES)
    if isinstance(a_in, tuple):
        a, a_bs, a_im = a_in
        a_bs = tuple(tm if s == "tm" else s for s in a_bs)
    else:
        a, a_bs, a_im = a_in, (tm, ns), lambda m, o, b: (m, b)
    if nb > 1:
        assert out_dtype == F32

    def body(a_ref, w_ref, o_ref):
        p = _dot(a_ref[...], w_ref[...], _NT)
        if nb == 1:
            o_ref[...] = p.astype(o_ref.dtype)
        else:
            _accumulate(o_ref, p, pl.program_id(2) == 0)

    return _call(body, name, (M // tm, Ko // to, nb),
                 [(a, a_bs, a_im), (w3, (None, to, ns), lambda m, o, b: (b, o, 0))],
                 [((M, Ko), out_dtype, (tm, to), lambda m, o, b: (m, o))], exchange=exchange)[0]


def mm_nt_rms_bwd(a_in, w3, name, x, g, d, scale, exchange=None):
    nb, D, ns = w3.shape
    M = x.shape[0]
    tm = _div(M, 512, BF16_SUBLANES)
    if isinstance(a_in, tuple):
        a, a_bs, a_im = a_in
        a_bs = tuple(tm if s == "tm" else s for s in a_bs)
    else:
        a, a_bs, a_im = a_in, (tm, ns), lambda m, b: (m, b)

    rc = _div(tm, 64, BF16_SUBLANES)

    def body(a_ref, w_ref, x_ref, g_ref, d_ref, dx_ref, dxs_ref, dg_ref, acc_ref):
        m, b = pl.program_id(0), pl.program_id(1)
        _accumulate(acc_ref, _dot(a_ref[...], w_ref[...], _NT), b == 0)

        @pl.when(b == nb - 1)
        def _():
            gv = g_ref[...]

            def piece(c, dg):
                rows = pl.ds(pl.multiple_of(c * rc, rc), rc)
                dn = acc_ref[rows, :]
                xv = x_ref[rows, :]
                r = lax.rsqrt(jnp.mean(xv * xv, axis=-1, keepdims=True) + RMS_EPS)
                xh = xv * r
                dxh = dn * gv
                dx = r * (dxh - xh * jnp.mean(dxh * xh, axis=-1, keepdims=True)) + d_ref[rows, :]
                dx_ref[rows, :] = dx
                dxs_ref[rows, :] = (scale * dx).astype(BF16)
                return dg + jnp.sum(dn * xh, axis=0, keepdims=True)

            _accumulate(dg_ref, lax.fori_loop(0, tm // rc, piece, jnp.zeros((1, D), F32)), m == 0)

    row = lambda m, b: (m, 0)
    fix = lambda m, b: (0, 0)
    w_spec = (w3, (None, D, ns), lambda m, b: (b, 0, 0)) + (("single",) if nb == 1 else ())
    return _call(body, name, (M // tm, nb),
                 [(a, a_bs, a_im), w_spec,
                  (x, (tm, D), row), (g.reshape(1, D), (1, D), fix), (d, (tm, D), row)],
                 [((M, D), F32, (tm, D), row), ((M, D), BF16, (tm, D), row), ((1, D), F32, (1, D), fix)],
                 scratch=[pltpu.VMEM((tm, D), F32)], exchange=exchange)


def mm_tn(a, b_in, name, nbo, ns, tka_pref=1024, tt_pref=2048, tn_pref=2048, exchange=None):
    T, Ka = a.shape
    tt = _div(T, tt_pref, BF16_SUBLANES)
    tka = _div(Ka, tka_pref, LANES)
    if isinstance(b_in, tuple):
        b, b_bs, b_im = b_in
        b_bs = tuple(tt if s == "tt" else s for s in b_bs)
        tn, per = ns, 1
    else:
        tn = _div(ns, tn_pref, LANES)
        per = ns // tn
        b, b_bs, b_im = b_in, (tt, tn), lambda i, j, t: (t, j)
    nt = T // tt

    def body(a_ref, b_ref, o_ref, acc_ref):
        t = pl.program_id(2)
        _accumulate(acc_ref, _dot(a_ref[...], b_ref[...], _TN), t == 0)

        @pl.when(t == nt - 1)
        def _():
            o_ref[...] = acc_ref[...].astype(BF16)

    return _call(body, name, (Ka // tka, nbo * per, nt),
                 [(a, (tt, tka), lambda i, j, t: (t, i)), (b, b_bs, b_im)],
                 [((nbo, Ka, ns), BF16, (None, tka, tn), lambda i, j, t: (j // per, i, j % per))],
                 scratch=[pltpu.VMEM((tka, tn), F32)], exchange=exchange)[0]


def ffn_up(xn, w13, exchange=None):
    T, D = xn.shape
    nb, _, ns = w13.shape
    half = nb // 2
    F = half * ns
    tm = _div(T, 512, BF16_SUBLANES)
    pair = 2 if half % 2 == 0 else 1

    def columns(w_ref, c0, cw):
        k, off = divmod(c0, ns)
        if off + cw <= ns:
            return w_ref[k, :, off:off + cw]
        return jnp.concatenate([w_ref[k, :, off:ns], w_ref[k + 1, :, 0:off + cw - ns]], axis=1)

    def body(x_ref, wg_ref, wu_ref, fac_ref, act_ref):
        xv = x_ref[...]
        for c0, cw in _chunks(pair * ns, MXU_WIDTH):
            cols = slice(c0, c0 + cw)
            gate = _dot(xv, columns(wg_ref, c0, cw))
            up = _dot(xv, columns(wu_ref, c0, cw))
            s = _sigmoid(gate)
            silu = gate * s
            fac_ref[0, :, cols] = (up * (s * (1.0 + gate * (1.0 - s)))).astype(BF16)
            fac_ref[1, :, cols] = silu.astype(BF16)
            act_ref[:, cols] = (silu * up).astype(BF16)

    tn = pair * ns
    return _call(body, "ffn_up", (half // pair, T // tm),
                 [(xn, (tm, D), lambda j, m: (m, 0)),
                  (w13, (pair, D, ns), lambda j, m: (j, 0, 0), "single"),
                  (w13, (pair, D, ns), lambda j, m: (j + half // pair, 0, 0), "single")],
                 [((2, T, F), BF16, (2, tm, tn), lambda j, m: (0, m, j)),
                  ((T, F), BF16, (tm, tn), lambda j, m: (m, j))], exchange=exchange)


def ffn_dact(dy, w2, fac, exchange=None):
    T, D = dy.shape
    F = w2.shape[0]
    tm = _div(T, 512, BF16_SUBLANES)
    tn = _div(F, F // 2, MXU_WIDTH)

    def body(dy_ref, w_ref, fac_ref, dh_ref):
        dyv = dy_ref[...]
        for c0, cw in _chunks(tn, MXU_WIDTH):
            cols = slice(c0, c0 + cw)
            da = _dot(dyv, w_ref[cols, :], _NT)
            dh_ref[0, :, cols] = (da * fac_ref[0, :, cols].astype(F32)).astype(BF16)
            dh_ref[1, :, cols] = (da * fac_ref[1, :, cols].astype(F32)).astype(BF16)

    return _call(body, "ffn_dact", (F // tn, T // tm),
                 [(dy, (tm, D), lambda j, m: (m, 0)), (w2, (tn, D), lambda j, m: (j, 0), "single"),
                  (fac, (2, tm, tn), lambda j, m: (0, m, j))],
                 [((2, T, F), BF16, (2, tm, tn), lambda j, m: (0, m, j))], exchange=exchange)[0]


def ffn_fwd(x, norm_g, w13, get_w2, up_exchange=None, down_exchange=None):
    xn = rms_fwd(x, norm_g)
    fac, act = ffn_up(xn, w13, exchange=up_exchange)
    F = act.shape[1]
    y = mm_nn(act, get_w2().reshape(1, F, -1), "ffn_down", F32, res=x, res_scale=0.5, tm_pref=512,
              exchange=down_exchange)
    return y, (x, xn, fac, act)


def ffn_bwd(d, dys, saved, norm_g, w13, w2, scale_out):
    x, xn, fac, act = saved
    T, D = x.shape
    nb, _, ns = w13.shape
    half = nb // 2
    F = half * ns
    dh = ffn_dact(dys, w2.reshape(F, D), fac)
    dw2 = mm_tn(act, dys, "ffn_dw2", 1, D, tka_pref=ns, tn_pref=1024)
    send_w2 = scatter_exchange([dw2.reshape(N_DEV, F // N_DEV, D)])
    dw13 = mm_tn(xn, (dh, (None, "tt", ns), lambda i, j, t: (j // half, t, j % half)), "ffn_dw13", nb, ns,
                 exchange=send_w2)
    send_w13 = scatter_exchange([dw13])
    dx, dxs, dg = mm_nt_rms_bwd((dh, (None, "tm", ns), lambda m, b: (b // half, m, b % half)), w13, "ffn_dxn",
                                x, norm_g, d, scale_out, exchange=send_w13)
    return dx, dxs, dg, send_w13.results[0], send_w2.results[0]


def _gmlp_parts(p_ref, lng_ref, lnb_ref):
    E = lng_ref.shape[-1]
    z = _gelu(p_ref[...].astype(F32))
    u = z[:, :E]
    vp = z[:, E:]
    mu = jnp.mean(vp, axis=-1, keepdims=True)
    xc = vp - mu
    rstd = lax.rsqrt(jnp.mean(xc * xc, axis=-1, keepdims=True) + LN_EPS)
    vh = xc * rstd
    v = vh * lng_ref[...] + lnb_ref[...]
    return u, vh, rstd, v


def _causal_ws(ws_ref, g):
    keep = lax.broadcasted_iota(jnp.int32, (CHUNK, CHUNK), 0) >= lax.broadcasted_iota(jnp.int32, (CHUNK, CHUNK), 1)
    return jnp.where(keep, ws_ref[g], 0.0).astype(BF16), keep


def gmlp_mid_fwd(p, ln_g, ln_b, w_s, bias_full):
    T, E2 = p.shape
    E = E2 // 2
    gd = E // GROUPS
    tm = _div(T, 256, CHUNK)
    fix2 = lambda i: (0, 0)

    def body(p_ref, lng_ref, lnb_ref, ws_ref, bias_ref, o_ref):
        u, _, _, v = _gmlp_parts(p_ref, lng_ref, lnb_ref)
        vb = v.astype(BF16)
        for g in range(GROUPS):
            wm, _ = _causal_ws(ws_ref, g)
            cols = slice(g * gd, (g + 1) * gd)
            for c in range(tm // CHUNK):
                rows = slice(c * CHUNK, (c + 1) * CHUNK)
                f = _dot(wm, vb[rows, cols]) + bias_ref[:, cols]
                o_ref[rows, cols] = (u[rows, cols] * f).astype(BF16)

    return _call(body, "gmlp_mid_fwd", (T // tm,),
                 [(p, (tm, E2), lambda i: (i, 0)), (ln_g, (1, E), fix2), (ln_b, (1, E), fix2),
                  (w_s, (GROUPS, CHUNK, CHUNK), lambda i: (0, 0, 0)), (bias_full, (CHUNK, E), fix2)],
                 [((T, E), BF16, (tm, E), lambda i: (i, 0))])[0]


def gmlp_mid_bwd(p, dgated, ln_g, ln_b, w_s, bias_full, exchange=None):
    T, E2 = p.shape
    E = E2 // 2
    gd = E // GROUPS
    tm = _div(T, 256, CHUNK)
    nsteps = T // tm
    fix2 = lambda i: (0, 0)
    fix3 = lambda i: (0, 0, 0)

    def body(p_ref, dg_ref, lng_ref, lnb_ref, ws_ref, bias_ref,
             dp_ref, dws_ref, dbs_ref, dlng_ref, dlnb_ref, f_sc, dv_sc, db_sc):
        i = pl.program_id(0)
        first = i == 0
        u, vh, rstd, v = _gmlp_parts(p_ref, lng_ref, lnb_ref)
        vb = v.astype(BF16)
        dgt = dg_ref[...].astype(F32)
        df = dgt * u
        dfb = df.astype(BF16)
        for g in range(GROUPS):
            wm, keep = _causal_ws(ws_ref, g)
            cols = slice(g * gd, (g + 1) * gd)
            dw = None
            dbg = None
            for c in range(tm // CHUNK):
                rows = slice(c * CHUNK, (c + 1) * CHUNK)
                f_sc[rows, cols] = _dot(wm, vb[rows, cols]) + bias_ref[:, cols]
                dv_sc[rows, cols] = _dot(wm, dfb[rows, cols], _TN)
                part = _dot(dfb[rows, cols], vb[rows, cols], _NT)
                dw = part if dw is None else dw + part
                dbg = df[rows, cols] if dbg is None else dbg + df[rows, cols]
            dw = jnp.where(keep, dw, 0.0)

            @pl.when(first)
            def _():
                dws_ref[g] = dw
                db_sc[:, cols] = dbg

            @pl.when(jnp.logical_not(first))
            def _():
                dws_ref[g] += dw
                db_sc[:, cols] += dbg

        du = dgt * f_sc[...]
        dv = dv_sc[...]
        _accumulate(dlng_ref, jnp.sum(dv * vh, axis=0, keepdims=True), first)
        _accumulate(dlnb_ref, jnp.sum(dv, axis=0, keepdims=True), first)
        dvh = dv * lng_ref[...]
        dvp = rstd * (dvh - jnp.mean(dvh, axis=-1, keepdims=True)
                      - vh * jnp.mean(dvh * vh, axis=-1, keepdims=True))
        gp = _gelu_grad(p_ref[...].astype(F32))
        dp_ref[:, :E] = (du * gp[:, :E]).astype(BF16)
        dp_ref[:, E:] = (dvp * gp[:, E:]).astype(BF16)

        @pl.when(i == nsteps - 1)
        def _():
            for g in range(GROUPS):
                tot = jnp.sum(db_sc[:, g * gd:(g + 1) * gd], axis=-1, keepdims=True)
                dbs_ref[g] = jnp.broadcast_to(tot, (CHUNK, LANES))

    return _call(body, "gmlp_mid_bwd", (nsteps,),
                 [(p, (tm, E2), lambda i: (i, 0)), (dgated, (tm, E), lambda i: (i, 0)),
                  (ln_g, (1, E), fix2), (ln_b, (1, E), fix2),
                  (w_s, (GROUPS, CHUNK, CHUNK), fix3), (bias_full, (CHUNK, E), fix2)],
                 [((T, E2), BF16, (tm, E2), lambda i: (i, 0)),
                  ((GROUPS, CHUNK, CHUNK), F32, (GROUPS, CHUNK, CHUNK), fix3),
                  ((GROUPS, CHUNK, LANES), F32, (GROUPS, CHUNK, LANES), fix3),
                  ((1, E), F32, (1, E), fix2), ((1, E), F32, (1, E), fix2)],
                 scratch=[pltpu.VMEM((tm, E), F32), pltpu.VMEM((tm, E), F32), pltpu.VMEM((CHUNK, E), F32)],
                 exchange=exchange)


HALO = 16


def _row_of(block, r):
    rows = lax.broadcasted_iota(jnp.int32, block.shape, 0)
    return jnp.sum(jnp.where(rows == r, block, 0.0), axis=0, keepdims=True)


def _shift_down(z, k, fill):
    out = pltpu.roll(z, k, 0)
    rows = lax.broadcasted_iota(jnp.int32, z.shape, 0)
    for t in range(k):
        out = jnp.where(rows == t, fill[t], out)
    return out


def _shift_up(z, k, fill):
    n = z.shape[0]
    out = pltpu.roll(z, n - k, 0)
    rows = lax.broadcasted_iota(jnp.int32, z.shape, 0)
    for j in range(k):
        out = jnp.where(rows == n - k + j, fill[j], out)
    return out


def _conv_parts(p_ref, prev_ref, cw_ref, is_first):
    D = cw_ref.shape[-1]
    pv = p_ref[...].astype(F32)
    bg, cg, val = pv[:, :D], pv[:, D:2 * D], pv[:, 2 * D:]
    z = cg * val
    pp = prev_ref[...].astype(F32)
    zp = jnp.where(is_first, 0.0, pp[:, D:2 * D] * pp[:, 2 * D:])
    zl1 = _row_of(zp, HALO - 1)
    zl2 = _row_of(zp, HALO - 2)
    z1 = _shift_down(z, 1, [zl1])
    z2 = _shift_down(z, 2, [zl2, zl1])
    conv = z2 * cw_ref[0:1, :] + z1 * cw_ref[1:2, :] + z * cw_ref[2:3, :]
    return bg, cg, val, z, z1, z2, conv


def conv_mid_fwd(p, cw):
    T, D3 = p.shape
    D = D3 // 3
    tm = _div(T, 256, HALO)
    per = tm // HALO

    def body(p_ref, prev_ref, cw_ref, o_ref):
        bg, _, _, _, _, _, conv = _conv_parts(p_ref, prev_ref, cw_ref, pl.program_id(0) == 0)
        o_ref[...] = (bg * conv).astype(BF16)

    return _call(body, "conv_mid_fwd", (T // tm,),
                 [(p, (tm, D3), lambda i: (i, 0)),
                  (p, (HALO, D3), lambda i: (jnp.maximum(i * per - 1, 0), 0)),
                  (cw, (CONV_WIDTH, D), lambda i: (0, 0))],
                 [((T, D), BF16, (tm, D), lambda i: (i, 0))])[0]


def conv_mid_bwd(p, dgated, cw, exchange=None):
    T, D3 = p.shape
    D = D3 // 3
    tm = _div(T, 256, HALO)
    per = tm // HALO
    nsteps = T // tm
    last_halo = T // HALO - 1
    nxt = lambda i: (jnp.minimum((i + 1) * per, last_halo), 0)

    def body(p_ref, prev_ref, next_ref, dg_ref, dgn_ref, cw_ref, dp_ref, dcw_ref):
        i = pl.program_id(0)
        bg, cg, val, z, z1, z2, conv = _conv_parts(p_ref, prev_ref, cw_ref, i == 0)
        dgt = dg_ref[...].astype(F32)
        dconv = dgt * bg
        dcn = jnp.where(i == nsteps - 1, 0.0, dgn_ref[...].astype(F32) * next_ref[:, :D].astype(F32))
        n0 = _row_of(dcn, 0)
        n1 = _row_of(dcn, 1)
        up1 = _shift_up(dconv, 1, [n0])
        up2 = _shift_up(dconv, 2, [n0, n1])
        dz = dconv * cw_ref[2:3, :] + up1 * cw_ref[1:2, :] + up2 * cw_ref[0:1, :]
        dp_ref[:, :D] = (dgt * conv).astype(BF16)
        dp_ref[:, D:2 * D] = (dz * val).astype(BF16)
        dp_ref[:, 2 * D:] = (dz * cg).astype(BF16)
        first = i == 0
        parts = (jnp.sum(dconv * z2, axis=0, keepdims=True), jnp.sum(dconv * z1, axis=0, keepdims=True),
                 jnp.sum(dconv * z, axis=0, keepdims=True))

        @pl.when(first)
        def _():
            for k in range(CONV_WIDTH):
                dcw_ref[k:k + 1, :] = parts[k]

        @pl.when(jnp.logical_not(first))
        def _():
            for k in range(CONV_WIDTH):
                dcw_ref[k:k + 1, :] += parts[k]

    return _call(body, "conv_mid_bwd", (nsteps,),
                 [(p, (tm, D3), lambda i: (i, 0)),
                  (p, (HALO, D3), lambda i: (jnp.maximum(i * per - 1, 0), 0)),
                  (p, (HALO, D3), nxt),
                  (dgated, (tm, D), lambda i: (i, 0)),
                  (dgated, (HALO, D), nxt),
                  (cw, (CONV_WIDTH, D), lambda i: (0, 0))],
                 [((T, D3), BF16, (tm, D3), lambda i: (i, 0)),
                  ((CONV_WIDTH, D), F32, (CONV_WIDTH, D), lambda i: (0, 0))], exchange=exchange)


def mixer_bwd_common(d, dys, saved, norm_g, w_in, w_out, mid_bwd, scale_out, dwq):
    x, hn, p, gated = saved
    T, D = x.shape
    E = gated.shape[1]
    w_out3 = w_out.reshape(1, E, D)
    dgated = mm_nt(dys, w_out3, "mix_dgated", BF16, T)
    send_wq = scatter_exchange([dwq])
    dw_out = mm_tn(gated, dys, "mix_dwout", 1, D, exchange=send_wq)
    send_wout = scatter_exchange([dw_out.reshape(N_DEV, E // N_DEV, D)])
    dp, extra = mid_bwd(p, dgated, send_wout)
    nb, _, ns = w_in.shape
    dw_in = mm_tn(hn, dp, "mix_dwin", nb, ns)
    send_win = scatter_exchange([dw_in])
    dx, dxs, dg = mm_nt_rms_bwd(dp, w_in, "mix_dhn", x, norm_g, d, scale_out, exchange=send_win)
    return dx, dxs, dg, send_win.results[0], send_wout.results[0], send_wq.results[0], extra


def _softmax_rows(s):
    e = jnp.exp(s - jnp.max(s, axis=-1, keepdims=True))
    return e / jnp.sum(e, axis=-1, keepdims=True)


def attn_fwd(q, kv):
    T, D = q.shape
    M = kv.shape[0]
    hd = D // HEADS
    scale = hd ** -0.5
    tm = _div(T, 512, BF16_SUBLANES)

    def body(q_ref, kv_ref, o_ref):
        for h in range(HEADS):
            cols = slice(h * hd, (h + 1) * hd)
            s = _dot(q_ref[:, cols], kv_ref[:, cols], _NT) * scale
            pr = _softmax_rows(s).astype(BF16)
            o_ref[:, cols] = _dot(pr, kv_ref[:, D + h * hd:D + (h + 1) * hd]).astype(BF16)

    return _call(body, "attn_fwd", (T // tm,),
                 [(q, (tm, D), lambda i: (i, 0)), (kv, (M, 2 * D), lambda i: (0, 0))],
                 [((T, D), BF16, (tm, D), lambda i: (i, 0))])[0]


def attn_bwd(q, do, kv, exchange=None):
    T, D = q.shape
    M = kv.shape[0]
    hd = D // HEADS
    scale = hd ** -0.5
    tm = _div(T, 512, BF16_SUBLANES)
    nsteps = T // tm

    def body(q_ref, do_ref, kv_ref, dq_ref, dkv_ref, acc_ref):
        i = pl.program_id(0)
        for h in range(HEADS):
            cols = slice(h * hd, (h + 1) * hd)
            vcols = slice(D + h * hd, D + (h + 1) * hd)
            qh = q_ref[:, cols]
            kh = kv_ref[:, cols]
            doh = do_ref[:, cols]
            pr = _softmax_rows(_dot(qh, kh, _NT) * scale)
            dpr = _dot(doh, kv_ref[:, vcols], _NT)
            ds = (pr * (dpr - jnp.sum(dpr * pr, axis=-1, keepdims=True)) * scale).astype(BF16)
            dq_ref[:, cols] = _dot(ds, kh).astype(BF16)
            dk = _dot(ds, qh, _TN)
            dv = _dot(pr.astype(BF16), doh, _TN)

            @pl.when(i == 0)
            def _():
                acc_ref[:, cols] = dk
                acc_ref[:, vcols] = dv

            @pl.when(i > 0)
            def _():
                acc_ref[:, cols] += dk
                acc_ref[:, vcols] += dv

        @pl.when(i == nsteps - 1)
        def _():
            dkv_ref[...] = acc_ref[...].astype(BF16)

    return _call(body, "attn_bwd", (nsteps,),
                 [(q, (tm, D), lambda i: (i, 0)), (do, (tm, D), lambda i: (i, 0)),
                  (kv, (M, 2 * D), lambda i: (0, 0))],
                 [((T, D), BF16, (tm, D), lambda i: (i, 0)), ((M, 2 * D), BF16, (M, 2 * D), lambda i: (0, 0))],
                 scratch=[pltpu.VMEM((M, 2 * D), F32)], exchange=exchange)


def xattn_fwd(x, mem, xnorm_g, mnorm_g, wq, wkv, wo):
    D = x.shape[1]
    hq = rms_fwd(x, xnorm_g)
    mn = rms_fwd(mem, mnorm_g)
    q = mm_nn(hq, wq.reshape(1, D, D), "xattn_q")
    kv = mm_nn(mn, wkv, "xattn_kv")
    o = attn_fwd(q, kv)
    y = mm_nn(o, wo.reshape(1, D, D), "xattn_out", F32, res=x)
    return y, (x, hq, mn, q, kv, o)


def xattn_bwd(d, dys, saved, mem, xnorm_g, wq, wkv, wo, scale_out):
    x, hq, mn, q, kv, o = saved
    T, D = x.shape
    M = mem.shape[0]
    do = mm_nt(dys, wo.reshape(1, D, D), "xattn_do", BF16, T)
    rows = D // N_DEV
    dwo = mm_tn(o, dys, "xattn_dwo", 1, D)
    send_wo = scatter_exchange([dwo.reshape(N_DEV, rows, D)])
    dq, dkv = attn_bwd(q, do, kv, exchange=send_wo)
    nb, _, ns = wkv.shape
    dwkv = mm_tn(mn, dkv, "xattn_dwkv", nb, ns)
    dmn = mm_nt(dkv, wkv, "xattn_dmn", F32, M)
    dgm = rms_gain_grad(mem, dmn)
    dwq = mm_tn(hq, dq, "xattn_dwq", 1, D)
    send_wkv = scatter_exchange([dwkv])
    dx, dxs, dgx = mm_nt_rms_bwd(dq, wq.reshape(1, D, D), "xattn_dhq", x, xnorm_g, d, scale_out,
                                 exchange=send_wkv)
    return dx, dxs, dgx, dgm, dwq.reshape(N_DEV, rows, D), send_wkv.results[0], send_wo.results[0]


def _mesh_places():
    x, y, c = lax.axis_index("x"), lax.axis_index("y"), lax.axis_index("c")
    chips = [(1 - x, y), (x, 1 - y), (1 - x, 1 - y)]
    return (x, y, c), (x, y, 1 - c), chips


def _slot(place):
    return 4 * place[0] + 2 * place[1] + place[2]


def _exchange_sems(n):
    return [pltpu.SemaphoreType.DMA((n * N_PEERS,)), pltpu.SemaphoreType.DMA((n * N_PEERS,)),
            pltpu.SemaphoreType.DMA((n,))]


def gather_exchange(shards):
    n = len(shards)
    shapes = [a.shape if l is None else a.shape[1:] for a, l in shards]

    def parts(x_in, x_out, sems):
        ins = [r if l is None else r.at[l] for r, (_, l) in zip(x_in, shards)]
        send_sems, recv_sems, local_sems = sems
        me, sibling, chips = _mesh_places()

        def copy(a, k, block, to, src=None):
            dst = x_out[a].at[_slot(block)]
            return pltpu.make_async_remote_copy(
                src_ref=dst if src is None else src, dst_ref=dst,
                send_sem=send_sems.at[a * N_PEERS + k], recv_sem=recv_sems.at[a * N_PEERS + k],
                device_id=to, device_id_type=pl.DeviceIdType.MESH)

        mine = [pltpu.make_async_copy(ins[a], x_out[a].at[_slot(me)], local_sems.at[a]) for a in range(n)]
        first = []
        for a in range(n):
            first.append(copy(a, 0, me, sibling, src=ins[a]))
            first += [copy(a, 1 + j, me, (*chip, me[2]), src=ins[a]) for j, chip in enumerate(chips)]
        return me, sibling, chips, copy, mine, first

    def start(x_in, x_out, sems):
        _, _, _, _, mine, first = parts(x_in, x_out, sems)
        for cp in mine + first:
            cp.start()

    def forward(x_in, x_out, sems):
        me, sibling, chips, copy, _, _ = parts(x_in, x_out, sems)
        for j, chip in enumerate(chips):
            for a in range(n):
                copy(a, 1 + j, (*chip, me[2]), me).wait_recv()
                copy(a, 4 + j, (*chip, me[2]), sibling).start()

    def finish(x_in, x_out, sems):
        me, sibling, chips, copy, mine, first = parts(x_in, x_out, sems)
        for a in range(n):
            copy(a, 0, sibling, me).wait_recv()
        for j, chip in enumerate(chips):
            for a in range(n):
                copy(a, 4 + j, (*chip, 1 - me[2]), me).wait_recv()
        for cp in first:
            cp.wait_send()
        for j, chip in enumerate(chips):
            for a in range(n):
                copy(a, 4 + j, (*chip, me[2]), sibling).wait_send()
        for cp in mine:
            cp.wait()

    return Exchange([a for a, _ in shards],
                    [jax.ShapeDtypeStruct((N_DEV,) + tuple(s), BF16) for s in shapes],
                    _exchange_sems(n), start, finish, forward)


def _all_peers(me, chips):
    c = me[2]
    return [(me[0], me[1], 1 - c)] + [(*chip, c) for chip in chips] + [(*chip, 1 - c) for chip in chips]


def scatter_exchange(grads):
    n = len(grads)

    def parts(x_in, x_out, sems):
        send_sems, recv_sems, local_sems = sems
        me, _, chips = _mesh_places()
        mine = [pltpu.make_async_copy(x_in[a].at[_slot(me)], x_out[a].at[_slot(me)], local_sems.at[a])
                for a in range(n)]
        sends, recvs = [], []
        for a in range(n):
            for k, peer in enumerate(_all_peers(me, chips)):
                sem = dict(send_sem=send_sems.at[a * N_PEERS + k], recv_sem=recv_sems.at[a * N_PEERS + k],
                           device_id=peer, device_id_type=pl.DeviceIdType.MESH)
                sends.append(pltpu.make_async_remote_copy(
                    src_ref=x_in[a].at[_slot(peer)], dst_ref=x_out[a].at[_slot(me)], **sem))
                recvs.append(pltpu.make_async_remote_copy(
                    src_ref=x_in[a].at[_slot(peer)], dst_ref=x_out[a].at[_slot(peer)], **sem))
        return mine, sends, recvs

    def start(x_in, x_out, sems):
        mine, sends, _ = parts(x_in, x_out, sems)
        for cp in mine + sends:
            cp.start()

    def finish(x_in, x_out, sems):
        mine, sends, recvs = parts(x_in, x_out, sems)
        for cp in recvs:
            cp.wait_recv()
        for cp in sends:
            cp.wait_send()
        for cp in mine:
            cp.wait()

    return Exchange(grads, [jax.ShapeDtypeStruct(g.shape, g.dtype) for g in grads],
                    _exchange_sems(n), start, finish)


def run_exchange(exchange, name):
    n_in, n_out = len(exchange.arrays), len(exchange.out_shapes)

    def body(*refs):
        x_in, x_out, sems = refs[:n_in], refs[n_in:n_in + n_out], refs[n_in + n_out:]
        exchange.start(x_in, x_out, sems)
        if exchange.forward is not None:
            exchange.forward(x_in, x_out, sems)
        exchange.finish(x_in, x_out, sems)

    exchange.results = list(pl.pallas_call(
        body, name=name, out_shape=exchange.out_shapes, in_specs=[_ANY] * n_in, out_specs=[_ANY] * n_out,
        scratch_shapes=exchange.sems)(*exchange.arrays))
    return exchange.results


def small_all_reduce(vec):
    R = vec.shape[0]

    def body(v_ref, o_ref, all_ref, send_sems, recv_sems):
        me, _, chips = _mesh_places()
        peers = _all_peers(me, chips)
        all_ref[_slot(me)] = v_ref[...]
        sends, recvs = [], []
        for k, peer in enumerate(peers):
            sem = dict(send_sem=send_sems.at[k], recv_sem=recv_sems.at[k],
                       device_id=peer, device_id_type=pl.DeviceIdType.MESH)
            sends.append(pltpu.make_async_remote_copy(src_ref=v_ref, dst_ref=all_ref.at[_slot(me)], **sem))
            recvs.append(pltpu.make_async_remote_copy(src_ref=v_ref, dst_ref=all_ref.at[_slot(peer)], **sem))
        for cp in sends:
            cp.start()
        for cp in recvs:
            cp.wait_recv()
        for cp in sends:
            cp.wait_send()
        acc = all_ref[0]
        for s in range(1, N_DEV):
            acc = acc + all_ref[s]
        o_ref[...] = acc

    return pl.pallas_call(
        body, name="small_all_reduce",
        out_shape=jax.ShapeDtypeStruct(vec.shape, F32),
        in_specs=[pl.BlockSpec(memory_space=pltpu.VMEM)], out_specs=pl.BlockSpec(memory_space=pltpu.VMEM),
        scratch_shapes=[pltpu.VMEM((N_DEV, R, LANES), F32), pltpu.SemaphoreType.DMA((N_PEERS,)),
                        pltpu.SemaphoreType.DMA((N_PEERS,))],
    )(vec)


def _adamw_math(w, g, m, v):
    m2 = ADAM_B1 * m + (1.0 - ADAM_B1) * g
    v2 = ADAM_B2 * v + (1.0 - ADAM_B2) * (g * g)
    m_hat = m2 / (1.0 - ADAM_B1 ** ADAM_STEP)
    v_hat = v2 / (1.0 - ADAM_B2 ** ADAM_STEP)
    delta = -ADAM_LR * (m_hat / (jnp.sqrt(v_hat) + ADAM_EPS) + ADAM_WD * w)
    return delta, m2, v2


def adamw_sharded(partials, w, m, v):
    L, r, c = w.shape
    tr = _div(r, max(BF16_SUBLANES, (1 << 18) // c), BF16_SUBLANES)
    nt = r // tr

    def part_map(l0):
        return lambda l, t: (0, jnp.where(l == l0, t, jnp.where(l < l0, 0, nt - 1)), 0)

    def body(*refs):
        parts = refs[:L]
        w_ref, m_ref, v_ref, g_out, d_out, m_out, v_out = refs[L:]
        layer = pl.program_id(0)
        for l0 in range(L):
            @pl.when(layer == l0)
            def _():
                g = parts[l0][0].astype(F32)
                for s in range(1, N_DEV):
                    g = g + parts[l0][s].astype(F32)
                delta, m2, v2 = _adamw_math(w_ref[...], g, m_ref[...], v_ref[...])
                g_out[...] = g
                d_out[...] = delta
                m_out[...] = m2
                v_out[...] = v2

    own = lambda l, t: (l, t, 0)
    return _call(body, "adamw_sharded", (L, nt),
                 [(p, (N_DEV, tr, c), part_map(l0)) for l0, p in enumerate(partials)]
                 + [(w, (None, tr, c), own), (m, (None, tr, c), own), (v, (None, tr, c), own)],
                 [((L, r, c), F32, (None, tr, c), own)] * 4)


def adamw_flat(g, w, m, v):
    shape = g.shape

    def body(g_ref, w_ref, m_ref, v_ref, d_out, m_out, v_out):
        delta, m2, v2 = _adamw_math(w_ref[...], g_ref[...], m_ref[...], v_ref[...])
        d_out[...] = delta
        m_out[...] = m2
        v_out[...] = v2

    whole = lambda: (0, 0)
    return _call(body, "adamw_flat", (), [(a, shape, whole) for a in (g, w, m, v)],
                 [(shape, F32, shape, whole)] * 3)


def _pack(parts):
    flat = jnp.concatenate([p.reshape(-1).astype(F32) for p in parts])
    rows = -(-flat.shape[0] // (8 * LANES)) * 8
    return jnp.pad(flat, (0, rows * LANES - flat.shape[0])).reshape(rows, LANES)


def _unpack(packed, shapes):
    flat = packed.reshape(-1)
    out, off = [], 0
    for s in shapes:
        size = math.prod(s)
        out.append(flat[off:off + size].reshape(s))
        off += size
    return out


def kernel(x, mem, ffn1_norm, ffn1_w13, ffn1_w2, mix_norm, gmlp_w_in, gmlp_ln_g, gmlp_ln_b, gmlp_w_s, gmlp_b_s, gmlp_w_out, conv_w_in, conv_w, conv_w_out, xattn_norm, mem_norm, xattn_wq, xattn_wkv, xattn_wo, ffn2_norm, ffn2_w13, ffn2_w2, final_norm, loss_target, m_ffn1_norm, m_ffn1_w13, m_ffn1_w2, m_mix_norm, m_gmlp_w_in, m_gmlp_ln_g, m_gmlp_ln_b, m_gmlp_w_s, m_gmlp_b_s, m_gmlp_w_out, m_conv_w_in, m_conv_w, m_conv_w_out, m_xattn_norm, m_mem_norm, m_xattn_wq, m_xattn_wkv, m_xattn_wo, m_ffn2_norm, m_ffn2_w13, m_ffn2_w2, m_final_norm, v_ffn1_norm, v_ffn1_w13, v_ffn1_w2, v_mix_norm, v_gmlp_w_in, v_gmlp_ln_g, v_gmlp_ln_b, v_gmlp_w_s, v_gmlp_b_s, v_gmlp_w_out, v_conv_w_in, v_conv_w, v_conv_w_out, v_xattn_norm, v_mem_norm, v_xattn_wq, v_xattn_wkv, v_xattn_wo, v_ffn2_norm, v_ffn2_w13, v_ffn2_w2, v_final_norm):
    given = dict(locals())
    T, D = x.shape[1], x.shape[2]
    depth = ffn1_norm.shape[0]
    xs = x.reshape(T, D)
    mems = mem.reshape(mem.shape[1], D)
    target = loss_target.reshape(T, D)
    me = 4 * lax.axis_index("x") + 2 * lax.axis_index("y") + lax.axis_index("c")
    E = gmlp_ln_g.shape[1]
    gd = E // GROUPS
    cshard = conv_w.shape[2]

    bf = {k: given[k].astype(BF16) for k in
          ("ffn1_w13", "ffn1_w2", "gmlp_w_in", "gmlp_w_out", "conv_w_in", "conv_w_out",
           "xattn_wq", "xattn_wkv", "xattn_wo", "ffn2_w13", "ffn2_w2")}

    W = {}

    def gather(names_layers):
        return names_layers, gather_exchange([(bf[k], l) for k, l in names_layers])

    def landed(tagged):
        names_layers, exchange = tagged
        W.update(zip(names_layers, exchange.results))

    cw_place = lax.dynamic_update_slice(jnp.zeros((CONV_WIDTH, D), F32), conv_w[0], (jnp.int32(0), me * cshard))
    cw_full = small_all_reduce(_pack([cw_place])).reshape(-1)[:CONV_WIDTH * D].reshape(CONV_WIDTH, D)
    bias_full = jnp.repeat(gmlp_b_s[0].T, gd, axis=1)

    first = gather([("ffn1_w13", 0)])
    run_exchange(first[1], "gather_first")
    landed(first)
    saved = []
    h = xs
    for i in range(depth):
        j = i // 2
        is_gmlp = i % 2 == 0
        mix = ("gmlp_w_in", "gmlp_w_out") if is_gmlp else ("conv_w_in", "conv_w_out")
        on_up1 = gather(([("ffn1_w2", i)] if i == 0 else []) + [(mix[0], j), ("ffn2_w13", i)])
        on_down1 = gather([(mix[1], j), ("xattn_wq", i), ("xattn_wkv", i)])

        def w2_after_up(on_up=on_up1, i=i):
            landed(on_up)
            return W["ffn1_w2", i]

        h, sv1 = ffn_fwd(h, ffn1_norm[i], W["ffn1_w13", i], w2_after_up, on_up1[1], on_down1[1])
        landed(on_down1)
        hn = rms_fwd(h, mix_norm[i])
        on_mix_in = gather([("xattn_wo", i), ("ffn2_w2", i)])
        p = mm_nn(hn, W[mix[0], j], "mix_in", tn_pref=768, exchange=on_mix_in[1])
        landed(on_mix_in)
        if is_gmlp:
            gated = gmlp_mid_fwd(p, gmlp_ln_g[j:j + 1], gmlp_ln_b[j:j + 1], gmlp_w_s[j], bias_full)
        else:
            gated = conv_mid_fwd(p, cw_full)
        h_mix = mm_nn(gated, W[mix[1], j].reshape(1, gated.shape[1], D), "mix_out", F32, res=h)
        sv2 = (h, hn, p, gated)
        h, sv3 = xattn_fwd(h_mix, mems, xattn_norm[i], mem_norm[i],
                           W["xattn_wq", i], W["xattn_wkv", i], W["xattn_wo", i])
        on_up2 = gather([("ffn1_w13", i + 1), ("ffn1_w2", i + 1)]) if i + 1 < depth else None
        h, sv4 = ffn_fwd(h, ffn2_norm[i], W["ffn2_w13", i], lambda i=i: W["ffn2_w2", i],
                         None if on_up2 is None else on_up2[1])
        if on_up2 is not None:
            landed(on_up2)
        saved.append((sv1, sv2, sv3, sv4))

    loss_part, d, dys, d_final_norm = loss_head(h, final_norm, target, 0.5)

    small = {k: [None] * depth for k in ("ffn1_norm", "mix_norm", "xattn_norm", "mem_norm", "ffn2_norm")}
    partial = {}
    for i in reversed(range(depth)):
        j = i // 2
        is_gmlp = i % 2 == 0
        sv1, sv2, sv3, sv4 = saved[i]
        d, dys, small["ffn2_norm"][i], partial["ffn2_w13", i], partial["ffn2_w2", i] = ffn_bwd(
            d, dys, sv4, ffn2_norm[i], W["ffn2_w13", i], W["ffn2_w2", i], 1.0)
        (d, dys, small["xattn_norm"][i], small["mem_norm"][i], dwq,
         partial["xattn_wkv", i], partial["xattn_wo", i]) = xattn_bwd(
            d, dys, sv3, mems, xattn_norm[i], W["xattn_wq", i], W["xattn_wkv", i], W["xattn_wo", i], 1.0)
        if is_gmlp:
            mix = ("gmlp_w_in", "gmlp_w_out")
            mid = lambda p, dg, send: (lambda r: (r[0], r[1:]))(gmlp_mid_bwd(
                p, dg, gmlp_ln_g[j:j + 1], gmlp_ln_b[j:j + 1], gmlp_w_s[j], bias_full, exchange=send))
        else:
            mix = ("conv_w_in", "conv_w_out")
            mid = lambda p, dg, send: (lambda r: (r[0], r[1:]))(conv_mid_bwd(p, dg, cw_full, exchange=send))
        (d, dys, small["mix_norm"][i], partial[mix[0], j], partial[mix[1], j], partial["xattn_wq", i],
         extra) = mixer_bwd_common(d, dys, sv2, mix_norm[i], W[mix[0], j], W[mix[1], j], mid, 0.5, dwq)
        if is_gmlp:
            d_ws, d_bs_wide, d_lng, d_lnb = extra
        else:
            (d_cw,) = extra
        d, dys, small["ffn1_norm"][i], partial["ffn1_w13", i], partial["ffn1_w2", i] = ffn_bwd(
            d, dys, sv1, ffn1_norm[i], W["ffn1_w13", i], W["ffn1_w2", i], 0.5)
    grad_x = d.reshape(x.shape)

    small_grads = {k: jnp.concatenate(v, axis=0) for k, v in small.items()}
    small_grads["gmlp_ln_g"] = d_lng
    small_grads["gmlp_ln_b"] = d_lnb
    small_grads["gmlp_w_s"] = d_ws[None]
    small_grads["gmlp_b_s"] = d_bs_wide[None, :, :, 0]
    small_grads["final_norm"] = d_final_norm.reshape(-1)
    small_names = ["ffn1_norm", "mix_norm", "gmlp_ln_g", "gmlp_ln_b", "gmlp_w_s", "gmlp_b_s",
                   "xattn_norm", "mem_norm", "ffn2_norm", "final_norm"]
    summed = small_all_reduce(_pack([small_grads[k] for k in small_names] + [d_cw, loss_part]))
    parts = _unpack(summed, [given[k].shape for k in small_names] + [(CONV_WIDTH, D), (1, LANES)])
    grads = dict(zip(small_names, parts[:len(small_names)]))
    grads["conv_w"] = lax.dynamic_slice(parts[-2], (jnp.int32(0), me * cshard), (CONV_WIDTH, cshard))[None]
    loss = parts[-1][0, 0]
    flat_names = small_names + ["conv_w"]
    flat = adamw_flat(*[_pack([src[k] for k in flat_names]) for src in
                        (grads, given, {k: given["m_" + k] for k in flat_names},
                         {k: given["v_" + k] for k in flat_names})])
    delta, new_m, new_v = [dict(zip(flat_names, _unpack(f, [given[k].shape for k in flat_names]))) for f in flat]

    for k in bf:
        w = given[k]
        L = w.shape[0]
        shard = w.shape[1:]
        view = lambda a: a.reshape((L,) + shard)
        g, dl, m2, v2 = adamw_sharded([partial[k, l] for l in range(L)], w, given["m_" + k], given["v_" + k])
        grads[k], delta[k], new_m[k], new_v[k] = view(g), view(dl), view(m2), view(v2)

    order = ["ffn1_norm", "ffn1_w13", "ffn1_w2", "mix_norm", "gmlp_w_in", "gmlp_ln_g", "gmlp_ln_b", "gmlp_w_s",
             "gmlp_b_s", "gmlp_w_out", "conv_w_in", "conv_w", "conv_w_out", "xattn_norm", "mem_norm", "xattn_wq",
             "xattn_wkv", "xattn_wo", "ffn2_norm", "ffn2_w13", "ffn2_w2", "final_norm"]
    return (loss, grad_x, *[grads[k] for k in order], *[delta[k] for k in order],
            *[new_m[k] for k in order], *[new_v[k] for k in order])
```

```python
import functools
import math

import jax
import jax.numpy as jnp
from jax import lax
from jax.experimental import pallas as pl
from jax.experimental.pallas import tpu as pltpu

F32 = jnp.float32
BF16 = jnp.bfloat16

N_DEV = 8
N_PEERS = N_DEV - 1
CHUNK = 128
GROUPS = 8
HEADS = 4
CONV_WIDTH = 3
RMS_EPS = 1e-6
LN_EPS = 1e-5
ADAM_LR = 0.001
ADAM_B1 = 0.9
ADAM_B2 = 0.999
ADAM_EPS = 1e-08
ADAM_WD = 0.01
ADAM_STEP = 10
LANES = 128
BF16_SUBLANES = 16
MXU_WIDTH = 256
VMEM_LIMIT_BYTES = 56 * 1024 * 1024

_NT = (((1,), (1,)), ((), ()))
_TN = (((0,), (0,)), ((), ()))
_SQRT_HALF = 0.7071067811865476
_INV_SQRT_2PI = 0.3989422804014327


def _div(n, pref, align):
    best = None
    for t in range(align, min(n, pref) + 1, align):
        if n % t == 0:
            best = t
    return n if best is None else best


def _chunks(n, width):
    return [(c0, min(width, n - c0)) for c0 in range(0, n, width)]


_ANY = pl.BlockSpec(memory_space=pl.ANY)


class Exchange:
    def __init__(self, arrays, out_shapes, sems, start, finish, forward=None):
        self.arrays, self.out_shapes, self.sems = list(arrays), list(out_shapes), list(sems)
        self.start, self.finish, self.forward = start, finish, forward
        self.results = None


def _call(body, name, grid, ins, outs, scratch=(), exchange=None):
    in_specs = [pl.BlockSpec(*spec[1:3], **({"pipeline_mode": pl.Buffered(1)} if len(spec) > 3 else {}))
                for spec in ins]
    ins = [spec[:3] for spec in ins]
    out_specs = [pl.BlockSpec(bs, im) for _, _, bs, im in outs]
    out_shape = [jax.ShapeDtypeStruct(s, d) for s, d, _, _ in outs]
    arrays = [a for a, _, _ in ins]
    scratch = list(scratch)
    kernel_fn = body
    if exchange is not None:
        n_in, n_out, n_scr = len(ins), len(outs), len(scratch)
        n_xin, n_xout = len(exchange.arrays), len(exchange.out_shapes)
        steps = math.prod(grid)
        forward_step = min(steps - 1, (15 * steps) // 16)

        def kernel_fn(*refs):
            refs = list(refs)
            own_in, x_in = refs[:n_in], refs[n_in:n_in + n_xin]
            refs = refs[n_in + n_xin:]
            own_out, x_out = refs[:n_out], refs[n_out:n_out + n_xout]
            refs = refs[n_out + n_xout:]
            own_scr, x_sems = refs[:n_scr], refs[n_scr:]
            step = 0
            for axis, size in enumerate(grid):
                step = step * size + pl.program_id(axis)

            @pl.when(step == 0)
            def _():
                exchange.start(x_in, x_out, x_sems)

            if exchange.forward is not None:
                @pl.when(step == forward_step)
                def _():
                    exchange.forward(x_in, x_out, x_sems)

            body(*own_in, *own_out, *own_scr)

            @pl.when(step == steps - 1)
            def _():
                exchange.finish(x_in, x_out, x_sems)

        in_specs += [_ANY] * n_xin
        out_specs += [_ANY] * n_xout
        out_shape += exchange.out_shapes
        arrays += exchange.arrays
        scratch += exchange.sems
    res = pl.pallas_call(
        kernel_fn,
        name=name,
        grid=grid,
        in_specs=in_specs,
        out_specs=out_specs,
        out_shape=out_shape,
        scratch_shapes=scratch,
        compiler_params=pltpu.CompilerParams(
            dimension_semantics=("arbitrary",) * len(grid), vmem_limit_bytes=VMEM_LIMIT_BYTES),
    )(*arrays)
    if exchange is not None:
        exchange.results = list(res[len(outs):])
        res = res[:len(outs)]
    return res


def _dot(a, b, dims=None):
    if dims is None:
        return jnp.dot(a, b, preferred_element_type=F32)
    return lax.dot_general(a, b, dims, preferred_element_type=F32)


def _sigmoid(v):
    return 1.0 / (1.0 + jnp.exp(-v))


def _normal_cdf(v):
    return 0.5 * (1.0 + lax.erf(v * _SQRT_HALF))


def _normal_pdf(v):
    return _INV_SQRT_2PI * jnp.exp(-0.5 * v * v)


def _accumulate(ref, part, first):
    @pl.when(first)
    def _():
        ref[...] = part

    @pl.when(jnp.logical_not(first))
    def _():
        ref[...] += part


def rms_fwd(x, g):
    T, D = x.shape
    tt = _div(T, 512, BF16_SUBLANES)

    def body(x_ref, g_ref, o_ref):
        xv = x_ref[...]
        r = lax.rsqrt(jnp.mean(xv * xv, axis=-1, keepdims=True) + RMS_EPS)
        o_ref[...] = ((xv * r) * g_ref[...]).astype(BF16)

    return _call(body, "rms_fwd", (T // tt,),
                 [(x, (tt, D), lambda i: (i, 0)), (g.reshape(1, D), (1, D), lambda i: (0, 0))],
                 [((T, D), BF16, (tt, D), lambda i: (i, 0))])[0]


def rms_bwd(x, g, dxn, d, scale):
    T, D = x.shape
    tt = _div(T, 256, BF16_SUBLANES)

    def body(x_ref, g_ref, dn_ref, d_ref, dx_ref, dxs_ref, dg_ref):
        xv = x_ref[...]
        r = lax.rsqrt(jnp.mean(xv * xv, axis=-1, keepdims=True) + RMS_EPS)
        xh = xv * r
        dn = dn_ref[...]
        dxh = dn * g_ref[...]
        dx = r * (dxh - xh * jnp.mean(dxh * xh, axis=-1, keepdims=True)) + d_ref[...]
        dx_ref[...] = dx
        dxs_ref[...] = (scale * dx).astype(BF16)
        _accumulate(dg_ref, jnp.sum(dn * xh, axis=0, keepdims=True), pl.program_id(0) == 0)

    row = lambda i: (i, 0)
    fix = lambda i: (0, 0)
    return _call(body, "rms_bwd", (T // tt,),
                 [(x, (tt, D), row), (g.reshape(1, D), (1, D), fix), (dxn, (tt, D), row), (d, (tt, D), row)],
                 [((T, D), F32, (tt, D), row), ((T, D), BF16, (tt, D), row), ((1, D), F32, (1, D), fix)])


def rms_gain_grad(x, dxn):
    T, D = x.shape
    tt = _div(T, 256, 8)

    def body(x_ref, dn_ref, dg_ref):
        xv = x_ref[...]
        r = lax.rsqrt(jnp.mean(xv * xv, axis=-1, keepdims=True) + RMS_EPS)
        _accumulate(dg_ref, jnp.sum(dn_ref[...] * (xv * r), axis=0, keepdims=True), pl.program_id(0) == 0)

    row = lambda i: (i, 0)
    return _call(body, "rms_gain_grad", (T // tt,), [(x, (tt, D), row), (dxn, (tt, D), row)],
                 [((1, D), F32, (1, D), lambda i: (0, 0))])[0]


def loss_head(x, g, target, scale):
    T, D = x.shape
    tt = _div(T, 256, BF16_SUBLANES)

    def body(x_ref, g_ref, t_ref, loss_ref, dx_ref, dxs_ref, dg_ref):
        first = pl.program_id(0) == 0
        xv = x_ref[...]
        gv = g_ref[...]
        r = lax.rsqrt(jnp.mean(xv * xv, axis=-1, keepdims=True) + RMS_EPS)
        xh = xv * r
        err = xh * gv - t_ref[...]
        part = 0.5 * jnp.sum(jnp.mean(err * err, axis=-1, keepdims=True), axis=0, keepdims=True)
        _accumulate(loss_ref, jnp.broadcast_to(part, (1, LANES)), first)
        dy = err * (1.0 / D)
        dxh = dy * gv
        dx = r * (dxh - xh * jnp.mean(dxh * xh, axis=-1, keepdims=True))
        dx_ref[...] = dx
        dxs_ref[...] = (scale * dx).astype(BF16)
        _accumulate(dg_ref, jnp.sum(dy * xh, axis=0, keepdims=True), first)

    row = lambda i: (i, 0)
    fix = lambda i: (0, 0)
    return _call(body, "loss_head", (T // tt,),
                 [(x, (tt, D), row), (g.reshape(1, D), (1, D), fix), (target, (tt, D), row)],
                 [((1, LANES), F32, (1, LANES), fix), ((T, D), F32, (tt, D), row),
                  ((T, D), BF16, (tt, D), row), ((1, D), F32, (1, D), fix)])


def mm_nn(a, w3, name, out_dtype=BF16, res=None, res_scale=1.0, tm_pref=1024, tn_pref=1024, exchange=None):
    M, K = a.shape
    nb, _, ns = w3.shape
    tn = _div(ns, tn_pref, LANES)
    per = ns // tn
    tm = _div(M, tm_pref, BF16_SUBLANES)
    ins = [(a, (tm, K), lambda j, m: (m, 0)), (w3, (None, K, tn), lambda j, m: (j // per, 0, j % per))]
    if res is not None:
        ins.append((res, (tm, tn), lambda j, m: (m, j)))

    def body(*refs):
        a_ref, w_ref = refs[0], refs[1]
        o_ref = refs[-1]
        acc = _dot(a_ref[...], w_ref[...])
        if res is not None:
            acc = refs[2][...] + res_scale * acc
        o_ref[...] = acc.astype(o_ref.dtype)

    return _call(body, name, (nb * per, M // tm), ins,
                 [((M, nb * ns), out_dtype, (tm, tn), lambda j, m: (m, j))], exchange=exchange)[0]


def mm_nt(a_in, w3, name, out_dtype, M, tm_pref=1024, to_pref=2048, group=1, exchange=None):
    nb, Ko, ns = w3.shape
    to = _div(Ko, to_pref, LANES)
    tm = _div(M, tm_pref, BF16_SUBLANES)
    if isinstance(a_in, tuple):
        a, a_bs, a_im = a_in
        a_bs = tuple(tm if s == "tm" else s for s in a_bs)
    else:
        a, a_bs, a_im = a_in, (tm, group * ns), lambda m, o, b: (m, b)
    steps = nb // group
    if steps > 1:
        assert out_dtype == F32

    def body(a_ref, w_ref, o_ref):
        p = None
        for k in range(group):
            pk = _dot(a_ref[:, k * ns:(k + 1) * ns], w_ref[k], _NT)
            p = pk if p is None else p + pk
        if steps == 1:
            o_ref[...] = p.astype(o_ref.dtype)
        else:
            _accumulate(o_ref, p, pl.program_id(2) == 0)

    return _call(body, name, (M // tm, Ko // to, steps),
                 [(a, a_bs, a_im), (w3, (group, to, ns), lambda m, o, b: (b, o, 0))],
                 [((M, Ko), out_dtype, (tm, to), lambda m, o, b: (m, o))], exchange=exchange)[0]


def mm_nt_rms_bwd(a, w3, name, x, g, d, scale, exchange=None):
    nb, D, ns = w3.shape
    M = x.shape[0]
    tm = _div(M, 512, BF16_SUBLANES)
    a_bs, a_im = (tm, ns), lambda m, b: (m, b)

    rc = _div(tm, 64, BF16_SUBLANES)

    def body(a_ref, w_ref, x_ref, g_ref, d_ref, dx_ref, dxs_ref, dg_ref, acc_ref):
        m, b = pl.program_id(0), pl.program_id(1)
        _accumulate(acc_ref, _dot(a_ref[...], w_ref[...], _NT), b == 0)

        @pl.when(b == nb - 1)
        def _():
            gv = g_ref[...]

            def piece(c, dg):
                rows = pl.ds(pl.multiple_of(c * rc, rc), rc)
                dn = acc_ref[rows, :]
                xv = x_ref[rows, :]
                r = lax.rsqrt(jnp.mean(xv * xv, axis=-1, keepdims=True) + RMS_EPS)
                xh = xv * r
                dxh = dn * gv
                dx = r * (dxh - xh * jnp.mean(dxh * xh, axis=-1, keepdims=True)) + d_ref[rows, :]
                dx_ref[rows, :] = dx
                dxs_ref[rows, :] = (scale * dx).astype(BF16)
                return dg + jnp.sum(dn * xh, axis=0, keepdims=True)

            _accumulate(dg_ref, lax.fori_loop(0, tm // rc, piece, jnp.zeros((1, D), F32)), m == 0)

    row = lambda m, b: (m, 0)
    fix = lambda m, b: (0, 0)
    w_spec = (w3, (None, D, ns), lambda m, b: (b, 0, 0)) + (("single",) if nb == 1 else ())
    return _call(body, name, (M // tm, nb),
                 [(a, a_bs, a_im), w_spec,
                  (x, (tm, D), row), (g.reshape(1, D), (1, D), fix), (d, (tm, D), row)],
                 [((M, D), F32, (tm, D), row), ((M, D), BF16, (tm, D), row), ((1, D), F32, (1, D), fix)],
                 scratch=[pltpu.VMEM((tm, D), F32)], exchange=exchange)


def mm_tn(a, b_in, name, nbo, ns, tka_pref=1024, tt_pref=2048, tn_pref=2048, exchange=None):
    T, Ka = a.shape
    tt = _div(T, tt_pref, BF16_SUBLANES)
    tka = _div(Ka, tka_pref, LANES)
    if isinstance(b_in, tuple):
        b, b_bs, b_im = b_in
        b_bs = tuple(tt if s == "tt" else s for s in b_bs)
        tn, per = ns, 1
    else:
        tn = _div(ns, tn_pref, LANES)
        per = ns // tn
        b, b_bs, b_im = b_in, (tt, tn), lambda i, j, t: (t, j)
    nt = T // tt

    def body(a_ref, b_ref, o_ref, acc_ref):
        t = pl.program_id(2)
        _accumulate(acc_ref, _dot(a_ref[...], b_ref[...], _TN), t == 0)

        @pl.when(t == nt - 1)
        def _():
            o_ref[...] = acc_ref[...].astype(BF16)

    return _call(body, name, (Ka // tka, nbo * per, nt),
                 [(a, (tt, tka), lambda i, j, t: (t, i)), (b, b_bs, b_im)],
                 [((nbo, Ka, ns), BF16, (None, tka, tn), lambda i, j, t: (j // per, i, j % per))],
                 scratch=[pltpu.VMEM((tka, tn), F32)], exchange=exchange)[0]


def ffn_up(xn, w13, exchange=None):
    T, D = xn.shape
    nb, _, ns = w13.shape
    half = nb // 2
    F = half * ns
    tm = _div(T, 512, BF16_SUBLANES)
    pair = 2 if half % 2 == 0 else 1

    def columns(w_ref, c0, cw):
        k, off = divmod(c0, ns)
        if off + cw <= ns:
            return w_ref[k, :, off:off + cw]
        return jnp.concatenate([w_ref[k, :, off:ns], w_ref[k + 1, :, 0:off + cw - ns]], axis=1)

    def body(x_ref, wg_ref, wu_ref, fac_ref, act_ref):
        xv = x_ref[...]
        for c0, cw in _chunks(pair * ns, MXU_WIDTH):
            cols = slice(c0, c0 + cw)
            gate = _dot(xv, columns(wg_ref, c0, cw))
            up = _dot(xv, columns(wu_ref, c0, cw))
            s = _sigmoid(gate)
            silu = gate * s
            fac_ref[0, :, cols] = (up * (s * (1.0 + gate * (1.0 - s)))).astype(BF16)
            fac_ref[1, :, cols] = silu.astype(BF16)
            act_ref[:, cols] = (silu * up).astype(BF16)

    tn = pair * ns
    return _call(body, "ffn_up", (half // pair, T // tm),
                 [(xn, (tm, D), lambda j, m: (m, 0)),
                  (w13, (pair, D, ns), lambda j, m: (j, 0, 0), "single"),
                  (w13, (pair, D, ns), lambda j, m: (j + half // pair, 0, 0), "single")],
                 [((2, T, F), BF16, (2, tm, tn), lambda j, m: (0, m, j)),
                  ((T, F), BF16, (tm, tn), lambda j, m: (m, j))], exchange=exchange)


def ffn_dact(dy, w2, fac, exchange=None):
    T, D = dy.shape
    F = w2.shape[0]
    tm = _div(T, 512, BF16_SUBLANES)
    tn = _div(F, F // 2, MXU_WIDTH)

    def body(dy_ref, w_ref, fac_ref, dh_ref):
        dyv = dy_ref[...]
        for c0, cw in _chunks(tn, MXU_WIDTH):
            cols = slice(c0, c0 + cw)
            da = _dot(dyv, w_ref[cols, :], _NT)
            dh_ref[0, :, cols] = (da * fac_ref[0, :, cols].astype(F32)).astype(BF16)
            dh_ref[1, :, cols] = (da * fac_ref[1, :, cols].astype(F32)).astype(BF16)

    return _call(body, "ffn_dact", (F // tn, T // tm),
                 [(dy, (tm, D), lambda j, m: (m, 0)), (w2, (tn, D), lambda j, m: (j, 0), "single"),
                  (fac, (2, tm, tn), lambda j, m: (0, m, j))],
                 [((2, T, F), BF16, (2, tm, tn), lambda j, m: (0, m, j))], exchange=exchange)[0]


def ffn_fwd(x, norm_g, w13, get_w2, up_exchange=None, down_exchange=None):
    xn = rms_fwd(x, norm_g)
    fac, act = ffn_up(xn, w13, exchange=up_exchange)
    F = act.shape[1]
    y = mm_nn(act, get_w2().reshape(1, F, -1), "ffn_down", F32, res=x, res_scale=0.5, tm_pref=512,
              exchange=down_exchange)
    return y, (x, xn, fac, act)


def ffn_bwd(d, dys, saved, norm_g, w13, w2, scale_out):
    x, xn, fac, act = saved
    T, D = x.shape
    nb, _, ns = w13.shape
    half = nb // 2
    F = half * ns
    dh = ffn_dact(dys, w2.reshape(F, D), fac)
    dw2 = mm_tn(act, dys, "ffn_dw2", 1, D, tka_pref=ns, tn_pref=1024)
    send_w2 = scatter_exchange([dw2.reshape(N_DEV, F // N_DEV, D)])
    dw13 = mm_tn(xn, (dh, (None, "tt", ns), lambda i, j, t: (j // half, t, j % half)), "ffn_dw13", nb, ns,
                 exchange=send_w2)
    send_w13 = scatter_exchange([dw13])
    pair = 2 if half % 2 == 0 else 1
    per = half // pair
    dxn = mm_nt((dh, (None, "tm", pair * ns), lambda m, o, b: (b // per, m, b % per)), w13, "ffn_dxn", F32, T,
                tm_pref=512, group=pair, exchange=send_w13)
    dx, dxs, dg = rms_bwd(x, norm_g, dxn, d, scale_out)
    return dx, dxs, dg, send_w13.results[0], send_w2.results[0]


def _gmlp_parts(p_ref, lng_ref, lnb_ref):
    E = lng_ref.shape[-1]
    pv = p_ref[...].astype(F32)
    cdf = _normal_cdf(pv)
    z = pv * cdf
    u = z[:, :E]
    vp = z[:, E:]
    mu = jnp.mean(vp, axis=-1, keepdims=True)
    xc = vp - mu
    rstd = lax.rsqrt(jnp.mean(xc * xc, axis=-1, keepdims=True) + LN_EPS)
    vh = xc * rstd
    v = vh * lng_ref[...] + lnb_ref[...]
    return u, vh, rstd, v, pv, cdf


def _causal_ws(ws_ref, g):
    keep = lax.broadcasted_iota(jnp.int32, (CHUNK, CHUNK), 0) >= lax.broadcasted_iota(jnp.int32, (CHUNK, CHUNK), 1)
    return jnp.where(keep, ws_ref[g], 0.0).astype(BF16), keep


def gmlp_mid_fwd(p, ln_g, ln_b, w_s, bias_full):
    T, E2 = p.shape
    E = E2 // 2
    gd = E // GROUPS
    tm = _div(T, 256, CHUNK)
    fix2 = lambda i: (0, 0)

    def body(p_ref, lng_ref, lnb_ref, ws_ref, bias_ref, o_ref):
        u, _, _, v, _, _ = _gmlp_parts(p_ref, lng_ref, lnb_ref)
        vb = v.astype(BF16)
        for g in range(GROUPS):
            wm, _ = _causal_ws(ws_ref, g)
            cols = slice(g * gd, (g + 1) * gd)
            for c in range(tm // CHUNK):
                rows = slice(c * CHUNK, (c + 1) * CHUNK)
                f = _dot(wm, vb[rows, cols]) + bias_ref[:, cols]
                o_ref[rows, cols] = (u[rows, cols] * f).astype(BF16)

    return _call(body, "gmlp_mid_fwd", (T // tm,),
                 [(p, (tm, E2), lambda i: (i, 0)), (ln_g, (1, E), fix2), (ln_b, (1, E), fix2),
                  (w_s, (GROUPS, CHUNK, CHUNK), lambda i: (0, 0, 0)), (bias_full, (CHUNK, E), fix2)],
                 [((T, E), BF16, (tm, E), lambda i: (i, 0))])[0]


def gmlp_mid_bwd(p, dgated, ln_g, ln_b, w_s, bias_full, exchange=None):
    T, E2 = p.shape
    E = E2 // 2
    gd = E // GROUPS
    tm = _div(T, 256, CHUNK)
    nsteps = T // tm
    fix2 = lambda i: (0, 0)
    fix3 = lambda i: (0, 0, 0)

    def body(p_ref, dg_ref, lng_ref, lnb_ref, ws_ref, bias_ref,
             dp_ref, dws_ref, dbs_ref, dlng_ref, dlnb_ref, f_sc, dv_sc, db_sc):
        i = pl.program_id(0)
        first = i == 0
        u, vh, rstd, v, pv, cdf = _gmlp_parts(p_ref, lng_ref, lnb_ref)
        vb = v.astype(BF16)
        dgt = dg_ref[...].astype(F32)
        df = dgt * u
        dfb = df.astype(BF16)
        for g in range(GROUPS):
            wm, keep = _causal_ws(ws_ref, g)
            cols = slice(g * gd, (g + 1) * gd)
            dw = None
            dbg = None
            for c in range(tm // CHUNK):
                rows = slice(c * CHUNK, (c + 1) * CHUNK)
                f_sc[rows, cols] = _dot(wm, vb[rows, cols]) + bias_ref[:, cols]
                dv_sc[rows, cols] = _dot(wm, dfb[rows, cols], _TN)
                part = _dot(dfb[rows, cols], vb[rows, cols], _NT)
                dw = part if dw is None else dw + part
                dbg = df[rows, cols] if dbg is None else dbg + df[rows, cols]
            dw = jnp.where(keep, dw, 0.0)

            @pl.when(first)
            def _():
                dws_ref[g] = dw
                db_sc[:, cols] = dbg

            @pl.when(jnp.logical_not(first))
            def _():
                dws_ref[g] += dw
                db_sc[:, cols] += dbg

        du = dgt * f_sc[...]
        dv = dv_sc[...]
        _accumulate(dlng_ref, jnp.sum(dv * vh, axis=0, keepdims=True), first)
        _accumulate(dlnb_ref, jnp.sum(dv, axis=0, keepdims=True), first)
        dvh = dv * lng_ref[...]
        dvp = rstd * (dvh - jnp.mean(dvh, axis=-1, keepdims=True)
                      - vh * jnp.mean(dvh * vh, axis=-1, keepdims=True))
        gp = cdf + pv * _normal_pdf(pv)
        dp_ref[:, :E] = (du * gp[:, :E]).astype(BF16)
        dp_ref[:, E:] = (dvp * gp[:, E:]).astype(BF16)

        @pl.when(i == nsteps - 1)
        def _():
            for g in range(GROUPS):
                tot = jnp.sum(db_sc[:, g * gd:(g + 1) * gd], axis=-1, keepdims=True)
                dbs_ref[g] = jnp.broadcast_to(tot, (CHUNK, LANES))

    return _call(body, "gmlp_mid_bwd", (nsteps,),
                 [(p, (tm, E2), lambda i: (i, 0)), (dgated, (tm, E), lambda i: (i, 0)),
                  (ln_g, (1, E), fix2), (ln_b, (1, E), fix2),
                  (w_s, (GROUPS, CHUNK, CHUNK), fix3), (bias_full, (CHUNK, E), fix2)],
                 [((T, E2), BF16, (tm, E2), lambda i: (i, 0)),
                  ((GROUPS, CHUNK, CHUNK), F32, (GROUPS, CHUNK, CHUNK), fix3),
                  ((GROUPS, CHUNK, LANES), F32, (GROUPS, CHUNK, LANES), fix3),
                  ((1, E), F32, (1, E), fix2), ((1, E), F32, (1, E), fix2)],
                 scratch=[pltpu.VMEM((tm, E), F32), pltpu.VMEM((tm, E), F32), pltpu.VMEM((CHUNK, E), F32)],
                 exchange=exchange)


HALO = 16


def _row_of(block, r):
    rows = lax.broadcasted_iota(jnp.int32, block.shape, 0)
    return jnp.sum(jnp.where(rows == r, block, 0.0), axis=0, keepdims=True)


def _shift_down(z, k, fill):
    out = pltpu.roll(z, k, 0)
    rows = lax.broadcasted_iota(jnp.int32, z.shape, 0)
    for t in range(k):
        out = jnp.where(rows == t, fill[t], out)
    return out


def _shift_up(z, k, fill):
    n = z.shape[0]
    out = pltpu.roll(z, n - k, 0)
    rows = lax.broadcasted_iota(jnp.int32, z.shape, 0)
    for j in range(k):
        out = jnp.where(rows == n - k + j, fill[j], out)
    return out


def _conv_parts(p_ref, prev_ref, cw_ref, is_first):
    D = cw_ref.shape[-1]
    pv = p_ref[...].astype(F32)
    bg, cg, val = pv[:, :D], pv[:, D:2 * D], pv[:, 2 * D:]
    z = cg * val
    pp = prev_ref[...].astype(F32)
    zp = jnp.where(is_first, 0.0, pp[:, D:2 * D] * pp[:, 2 * D:])
    zl1 = _row_of(zp, HALO - 1)
    zl2 = _row_of(zp, HALO - 2)
    z1 = _shift_down(z, 1, [zl1])
    z2 = _shift_down(z, 2, [zl2, zl1])
    conv = z2 * cw_ref[0:1, :] + z1 * cw_ref[1:2, :] + z * cw_ref[2:3, :]
    return bg, cg, val, z, z1, z2, conv


def conv_mid_fwd(p, cw):
    T, D3 = p.shape
    D = D3 // 3
    tm = _div(T, 256, HALO)
    per = tm // HALO

    def body(p_ref, prev_ref, cw_ref, o_ref):
        bg, _, _, _, _, _, conv = _conv_parts(p_ref, prev_ref, cw_ref, pl.program_id(0) == 0)
        o_ref[...] = (bg * conv).astype(BF16)

    return _call(body, "conv_mid_fwd", (T // tm,),
                 [(p, (tm, D3), lambda i: (i, 0)),
                  (p, (HALO, D3), lambda i: (jnp.maximum(i * per - 1, 0), 0)),
                  (cw, (CONV_WIDTH, D), lambda i: (0, 0))],
                 [((T, D), BF16, (tm, D), lambda i: (i, 0))])[0]


def conv_mid_bwd(p, dgated, cw, exchange=None):
    T, D3 = p.shape
    D = D3 // 3
    tm = _div(T, 256, HALO)
    per = tm // HALO
    nsteps = T // tm
    last_halo = T // HALO - 1
    nxt = lambda i: (jnp.minimum((i + 1) * per, last_halo), 0)

    def body(p_ref, prev_ref, next_ref, dg_ref, dgn_ref, cw_ref, dp_ref, dcw_ref):
        i = pl.program_id(0)
        bg, cg, val, z, z1, z2, conv = _conv_parts(p_ref, prev_ref, cw_ref, i == 0)
        dgt = dg_ref[...].astype(F32)
        dconv = dgt * bg
        dcn = jnp.where(i == nsteps - 1, 0.0, dgn_ref[...].astype(F32) * next_ref[:, :D].astype(F32))
        n0 = _row_of(dcn, 0)
        n1 = _row_of(dcn, 1)
        up1 = _shift_up(dconv, 1, [n0])
        up2 = _shift_up(dconv, 2, [n0, n1])
        dz = dconv * cw_ref[2:3, :] + up1 * cw_ref[1:2, :] + up2 * cw_ref[0:1, :]
        dp_ref[:, :D] = (dgt * conv).astype(BF16)
        dp_ref[:, D:2 * D] = (dz * val).astype(BF16)
        dp_ref[:, 2 * D:] = (dz * cg).astype(BF16)
        first = i == 0
        parts = (jnp.sum(dconv * z2, axis=0, keepdims=True), jnp.sum(dconv * z1, axis=0, keepdims=True),
                 jnp.sum(dconv * z, axis=0, keepdims=True))

        @pl.when(first)
        def _():
            for k in range(CONV_WIDTH):
                dcw_ref[k:k + 1, :] = parts[k]

        @pl.when(jnp.logical_not(first))
        def _():
            for k in range(CONV_WIDTH):
                dcw_ref[k:k + 1, :] += parts[k]

    return _call(body, "conv_mid_bwd", (nsteps,),
                 [(p, (tm, D3), lambda i: (i, 0)),
                  (p, (HALO, D3), lambda i: (jnp.maximum(i * per - 1, 0), 0)),
                  (p, (HALO, D3), nxt),
                  (dgated, (tm, D), lambda i: (i, 0)),
                  (dgated, (HALO, D), nxt),
                  (cw, (CONV_WIDTH, D), lambda i: (0, 0))],
                 [((T, D3), BF16, (tm, D3), lambda i: (i, 0)),
                  ((CONV_WIDTH, D), F32, (CONV_WIDTH, D), lambda i: (0, 0))], exchange=exchange)


def mixer_bwd_common(d, dys, saved, norm_g, w_in, w_out, mid_bwd, scale_out, dwq):
    x, hn, p, gated = saved
    T, D = x.shape
    E = gated.shape[1]
    w_out3 = w_out.reshape(1, E, D)
    dgated = mm_nt(dys, w_out3, "mix_dgated", BF16, T)
    send_wq = scatter_exchange([dwq])
    dw_out = mm_tn(gated, dys, "mix_dwout", 1, D, exchange=send_wq)
    send_wout = scatter_exchange([dw_out.reshape(N_DEV, E // N_DEV, D)])
    dp, extra = mid_bwd(p, dgated, send_wout)
    nb, _, ns = w_in.shape
    dw_in = mm_tn(hn, dp, "mix_dwin", nb, ns)
    send_win = scatter_exchange([dw_in])
    group = max(g for g in (1, 2, 4, 8) if nb % g == 0 and g * ns <= D)
    dhn = mm_nt(dp, w_in, "mix_dhn", F32, T, tm_pref=512, group=group, exchange=send_win)
    dx, dxs, dg = rms_bwd(x, norm_g, dhn, d, scale_out)
    return dx, dxs, dg, send_win.results[0], send_wout.results[0], send_wq.results[0], extra


def _softmax_rows(s):
    e = jnp.exp(s - jnp.max(s, axis=-1, keepdims=True))
    return e / jnp.sum(e, axis=-1, keepdims=True)


def attn_fwd(q, kv):
    T, D = q.shape
    M = kv.shape[0]
    hd = D // HEADS
    scale = hd ** -0.5
    tm = _div(T, 512, BF16_SUBLANES)

    def body(q_ref, kv_ref, o_ref):
        for h in range(HEADS):
            cols = slice(h * hd, (h + 1) * hd)
            s = _dot(q_ref[:, cols], kv_ref[:, cols], _NT) * scale
            pr = _softmax_rows(s).astype(BF16)
            o_ref[:, cols] = _dot(pr, kv_ref[:, D + h * hd:D + (h + 1) * hd]).astype(BF16)

    return _call(body, "attn_fwd", (T // tm,),
                 [(q, (tm, D), lambda i: (i, 0)), (kv, (M, 2 * D), lambda i: (0, 0))],
                 [((T, D), BF16, (tm, D), lambda i: (i, 0))])[0]


def attn_bwd(q, do, kv, exchange=None):
    T, D = q.shape
    M = kv.shape[0]
    hd = D // HEADS
    scale = hd ** -0.5
    tm = _div(T, 512, BF16_SUBLANES)
    nsteps = T // tm

    def body(q_ref, do_ref, kv_ref, dq_ref, dkv_ref, acc_ref):
        i = pl.program_id(0)
        for h in range(HEADS):
            cols = slice(h * hd, (h + 1) * hd)
            vcols = slice(D + h * hd, D + (h + 1) * hd)
            qh = q_ref[:, cols]
            kh = kv_ref[:, cols]
            doh = do_ref[:, cols]
            pr = _softmax_rows(_dot(qh, kh, _NT) * scale)
            dpr = _dot(doh, kv_ref[:, vcols], _NT)
            ds = (pr * (dpr - jnp.sum(dpr * pr, axis=-1, keepdims=True)) * scale).astype(BF16)
            dq_ref[:, cols] = _dot(ds, kh).astype(BF16)
            dk = _dot(ds, qh, _TN)
            dv = _dot(pr.astype(BF16), doh, _TN)

            @pl.when(i == 0)
            def _():
                acc_ref[:, cols] = dk
                acc_ref[:, vcols] = dv

            @pl.when(i > 0)
            def _():
                acc_ref[:, cols] += dk
                acc_ref[:, vcols] += dv

        @pl.when(i == nsteps - 1)
        def _():
            dkv_ref[...] = acc_ref[...].astype(BF16)

    return _call(body, "attn_bwd", (nsteps,),
                 [(q, (tm, D), lambda i: (i, 0)), (do, (tm, D), lambda i: (i, 0)),
                  (kv, (M, 2 * D), lambda i: (0, 0))],
                 [((T, D), BF16, (tm, D), lambda i: (i, 0)), ((M, 2 * D), BF16, (M, 2 * D), lambda i: (0, 0))],
                 scratch=[pltpu.VMEM((M, 2 * D), F32)], exchange=exchange)


def xattn_fwd(x, mem, xnorm_g, mnorm_g, wq, wkv, wo):
    D = x.shape[1]
    hq = rms_fwd(x, xnorm_g)
    mn = rms_fwd(mem, mnorm_g)
    q = mm_nn(hq, wq.reshape(1, D, D), "xattn_q")
    kv = mm_nn(mn, wkv, "xattn_kv")
    o = attn_fwd(q, kv)
    y = mm_nn(o, wo.reshape(1, D, D), "xattn_out", F32, res=x)
    return y, (x, hq, mn, q, kv, o)


def xattn_bwd(d, dys, saved, mem, xnorm_g, wq, wkv, wo, scale_out):
    x, hq, mn, q, kv, o = saved
    T, D = x.shape
    M = mem.shape[0]
    do = mm_nt(dys, wo.reshape(1, D, D), "xattn_do", BF16, T)
    rows = D // N_DEV
    dwo = mm_tn(o, dys, "xattn_dwo", 1, D)
    send_wo = scatter_exchange([dwo.reshape(N_DEV, rows, D)])
    dq, dkv = attn_bwd(q, do, kv, exchange=send_wo)
    nb, _, ns = wkv.shape
    dwkv = mm_tn(mn, dkv, "xattn_dwkv", nb, ns)
    dmn = mm_nt(dkv, wkv, "xattn_dmn", F32, M)
    dgm = rms_gain_grad(mem, dmn)
    dwq = mm_tn(hq, dq, "xattn_dwq", 1, D)
    send_wkv = scatter_exchange([dwkv])
    dx, dxs, dgx = mm_nt_rms_bwd(dq, wq.reshape(1, D, D), "xattn_dhq", x, xnorm_g, d, scale_out,
                                 exchange=send_wkv)
    return dx, dxs, dgx, dgm, dwq.reshape(N_DEV, rows, D), send_wkv.results[0], send_wo.results[0]


def _mesh_places():
    x, y, c = lax.axis_index("x"), lax.axis_index("y"), lax.axis_index("c")
    chips = [(1 - x, y), (x, 1 - y), (1 - x, 1 - y)]
    return (x, y, c), (x, y, 1 - c), chips


def _slot(place):
    return 4 * place[0] + 2 * place[1] + place[2]


def _exchange_sems(n):
    return [pltpu.SemaphoreType.DMA((n * N_PEERS,)), pltpu.SemaphoreType.DMA((n * N_PEERS,)),
            pltpu.SemaphoreType.DMA((n,))]


def gather_exchange(shards):
    n = len(shards)
    shapes = [a.shape if l is None else a.shape[1:] for a, l in shards]

    def parts(x_in, x_out, sems):
        ins = [r if l is None else r.at[l] for r, (_, l) in zip(x_in, shards)]
        send_sems, recv_sems, local_sems = sems
        me, sibling, chips = _mesh_places()

        def copy(a, k, block, to, src=None):
            dst = x_out[a].at[_slot(block)]
            return pltpu.make_async_remote_copy(
                src_ref=dst if src is None else src, dst_ref=dst,
                send_sem=send_sems.at[a * N_PEERS + k], recv_sem=recv_sems.at[a * N_PEERS + k],
                device_id=to, device_id_type=pl.DeviceIdType.MESH)

        mine = [pltpu.make_async_copy(ins[a], x_out[a].at[_slot(me)], local_sems.at[a]) for a in range(n)]
        first = []
        for a in range(n):
            first.append(copy(a, 0, me, sibling, src=ins[a]))
            first += [copy(a, 1 + j, me, (*chip, me[2]), src=ins[a]) for j, chip in enumerate(chips)]
        return me, sibling, chips, copy, mine, first

    def start(x_in, x_out, sems):
        _, _, _, _, mine, first = parts(x_in, x_out, sems)
        for cp in mine + first:
            cp.start()

    def forward(x_in, x_out, sems):
        me, sibling, chips, copy, _, _ = parts(x_in, x_out, sems)
        for j, chip in enumerate(chips):
            for a in range(n):
                copy(a, 1 + j, (*chip, me[2]), me).wait_recv()
                copy(a, 4 + j, (*chip, me[2]), sibling).start()

    def finish(x_in, x_out, sems):
        me, sibling, chips, copy, mine, first = parts(x_in, x_out, sems)
        for a in range(n):
            copy(a, 0, sibling, me).wait_recv()
        for j, chip in enumerate(chips):
            for a in range(n):
                copy(a, 4 + j, (*chip, 1 - me[2]), me).wait_recv()
        for cp in first:
            cp.wait_send()
        for j, chip in enumerate(chips):
            for a in range(n):
                copy(a, 4 + j, (*chip, me[2]), sibling).wait_send()
        for cp in mine:
            cp.wait()

    return Exchange([a for a, _ in shards],
                    [jax.ShapeDtypeStruct((N_DEV,) + tuple(s), BF16) for s in shapes],
                    _exchange_sems(n), start, finish, forward)


def _all_peers(me, chips):
    c = me[2]
    return [(me[0], me[1], 1 - c)] + [(*chip, c) for chip in chips] + [(*chip, 1 - c) for chip in chips]


def scatter_exchange(grads):
    n = len(grads)

    def parts(x_in, x_out, sems):
        send_sems, recv_sems, local_sems = sems
        me, _, chips = _mesh_places()
        mine = [pltpu.make_async_copy(x_in[a].at[_slot(me)], x_out[a].at[_slot(me)], local_sems.at[a])
                for a in range(n)]
        sends, recvs = [], []
        for a in range(n):
            for k, peer in enumerate(_all_peers(me, chips)):
                sem = dict(send_sem=send_sems.at[a * N_PEERS + k], recv_sem=recv_sems.at[a * N_PEERS + k],
                           device_id=peer, device_id_type=pl.DeviceIdType.MESH)
                sends.append(pltpu.make_async_remote_copy(
                    src_ref=x_in[a].at[_slot(peer)], dst_ref=x_out[a].at[_slot(me)], **sem))
                recvs.append(pltpu.make_async_remote_copy(
                    src_ref=x_in[a].at[_slot(peer)], dst_ref=x_out[a].at[_slot(peer)], **sem))
        return mine, sends, recvs

    def start(x_in, x_out, sems):
        mine, sends, _ = parts(x_in, x_out, sems)
        for cp in mine + sends:
            cp.start()

    def finish(x_in, x_out, sems):
        mine, sends, recvs = parts(x_in, x_out, sems)
        for cp in recvs:
            cp.wait_recv()
        for cp in sends:
            cp.wait_send()
        for cp in mine:
            cp.wait()

    return Exchange(grads, [jax.ShapeDtypeStruct(g.shape, g.dtype) for g in grads],
                    _exchange_sems(n), start, finish)


def run_exchange(exchange, name):
    n_in, n_out = len(exchange.arrays), len(exchange.out_shapes)

    def body(*refs):
        x_in, x_out, sems = refs[:n_in], refs[n_in:n_in + n_out], refs[n_in + n_out:]
        exchange.start(x_in, x_out, sems)
        if exchange.forward is not None:
            exchange.forward(x_in, x_out, sems)
        exchange.finish(x_in, x_out, sems)

    exchange.results = list(pl.pallas_call(
        body, name=name, out_shape=exchange.out_shapes, in_specs=[_ANY] * n_in, out_specs=[_ANY] * n_out,
        scratch_shapes=exchange.sems)(*exchange.arrays))
    return exchange.results


def small_all_reduce(vec):
    R = vec.shape[0]

    def body(v_ref, o_ref, all_ref, send_sems, recv_sems):
        me, _, chips = _mesh_places()
        peers = _all_peers(me, chips)
        all_ref[_slot(me)] = v_ref[...]
        sends, recvs = [], []
        for k, peer in enumerate(peers):
            sem = dict(send_sem=send_sems.at[k], recv_sem=recv_sems.at[k],
                       device_id=peer, device_id_type=pl.DeviceIdType.MESH)
            sends.append(pltpu.make_async_remote_copy(src_ref=v_ref, dst_ref=all_ref.at[_slot(me)], **sem))
            recvs.append(pltpu.make_async_remote_copy(src_ref=v_ref, dst_ref=all_ref.at[_slot(peer)], **sem))
        for cp in sends:
            cp.start()
        for cp in recvs:
            cp.wait_recv()
        for cp in sends:
            cp.wait_send()
        acc = all_ref[0]
        for s in range(1, N_DEV):
            acc = acc + all_ref[s]
        o_ref[...] = acc

    return pl.pallas_call(
        body, name="small_all_reduce",
        out_shape=jax.ShapeDtypeStruct(vec.shape, F32),
        in_specs=[pl.BlockSpec(memory_space=pltpu.VMEM)], out_specs=pl.BlockSpec(memory_space=pltpu.VMEM),
        scratch_shapes=[pltpu.VMEM((N_DEV, R, LANES), F32), pltpu.SemaphoreType.DMA((N_PEERS,)),
                        pltpu.SemaphoreType.DMA((N_PEERS,))],
    )(vec)


def _adamw_math(w, g, m, v):
    m2 = ADAM_B1 * m + (1.0 - ADAM_B1) * g
    v2 = ADAM_B2 * v + (1.0 - ADAM_B2) * (g * g)
    m_hat = m2 / (1.0 - ADAM_B1 ** ADAM_STEP)
    v_hat = v2 / (1.0 - ADAM_B2 ** ADAM_STEP)
    delta = -ADAM_LR * (m_hat / (jnp.sqrt(v_hat) + ADAM_EPS) + ADAM_WD * w)
    return delta, m2, v2


def adamw_sharded(partials, w, m, v):
    L, r, c = w.shape
    tr = _div(r, max(BF16_SUBLANES, (1 << 18) // c), BF16_SUBLANES)
    nt = r // tr

    def part_map(l0):
        return lambda l, t: (0, jnp.where(l == l0, t, jnp.where(l < l0, 0, nt - 1)), 0)

    def body(*refs):
        parts = refs[:L]
        w_ref, m_ref, v_ref, g_out, d_out, m_out, v_out = refs[L:]
        layer = pl.program_id(0)
        for l0 in range(L):
            @pl.when(layer == l0)
            def _():
                g = parts[l0][0].astype(F32)
                for s in range(1, N_DEV):
                    g = g + parts[l0][s].astype(F32)
                delta, m2, v2 = _adamw_math(w_ref[...], g, m_ref[...], v_ref[...])
                g_out[...] = g
                d_out[...] = delta
                m_out[...] = m2
                v_out[...] = v2

    own = lambda l, t: (l, t, 0)
    return _call(body, "adamw_sharded", (L, nt),
                 [(p, (N_DEV, tr, c), part_map(l0)) for l0, p in enumerate(partials)]
                 + [(w, (None, tr, c), own), (m, (None, tr, c), own), (v, (None, tr, c), own)],
                 [((L, r, c), F32, (None, tr, c), own)] * 4)


def adamw_flat(g, w, m, v):
    shape = g.shape

    def body(g_ref, w_ref, m_ref, v_ref, d_out, m_out, v_out):
        delta, m2, v2 = _adamw_math(w_ref[...], g_ref[...], m_ref[...], v_ref[...])
        d_out[...] = delta
        m_out[...] = m2
        v_out[...] = v2

    whole = lambda: (0, 0)
    return _call(body, "adamw_flat", (), [(a, shape, whole) for a in (g, w, m, v)],
                 [(shape, F32, shape, whole)] * 3)


def _pack(parts):
    flat = jnp.concatenate([p.reshape(-1).astype(F32) for p in parts])
    rows = -(-flat.shape[0] // (8 * LANES)) * 8
    return jnp.pad(flat, (0, rows * LANES - flat.shape[0])).reshape(rows, LANES)


def _unpack(packed, shapes):
    flat = packed.reshape(-1)
    out, off = [], 0
    for s in shapes:
        size = math.prod(s)
        out.append(flat[off:off + size].reshape(s))
        off += size
    return out


def kernel(x, mem, ffn1_norm, ffn1_w13, ffn1_w2, mix_norm, gmlp_w_in, gmlp_ln_g, gmlp_ln_b, gmlp_w_s, gmlp_b_s, gmlp_w_out, conv_w_in, conv_w, conv_w_out, xattn_norm, mem_norm, xattn_wq, xattn_wkv, xattn_wo, ffn2_norm, ffn2_w13, ffn2_w2, final_norm, loss_target, m_ffn1_norm, m_ffn1_w13, m_ffn1_w2, m_mix_norm, m_gmlp_w_in, m_gmlp_ln_g, m_gmlp_ln_b, m_gmlp_w_s, m_gmlp_b_s, m_gmlp_w_out, m_conv_w_in, m_conv_w, m_conv_w_out, m_xattn_norm, m_mem_norm, m_xattn_wq, m_xattn_wkv, m_xattn_wo, m_ffn2_norm, m_ffn2_w13, m_ffn2_w2, m_final_norm, v_ffn1_norm, v_ffn1_w13, v_ffn1_w2, v_mix_norm, v_gmlp_w_in, v_gmlp_ln_g, v_gmlp_ln_b, v_gmlp_w_s, v_gmlp_b_s, v_gmlp_w_out, v_conv_w_in, v_conv_w, v_conv_w_out, v_xattn_norm, v_mem_norm, v_xattn_wq, v_xattn_wkv, v_xattn_wo, v_ffn2_norm, v_ffn2_w13, v_ffn2_w2, v_final_norm):
    given = dict(locals())
    T, D = x.shape[1], x.shape[2]
    depth = ffn1_norm.shape[0]
    xs = x.reshape(T, D)
    mems = mem.reshape(mem.shape[1], D)
    target = loss_target.reshape(T, D)
    me = 4 * lax.axis_index("x") + 2 * lax.axis_index("y") + lax.axis_index("c")
    E = gmlp_ln_g.shape[1]
    gd = E // GROUPS
    cshard = conv_w.shape[2]

    bf = {k: given[k].astype(BF16) for k in
          ("ffn1_w13", "ffn1_w2", "gmlp_w_in", "gmlp_w_out", "conv_w_in", "conv_w_out",
           "xattn_wq", "xattn_wkv", "xattn_wo", "ffn2_w13", "ffn2_w2")}

    W = {}

    def gather(names_layers):
        return names_layers, gather_exchange([(bf[k], l) for k, l in names_layers])

    def landed(tagged):
        names_layers, exchange = tagged
        W.update(zip(names_layers, exchange.results))

    cw_place = lax.dynamic_update_slice(jnp.zeros((CONV_WIDTH, D), F32), conv_w[0], (jnp.int32(0), me * cshard))
    cw_full = small_all_reduce(_pack([cw_place])).reshape(-1)[:CONV_WIDTH * D].reshape(CONV_WIDTH, D)
    bias_full = jnp.repeat(gmlp_b_s[0].T, gd, axis=1)

    first = gather([("ffn1_w13", 0)])
    run_exchange(first[1], "gather_first")
    landed(first)
    saved = []
    h = xs
    for i in range(depth):
        j = i // 2
        is_gmlp = i % 2 == 0
        mix = ("gmlp_w_in", "gmlp_w_out") if is_gmlp else ("conv_w_in", "conv_w_out")
        on_up1 = gather(([("ffn1_w2", i)] if i == 0 else []) + [(mix[0], j), ("ffn2_w13", i)])
        on_down1 = gather([(mix[1], j), ("xattn_wq", i), ("xattn_wkv", i)])

        def w2_after_up(on_up=on_up1, i=i):
            landed(on_up)
            return W["ffn1_w2", i]

        h, sv1 = ffn_fwd(h, ffn1_norm[i], W["ffn1_w13", i], w2_after_up, on_up1[1], on_down1[1])
        landed(on_down1)
        hn = rms_fwd(h, mix_norm[i])
        on_mix_in = gather([("xattn_wo", i), ("ffn2_w2", i)])
        p = mm_nn(hn, W[mix[0], j], "mix_in", tn_pref=768, exchange=on_mix_in[1])
        landed(on_mix_in)
        if is_gmlp:
            gated = gmlp_mid_fwd(p, gmlp_ln_g[j:j + 1], gmlp_ln_b[j:j + 1], gmlp_w_s[j], bias_full)
        else:
            gated = conv_mid_fwd(p, cw_full)
        h_mix = mm_nn(gated, W[mix[1], j].reshape(1, gated.shape[1], D), "mix_out", F32, res=h)
        sv2 = (h, hn, p, gated)
        h, sv3 = xattn_fwd(h_mix, mems, xattn_norm[i], mem_norm[i],
                           W["xattn_wq", i], W["xattn_wkv", i], W["xattn_wo", i])
        on_up2 = gather([("ffn1_w13", i + 1), ("ffn1_w2", i + 1)]) if i + 1 < depth else None
        h, sv4 = ffn_fwd(h, ffn2_norm[i], W["ffn2_w13", i], lambda i=i: W["ffn2_w2", i],
                         None if on_up2 is None else on_up2[1])
        if on_up2 is not None:
            landed(on_up2)
        saved.append((sv1, sv2, sv3, sv4))

    loss_part, d, dys, d_final_norm = loss_head(h, final_norm, target, 0.5)

    small = {k: [None] * depth for k in ("ffn1_norm", "mix_norm", "xattn_norm", "mem_norm", "ffn2_norm")}
    partial = {}
    for i in reversed(range(depth)):
        j = i // 2
        is_gmlp = i % 2 == 0
        sv1, sv2, sv3, sv4 = saved[i]
        d, dys, small["ffn2_norm"][i], partial["ffn2_w13", i], partial["ffn2_w2", i] = ffn_bwd(
            d, dys, sv4, ffn2_norm[i], W["ffn2_w13", i], W["ffn2_w2", i], 1.0)
        (d, dys, small["xattn_norm"][i], small["mem_norm"][i], dwq,
         partial["xattn_wkv", i], partial["xattn_wo", i]) = xattn_bwd(
            d, dys, sv3, mems, xattn_norm[i], W["xattn_wq", i], W["xattn_wkv", i], W["xattn_wo", i], 1.0)
        if is_gmlp:
            mix = ("gmlp_w_in", "gmlp_w_out")
            mid = lambda p, dg, send: (lambda r: (r[0], r[1:]))(gmlp_mid_bwd(
                p, dg, gmlp_ln_g[j:j + 1], gmlp_ln_b[j:j + 1], gmlp_w_s[j], bias_full, exchange=send))
        else:
            mix = ("conv_w_in", "conv_w_out")
            mid = lambda p, dg, send: (lambda r: (r[0], r[1:]))(conv_mid_bwd(p, dg, cw_full, exchange=send))
        (d, dys, small["mix_norm"][i], partial[mix[0], j], partial[mix[1], j], partial["xattn_wq", i],
         extra) = mixer_bwd_common(d, dys, sv2, mix_norm[i], W[mix[0], j], W[mix[1], j], mid, 0.5, dwq)
        if is_gmlp:
            d_ws, d_bs_wide, d_lng, d_lnb = extra
        else:
            (d_cw,) = extra
        d, dys, small["ffn1_norm"][i], partial["ffn1_w13", i], partial["ffn1_w2", i] = ffn_bwd(
            d, dys, sv1, ffn1_norm[i], W["ffn1_w13", i], W["ffn1_w2", i], 0.5)
    grad_x = d.reshape(x.shape)

    small_grads = {k: jnp.concatenate(v, axis=0) for k, v in small.items()}
    small_grads["gmlp_ln_g"] = d_lng
    small_grads["gmlp_ln_b"] = d_lnb
    small_grads["gmlp_w_s"] = d_ws[None]
    small_grads["gmlp_b_s"] = d_bs_wide[None, :, :, 0]
    small_grads["final_norm"] = d_final_norm.reshape(-1)
    small_names = ["ffn1_norm", "mix_norm", "gmlp_ln_g", "gmlp_ln_b", "gmlp_w_s", "gmlp_b_s",
                   "xattn_norm", "mem_norm", "ffn2_norm", "final_norm"]
    summed = small_all_reduce(_pack([small_grads[k] for k in small_names] + [d_cw, loss_part]))
    parts = _unpack(summed, [given[k].shape for k in small_names] + [(CONV_WIDTH, D), (1, LANES)])
    grads = dict(zip(small_names, parts[:len(small_names)]))
    grads["conv_w"] = lax.dynamic_slice(parts[-2], (jnp.int32(0), me * cshard), (CONV_WIDTH, cshard))[None]
    loss = parts[-1][0, 0]
    flat_names = small_names + ["conv_w"]
    flat = adamw_flat(*[_pack([src[k] for k in flat_names]) for src in
                        (grads, given, {k: given["m_" + k] for k in flat_names},
                         {k: given["v_" + k] for k in flat_names})])
    delta, new_m, new_v = [dict(zip(flat_names, _unpack(f, [given[k].shape for k in flat_names]))) for f in flat]

    for k in bf:
        w = given[k]
        L = w.shape[0]
        shard = w.shape[1:]
        view = lambda a: a.reshape((L,) + shard)
        g, dl, m2, v2 = adamw_sharded([partial[k, l] for l in range(L)], w, given["m_" + k], given["v_" + k])
        grads[k], delta[k], new_m[k], new_v[k] = view(g), view(dl), view(m2), view(v2)

    order = ["ffn1_norm", "ffn1_w13", "ffn1_w2", "mix_norm", "gmlp_w_in", "gmlp_ln_g", "gmlp_ln_b", "gmlp_w_s",
             "gmlp_b_s", "gmlp_w_out", "conv_w_in", "conv_w", "conv_w_out", "xattn_norm", "mem_norm", "xattn_wq",
             "xattn_wkv", "xattn_wo", "ffn2_norm", "ffn2_w13", "ffn2_w2", "final_norm"]
    return (loss, grad_x, *[grads[k] for k in order], *[delta[k] for k in order],
            *[new_m[k] for k in order], *[new_v[k] for k in order])
```

```python
import functools
import math

import jax
import jax.numpy as jnp
from jax import lax
from jax.experimental import pallas as pl
from jax.experimental.pallas import tpu as pltpu

F32 = jnp.float32
BF16 = jnp.bfloat16

N_DEV = 8
N_PEERS = N_DEV - 1
CHUNK = 128
GROUPS = 8
HEADS = 4
CONV_WIDTH = 3
RMS_EPS = 1e-6
LN_EPS = 1e-5
ADAM_LR = 0.001
ADAM_B1 = 0.9
ADAM_B2 = 0.999
ADAM_EPS = 1e-08
ADAM_WD = 0.01
ADAM_STEP = 10
LANES = 128
BF16_SUBLANES = 16
MXU_WIDTH = 256
VMEM_LIMIT_BYTES = 56 * 1024 * 1024

_NT = (((1,), (1,)), ((), ()))
_TN = (((0,), (0,)), ((), ()))
_SQRT_HALF = 0.7071067811865476
_INV_SQRT_2PI = 0.3989422804014327


def _div(n, pref, align):
    best = None
    for t in range(align, min(n, pref) + 1, align):
        if n % t == 0:
            best = t
    return n if best is None else best


def _chunks(n, width):
    return [(c0, min(width, n - c0)) for c0 in range(0, n, width)]


_ANY = pl.BlockSpec(memory_space=pl.ANY)


class Exchange:
    def __init__(self, arrays, out_shapes, sems, start, finish, forward=None):
        self.arrays, self.out_shapes, self.sems = list(arrays), list(out_shapes), list(sems)
        self.start, self.finish, self.forward = start, finish, forward
        self.results = None


def _call(body, name, grid, ins, outs, scratch=(), exchange=None):
    in_specs = [pl.BlockSpec(*spec[1:3], **({"pipeline_mode": pl.Buffered(1)} if len(spec) > 3 else {}))
                for spec in ins]
    ins = [spec[:3] for spec in ins]
    out_specs = [pl.BlockSpec(bs, im) for _, _, bs, im in outs]
    out_shape = [jax.ShapeDtypeStruct(s, d) for s, d, _, _ in outs]
    arrays = [a for a, _, _ in ins]
    scratch = list(scratch)
    kernel_fn = body
    if exchange is not None:
        n_in, n_out, n_scr = len(ins), len(outs), len(scratch)
        n_xin, n_xout = len(exchange.arrays), len(exchange.out_shapes)
        steps = math.prod(grid)
        forward_step = min(steps - 1, (15 * steps) // 16)

        def kernel_fn(*refs):
            refs = list(refs)
            own_in, x_in = refs[:n_in], refs[n_in:n_in + n_xin]
            refs = refs[n_in + n_xin:]
            own_out, x_out = refs[:n_out], refs[n_out:n_out + n_xout]
            refs = refs[n_out + n_xout:]
            own_scr, x_sems = refs[:n_scr], refs[n_scr:]
            step = 0
            for axis, size in enumerate(grid):
                step = step * size + pl.program_id(axis)

            @pl.when(step == 0)
            def _():
                exchange.start(x_in, x_out, x_sems)

            if exchange.forward is not None:
                @pl.when(step == forward_step)
                def _():
                    exchange.forward(x_in, x_out, x_sems)

            body(*own_in, *own_out, *own_scr)

            @pl.when(step == steps - 1)
            def _():
                exchange.finish(x_in, x_out, x_sems)

        in_specs += [_ANY] * n_xin
        out_specs += [_ANY] * n_xout
        out_shape += exchange.out_shapes
        arrays += exchange.arrays
        scratch += exchange.sems
    res = pl.pallas_call(
        kernel_fn,
        name=name,
        grid=grid,
        in_specs=in_specs,
        out_specs=out_specs,
        out_shape=out_shape,
        scratch_shapes=scratch,
        compiler_params=pltpu.CompilerParams(
            dimension_semantics=("arbitrary",) * len(grid), vmem_limit_bytes=VMEM_LIMIT_BYTES),
    )(*arrays)
    if exchange is not None:
        exchange.results = list(res[len(outs):])
        res = res[:len(outs)]
    return res


def _dot(a, b, dims=None):
    if dims is None:
        return jnp.dot(a, b, preferred_element_type=F32)
    return lax.dot_general(a, b, dims, preferred_element_type=F32)


def _sigmoid(v):
    return 1.0 / (1.0 + jnp.exp(-v))


def _normal_cdf(v):
    return 0.5 * (1.0 + lax.erf(v * _SQRT_HALF))


def _normal_pdf(v):
    return _INV_SQRT_2PI * jnp.exp(-0.5 * v * v)


def _accumulate_product(ref, first, product):
    @pl.when(first)
    def _():
        ref[...] = product()

    @pl.when(jnp.logical_not(first))
    def _():
        ref[...] += product()


def _accumulate(ref, part, first):
    @pl.when(first)
    def _():
        ref[...] = part

    @pl.when(jnp.logical_not(first))
    def _():
        ref[...] += part


def rms_fwd(x, g):
    T, D = x.shape
    tt = _div(T, 512, BF16_SUBLANES)

    def body(x_ref, g_ref, o_ref):
        xv = x_ref[...]
        r = lax.rsqrt(jnp.mean(xv * xv, axis=-1, keepdims=True) + RMS_EPS)
        o_ref[...] = ((xv * r) * g_ref[...]).astype(BF16)

    return _call(body, "rms_fwd", (T // tt,),
                 [(x, (tt, D), lambda i: (i, 0)), (g.reshape(1, D), (1, D), lambda i: (0, 0))],
                 [((T, D), BF16, (tt, D), lambda i: (i, 0))])[0]


def rms_bwd(x, g, dxn, d, scale):
    T, D = x.shape
    tt = _div(T, 256, BF16_SUBLANES)

    def body(x_ref, g_ref, dn_ref, d_ref, dx_ref, dxs_ref, dg_ref):
        xv = x_ref[...]
        r = lax.rsqrt(jnp.mean(xv * xv, axis=-1, keepdims=True) + RMS_EPS)
        xh = xv * r
        dn = dn_ref[...]
        dxh = dn * g_ref[...]
        dx = r * (dxh - xh * jnp.mean(dxh * xh, axis=-1, keepdims=True)) + d_ref[...]
        dx_ref[...] = dx
        dxs_ref[...] = (scale * dx).astype(BF16)
        _accumulate(dg_ref, jnp.sum(dn * xh, axis=0, keepdims=True), pl.program_id(0) == 0)

    row = lambda i: (i, 0)
    fix = lambda i: (0, 0)
    return _call(body, "rms_bwd", (T // tt,),
                 [(x, (tt, D), row), (g.reshape(1, D), (1, D), fix), (dxn, (tt, D), row), (d, (tt, D), row)],
                 [((T, D), F32, (tt, D), row), ((T, D), BF16, (tt, D), row), ((1, D), F32, (1, D), fix)])


def rms_gain_grad(x, dxn):
    T, D = x.shape
    tt = _div(T, 256, 8)

    def body(x_ref, dn_ref, dg_ref):
        xv = x_ref[...]
        r = lax.rsqrt(jnp.mean(xv * xv, axis=-1, keepdims=True) + RMS_EPS)
        _accumulate(dg_ref, jnp.sum(dn_ref[...] * (xv * r), axis=0, keepdims=True), pl.program_id(0) == 0)

    row = lambda i: (i, 0)
    return _call(body, "rms_gain_grad", (T // tt,), [(x, (tt, D), row), (dxn, (tt, D), row)],
                 [((1, D), F32, (1, D), lambda i: (0, 0))])[0]


def loss_head(x, g, target, scale):
    T, D = x.shape
    tt = _div(T, 256, BF16_SUBLANES)

    def body(x_ref, g_ref, t_ref, loss_ref, dx_ref, dxs_ref, dg_ref):
        first = pl.program_id(0) == 0
        xv = x_ref[...]
        gv = g_ref[...]
        r = lax.rsqrt(jnp.mean(xv * xv, axis=-1, keepdims=True) + RMS_EPS)
        xh = xv * r
        err = xh * gv - t_ref[...]
        part = 0.5 * jnp.sum(jnp.mean(err * err, axis=-1, keepdims=True), axis=0, keepdims=True)
        _accumulate(loss_ref, jnp.broadcast_to(part, (1, LANES)), first)
        dy = err * (1.0 / D)
        dxh = dy * gv
        dx = r * (dxh - xh * jnp.mean(dxh * xh, axis=-1, keepdims=True))
        dx_ref[...] = dx
        dxs_ref[...] = (scale * dx).astype(BF16)
        _accumulate(dg_ref, jnp.sum(dy * xh, axis=0, keepdims=True), first)

    row = lambda i: (i, 0)
    fix = lambda i: (0, 0)
    return _call(body, "loss_head", (T // tt,),
                 [(x, (tt, D), row), (g.reshape(1, D), (1, D), fix), (target, (tt, D), row)],
                 [((1, LANES), F32, (1, LANES), fix), ((T, D), F32, (tt, D), row),
                  ((T, D), BF16, (tt, D), row), ((1, D), F32, (1, D), fix)])


def mm_nn(a, w3, name, out_dtype=BF16, res=None, res_scale=1.0, tm_pref=1024, tn_pref=1024, exchange=None):
    M, K = a.shape
    nb, _, ns = w3.shape
    tn = _div(ns, tn_pref, LANES)
    per = ns // tn
    tm = _div(M, tm_pref, BF16_SUBLANES)
    ins = [(a, (tm, K), lambda j, m: (m, 0)), (w3, (None, K, tn), lambda j, m: (j // per, 0, j % per))]
    if res is not None:
        ins.append((res, (tm, tn), lambda j, m: (m, j)))

    def body(*refs):
        a_ref, w_ref = refs[0], refs[1]
        o_ref = refs[-1]
        acc = _dot(a_ref[...], w_ref[...])
        if res is not None:
            acc = refs[2][...] + res_scale * acc
        o_ref[...] = acc.astype(o_ref.dtype)

    return _call(body, name, (nb * per, M // tm), ins,
                 [((M, nb * ns), out_dtype, (tm, tn), lambda j, m: (m, j))], exchange=exchange)[0]


def mm_nt(a_in, w3, name, out_dtype, M, tm_pref=1024, to_pref=2048, group=1, exchange=None):
    nb, Ko, ns = w3.shape
    to = _div(Ko, to_pref, LANES)
    tm = _div(M, tm_pref, BF16_SUBLANES)
    if isinstance(a_in, tuple):
        a, a_bs, a_im = a_in
        a_bs = tuple(tm if s == "tm" else s for s in a_bs)
    else:
        a, a_bs, a_im = a_in, (tm, group * ns), lambda m, o, b: (m, b)
    steps = nb // group
    if steps > 1:
        assert out_dtype == F32

    def product(a_ref, w_ref):
        p = None
        for k in range(group):
            pk = _dot(a_ref[:, k * ns:(k + 1) * ns], w_ref[k], _NT)
            p = pk if p is None else p + pk
        return p

    def body(a_ref, w_ref, o_ref):
        if steps == 1:
            o_ref[...] = product(a_ref, w_ref).astype(o_ref.dtype)
        else:
            _accumulate_product(o_ref, pl.program_id(2) == 0, lambda: product(a_ref, w_ref))

    return _call(body, name, (M // tm, Ko // to, steps),
                 [(a, a_bs, a_im), (w3, (group, to, ns), lambda m, o, b: (b, o, 0))],
                 [((M, Ko), out_dtype, (tm, to), lambda m, o, b: (m, o))], exchange=exchange)[0]


def mm_nt_rms_bwd(a, w3, name, x, g, d, scale, exchange=None):
    nb, D, ns = w3.shape
    M = x.shape[0]
    tm = _div(M, 512, BF16_SUBLANES)
    a_bs, a_im = (tm, ns), lambda m, b: (m, b)

    rc = _div(tm, 64, BF16_SUBLANES)

    def body(a_ref, w_ref, x_ref, g_ref, d_ref, dx_ref, dxs_ref, dg_ref, acc_ref):
        m, b = pl.program_id(0), pl.program_id(1)
        _accumulate_product(acc_ref, b == 0, lambda: _dot(a_ref[...], w_ref[...], _NT))

        @pl.when(b == nb - 1)
        def _():
            gv = g_ref[...]

            def piece(c, dg):
                rows = pl.ds(pl.multiple_of(c * rc, rc), rc)
                dn = acc_ref[rows, :]
                xv = x_ref[rows, :]
                r = lax.rsqrt(jnp.mean(xv * xv, axis=-1, keepdims=True) + RMS_EPS)
                xh = xv * r
                dxh = dn * gv
                dx = r * (dxh - xh * jnp.mean(dxh * xh, axis=-1, keepdims=True)) + d_ref[rows, :]
                dx_ref[rows, :] = dx
                dxs_ref[rows, :] = (scale * dx).astype(BF16)
                return dg + jnp.sum(dn * xh, axis=0, keepdims=True)

            _accumulate(dg_ref, lax.fori_loop(0, tm // rc, piece, jnp.zeros((1, D), F32)), m == 0)

    row = lambda m, b: (m, 0)
    fix = lambda m, b: (0, 0)
    w_spec = (w3, (None, D, ns), lambda m, b: (b, 0, 0)) + (("single",) if nb == 1 else ())
    return _call(body, name, (M // tm, nb),
                 [(a, a_bs, a_im), w_spec,
                  (x, (tm, D), row), (g.reshape(1, D), (1, D), fix), (d, (tm, D), row)],
                 [((M, D), F32, (tm, D), row), ((M, D), BF16, (tm, D), row), ((1, D), F32, (1, D), fix)],
                 scratch=[pltpu.VMEM((tm, D), F32)], exchange=exchange)


def mm_tn(a, b_in, name, nbo, ns, tka_pref=1024, tt_pref=2048, tn_pref=2048, exchange=None):
    T, Ka = a.shape
    tt = _div(T, tt_pref, BF16_SUBLANES)
    tka = _div(Ka, tka_pref, LANES)
    if isinstance(b_in, tuple):
        b, b_bs, b_im = b_in
        b_bs = tuple(tt if s == "tt" else s for s in b_bs)
        tn, per = ns, 1
    else:
        tn = _div(ns, tn_pref, LANES)
        per = ns // tn
        b, b_bs, b_im = b_in, (tt, tn), lambda i, j, t: (t, j)
    nt = T // tt

    def body(a_ref, b_ref, o_ref, acc_ref):
        t = pl.program_id(2)
        _accumulate_product(acc_ref, t == 0, lambda: _dot(a_ref[...], b_ref[...], _TN))

        @pl.when(t == nt - 1)
        def _():
            o_ref[...] = acc_ref[...].astype(BF16)

    return _call(body, name, (Ka // tka, nbo * per, nt),
                 [(a, (tt, tka), lambda i, j, t: (t, i)), (b, b_bs, b_im)],
                 [((nbo, Ka, ns), BF16, (None, tka, tn), lambda i, j, t: (j // per, i, j % per))],
                 scratch=[pltpu.VMEM((tka, tn), F32)], exchange=exchange)[0]


def ffn_up(xn, w13, exchange=None):
    T, D = xn.shape
    nb, _, ns = w13.shape
    half = nb // 2
    F = half * ns
    tm = _div(T, 512, BF16_SUBLANES)
    pair = 2 if half % 2 == 0 else 1

    def columns(w_ref, c0, cw):
        k, off = divmod(c0, ns)
        if off + cw <= ns:
            return w_ref[k, :, off:off + cw]
        return jnp.concatenate([w_ref[k, :, off:ns], w_ref[k + 1, :, 0:off + cw - ns]], axis=1)

    def body(x_ref, wg_ref, wu_ref, fac_ref, act_ref):
        xv = x_ref[...]
        for c0, cw in _chunks(pair * ns, MXU_WIDTH):
            cols = slice(c0, c0 + cw)
            gate = _dot(xv, columns(wg_ref, c0, cw))
            up = _dot(xv, columns(wu_ref, c0, cw))
            s = _sigmoid(gate)
            silu = gate * s
            fac_ref[0, :, cols] = (up * (s * (1.0 + gate * (1.0 - s)))).astype(BF16)
            fac_ref[1, :, cols] = silu.astype(BF16)
            act_ref[:, cols] = (silu * up).astype(BF16)

    tn = pair * ns
    return _call(body, "ffn_up", (half // pair, T // tm),
                 [(xn, (tm, D), lambda j, m: (m, 0)),
                  (w13, (pair, D, ns), lambda j, m: (j, 0, 0), "single"),
                  (w13, (pair, D, ns), lambda j, m: (j + half // pair, 0, 0), "single")],
                 [((2, T, F), BF16, (2, tm, tn), lambda j, m: (0, m, j)),
                  ((T, F), BF16, (tm, tn), lambda j, m: (m, j))], exchange=exchange)


def ffn_dact(dy, w2, fac, exchange=None):
    T, D = dy.shape
    F = w2.shape[0]
    tm = _div(T, 512, BF16_SUBLANES)
    tn = _div(F, F // 2, MXU_WIDTH)

    def body(dy_ref, w_ref, fac_ref, dh_ref):
        dyv = dy_ref[...]
        for c0, cw in _chunks(tn, MXU_WIDTH):
            cols = slice(c0, c0 + cw)
            da = _dot(dyv, w_ref[cols, :], _NT)
            dh_ref[0, :, cols] = (da * fac_ref[0, :, cols].astype(F32)).astype(BF16)
            dh_ref[1, :, cols] = (da * fac_ref[1, :, cols].astype(F32)).astype(BF16)

    return _call(body, "ffn_dact", (F // tn, T // tm),
                 [(dy, (tm, D), lambda j, m: (m, 0)), (w2, (tn, D), lambda j, m: (j, 0), "single"),
                  (fac, (2, tm, tn), lambda j, m: (0, m, j))],
                 [((2, T, F), BF16, (2, tm, tn), lambda j, m: (0, m, j))], exchange=exchange)[0]


def ffn_fwd(x, norm_g, w13, get_w2, up_exchange=None, down_exchange=None):
    xn = rms_fwd(x, norm_g)
    fac, act = ffn_up(xn, w13, exchange=up_exchange)
    F = act.shape[1]
    y = mm_nn(act, get_w2().reshape(1, F, -1), "ffn_down", F32, res=x, res_scale=0.5, tm_pref=512,
              exchange=down_exchange)
    return y, (x, xn, fac, act)


def ffn_bwd(d, dys, saved, norm_g, w13, w2, scale_out):
    x, xn, fac, act = saved
    T, D = x.shape
    nb, _, ns = w13.shape
    half = nb // 2
    F = half * ns
    dh = ffn_dact(dys, w2.reshape(F, D), fac)
    dw2 = mm_tn(act, dys, "ffn_dw2", 1, D, tka_pref=ns, tn_pref=1024)
    send_w2 = scatter_exchange([dw2.reshape(N_DEV, F // N_DEV, D)])
    dw13 = mm_tn(xn, (dh, (None, "tt", ns), lambda i, j, t: (j // half, t, j % half)), "ffn_dw13", nb, ns,
                 exchange=send_w2)
    send_w13 = scatter_exchange([dw13])
    pair = 2 if half % 2 == 0 else 1
    per = half // pair
    dxn = mm_nt((dh, (None, "tm", pair * ns), lambda m, o, b: (b // per, m, b % per)), w13, "ffn_dxn", F32, T,
                tm_pref=512, group=pair, exchange=send_w13)
    dx, dxs, dg = rms_bwd(x, norm_g, dxn, d, scale_out)
    return dx, dxs, dg, send_w13.results[0], send_w2.results[0]


def _gmlp_parts(p_ref, lng_ref, lnb_ref):
    E = lng_ref.shape[-1]
    pv = p_ref[...].astype(F32)
    cdf = _normal_cdf(pv)
    z = pv * cdf
    u = z[:, :E]
    vp = z[:, E:]
    mu = jnp.mean(vp, axis=-1, keepdims=True)
    xc = vp - mu
    rstd = lax.rsqrt(jnp.mean(xc * xc, axis=-1, keepdims=True) + LN_EPS)
    vh = xc * rstd
    v = vh * lng_ref[...] + lnb_ref[...]
    return u, vh, rstd, v, pv, cdf


def _causal_ws(ws_ref, g):
    keep = lax.broadcasted_iota(jnp.int32, (CHUNK, CHUNK), 0) >= lax.broadcasted_iota(jnp.int32, (CHUNK, CHUNK), 1)
    return jnp.where(keep, ws_ref[g], 0.0).astype(BF16), keep


def gmlp_mid_fwd(p, ln_g, ln_b, w_s, bias_full):
    T, E2 = p.shape
    E = E2 // 2
    gd = E // GROUPS
    tm = _div(T, 256, CHUNK)
    fix2 = lambda i: (0, 0)

    def body(p_ref, lng_ref, lnb_ref, ws_ref, bias_ref, o_ref):
        u, _, _, v, _, _ = _gmlp_parts(p_ref, lng_ref, lnb_ref)
        vb = v.astype(BF16)
        for g in range(GROUPS):
            wm, _ = _causal_ws(ws_ref, g)
            cols = slice(g * gd, (g + 1) * gd)
            for c in range(tm // CHUNK):
                rows = slice(c * CHUNK, (c + 1) * CHUNK)
                f = _dot(wm, vb[rows, cols]) + bias_ref[:, cols]
                o_ref[rows, cols] = (u[rows, cols] * f).astype(BF16)

    return _call(body, "gmlp_mid_fwd", (T // tm,),
                 [(p, (tm, E2), lambda i: (i, 0)), (ln_g, (1, E), fix2), (ln_b, (1, E), fix2),
                  (w_s, (GROUPS, CHUNK, CHUNK), lambda i: (0, 0, 0)), (bias_full, (CHUNK, E), fix2)],
                 [((T, E), BF16, (tm, E), lambda i: (i, 0))])[0]


def gmlp_mid_bwd(p, dgated, ln_g, ln_b, w_s, bias_full, exchange=None):
    T, E2 = p.shape
    E = E2 // 2
    gd = E // GROUPS
    tm = _div(T, 256, CHUNK)
    nsteps = T // tm
    fix2 = lambda i: (0, 0)
    fix3 = lambda i: (0, 0, 0)

    def body(p_ref, dg_ref, lng_ref, lnb_ref, ws_ref, bias_ref,
             dp_ref, dws_ref, dbs_ref, dlng_ref, dlnb_ref, f_sc, dv_sc, db_sc):
        i = pl.program_id(0)
        first = i == 0
        u, vh, rstd, v, pv, cdf = _gmlp_parts(p_ref, lng_ref, lnb_ref)
        vb = v.astype(BF16)
        dgt = dg_ref[...].astype(F32)
        df = dgt * u
        dfb = df.astype(BF16)
        for g in range(GROUPS):
            wm, keep = _causal_ws(ws_ref, g)
            cols = slice(g * gd, (g + 1) * gd)
            dw = None
            dbg = None
            for c in range(tm // CHUNK):
                rows = slice(c * CHUNK, (c + 1) * CHUNK)
                f_sc[rows, cols] = _dot(wm, vb[rows, cols]) + bias_ref[:, cols]
                dv_sc[rows, cols] = _dot(wm, dfb[rows, cols], _TN)
                part = _dot(dfb[rows, cols], vb[rows, cols], _NT)
                dw = part if dw is None else dw + part
                dbg = df[rows, cols] if dbg is None else dbg + df[rows, cols]
            dw = jnp.where(keep, dw, 0.0)

            @pl.when(first)
            def _():
                dws_ref[g] = dw
                db_sc[:, cols] = dbg

            @pl.when(jnp.logical_not(first))
            def _():
                dws_ref[g] += dw
                db_sc[:, cols] += dbg

        du = dgt * f_sc[...]
        dv = dv_sc[...]
        _accumulate(dlng_ref, jnp.sum(dv * vh, axis=0, keepdims=True), first)
        _accumulate(dlnb_ref, jnp.sum(dv, axis=0, keepdims=True), first)
        dvh = dv * lng_ref[...]
        dvp = rstd * (dvh - jnp.mean(dvh, axis=-1, keepdims=True)
                      - vh * jnp.mean(dvh * vh, axis=-1, keepdims=True))
        gp = cdf + pv * _normal_pdf(pv)
        dp_ref[:, :E] = (du * gp[:, :E]).astype(BF16)
        dp_ref[:, E:] = (dvp * gp[:, E:]).astype(BF16)

        @pl.when(i == nsteps - 1)
        def _():
            for g in range(GROUPS):
                tot = jnp.sum(db_sc[:, g * gd:(g + 1) * gd], axis=-1, keepdims=True)
                dbs_ref[g] = jnp.broadcast_to(tot, (CHUNK, LANES))

    return _call(body, "gmlp_mid_bwd", (nsteps,),
                 [(p, (tm, E2), lambda i: (i, 0)), (dgated, (tm, E), lambda i: (i, 0)),
                  (ln_g, (1, E), fix2), (ln_b, (1, E), fix2),
                  (w_s, (GROUPS, CHUNK, CHUNK), fix3), (bias_full, (CHUNK, E), fix2)],
                 [((T, E2), BF16, (tm, E2), lambda i: (i, 0)),
                  ((GROUPS, CHUNK, CHUNK), F32, (GROUPS, CHUNK, CHUNK), fix3),
                  ((GROUPS, CHUNK, LANES), F32, (GROUPS, CHUNK, LANES), fix3),
                  ((1, E), F32, (1, E), fix2), ((1, E), F32, (1, E), fix2)],
                 scratch=[pltpu.VMEM((tm, E), F32), pltpu.VMEM((tm, E), F32), pltpu.VMEM((CHUNK, E), F32)],
                 exchange=exchange)


HALO = 16


def _row_of(block, r):
    rows = lax.broadcasted_iota(jnp.int32, block.shape, 0)
    return jnp.sum(jnp.where(rows == r, block, 0.0), axis=0, keepdims=True)


def _shift_down(z, k, fill):
    out = pltpu.roll(z, k, 0)
    rows = lax.broadcasted_iota(jnp.int32, z.shape, 0)
    for t in range(k):
        out = jnp.where(rows == t, fill[t], out)
    return out


def _shift_up(z, k, fill):
    n = z.shape[0]
    out = pltpu.roll(z, n - k, 0)
    rows = lax.broadcasted_iota(jnp.int32, z.shape, 0)
    for j in range(k):
        out = jnp.where(rows == n - k + j, fill[j], out)
    return out


def _conv_parts(p_ref, prev_ref, cw_ref, is_first):
    D = cw_ref.shape[-1]
    pv = p_ref[...].astype(F32)
    bg, cg, val = pv[:, :D], pv[:, D:2 * D], pv[:, 2 * D:]
    z = cg * val
    pp = prev_ref[...].astype(F32)
    zp = jnp.where(is_first, 0.0, pp[:, D:2 * D] * pp[:, 2 * D:])
    zl1 = _row_of(zp, HALO - 1)
    zl2 = _row_of(zp, HALO - 2)
    z1 = _shift_down(z, 1, [zl1])
    z2 = _shift_down(z, 2, [zl2, zl1])
    conv = z2 * cw_ref[0:1, :] + z1 * cw_ref[1:2, :] + z * cw_ref[2:3, :]
    return bg, cg, val, z, z1, z2, conv


def conv_mid_fwd(p, cw):
    T, D3 = p.shape
    D = D3 // 3
    tm = _div(T, 256, HALO)
    per = tm // HALO

    def body(p_ref, prev_ref, cw_ref, o_ref):
        bg, _, _, _, _, _, conv = _conv_parts(p_ref, prev_ref, cw_ref, pl.program_id(0) == 0)
        o_ref[...] = (bg * conv).astype(BF16)

    return _call(body, "conv_mid_fwd", (T // tm,),
                 [(p, (tm, D3), lambda i: (i, 0)),
                  (p, (HALO, D3), lambda i: (jnp.maximum(i * per - 1, 0), 0)),
                  (cw, (CONV_WIDTH, D), lambda i: (0, 0))],
                 [((T, D), BF16, (tm, D), lambda i: (i, 0))])[0]


def conv_mid_bwd(p, dgated, cw, exchange=None):
    T, D3 = p.shape
    D = D3 // 3
    tm = _div(T, 256, HALO)
    per = tm // HALO
    nsteps = T // tm
    last_halo = T // HALO - 1
    nxt = lambda i: (jnp.minimum((i + 1) * per, last_halo), 0)

    def body(p_ref, prev_ref, next_ref, dg_ref, dgn_ref, cw_ref, dp_ref, dcw_ref):
        i = pl.program_id(0)
        bg, cg, val, z, z1, z2, conv = _conv_parts(p_ref, prev_ref, cw_ref, i == 0)
        dgt = dg_ref[...].astype(F32)
        dconv = dgt * bg
        dcn = jnp.where(i == nsteps - 1, 0.0, dgn_ref[...].astype(F32) * next_ref[:, :D].astype(F32))
        n0 = _row_of(dcn, 0)
        n1 = _row_of(dcn, 1)
        up1 = _shift_up(dconv, 1, [n0])
        up2 = _shift_up(dconv, 2, [n0, n1])
        dz = dconv * cw_ref[2:3, :] + up1 * cw_ref[1:2, :] + up2 * cw_ref[0:1, :]
        dp_ref[:, :D] = (dgt * conv).astype(BF16)
        dp_ref[:, D:2 * D] = (dz * val).astype(BF16)
        dp_ref[:, 2 * D:] = (dz * cg).astype(BF16)
        first = i == 0
        parts = (jnp.sum(dconv * z2, axis=0, keepdims=True), jnp.sum(dconv * z1, axis=0, keepdims=True),
                 jnp.sum(dconv * z, axis=0, keepdims=True))

        @pl.when(first)
        def _():
            for k in range(CONV_WIDTH):
                dcw_ref[k:k + 1, :] = parts[k]

        @pl.when(jnp.logical_not(first))
        def _():
            for k in range(CONV_WIDTH):
                dcw_ref[k:k + 1, :] += parts[k]

    return _call(body, "conv_mid_bwd", (nsteps,),
                 [(p, (tm, D3), lambda i: (i, 0)),
                  (p, (HALO, D3), lambda i: (jnp.maximum(i * per - 1, 0), 0)),
                  (p, (HALO, D3), nxt),
                  (dgated, (tm, D), lambda i: (i, 0)),
                  (dgated, (HALO, D), nxt),
                  (cw, (CONV_WIDTH, D), lambda i: (0, 0))],
                 [((T, D3), BF16, (tm, D3), lambda i: (i, 0)),
                  ((CONV_WIDTH, D), F32, (CONV_WIDTH, D), lambda i: (0, 0))], exchange=exchange)


def mixer_bwd_common(d, dys, saved, norm_g, w_in, w_out, mid_bwd, scale_out, dwkv):
    x, hn, p, gated = saved
    T, D = x.shape
    E = gated.shape[1]
    w_out3 = w_out.reshape(1, E, D)
    dgated = mm_nt(dys, w_out3, "mix_dgated", BF16, T)
    dw_out = mm_tn(gated, dys, "mix_dwout", 1, D)
    send_wout = scatter_exchange([dw_out.reshape(N_DEV, E // N_DEV, D)])
    dp, extra = mid_bwd(p, dgated, send_wout)
    nb, _, ns = w_in.shape
    send_wkv = scatter_exchange([dwkv])
    dw_in = mm_tn(hn, dp, "mix_dwin", nb, ns, exchange=send_wkv)
    send_win = scatter_exchange([dw_in])
    group = max(g for g in (1, 2, 4, 8) if nb % g == 0 and g * ns <= D)
    dhn = mm_nt(dp, w_in, "mix_dhn", F32, T, tm_pref=512, group=group, exchange=send_win)
    dx, dxs, dg = rms_bwd(x, norm_g, dhn, d, scale_out)
    return dx, dxs, dg, send_win.results[0], send_wout.results[0], send_wkv.results[0], extra


def _softmax_rows(s):
    e = jnp.exp(s - jnp.max(s, axis=-1, keepdims=True))
    return e / jnp.sum(e, axis=-1, keepdims=True)


def attn_fwd(q, kv):
    T, D = q.shape
    M = kv.shape[0]
    hd = D // HEADS
    scale = hd ** -0.5
    tm = _div(T, 512, BF16_SUBLANES)

    def body(q_ref, kv_ref, o_ref):
        for h in range(HEADS):
            cols = slice(h * hd, (h + 1) * hd)
            s = _dot(q_ref[:, cols], kv_ref[:, cols], _NT) * scale
            pr = _softmax_rows(s).astype(BF16)
            o_ref[:, cols] = _dot(pr, kv_ref[:, D + h * hd:D + (h + 1) * hd]).astype(BF16)

    return _call(body, "attn_fwd", (T // tm,),
                 [(q, (tm, D), lambda i: (i, 0)), (kv, (M, 2 * D), lambda i: (0, 0))],
                 [((T, D), BF16, (tm, D), lambda i: (i, 0))])[0]


def attn_bwd(q, do, kv, exchange=None):
    T, D = q.shape
    M = kv.shape[0]
    hd = D // HEADS
    scale = hd ** -0.5
    tm = _div(T, 512, BF16_SUBLANES)
    nsteps = T // tm

    def body(q_ref, do_ref, kv_ref, dq_ref, dkv_ref, acc_ref):
        i = pl.program_id(0)
        for h in range(HEADS):
            cols = slice(h * hd, (h + 1) * hd)
            vcols = slice(D + h * hd, D + (h + 1) * hd)
            qh = q_ref[:, cols]
            kh = kv_ref[:, cols]
            doh = do_ref[:, cols]
            pr = _softmax_rows(_dot(qh, kh, _NT) * scale)
            dpr = _dot(doh, kv_ref[:, vcols], _NT)
            ds = (pr * (dpr - jnp.sum(dpr * pr, axis=-1, keepdims=True)) * scale).astype(BF16)
            dq_ref[:, cols] = _dot(ds, kh).astype(BF16)
            dk = _dot(ds, qh, _TN)
            dv = _dot(pr.astype(BF16), doh, _TN)

            @pl.when(i == 0)
            def _():
                acc_ref[:, cols] = dk
                acc_ref[:, vcols] = dv

            @pl.when(i > 0)
            def _():
                acc_ref[:, cols] += dk
                acc_ref[:, vcols] += dv

        @pl.when(i == nsteps - 1)
        def _():
            dkv_ref[...] = acc_ref[...].astype(BF16)

    return _call(body, "attn_bwd", (nsteps,),
                 [(q, (tm, D), lambda i: (i, 0)), (do, (tm, D), lambda i: (i, 0)),
                  (kv, (M, 2 * D), lambda i: (0, 0))],
                 [((T, D), BF16, (tm, D), lambda i: (i, 0)), ((M, 2 * D), BF16, (M, 2 * D), lambda i: (0, 0))],
                 scratch=[pltpu.VMEM((M, 2 * D), F32)], exchange=exchange)


def xattn_fwd(x, mem, xnorm_g, mnorm_g, wq, wkv, wo):
    D = x.shape[1]
    hq = rms_fwd(x, xnorm_g)
    mn = rms_fwd(mem, mnorm_g)
    q = mm_nn(hq, wq.reshape(1, D, D), "xattn_q")
    kv = mm_nn(mn, wkv, "xattn_kv")
    o = attn_fwd(q, kv)
    y = mm_nn(o, wo.reshape(1, D, D), "xattn_out", F32, res=x)
    return y, (x, hq, mn, q, kv, o)


def xattn_bwd(d, dys, saved, mem, xnorm_g, wq, wkv, wo, scale_out):
    x, hq, mn, q, kv, o = saved
    T, D = x.shape
    M = mem.shape[0]
    do = mm_nt(dys, wo.reshape(1, D, D), "xattn_do", BF16, T)
    rows = D // N_DEV
    dwo = mm_tn(o, dys, "xattn_dwo", 1, D)
    send_wo = scatter_exchange([dwo.reshape(N_DEV, rows, D)])
    dq, dkv = attn_bwd(q, do, kv, exchange=send_wo)
    nb, _, ns = wkv.shape
    dwkv = mm_tn(mn, dkv, "xattn_dwkv", nb, ns)
    dmn = mm_nt(dkv, wkv, "xattn_dmn", F32, M)
    dgm = rms_gain_grad(mem, dmn)
    dwq = mm_tn(hq, dq, "xattn_dwq", 1, D)
    send_wq = scatter_exchange([dwq.reshape(N_DEV, rows, D)])
    dx, dxs, dgx = mm_nt_rms_bwd(dq, wq.reshape(1, D, D), "xattn_dhq", x, xnorm_g, d, scale_out,
                                 exchange=send_wq)
    return dx, dxs, dgx, dgm, send_wq.results[0], dwkv, send_wo.results[0]


def _mesh_places():
    x, y, c = lax.axis_index("x"), lax.axis_index("y"), lax.axis_index("c")
    chips = [(1 - x, y), (x, 1 - y), (1 - x, 1 - y)]
    return (x, y, c), (x, y, 1 - c), chips


def _slot(place):
    return 4 * place[0] + 2 * place[1] + place[2]


def _exchange_sems(n):
    return [pltpu.SemaphoreType.DMA((n * N_PEERS,)), pltpu.SemaphoreType.DMA((n * N_PEERS,)),
            pltpu.SemaphoreType.DMA((n,))]


def gather_exchange(shards):
    n = len(shards)
    shapes = [a.shape if l is None else a.shape[1:] for a, l in shards]

    def parts(x_in, x_out, sems):
        ins = [r if l is None else r.at[l] for r, (_, l) in zip(x_in, shards)]
        send_sems, recv_sems, local_sems = sems
        me, sibling, chips = _mesh_places()

        def copy(a, k, block, to, src=None):
            dst = x_out[a].at[_slot(block)]
            return pltpu.make_async_remote_copy(
                src_ref=dst if src is None else src, dst_ref=dst,
                send_sem=send_sems.at[a * N_PEERS + k], recv_sem=recv_sems.at[a * N_PEERS + k],
                device_id=to, device_id_type=pl.DeviceIdType.MESH)

        mine = [pltpu.make_async_copy(ins[a], x_out[a].at[_slot(me)], local_sems.at[a]) for a in range(n)]
        first = []
        for a in range(n):
            first.append(copy(a, 0, me, sibling, src=ins[a]))
            first += [copy(a, 1 + j, me, (*chip, me[2]), src=ins[a]) for j, chip in enumerate(chips)]
        return me, sibling, chips, copy, mine, first

    def start(x_in, x_out, sems):
        _, _, _, _, mine, first = parts(x_in, x_out, sems)
        for cp in mine + first:
            cp.start()

    def forward(x_in, x_out, sems):
        me, sibling, chips, copy, _, _ = parts(x_in, x_out, sems)
        for j, chip in enumerate(chips):
            for a in range(n):
                copy(a, 1 + j, (*chip, me[2]), me).wait_recv()
                copy(a, 4 + j, (*chip, me[2]), sibling).start()

    def finish(x_in, x_out, sems):
        me, sibling, chips, copy, mine, first = parts(x_in, x_out, sems)
        for a in range(n):
            copy(a, 0, sibling, me).wait_recv()
        for j, chip in enumerate(chips):
            for a in range(n):
                copy(a, 4 + j, (*chip, 1 - me[2]), me).wait_recv()
        for cp in first:
            cp.wait_send()
        for j, chip in enumerate(chips):
            for a in range(n):
                copy(a, 4 + j, (*chip, me[2]), sibling).wait_send()
        for cp in mine:
            cp.wait()

    return Exchange([a for a, _ in shards],
                    [jax.ShapeDtypeStruct((N_DEV,) + tuple(s), BF16) for s in shapes],
                    _exchange_sems(n), start, finish, forward)


def _all_peers(me, chips):
    c = me[2]
    return [(me[0], me[1], 1 - c)] + [(*chip, c) for chip in chips] + [(*chip, 1 - c) for chip in chips]


def scatter_exchange(grads):
    n = len(grads)

    def parts(x_in, x_out, sems):
        send_sems, recv_sems, local_sems = sems
        me, _, chips = _mesh_places()
        mine = [pltpu.make_async_copy(x_in[a].at[_slot(me)], x_out[a].at[_slot(me)], local_sems.at[a])
                for a in range(n)]
        sends, recvs = [], []
        for a in range(n):
            for k, peer in enumerate(_all_peers(me, chips)):
                sem = dict(send_sem=send_sems.at[a * N_PEERS + k], recv_sem=recv_sems.at[a * N_PEERS + k],
                           device_id=peer, device_id_type=pl.DeviceIdType.MESH)
                sends.append(pltpu.make_async_remote_copy(
                    src_ref=x_in[a].at[_slot(peer)], dst_ref=x_out[a].at[_slot(me)], **sem))
                recvs.append(pltpu.make_async_remote_copy(
                    src_ref=x_in[a].at[_slot(peer)], dst_ref=x_out[a].at[_slot(peer)], **sem))
        return mine, sends, recvs

    def start(x_in, x_out, sems):
        mine, sends, _ = parts(x_in, x_out, sems)
        for cp in mine + sends:
            cp.start()

    def finish(x_in, x_out, sems):
        mine, sends, recvs = parts(x_in, x_out, sems)
        for cp in recvs:
            cp.wait_recv()
        for cp in sends:
            cp.wait_send()
        for cp in mine:
            cp.wait()

    return Exchange(grads, [jax.ShapeDtypeStruct(g.shape, g.dtype) for g in grads],
                    _exchange_sems(n), start, finish)


def run_exchange(exchange, name):
    n_in, n_out = len(exchange.arrays), len(exchange.out_shapes)

    def body(*refs):
        x_in, x_out, sems = refs[:n_in], refs[n_in:n_in + n_out], refs[n_in + n_out:]
        exchange.start(x_in, x_out, sems)
        if exchange.forward is not None:
            exchange.forward(x_in, x_out, sems)
        exchange.finish(x_in, x_out, sems)

    exchange.results = list(pl.pallas_call(
        body, name=name, out_shape=exchange.out_shapes, in_specs=[_ANY] * n_in, out_specs=[_ANY] * n_out,
        scratch_shapes=exchange.sems)(*exchange.arrays))
    return exchange.results


def small_all_reduce(vec):
    R = vec.shape[0]

    def body(v_ref, o_ref, all_ref, send_sems, recv_sems):
        me, _, chips = _mesh_places()
        peers = _all_peers(me, chips)
        all_ref[_slot(me)] = v_ref[...]
        sends, recvs = [], []
        for k, peer in enumerate(peers):
            sem = dict(send_sem=send_sems.at[k], recv_sem=recv_sems.at[k],
                       device_id=peer, device_id_type=pl.DeviceIdType.MESH)
            sends.append(pltpu.make_async_remote_copy(src_ref=v_ref, dst_ref=all_ref.at[_slot(me)], **sem))
            recvs.append(pltpu.make_async_remote_copy(src_ref=v_ref, dst_ref=all_ref.at[_slot(peer)], **sem))
        for cp in sends:
            cp.start()
        for cp in recvs:
            cp.wait_recv()
        for cp in sends:
            cp.wait_send()
        acc = all_ref[0]
        for s in range(1, N_DEV):
            acc = acc + all_ref[s]
        o_ref[...] = acc

    return pl.pallas_call(
        body, name="small_all_reduce",
        out_shape=jax.ShapeDtypeStruct(vec.shape, F32),
        in_specs=[pl.BlockSpec(memory_space=pltpu.VMEM)], out_specs=pl.BlockSpec(memory_space=pltpu.VMEM),
        scratch_shapes=[pltpu.VMEM((N_DEV, R, LANES), F32), pltpu.SemaphoreType.DMA((N_PEERS,)),
                        pltpu.SemaphoreType.DMA((N_PEERS,))],
    )(vec)


def _adamw_math(w, g, m, v):
    m2 = ADAM_B1 * m + (1.0 - ADAM_B1) * g
    v2 = ADAM_B2 * v + (1.0 - ADAM_B2) * (g * g)
    m_hat = m2 / (1.0 - ADAM_B1 ** ADAM_STEP)
    v_hat = v2 / (1.0 - ADAM_B2 ** ADAM_STEP)
    delta = -ADAM_LR * (m_hat / (jnp.sqrt(v_hat) + ADAM_EPS) + ADAM_WD * w)
    return delta, m2, v2


def adamw_sharded(partials, w, m, v):
    L, r, c = w.shape
    tr = _div(r, max(BF16_SUBLANES, (1 << 18) // c), BF16_SUBLANES)
    nt = r // tr

    def part_map(l0):
        return lambda l, t: (0, jnp.where(l == l0, t, jnp.where(l < l0, 0, nt - 1)), 0)

    def body(*refs):
        parts = refs[:L]
        w_ref, m_ref, v_ref, g_out, d_out, m_out, v_out = refs[L:]
        layer = pl.program_id(0)
        for l0 in range(L):
            @pl.when(layer == l0)
            def _():
                g = parts[l0][0].astype(F32)
                for s in range(1, N_DEV):
                    g = g + parts[l0][s].astype(F32)
                delta, m2, v2 = _adamw_math(w_ref[...], g, m_ref[...], v_ref[...])
                g_out[...] = g
                d_out[...] = delta
                m_out[...] = m2
                v_out[...] = v2

    own = lambda l, t: (l, t, 0)
    return _call(body, "adamw_sharded", (L, nt),
                 [(p, (N_DEV, tr, c), part_map(l0)) for l0, p in enumerate(partials)]
                 + [(w, (None, tr, c), own), (m, (None, tr, c), own), (v, (None, tr, c), own)],
                 [((L, r, c), F32, (None, tr, c), own)] * 4)


def adamw_flat(g, w, m, v):
    shape = g.shape

    def body(g_ref, w_ref, m_ref, v_ref, d_out, m_out, v_out):
        delta, m2, v2 = _adamw_math(w_ref[...], g_ref[...], m_ref[...], v_ref[...])
        d_out[...] = delta
        m_out[...] = m2
        v_out[...] = v2

    whole = lambda: (0, 0)
    return _call(body, "adamw_flat", (), [(a, shape, whole) for a in (g, w, m, v)],
                 [(shape, F32, shape, whole)] * 3)


def _pack(parts):
    flat = jnp.concatenate([p.reshape(-1).astype(F32) for p in parts])
    rows = -(-flat.shape[0] // (8 * LANES)) * 8
    return jnp.pad(flat, (0, rows * LANES - flat.shape[0])).reshape(rows, LANES)


def _unpack(packed, shapes):
    flat = packed.reshape(-1)
    out, off = [], 0
    for s in shapes:
        size = math.prod(s)
        out.append(flat[off:off + size].reshape(s))
        off += size
    return out


def kernel(x, mem, ffn1_norm, ffn1_w13, ffn1_w2, mix_norm, gmlp_w_in, gmlp_ln_g, gmlp_ln_b, gmlp_w_s, gmlp_b_s, gmlp_w_out, conv_w_in, conv_w, conv_w_out, xattn_norm, mem_norm, xattn_wq, xattn_wkv, xattn_wo, ffn2_norm, ffn2_w13, ffn2_w2, final_norm, loss_target, m_ffn1_norm, m_ffn1_w13, m_ffn1_w2, m_mix_norm, m_gmlp_w_in, m_gmlp_ln_g, m_gmlp_ln_b, m_gmlp_w_s, m_gmlp_b_s, m_gmlp_w_out, m_conv_w_in, m_conv_w, m_conv_w_out, m_xattn_norm, m_mem_norm, m_xattn_wq, m_xattn_wkv, m_xattn_wo, m_ffn2_norm, m_ffn2_w13, m_ffn2_w2, m_final_norm, v_ffn1_norm, v_ffn1_w13, v_ffn1_w2, v_mix_norm, v_gmlp_w_in, v_gmlp_ln_g, v_gmlp_ln_b, v_gmlp_w_s, v_gmlp_b_s, v_gmlp_w_out, v_conv_w_in, v_conv_w, v_conv_w_out, v_xattn_norm, v_mem_norm, v_xattn_wq, v_xattn_wkv, v_xattn_wo, v_ffn2_norm, v_ffn2_w13, v_ffn2_w2, v_final_norm):
    given = dict(locals())
    T, D = x.shape[1], x.shape[2]
    depth = ffn1_norm.shape[0]
    xs = x.reshape(T, D)
    mems = mem.reshape(mem.shape[1], D)
    target = loss_target.reshape(T, D)
    me = 4 * lax.axis_index("x") + 2 * lax.axis_index("y") + lax.axis_index("c")
    E = gmlp_ln_g.shape[1]
    gd = E // GROUPS
    cshard = conv_w.shape[2]

    bf = {k: given[k].astype(BF16) for k in
          ("ffn1_w13", "ffn1_w2", "gmlp_w_in", "gmlp_w_out", "conv_w_in", "conv_w_out",
           "xattn_wq", "xattn_wkv", "xattn_wo", "ffn2_w13", "ffn2_w2")}

    W = {}

    def gather(names_layers):
        return names_layers, gather_exchange([(bf[k], l) for k, l in names_layers])

    def landed(tagged):
        names_layers, exchange = tagged
        W.update(zip(names_layers, exchange.results))

    cw_place = lax.dynamic_update_slice(jnp.zeros((CONV_WIDTH, D), F32), conv_w[0], (jnp.int32(0), me * cshard))
    cw_full = small_all_reduce(_pack([cw_place])).reshape(-1)[:CONV_WIDTH * D].reshape(CONV_WIDTH, D)
    bias_full = jnp.repeat(gmlp_b_s[0].T, gd, axis=1)

    first = gather([("ffn1_w13", 0)])
    run_exchange(first[1], "gather_first")
    landed(first)
    saved = []
    h = xs
    for i in range(depth):
        j = i // 2
        is_gmlp = i % 2 == 0
        mix = ("gmlp_w_in", "gmlp_w_out") if is_gmlp else ("conv_w_in", "conv_w_out")
        on_up1 = gather(([("ffn1_w2", i)] if i == 0 else []) + [(mix[0], j), ("ffn2_w13", i)])
        on_down1 = gather([(mix[1], j), ("xattn_wq", i), ("xattn_wkv", i)])

        def w2_after_up(on_up=on_up1, i=i):
            landed(on_up)
            return W["ffn1_w2", i]

        h, sv1 = ffn_fwd(h, ffn1_norm[i], W["ffn1_w13", i], w2_after_up, on_up1[1], on_down1[1])
        landed(on_down1)
        hn = rms_fwd(h, mix_norm[i])
        on_mix_in = gather([("xattn_wo", i), ("ffn2_w2", i)])
        p = mm_nn(hn, W[mix[0], j], "mix_in", tn_pref=768, exchange=on_mix_in[1])
        landed(on_mix_in)
        if is_gmlp:
            gated = gmlp_mid_fwd(p, gmlp_ln_g[j:j + 1], gmlp_ln_b[j:j + 1], gmlp_w_s[j], bias_full)
        else:
            gated = conv_mid_fwd(p, cw_full)
        h_mix = mm_nn(gated, W[mix[1], j].reshape(1, gated.shape[1], D), "mix_out", F32, res=h)
        sv2 = (h, hn, p, gated)
        h, sv3 = xattn_fwd(h_mix, mems, xattn_norm[i], mem_norm[i],
                           W["xattn_wq", i], W["xattn_wkv", i], W["xattn_wo", i])
        on_up2 = gather([("ffn1_w13", i + 1), ("ffn1_w2", i + 1)]) if i + 1 < depth else None
        h, sv4 = ffn_fwd(h, ffn2_norm[i], W["ffn2_w13", i], lambda i=i: W["ffn2_w2", i],
                         None if on_up2 is None else on_up2[1])
        if on_up2 is not None:
            landed(on_up2)
        saved.append((sv1, sv2, sv3, sv4))

    loss_part, d, dys, d_final_norm = loss_head(h, final_norm, target, 0.5)

    small = {k: [None] * depth for k in ("ffn1_norm", "mix_norm", "xattn_norm", "mem_norm", "ffn2_norm")}
    partial = {}
    for i in reversed(range(depth)):
        j = i // 2
        is_gmlp = i % 2 == 0
        sv1, sv2, sv3, sv4 = saved[i]
        d, dys, small["ffn2_norm"][i], partial["ffn2_w13", i], partial["ffn2_w2", i] = ffn_bwd(
            d, dys, sv4, ffn2_norm[i], W["ffn2_w13", i], W["ffn2_w2", i], 1.0)
        (d, dys, small["xattn_norm"][i], small["mem_norm"][i], partial["xattn_wq", i],
         dwkv, partial["xattn_wo", i]) = xattn_bwd(
            d, dys, sv3, mems, xattn_norm[i], W["xattn_wq", i], W["xattn_wkv", i], W["xattn_wo", i], 1.0)
        if is_gmlp:
            mix = ("gmlp_w_in", "gmlp_w_out")
            mid = lambda p, dg, send: (lambda r: (r[0], r[1:]))(gmlp_mid_bwd(
                p, dg, gmlp_ln_g[j:j + 1], gmlp_ln_b[j:j + 1], gmlp_w_s[j], bias_full, exchange=send))
        else:
            mix = ("conv_w_in", "conv_w_out")
            mid = lambda p, dg, send: (lambda r: (r[0], r[1:]))(conv_mid_bwd(p, dg, cw_full, exchange=send))
        (d, dys, small["mix_norm"][i], partial[mix[0], j], partial[mix[1], j], partial["xattn_wkv", i],
         extra) = mixer_bwd_common(d, dys, sv2, mix_norm[i], W[mix[0], j], W[mix[1], j], mid, 0.5, dwkv)
        if is_gmlp:
            d_ws, d_bs_wide, d_lng, d_lnb = extra
        else:
            (d_cw,) = extra
        d, dys, small["ffn1_norm"][i], partial["ffn1_w13", i], partial["ffn1_w2", i] = ffn_bwd(
            d, dys, sv1, ffn1_norm[i], W["ffn1_w13", i], W["ffn1_w2", i], 0.5)
    grad_x = d.reshape(x.shape)

    small_grads = {k: jnp.concatenate(v, axis=0) for k, v in small.items()}
    small_grads["gmlp_ln_g"] = d_lng
    small_grads["gmlp_ln_b"] = d_lnb
    small_grads["gmlp_w_s"] = d_ws[None]
    small_grads["gmlp_b_s"] = d_bs_wide[None, :, :, 0]
    small_grads["final_norm"] = d_final_norm.reshape(-1)
    small_names = ["ffn1_norm", "mix_norm", "gmlp_ln_g", "gmlp_ln_b", "gmlp_w_s", "gmlp_b_s",
                   "xattn_norm", "mem_norm", "ffn2_norm", "final_norm"]
    summed = small_all_reduce(_pack([small_grads[k] for k in small_names] + [d_cw, loss_part]))
    parts = _unpack(summed, [given[k].shape for k in small_names] + [(CONV_WIDTH, D), (1, LANES)])
    grads = dict(zip(small_names, parts[:len(small_names)]))
    grads["conv_w"] = lax.dynamic_slice(parts[-2], (jnp.int32(0), me * cshard), (CONV_WIDTH, cshard))[None]
    loss = parts[-1][0, 0]
    flat_names = small_names + ["conv_w"]
    flat = adamw_flat(*[_pack([src[k] for k in flat_names]) for src in
                        (grads, given, {k: given["m_" + k] for k in flat_names},
                         {k: given["v_" + k] for k in flat_names})])
    delta, new_m, new_v = [dict(zip(flat_names, _unpack(f, [given[k].shape for k in flat_names]))) for f in flat]

    for k in bf:
        w = given[k]
        L = w.shape[0]
        shard = w.shape[1:]
        view = lambda a: a.reshape((L,) + shard)
        g, dl, m2, v2 = adamw_sharded([partial[k, l] for l in range(L)], w, given["m_" + k], given["v_" + k])
        grads[k], delta[k], new_m[k], new_v[k] = view(g), view(dl), view(m2), view(v2)

    order = ["ffn1_norm", "ffn1_w13", "ffn1_w2", "mix_norm", "gmlp_w_in", "gmlp_ln_g", "gmlp_ln_b", "gmlp_w_s",
             "gmlp_b_s", "gmlp_w_out", "conv_w_in", "conv_w", "conv_w_out", "xattn_norm", "mem_norm", "xattn_wq",
             "xattn_wkv", "xattn_wo", "ffn2_norm", "ffn2_w13", "ffn2_w2", "final_norm"]
    return (loss, grad_x, *[grads[k] for k in order], *[delta[k] for k in order],
            *[new_m[k] for k in order], *[new_v[k] for k in order])
```

```python
import functools
import math

import jax
import jax.numpy as jnp
from jax import lax
from jax.experimental import pallas as pl
from jax.experimental.pallas import tpu as pltpu

F32 = jnp.float32
BF16 = jnp.bfloat16

N_DEV = 8
N_PEERS = N_DEV - 1
CHUNK = 128
GROUPS = 8
HEADS = 4
CONV_WIDTH = 3
RMS_EPS = 1e-6
LN_EPS = 1e-5
ADAM_LR = 0.001
ADAM_B1 = 0.9
ADAM_B2 = 0.999
ADAM_EPS = 1e-08
ADAM_WD = 0.01
ADAM_STEP = 10
LANES = 128
BF16_SUBLANES = 16
MXU_WIDTH = 256
VMEM_LIMIT_BYTES = 56 * 1024 * 1024

_NT = (((1,), (1,)), ((), ()))
_TN = (((0,), (0,)), ((), ()))
_SQRT_HALF = 0.7071067811865476
_INV_SQRT_2PI = 0.3989422804014327


def _div(n, pref, align):
    best = None
    for t in range(align, min(n, pref) + 1, align):
        if n % t == 0:
            best = t
    return n if best is None else best


def _chunks(n, width):
    return [(c0, min(width, n - c0)) for c0 in range(0, n, width)]


_ANY = pl.BlockSpec(memory_space=pl.ANY)


class Exchange:
    def __init__(self, arrays, out_shapes, sems, start, finish, forward=None):
        self.arrays, self.out_shapes, self.sems = list(arrays), list(out_shapes), list(sems)
        self.start, self.finish, self.forward = start, finish, forward
        self.results = None


def _call(body, name, grid, ins, outs, scratch=(), exchange=None):
    in_specs = [pl.BlockSpec(*spec[1:3], **({"pipeline_mode": pl.Buffered(1)} if len(spec) > 3 else {}))
                for spec in ins]
    ins = [spec[:3] for spec in ins]
    out_specs = [pl.BlockSpec(bs, im) for _, _, bs, im in outs]
    out_shape = [jax.ShapeDtypeStruct(s, d) for s, d, _, _ in outs]
    arrays = [a for a, _, _ in ins]
    scratch = list(scratch)
    kernel_fn = body
    if exchange is not None:
        n_in, n_out, n_scr = len(ins), len(outs), len(scratch)
        n_xin, n_xout = len(exchange.arrays), len(exchange.out_shapes)
        steps = math.prod(grid)
        forward_step = min(steps - 1, (15 * steps) // 16)

        def kernel_fn(*refs):
            refs = list(refs)
            own_in, x_in = refs[:n_in], refs[n_in:n_in + n_xin]
            refs = refs[n_in + n_xin:]
            own_out, x_out = refs[:n_out], refs[n_out:n_out + n_xout]
            refs = refs[n_out + n_xout:]
            own_scr, x_sems = refs[:n_scr], refs[n_scr:]
            step = 0
            for axis, size in enumerate(grid):
                step = step * size + pl.program_id(axis)

            @pl.when(step == 0)
            def _():
                exchange.start(x_in, x_out, x_sems)

            if exchange.forward is not None:
                @pl.when(step == forward_step)
                def _():
                    exchange.forward(x_in, x_out, x_sems)

            body(*own_in, *own_out, *own_scr)

            @pl.when(step == steps - 1)
            def _():
                exchange.finish(x_in, x_out, x_sems)

        in_specs += [_ANY] * n_xin
        out_specs += [_ANY] * n_xout
        out_shape += exchange.out_shapes
        arrays += exchange.arrays
        scratch += exchange.sems
    res = pl.pallas_call(
        kernel_fn,
        name=name,
        grid=grid,
        in_specs=in_specs,
        out_specs=out_specs,
        out_shape=out_shape,
        scratch_shapes=scratch,
        compiler_params=pltpu.CompilerParams(
            dimension_semantics=("arbitrary",) * len(grid), vmem_limit_bytes=VMEM_LIMIT_BYTES),
    )(*arrays)
    if exchange is not None:
        exchange.results = list(res[len(outs):])
        res = res[:len(outs)]
    return res


def _dot(a, b, dims=None):
    if dims is None:
        return jnp.dot(a, b, preferred_element_type=F32)
    return lax.dot_general(a, b, dims, preferred_element_type=F32)


def _sigmoid(v):
    return 1.0 / (1.0 + jnp.exp(-v))


def _normal_cdf(v):
    return 0.5 * (1.0 + lax.erf(v * _SQRT_HALF))


def _normal_pdf(v):
    return _INV_SQRT_2PI * jnp.exp(-0.5 * v * v)


def _accumulate_product(ref, first, product, skip=False):
    @pl.when(first)
    def _():
        ref[...] = product()

    @pl.when(jnp.logical_not(jnp.logical_or(first, skip)))
    def _():
        ref[...] += product()


def _accumulate(ref, part, first):
    @pl.when(first)
    def _():
        ref[...] = part

    @pl.when(jnp.logical_not(first))
    def _():
        ref[...] += part


def rms_fwd(x, g):
    T, D = x.shape
    tt = _div(T, 512, BF16_SUBLANES)

    def body(x_ref, g_ref, o_ref):
        xv = x_ref[...]
        r = lax.rsqrt(jnp.mean(xv * xv, axis=-1, keepdims=True) + RMS_EPS)
        o_ref[...] = ((xv * r) * g_ref[...]).astype(BF16)

    return _call(body, "rms_fwd", (T // tt,),
                 [(x, (tt, D), lambda i: (i, 0)), (g.reshape(1, D), (1, D), lambda i: (0, 0))],
                 [((T, D), BF16, (tt, D), lambda i: (i, 0))])[0]


def rms_bwd(x, g, dxn, d, scale):
    T, D = x.shape
    tt = _div(T, 256, BF16_SUBLANES)

    def body(x_ref, g_ref, dn_ref, d_ref, dx_ref, dxs_ref, dg_ref):
        xv = x_ref[...]
        r = lax.rsqrt(jnp.mean(xv * xv, axis=-1, keepdims=True) + RMS_EPS)
        xh = xv * r
        dn = dn_ref[...].astype(F32)
        dxh = dn * g_ref[...]
        dx = r * (dxh - xh * jnp.mean(dxh * xh, axis=-1, keepdims=True)) + d_ref[...]
        dx_ref[...] = dx
        dxs_ref[...] = (scale * dx).astype(BF16)
        _accumulate(dg_ref, jnp.sum(dn * xh, axis=0, keepdims=True), pl.program_id(0) == 0)

    row = lambda i: (i, 0)
    fix = lambda i: (0, 0)
    return _call(body, "rms_bwd", (T // tt,),
                 [(x, (tt, D), row), (g.reshape(1, D), (1, D), fix), (dxn, (tt, D), row), (d, (tt, D), row)],
                 [((T, D), F32, (tt, D), row), ((T, D), BF16, (tt, D), row), ((1, D), F32, (1, D), fix)])


def rms_gain_grad(x, dxn):
    T, D = x.shape
    tt = _div(T, 256, 8)

    def body(x_ref, dn_ref, dg_ref):
        xv = x_ref[...]
        r = lax.rsqrt(jnp.mean(xv * xv, axis=-1, keepdims=True) + RMS_EPS)
        _accumulate(dg_ref, jnp.sum(dn_ref[...] * (xv * r), axis=0, keepdims=True), pl.program_id(0) == 0)

    row = lambda i: (i, 0)
    return _call(body, "rms_gain_grad", (T // tt,), [(x, (tt, D), row), (dxn, (tt, D), row)],
                 [((1, D), F32, (1, D), lambda i: (0, 0))])[0]


def loss_head(x, g, target, scale):
    T, D = x.shape
    tt = _div(T, 256, BF16_SUBLANES)

    def body(x_ref, g_ref, t_ref, loss_ref, dx_ref, dxs_ref, dg_ref):
        first = pl.program_id(0) == 0
        xv = x_ref[...]
        gv = g_ref[...]
        r = lax.rsqrt(jnp.mean(xv * xv, axis=-1, keepdims=True) + RMS_EPS)
        xh = xv * r
        err = xh * gv - t_ref[...]
        part = 0.5 * jnp.sum(jnp.mean(err * err, axis=-1, keepdims=True), axis=0, keepdims=True)
        _accumulate(loss_ref, jnp.broadcast_to(part, (1, LANES)), first)
        dy = err * (1.0 / D)
        dxh = dy * gv
        dx = r * (dxh - xh * jnp.mean(dxh * xh, axis=-1, keepdims=True))
        dx_ref[...] = dx
        dxs_ref[...] = (scale * dx).astype(BF16)
        _accumulate(dg_ref, jnp.sum(dy * xh, axis=0, keepdims=True), first)

    row = lambda i: (i, 0)
    fix = lambda i: (0, 0)
    return _call(body, "loss_head", (T // tt,),
                 [(x, (tt, D), row), (g.reshape(1, D), (1, D), fix), (target, (tt, D), row)],
                 [((1, LANES), F32, (1, LANES), fix), ((T, D), F32, (tt, D), row),
                  ((T, D), BF16, (tt, D), row), ((1, D), F32, (1, D), fix)])


def mm_nn(a, w3, name, out_dtype=BF16, res=None, res_scale=1.0, tm_pref=1024, tn_pref=1024, group=1,
          exchange=None):
    M, K = a.shape
    nb, _, ns = w3.shape
    tn = ns if group > 1 else _div(ns, tn_pref, LANES)
    per = ns // tn
    tm = _div(M, tm_pref, BF16_SUBLANES)
    wide = group * tn
    ins = [(a, (tm, K), lambda j, m: (m, 0)), (w3, (group, K, tn), lambda j, m: (j // per, 0, j % per))]
    if res is not None:
        ins.append((res, (tm, wide), lambda j, m: (m, j)))

    def body(*refs):
        a_ref, w_ref = refs[0], refs[1]
        o_ref = refs[-1]
        for k in range(group):
            cols = slice(k * tn, (k + 1) * tn)
            acc = _dot(a_ref[...], w_ref[k])
            if res is not None:
                acc = refs[2][:, cols] + res_scale * acc
            o_ref[:, cols] = acc.astype(o_ref.dtype)

    return _call(body, name, (nb * per // group, M // tm), ins,
                 [((M, nb * ns), out_dtype, (tm, wide), lambda j, m: (m, j))], exchange=exchange)[0]


def mm_nt(a_in, w3, name, out_dtype, M, tm_pref=1024, to_pref=2048, group=1, exchange=None):
    nb, Ko, ns = w3.shape
    to = _div(Ko, to_pref, LANES)
    tm = _div(M, tm_pref, BF16_SUBLANES)
    if isinstance(a_in, tuple):
        a, a_bs, a_im = a_in
        a_bs = tuple(tm if s == "tm" else s for s in a_bs)
    else:
        a, a_bs, a_im = a_in, (tm, group * ns), lambda m, o, b: (m, b)
    steps = nb // group
    narrow_out = steps > 1 and out_dtype != F32

    def product(a_ref, w_ref):
        p = None
        for k in range(group):
            pk = _dot(a_ref[:, k * ns:(k + 1) * ns], w_ref[k], _NT)
            p = pk if p is None else p + pk
        return p

    def body(a_ref, w_ref, o_ref, *acc):
        b = pl.program_id(2)
        if steps == 1:
            o_ref[...] = product(a_ref, w_ref).astype(o_ref.dtype)
        elif not narrow_out:
            _accumulate_product(o_ref, b == 0, lambda: product(a_ref, w_ref))
        else:
            acc_ref, = acc
            _accumulate_product(acc_ref, b == 0, lambda: product(a_ref, w_ref), skip=b == steps - 1)

            @pl.when(b == steps - 1)
            def _():
                o_ref[...] = (acc_ref[...] + product(a_ref, w_ref)).astype(o_ref.dtype)

    return _call(body, name, (M // tm, Ko // to, steps),
                 [(a, a_bs, a_im), (w3, (group, to, ns), lambda m, o, b: (b, o, 0))],
                 [((M, Ko), out_dtype, (tm, to), lambda m, o, b: (m, o))],
                 scratch=[pltpu.VMEM((tm, to), F32)] if narrow_out else [], exchange=exchange)[0]


def mm_nt_rms_bwd(a, w3, name, x, g, d, scale, exchange=None):
    nb, D, ns = w3.shape
    M = x.shape[0]
    tm = _div(M, 512, BF16_SUBLANES)
    a_bs, a_im = (tm, ns), lambda m, b: (m, b)

    rc = _div(tm, 64, BF16_SUBLANES)

    def body(a_ref, w_ref, x_ref, g_ref, d_ref, dx_ref, dxs_ref, dg_ref, acc_ref):
        m, b = pl.program_id(0), pl.program_id(1)
        _accumulate_product(acc_ref, b == 0, lambda: _dot(a_ref[...], w_ref[...], _NT))

        @pl.when(b == nb - 1)
        def _():
            gv = g_ref[...]

            def piece(c, dg):
                rows = pl.ds(pl.multiple_of(c * rc, rc), rc)
                dn = acc_ref[rows, :]
                xv = x_ref[rows, :]
                r = lax.rsqrt(jnp.mean(xv * xv, axis=-1, keepdims=True) + RMS_EPS)
                xh = xv * r
                dxh = dn * gv
                dx = r * (dxh - xh * jnp.mean(dxh * xh, axis=-1, keepdims=True)) + d_ref[rows, :]
                dx_ref[rows, :] = dx
                dxs_ref[rows, :] = (scale * dx).astype(BF16)
                return dg + jnp.sum(dn * xh, axis=0, keepdims=True)

            _accumulate(dg_ref, lax.fori_loop(0, tm // rc, piece, jnp.zeros((1, D), F32)), m == 0)

    row = lambda m, b: (m, 0)
    fix = lambda m, b: (0, 0)
    w_spec = (w3, (None, D, ns), lambda m, b: (b, 0, 0)) + (("single",) if nb == 1 else ())
    return _call(body, name, (M // tm, nb),
                 [(a, a_bs, a_im), w_spec,
                  (x, (tm, D), row), (g.reshape(1, D), (1, D), fix), (d, (tm, D), row)],
                 [((M, D), F32, (tm, D), row), ((M, D), BF16, (tm, D), row), ((1, D), F32, (1, D), fix)],
                 scratch=[pltpu.VMEM((tm, D), F32)], exchange=exchange)


def mm_tn(a, b_in, name, nbo, ns, tka_pref=1024, tt_pref=2048, tn_pref=2048, exchange=None):
    T, Ka = a.shape
    tt = _div(T, tt_pref, BF16_SUBLANES)
    tka = _div(Ka, tka_pref, LANES)
    if isinstance(b_in, tuple):
        b, b_bs, b_im = b_in
        b_bs = tuple(tt if s == "tt" else s for s in b_bs)
        tn, per = ns, 1
    else:
        tn = _div(ns, tn_pref, LANES)
        per = ns // tn
        b, b_bs, b_im = b_in, (tt, tn), lambda i, j, t: (t, j)
    nt = T // tt

    def body(a_ref, b_ref, o_ref, acc_ref):
        t = pl.program_id(2)
        _accumulate_product(acc_ref, t == 0, lambda: _dot(a_ref[...], b_ref[...], _TN))

        @pl.when(t == nt - 1)
        def _():
            o_ref[...] = acc_ref[...].astype(BF16)

    return _call(body, name, (Ka // tka, nbo * per, nt),
                 [(a, (tt, tka), lambda i, j, t: (t, i)), (b, b_bs, b_im)],
                 [((nbo, Ka, ns), BF16, (None, tka, tn), lambda i, j, t: (j // per, i, j % per))],
                 scratch=[pltpu.VMEM((tka, tn), F32)], exchange=exchange)[0]


def ffn_up(xn, w13, exchange=None):
    T, D = xn.shape
    nb, _, ns = w13.shape
    half = nb // 2
    F = half * ns
    tm = _div(T, 512, BF16_SUBLANES)
    pair = 2 if half % 2 == 0 else 1

    def columns(w_ref, c0, cw):
        k, off = divmod(c0, ns)
        if off + cw <= ns:
            return w_ref[k, :, off:off + cw]
        return jnp.concatenate([w_ref[k, :, off:ns], w_ref[k + 1, :, 0:off + cw - ns]], axis=1)

    def body(x_ref, wg_ref, wu_ref, fac_ref, act_ref):
        xv = x_ref[...]
        for c0, cw in _chunks(pair * ns, MXU_WIDTH):
            cols = slice(c0, c0 + cw)
            gate = _dot(xv, columns(wg_ref, c0, cw))
            up = _dot(xv, columns(wu_ref, c0, cw))
            s = _sigmoid(gate)
            silu = gate * s
            fac_ref[0, :, cols] = (up * (s * (1.0 + gate * (1.0 - s)))).astype(BF16)
            fac_ref[1, :, cols] = silu.astype(BF16)
            act_ref[:, cols] = (silu * up).astype(BF16)

    tn = pair * ns
    return _call(body, "ffn_up", (half // pair, T // tm),
                 [(xn, (tm, D), lambda j, m: (m, 0)),
                  (w13, (pair, D, ns), lambda j, m: (j, 0, 0), "single"),
                  (w13, (pair, D, ns), lambda j, m: (j + half // pair, 0, 0), "single")],
                 [((2, T, F), BF16, (2, tm, tn), lambda j, m: (0, m, j)),
                  ((T, F), BF16, (tm, tn), lambda j, m: (m, j))], exchange=exchange)


def ffn_dact(dy, w2, fac, exchange=None):
    T, D = dy.shape
    F = w2.shape[0]
    tm = _div(T, 512, BF16_SUBLANES)
    tn = _div(F, F // 2, MXU_WIDTH)

    def body(dy_ref, w_ref, fac_ref, dh_ref):
        dyv = dy_ref[...]
        for c0, cw in _chunks(tn, MXU_WIDTH):
            cols = slice(c0, c0 + cw)
            da = _dot(dyv, w_ref[cols, :], _NT)
            dh_ref[0, :, cols] = (da * fac_ref[0, :, cols].astype(F32)).astype(BF16)
            dh_ref[1, :, cols] = (da * fac_ref[1, :, cols].astype(F32)).astype(BF16)

    return _call(body, "ffn_dact", (F // tn, T // tm),
                 [(dy, (tm, D), lambda j, m: (m, 0)), (w2, (tn, D), lambda j, m: (j, 0), "single"),
                  (fac, (2, tm, tn), lambda j, m: (0, m, j))],
                 [((2, T, F), BF16, (2, tm, tn), lambda j, m: (0, m, j))], exchange=exchange)[0]


def ffn_fwd(x, norm_g, w13, get_w2, up_exchange=None, down_exchange=None):
    xn = rms_fwd(x, norm_g)
    fac, act = ffn_up(xn, w13, exchange=up_exchange)
    F = act.shape[1]
    y = mm_nn(act, get_w2().reshape(1, F, -1), "ffn_down", F32, res=x, res_scale=0.5, tm_pref=512,
              exchange=down_exchange)
    return y, (x, xn, fac, act)


def ffn_bwd(d, dys, saved, norm_g, w13, w2, scale_out):
    x, xn, fac, act = saved
    T, D = x.shape
    nb, _, ns = w13.shape
    half = nb // 2
    F = half * ns
    dh = ffn_dact(dys, w2.reshape(F, D), fac)
    dw2 = mm_tn(act, dys, "ffn_dw2", 1, D, tka_pref=ns, tn_pref=1024)
    send_w2 = scatter_exchange([dw2.reshape(N_DEV, F // N_DEV, D)])
    dw13 = mm_tn(xn, (dh, (None, "tt", ns), lambda i, j, t: (j // half, t, j % half)), "ffn_dw13", nb, ns,
                 exchange=send_w2)
    send_w13 = scatter_exchange([dw13])
    pair = 2 if half % 2 == 0 else 1
    per = half // pair
    dxn = mm_nt((dh, (None, "tm", pair * ns), lambda m, o, b: (b // per, m, b % per)), w13, "ffn_dxn", BF16, T,
                tm_pref=512, group=pair, exchange=send_w13)
    dx, dxs, dg = rms_bwd(x, norm_g, dxn, d, scale_out)
    return dx, dxs, dg, send_w13.results[0], send_w2.results[0]


def _gmlp_parts(p_ref, lng_ref, lnb_ref):
    E = lng_ref.shape[-1]
    pv = p_ref[...].astype(F32)
    cdf = _normal_cdf(pv)
    z = pv * cdf
    u = z[:, :E]
    vp = z[:, E:]
    mu = jnp.mean(vp, axis=-1, keepdims=True)
    xc = vp - mu
    rstd = lax.rsqrt(jnp.mean(xc * xc, axis=-1, keepdims=True) + LN_EPS)
    vh = xc * rstd
    v = vh * lng_ref[...] + lnb_ref[...]
    return u, vh, rstd, v, pv, cdf


def _causal_ws(ws_ref, g):
    keep = lax.broadcasted_iota(jnp.int32, (CHUNK, CHUNK), 0) >= lax.broadcasted_iota(jnp.int32, (CHUNK, CHUNK), 1)
    return jnp.where(keep, ws_ref[g], 0.0).astype(BF16), keep


def gmlp_mid_fwd(p, ln_g, ln_b, w_s, bias_full):
    T, E2 = p.shape
    E = E2 // 2
    gd = E // GROUPS
    tm = _div(T, 256, CHUNK)
    fix2 = lambda i: (0, 0)

    def body(p_ref, lng_ref, lnb_ref, ws_ref, bias_ref, o_ref):
        u, _, _, v, _, _ = _gmlp_parts(p_ref, lng_ref, lnb_ref)
        vb = v.astype(BF16)
        for g in range(GROUPS):
            wm, _ = _causal_ws(ws_ref, g)
            cols = slice(g * gd, (g + 1) * gd)
            for c in range(tm // CHUNK):
                rows = slice(c * CHUNK, (c + 1) * CHUNK)
                f = _dot(wm, vb[rows, cols]) + bias_ref[:, cols]
                o_ref[rows, cols] = (u[rows, cols] * f).astype(BF16)

    return _call(body, "gmlp_mid_fwd", (T // tm,),
                 [(p, (tm, E2), lambda i: (i, 0)), (ln_g, (1, E), fix2), (ln_b, (1, E), fix2),
                  (w_s, (GROUPS, CHUNK, CHUNK), lambda i: (0, 0, 0)), (bias_full, (CHUNK, E), fix2)],
                 [((T, E), BF16, (tm, E), lambda i: (i, 0))])[0]


def gmlp_mid_bwd(p, dgated, ln_g, ln_b, w_s, bias_full, exchange=None):
    T, E2 = p.shape
    E = E2 // 2
    gd = E // GROUPS
    tm = _div(T, 256, CHUNK)
    nsteps = T // tm
    fix2 = lambda i: (0, 0)
    fix3 = lambda i: (0, 0, 0)

    def body(p_ref, dg_ref, lng_ref, lnb_ref, ws_ref, bias_ref,
             dp_ref, dws_ref, dbs_ref, dlng_ref, dlnb_ref, f_sc, dv_sc, db_sc):
        i = pl.program_id(0)
        first = i == 0
        u, vh, rstd, v, pv, cdf = _gmlp_parts(p_ref, lng_ref, lnb_ref)
        vb = v.astype(BF16)
        dgt = dg_ref[...].astype(F32)
        df = dgt * u
        dfb = df.astype(BF16)
        for g in range(GROUPS):
            wm, keep = _causal_ws(ws_ref, g)
            cols = slice(g * gd, (g + 1) * gd)
            dw = None
            dbg = None
            for c in range(tm // CHUNK):
                rows = slice(c * CHUNK, (c + 1) * CHUNK)
                f_sc[rows, cols] = _dot(wm, vb[rows, cols]) + bias_ref[:, cols]
                dv_sc[rows, cols] = _dot(wm, dfb[rows, cols], _TN)
                part = _dot(dfb[rows, cols], vb[rows, cols], _NT)
                dw = part if dw is None else dw + part
                dbg = df[rows, cols] if dbg is None else dbg + df[rows, cols]
            dw = jnp.where(keep, dw, 0.0)

            @pl.when(first)
            def _():
                dws_ref[g] = dw
                db_sc[:, cols] = dbg

            @pl.when(jnp.logical_not(first))
            def _():
                dws_ref[g] += dw
                db_sc[:, cols] += dbg

        du = dgt * f_sc[...]
        dv = dv_sc[...]
        _accumulate(dlng_ref, jnp.sum(dv * vh, axis=0, keepdims=True), first)
        _accumulate(dlnb_ref, jnp.sum(dv, axis=0, keepdims=True), first)
        dvh = dv * lng_ref[...]
        dvp = rstd * (dvh - jnp.mean(dvh, axis=-1, keepdims=True)
                      - vh * jnp.mean(dvh * vh, axis=-1, keepdims=True))
        gp = cdf + pv * _normal_pdf(pv)
        dp_ref[:, :E] = (du * gp[:, :E]).astype(BF16)
        dp_ref[:, E:] = (dvp * gp[:, E:]).astype(BF16)

        @pl.when(i == nsteps - 1)
        def _():
            for g in range(GROUPS):
                tot = jnp.sum(db_sc[:, g * gd:(g + 1) * gd], axis=-1, keepdims=True)
                dbs_ref[g] = jnp.broadcast_to(tot, (CHUNK, LANES))

    return _call(body, "gmlp_mid_bwd", (nsteps,),
                 [(p, (tm, E2), lambda i: (i, 0)), (dgated, (tm, E), lambda i: (i, 0)),
                  (ln_g, (1, E), fix2), (ln_b, (1, E), fix2),
                  (w_s, (GROUPS, CHUNK, CHUNK), fix3), (bias_full, (CHUNK, E), fix2)],
                 [((T, E2), BF16, (tm, E2), lambda i: (i, 0)),
                  ((GROUPS, CHUNK, CHUNK), F32, (GROUPS, CHUNK, CHUNK), fix3),
                  ((GROUPS, CHUNK, LANES), F32, (GROUPS, CHUNK, LANES), fix3),
                  ((1, E), F32, (1, E), fix2), ((1, E), F32, (1, E), fix2)],
                 scratch=[pltpu.VMEM((tm, E), F32), pltpu.VMEM((tm, E), F32), pltpu.VMEM((CHUNK, E), F32)],
                 exchange=exchange)


HALO = 16


def _row_of(block, r):
    rows = lax.broadcasted_iota(jnp.int32, block.shape, 0)
    return jnp.sum(jnp.where(rows == r, block, 0.0), axis=0, keepdims=True)


def _shift_down(z, k, fill):
    out = pltpu.roll(z, k, 0)
    rows = lax.broadcasted_iota(jnp.int32, z.shape, 0)
    for t in range(k):
        out = jnp.where(rows == t, fill[t], out)
    return out


def _shift_up(z, k, fill):
    n = z.shape[0]
    out = pltpu.roll(z, n - k, 0)
    rows = lax.broadcasted_iota(jnp.int32, z.shape, 0)
    for j in range(k):
        out = jnp.where(rows == n - k + j, fill[j], out)
    return out


def _conv_parts(p_ref, prev_ref, cw_ref, is_first):
    D = cw_ref.shape[-1]
    pv = p_ref[...].astype(F32)
    bg, cg, val = pv[:, :D], pv[:, D:2 * D], pv[:, 2 * D:]
    z = cg * val
    pp = prev_ref[...].astype(F32)
    zp = jnp.where(is_first, 0.0, pp[:, D:2 * D] * pp[:, 2 * D:])
    zl1 = _row_of(zp, HALO - 1)
    zl2 = _row_of(zp, HALO - 2)
    z1 = _shift_down(z, 1, [zl1])
    z2 = _shift_down(z, 2, [zl2, zl1])
    conv = z2 * cw_ref[0:1, :] + z1 * cw_ref[1:2, :] + z * cw_ref[2:3, :]
    return bg, cg, val, z, z1, z2, conv


def conv_mid_fwd(p, cw):
    T, D3 = p.shape
    D = D3 // 3
    tm = _div(T, 256, HALO)
    per = tm // HALO

    def body(p_ref, prev_ref, cw_ref, o_ref):
        bg, _, _, _, _, _, conv = _conv_parts(p_ref, prev_ref, cw_ref, pl.program_id(0) == 0)
        o_ref[...] = (bg * conv).astype(BF16)

    return _call(body, "conv_mid_fwd", (T // tm,),
                 [(p, (tm, D3), lambda i: (i, 0)),
                  (p, (HALO, D3), lambda i: (jnp.maximum(i * per - 1, 0), 0)),
                  (cw, (CONV_WIDTH, D), lambda i: (0, 0))],
                 [((T, D), BF16, (tm, D), lambda i: (i, 0))])[0]


def conv_mid_bwd(p, dgated, cw, exchange=None):
    T, D3 = p.shape
    D = D3 // 3
    tm = _div(T, 256, HALO)
    per = tm // HALO
    nsteps = T // tm
    last_halo = T // HALO - 1
    nxt = lambda i: (jnp.minimum((i + 1) * per, last_halo), 0)

    def body(p_ref, prev_ref, next_ref, dg_ref, dgn_ref, cw_ref, dp_ref, dcw_ref):
        i = pl.program_id(0)
        bg, cg, val, z, z1, z2, conv = _conv_parts(p_ref, prev_ref, cw_ref, i == 0)
        dgt = dg_ref[...].astype(F32)
        dconv = dgt * bg
        dcn = jnp.where(i == nsteps - 1, 0.0, dgn_ref[...].astype(F32) * next_ref[:, :D].astype(F32))
        n0 = _row_of(dcn, 0)
        n1 = _row_of(dcn, 1)
        up1 = _shift_up(dconv, 1, [n0])
        up2 = _shift_up(dconv, 2, [n0, n1])
        dz = dconv * cw_ref[2:3, :] + up1 * cw_ref[1:2, :] + up2 * cw_ref[0:1, :]
        dp_ref[:, :D] = (dgt * conv).astype(BF16)
        dp_ref[:, D:2 * D] = (dz * val).astype(BF16)
        dp_ref[:, 2 * D:] = (dz * cg).astype(BF16)
        first = i == 0
        parts = (jnp.sum(dconv * z2, axis=0, keepdims=True), jnp.sum(dconv * z1, axis=0, keepdims=True),
                 jnp.sum(dconv * z, axis=0, keepdims=True))

        @pl.when(first)
        def _():
            for k in range(CONV_WIDTH):
                dcw_ref[k:k + 1, :] = parts[k]

        @pl.when(jnp.logical_not(first))
        def _():
            for k in range(CONV_WIDTH):
                dcw_ref[k:k + 1, :] += parts[k]

    return _call(body, "conv_mid_bwd", (nsteps,),
                 [(p, (tm, D3), lambda i: (i, 0)),
                  (p, (HALO, D3), lambda i: (jnp.maximum(i * per - 1, 0), 0)),
                  (p, (HALO, D3), nxt),
                  (dgated, (tm, D), lambda i: (i, 0)),
                  (dgated, (HALO, D), nxt),
                  (cw, (CONV_WIDTH, D), lambda i: (0, 0))],
                 [((T, D3), BF16, (tm, D3), lambda i: (i, 0)),
                  ((CONV_WIDTH, D), F32, (CONV_WIDTH, D), lambda i: (0, 0))], exchange=exchange)


def mixer_bwd_common(d, dys, saved, norm_g, w_in, w_out, mid_bwd, scale_out, dwkv):
    x, hn, p, gated = saved
    T, D = x.shape
    E = gated.shape[1]
    w_out3 = w_out.reshape(1, E, D)
    dgated = mm_nt(dys, w_out3, "mix_dgated", BF16, T)
    dw_out = mm_tn(gated, dys, "mix_dwout", 1, D)
    send_wout = scatter_exchange([dw_out.reshape(N_DEV, E // N_DEV, D)])
    dp, extra = mid_bwd(p, dgated, send_wout)
    nb, _, ns = w_in.shape
    send_wkv = scatter_exchange([dwkv])
    dw_in = mm_tn(hn, dp, "mix_dwin", nb, ns, exchange=send_wkv)
    send_win = scatter_exchange([dw_in])
    group = max(g for g in (1, 2, 4, 8) if nb % g == 0 and g * ns <= D)
    dhn = mm_nt(dp, w_in, "mix_dhn", BF16, T, tm_pref=512, group=group, exchange=send_win)
    dx, dxs, dg = rms_bwd(x, norm_g, dhn, d, scale_out)
    return dx, dxs, dg, send_win.results[0], send_wout.results[0], send_wkv.results[0], extra


def _softmax_rows(s):
    e = jnp.exp(s - jnp.max(s, axis=-1, keepdims=True))
    return e / jnp.sum(e, axis=-1, keepdims=True)


def attn_fwd(q, kv):
    T, D = q.shape
    M = kv.shape[0]
    hd = D // HEADS
    scale = hd ** -0.5
    tm = _div(T, 512, BF16_SUBLANES)

    def body(q_ref, kv_ref, o_ref):
        for h in range(HEADS):
            cols = slice(h * hd, (h + 1) * hd)
            s = _dot(q_ref[:, cols], kv_ref[:, cols], _NT) * scale
            pr = _softmax_rows(s).astype(BF16)
            o_ref[:, cols] = _dot(pr, kv_ref[:, D + h * hd:D + (h + 1) * hd]).astype(BF16)

    return _call(body, "attn_fwd", (T // tm,),
                 [(q, (tm, D), lambda i: (i, 0)), (kv, (M, 2 * D), lambda i: (0, 0))],
                 [((T, D), BF16, (tm, D), lambda i: (i, 0))])[0]


def attn_bwd(q, do, kv, exchange=None):
    T, D = q.shape
    M = kv.shape[0]
    hd = D // HEADS
    scale = hd ** -0.5
    tm = _div(T, 512, BF16_SUBLANES)
    nsteps = T // tm

    def body(q_ref, do_ref, kv_ref, dq_ref, dkv_ref, acc_ref):
        i = pl.program_id(0)
        for h in range(HEADS):
            cols = slice(h * hd, (h + 1) * hd)
            vcols = slice(D + h * hd, D + (h + 1) * hd)
            qh = q_ref[:, cols]
            kh = kv_ref[:, cols]
            doh = do_ref[:, cols]
            pr = _softmax_rows(_dot(qh, kh, _NT) * scale)
            dpr = _dot(doh, kv_ref[:, vcols], _NT)
            ds = (pr * (dpr - jnp.sum(dpr * pr, axis=-1, keepdims=True)) * scale).astype(BF16)
            dq_ref[:, cols] = _dot(ds, kh).astype(BF16)
            dk = _dot(ds, qh, _TN)
            dv = _dot(pr.astype(BF16), doh, _TN)

            @pl.when(i == 0)
            def _():
                acc_ref[:, cols] = dk
                acc_ref[:, vcols] = dv

            @pl.when(i > 0)
            def _():
                acc_ref[:, cols] += dk
                acc_ref[:, vcols] += dv

        @pl.when(i == nsteps - 1)
        def _():
            dkv_ref[...] = acc_ref[...].astype(BF16)

    return _call(body, "attn_bwd", (nsteps,),
                 [(q, (tm, D), lambda i: (i, 0)), (do, (tm, D), lambda i: (i, 0)),
                  (kv, (M, 2 * D), lambda i: (0, 0))],
                 [((T, D), BF16, (tm, D), lambda i: (i, 0)), ((M, 2 * D), BF16, (M, 2 * D), lambda i: (0, 0))],
                 scratch=[pltpu.VMEM((M, 2 * D), F32)], exchange=exchange)


def xattn_fwd(x, mem, xnorm_g, mnorm_g, wq, wkv, wo):
    D = x.shape[1]
    hq = rms_fwd(x, xnorm_g)
    mn = rms_fwd(mem, mnorm_g)
    q = mm_nn(hq, wq.reshape(1, D, D), "xattn_q")
    kv = mm_nn(mn, wkv, "xattn_kv")
    o = attn_fwd(q, kv)
    y = mm_nn(o, wo.reshape(1, D, D), "xattn_out", F32, res=x)
    return y, (x, hq, mn, q, kv, o)


def xattn_bwd(d, dys, saved, mem, xnorm_g, wq, wkv, wo, scale_out):
    x, hq, mn, q, kv, o = saved
    T, D = x.shape
    M = mem.shape[0]
    do = mm_nt(dys, wo.reshape(1, D, D), "xattn_do", BF16, T)
    rows = D // N_DEV
    dwo = mm_tn(o, dys, "xattn_dwo", 1, D)
    send_wo = scatter_exchange([dwo.reshape(N_DEV, rows, D)])
    dq, dkv = attn_bwd(q, do, kv, exchange=send_wo)
    nb, _, ns = wkv.shape
    dwkv = mm_tn(mn, dkv, "xattn_dwkv", nb, ns)
    dmn = mm_nt(dkv, wkv, "xattn_dmn", F32, M)
    dgm = rms_gain_grad(mem, dmn)
    dwq = mm_tn(hq, dq, "xattn_dwq", 1, D)
    send_wq = scatter_exchange([dwq.reshape(N_DEV, rows, D)])
    dx, dxs, dgx = mm_nt_rms_bwd(dq, wq.reshape(1, D, D), "xattn_dhq", x, xnorm_g, d, scale_out,
                                 exchange=send_wq)
    return dx, dxs, dgx, dgm, send_wq.results[0], dwkv, send_wo.results[0]


def _mesh_places():
    x, y, c = lax.axis_index("x"), lax.axis_index("y"), lax.axis_index("c")
    chips = [(1 - x, y), (x, 1 - y), (1 - x, 1 - y)]
    return (x, y, c), (x, y, 1 - c), chips


def _slot(place):
    return 4 * place[0] + 2 * place[1] + place[2]


def _exchange_sems(n):
    return [pltpu.SemaphoreType.DMA((n * N_PEERS,)), pltpu.SemaphoreType.DMA((n * N_PEERS,)),
            pltpu.SemaphoreType.DMA((n,))]


def gather_exchange(shards):
    n = len(shards)
    shapes = [a.shape if l is None else a.shape[1:] for a, l in shards]

    def parts(x_in, x_out, sems):
        ins = [r if l is None else r.at[l] for r, (_, l) in zip(x_in, shards)]
        send_sems, recv_sems, local_sems = sems
        me, sibling, chips = _mesh_places()

        def copy(a, k, block, to, src=None):
            dst = x_out[a].at[_slot(block)]
            return pltpu.make_async_remote_copy(
                src_ref=dst if src is None else src, dst_ref=dst,
                send_sem=send_sems.at[a * N_PEERS + k], recv_sem=recv_sems.at[a * N_PEERS + k],
                device_id=to, device_id_type=pl.DeviceIdType.MESH)

        mine = [pltpu.make_async_copy(ins[a], x_out[a].at[_slot(me)], local_sems.at[a]) for a in range(n)]
        first = []
        for a in range(n):
            first.append(copy(a, 0, me, sibling, src=ins[a]))
            first += [copy(a, 1 + j, me, (*chip, me[2]), src=ins[a]) for j, chip in enumerate(chips)]
        return me, sibling, chips, copy, mine, first

    def start(x_in, x_out, sems):
        _, _, _, _, mine, first = parts(x_in, x_out, sems)
        for cp in mine + first:
            cp.start()

    def forward(x_in, x_out, sems):
        me, sibling, chips, copy, _, _ = parts(x_in, x_out, sems)
        for j, chip in enumerate(chips):
            for a in range(n):
                copy(a, 1 + j, (*chip, me[2]), me).wait_recv()
                copy(a, 4 + j, (*chip, me[2]), sibling).start()

    def finish(x_in, x_out, sems):
        me, sibling, chips, copy, mine, first = parts(x_in, x_out, sems)
        for a in range(n):
            copy(a, 0, sibling, me).wait_recv()
        for j, chip in enumerate(chips):
            for a in range(n):
                copy(a, 4 + j, (*chip, 1 - me[2]), me).wait_recv()
        for cp in first:
            cp.wait_send()
        for j, chip in enumerate(chips):
            for a in range(n):
                copy(a, 4 + j, (*chip, me[2]), sibling).wait_send()
        for cp in mine:
            cp.wait()

    return Exchange([a for a, _ in shards],
                    [jax.ShapeDtypeStruct((N_DEV,) + tuple(s), a.dtype) for s, (a, _) in zip(shapes, shards)],
                    _exchange_sems(n), start, finish, forward)


def _all_peers(me, chips):
    c = me[2]
    return [(me[0], me[1], 1 - c)] + [(*chip, c) for chip in chips] + [(*chip, 1 - c) for chip in chips]


def scatter_exchange(grads):
    n = len(grads)

    def parts(x_in, x_out, sems):
        send_sems, recv_sems, local_sems = sems
        me, _, chips = _mesh_places()
        mine = [pltpu.make_async_copy(x_in[a].at[_slot(me)], x_out[a].at[_slot(me)], local_sems.at[a])
                for a in range(n)]
        sends, recvs = [], []
        for a in range(n):
            for k, peer in enumerate(_all_peers(me, chips)):
                sem = dict(send_sem=send_sems.at[a * N_PEERS + k], recv_sem=recv_sems.at[a * N_PEERS + k],
                           device_id=peer, device_id_type=pl.DeviceIdType.MESH)
                sends.append(pltpu.make_async_remote_copy(
                    src_ref=x_in[a].at[_slot(peer)], dst_ref=x_out[a].at[_slot(me)], **sem))
                recvs.append(pltpu.make_async_remote_copy(
                    src_ref=x_in[a].at[_slot(peer)], dst_ref=x_out[a].at[_slot(peer)], **sem))
        return mine, sends, recvs

    def start(x_in, x_out, sems):
        mine, sends, _ = parts(x_in, x_out, sems)
        for cp in mine + sends:
            cp.start()

    def finish(x_in, x_out, sems):
        mine, sends, recvs = parts(x_in, x_out, sems)
        for cp in recvs:
            cp.wait_recv()
        for cp in sends:
            cp.wait_send()
        for cp in mine:
            cp.wait()

    return Exchange(grads, [jax.ShapeDtypeStruct(g.shape, g.dtype) for g in grads],
                    _exchange_sems(n), start, finish)


def run_exchange(exchange, name):
    n_in, n_out = len(exchange.arrays), len(exchange.out_shapes)

    def body(*refs):
        x_in, x_out, sems = refs[:n_in], refs[n_in:n_in + n_out], refs[n_in + n_out:]
        exchange.start(x_in, x_out, sems)
        if exchange.forward is not None:
            exchange.forward(x_in, x_out, sems)
        exchange.finish(x_in, x_out, sems)

    exchange.results = list(pl.pallas_call(
        body, name=name, out_shape=exchange.out_shapes, in_specs=[_ANY] * n_in, out_specs=[_ANY] * n_out,
        scratch_shapes=exchange.sems)(*exchange.arrays))
    return exchange.results


def small_all_reduce(vec):
    R = vec.shape[0]

    def body(v_ref, o_ref, all_ref, send_sems, recv_sems):
        me, _, chips = _mesh_places()
        peers = _all_peers(me, chips)
        all_ref[_slot(me)] = v_ref[...]
        sends, recvs = [], []
        for k, peer in enumerate(peers):
            sem = dict(send_sem=send_sems.at[k], recv_sem=recv_sems.at[k],
                       device_id=peer, device_id_type=pl.DeviceIdType.MESH)
            sends.append(pltpu.make_async_remote_copy(src_ref=v_ref, dst_ref=all_ref.at[_slot(me)], **sem))
            recvs.append(pltpu.make_async_remote_copy(src_ref=v_ref, dst_ref=all_ref.at[_slot(peer)], **sem))
        for cp in sends:
            cp.start()
        for cp in recvs:
            cp.wait_recv()
        for cp in sends:
            cp.wait_send()
        acc = all_ref[0]
        for s in range(1, N_DEV):
            acc = acc + all_ref[s]
        o_ref[...] = acc

    return pl.pallas_call(
        body, name="small_all_reduce",
        out_shape=jax.ShapeDtypeStruct(vec.shape, F32),
        in_specs=[pl.BlockSpec(memory_space=pltpu.VMEM)], out_specs=pl.BlockSpec(memory_space=pltpu.VMEM),
        scratch_shapes=[pltpu.VMEM((N_DEV, R, LANES), F32), pltpu.SemaphoreType.DMA((N_PEERS,)),
                        pltpu.SemaphoreType.DMA((N_PEERS,))],
    )(vec)


def _adamw_math(w, g, m, v):
    m2 = ADAM_B1 * m + (1.0 - ADAM_B1) * g
    v2 = ADAM_B2 * v + (1.0 - ADAM_B2) * (g * g)
    m_hat = m2 / (1.0 - ADAM_B1 ** ADAM_STEP)
    v_hat = v2 / (1.0 - ADAM_B2 ** ADAM_STEP)
    delta = -ADAM_LR * (m_hat / (jnp.sqrt(v_hat) + ADAM_EPS) + ADAM_WD * w)
    return delta, m2, v2


def adamw_sharded(partials, w, m, v):
    L, r, c = w.shape
    tr = _div(r, max(BF16_SUBLANES, (1 << 18) // c), BF16_SUBLANES)
    nt = r // tr

    def part_map(l0):
        return lambda l, t: (0, jnp.where(l == l0, t, jnp.where(l < l0, 0, nt - 1)), 0)

    def body(*refs):
        parts = refs[:L]
        w_ref, m_ref, v_ref, g_out, d_out, m_out, v_out = refs[L:]
        layer = pl.program_id(0)
        for l0 in range(L):
            @pl.when(layer == l0)
            def _():
                g = parts[l0][0].astype(F32)
                for s in range(1, N_DEV):
                    g = g + parts[l0][s].astype(F32)
                delta, m2, v2 = _adamw_math(w_ref[...], g, m_ref[...], v_ref[...])
                g_out[...] = g
                d_out[...] = delta
                m_out[...] = m2
                v_out[...] = v2

    own = lambda l, t: (l, t, 0)
    return _call(body, "adamw_sharded", (L, nt),
                 [(p, (N_DEV, tr, c), part_map(l0)) for l0, p in enumerate(partials)]
                 + [(w, (None, tr, c), own), (m, (None, tr, c), own), (v, (None, tr, c), own)],
                 [((L, r, c), F32, (None, tr, c), own)] * 4)


def adamw_flat(g, w, m, v):
    shape = g.shape

    def body(g_ref, w_ref, m_ref, v_ref, d_out, m_out, v_out):
        delta, m2, v2 = _adamw_math(w_ref[...], g_ref[...], m_ref[...], v_ref[...])
        d_out[...] = delta
        m_out[...] = m2
        v_out[...] = v2

    whole = lambda: (0, 0)
    return _call(body, "adamw_flat", (), [(a, shape, whole) for a in (g, w, m, v)],
                 [(shape, F32, shape, whole)] * 3)


def _pack(parts):
    flat = jnp.concatenate([p.reshape(-1).astype(F32) for p in parts])
    rows = -(-flat.shape[0] // (8 * LANES)) * 8
    return jnp.pad(flat, (0, rows * LANES - flat.shape[0])).reshape(rows, LANES)


def _unpack(packed, shapes):
    flat = packed.reshape(-1)
    out, off = [], 0
    for s in shapes:
        size = math.prod(s)
        out.append(flat[off:off + size].reshape(s))
        off += size
    return out


def kernel(x, mem, ffn1_norm, ffn1_w13, ffn1_w2, mix_norm, gmlp_w_in, gmlp_ln_g, gmlp_ln_b, gmlp_w_s, gmlp_b_s, gmlp_w_out, conv_w_in, conv_w, conv_w_out, xattn_norm, mem_norm, xattn_wq, xattn_wkv, xattn_wo, ffn2_norm, ffn2_w13, ffn2_w2, final_norm, loss_target, m_ffn1_norm, m_ffn1_w13, m_ffn1_w2, m_mix_norm, m_gmlp_w_in, m_gmlp_ln_g, m_gmlp_ln_b, m_gmlp_w_s, m_gmlp_b_s, m_gmlp_w_out, m_conv_w_in, m_conv_w, m_conv_w_out, m_xattn_norm, m_mem_norm, m_xattn_wq, m_xattn_wkv, m_xattn_wo, m_ffn2_norm, m_ffn2_w13, m_ffn2_w2, m_final_norm, v_ffn1_norm, v_ffn1_w13, v_ffn1_w2, v_mix_norm, v_gmlp_w_in, v_gmlp_ln_g, v_gmlp_ln_b, v_gmlp_w_s, v_gmlp_b_s, v_gmlp_w_out, v_conv_w_in, v_conv_w, v_conv_w_out, v_xattn_norm, v_mem_norm, v_xattn_wq, v_xattn_wkv, v_xattn_wo, v_ffn2_norm, v_ffn2_w13, v_ffn2_w2, v_final_norm):
    given = dict(locals())
    T, D = x.shape[1], x.shape[2]
    depth = ffn1_norm.shape[0]
    xs = x.reshape(T, D)
    mems = mem.reshape(mem.shape[1], D)
    target = loss_target.reshape(T, D)
    me = 4 * lax.axis_index("x") + 2 * lax.axis_index("y") + lax.axis_index("c")
    E = gmlp_ln_g.shape[1]
    gd = E // GROUPS
    cshard = conv_w.shape[2]

    bf = {k: given[k].astype(BF16) for k in
          ("ffn1_w13", "ffn1_w2", "gmlp_w_in", "gmlp_w_out", "conv_w_in", "conv_w_out",
           "xattn_wq", "xattn_wkv", "xattn_wo", "ffn2_w13", "ffn2_w2")}

    W = {}

    def gather(names_layers):
        return names_layers, gather_exchange([(bf[k], l) for k, l in names_layers])

    def landed(tagged):
        names_layers, exchange = tagged
        W.update(zip(names_layers, exchange.results))

    bias_full = jnp.repeat(gmlp_b_s[0].T, gd, axis=1)

    W["ffn1_w13", 0], cw_shards = run_exchange(gather_exchange([(bf["ffn1_w13"], 0), (conv_w, 0)]), "gather_first")
    cw_full = jnp.transpose(cw_shards, (1, 0, 2)).reshape(CONV_WIDTH, D)
    saved = []
    h = xs
    for i in range(depth):
        j = i // 2
        is_gmlp = i % 2 == 0
        mix = ("gmlp_w_in", "gmlp_w_out") if is_gmlp else ("conv_w_in", "conv_w_out")
        on_up1 = gather(([("ffn1_w2", i)] if i == 0 else []) + [(mix[0], j), ("ffn2_w13", i)])
        on_down1 = gather([(mix[1], j), ("xattn_wq", i), ("xattn_wkv", i)])

        def w2_after_up(on_up=on_up1, i=i):
            landed(on_up)
            return W["ffn1_w2", i]

        h, sv1 = ffn_fwd(h, ffn1_norm[i], W["ffn1_w13", i], w2_after_up, on_up1[1], on_down1[1])
        landed(on_down1)
        hn = rms_fwd(h, mix_norm[i])
        on_mix_in = gather([("xattn_wo", i), ("ffn2_w2", i)])
        p = mm_nn(hn, W[mix[0], j], "mix_in", group=2, exchange=on_mix_in[1])
        landed(on_mix_in)
        if is_gmlp:
            gated = gmlp_mid_fwd(p, gmlp_ln_g[j:j + 1], gmlp_ln_b[j:j + 1], gmlp_w_s[j], bias_full)
        else:
            gated = conv_mid_fwd(p, cw_full)
        h_mix = mm_nn(gated, W[mix[1], j].reshape(1, gated.shape[1], D), "mix_out", F32, res=h)
        sv2 = (h, hn, p, gated)
        h, sv3 = xattn_fwd(h_mix, mems, xattn_norm[i], mem_norm[i],
                           W["xattn_wq", i], W["xattn_wkv", i], W["xattn_wo", i])
        on_up2 = gather([("ffn1_w13", i + 1), ("ffn1_w2", i + 1)]) if i + 1 < depth else None
        h, sv4 = ffn_fwd(h, ffn2_norm[i], W["ffn2_w13", i], lambda i=i: W["ffn2_w2", i],
                         None if on_up2 is None else on_up2[1])
        if on_up2 is not None:
            landed(on_up2)
        saved.append((sv1, sv2, sv3, sv4))

    loss_part, d, dys, d_final_norm = loss_head(h, final_norm, target, 0.5)

    small = {k: [None] * depth for k in ("ffn1_norm", "mix_norm", "xattn_norm", "mem_norm", "ffn2_norm")}
    partial = {}
    for i in reversed(range(depth)):
        j = i // 2
        is_gmlp = i % 2 == 0
        sv1, sv2, sv3, sv4 = saved[i]
        d, dys, small["ffn2_norm"][i], partial["ffn2_w13", i], partial["ffn2_w2", i] = ffn_bwd(
            d, dys, sv4, ffn2_norm[i], W["ffn2_w13", i], W["ffn2_w2", i], 1.0)
        (d, dys, small["xattn_norm"][i], small["mem_norm"][i], partial["xattn_wq", i],
         dwkv, partial["xattn_wo", i]) = xattn_bwd(
            d, dys, sv3, mems, xattn_norm[i], W["xattn_wq", i], W["xattn_wkv", i], W["xattn_wo", i], 1.0)
        if is_gmlp:
            mix = ("gmlp_w_in", "gmlp_w_out")
            mid = lambda p, dg, send: (lambda r: (r[0], r[1:]))(gmlp_mid_bwd(
                p, dg, gmlp_ln_g[j:j + 1], gmlp_ln_b[j:j + 1], gmlp_w_s[j], bias_full, exchange=send))
        else:
            mix = ("conv_w_in", "conv_w_out")
            mid = lambda p, dg, send: (lambda r: (r[0], r[1:]))(conv_mid_bwd(p, dg, cw_full, exchange=send))
        (d, dys, small["mix_norm"][i], partial[mix[0], j], partial[mix[1], j], partial["xattn_wkv", i],
         extra) = mixer_bwd_common(d, dys, sv2, mix_norm[i], W[mix[0], j], W[mix[1], j], mid, 0.5, dwkv)
        if is_gmlp:
            d_ws, d_bs_wide, d_lng, d_lnb = extra
        else:
            (d_cw,) = extra
        d, dys, small["ffn1_norm"][i], partial["ffn1_w13", i], partial["ffn1_w2", i] = ffn_bwd(
            d, dys, sv1, ffn1_norm[i], W["ffn1_w13", i], W["ffn1_w2", i], 0.5)
    grad_x = d.reshape(x.shape)

    small_grads = {k: jnp.concatenate(v, axis=0) for k, v in small.items()}
    small_grads["gmlp_ln_g"] = d_lng
    small_grads["gmlp_ln_b"] = d_lnb
    small_grads["gmlp_w_s"] = d_ws[None]
    small_grads["gmlp_b_s"] = d_bs_wide[None, :, :, 0]
    small_grads["final_norm"] = d_final_norm.reshape(-1)
    small_names = ["ffn1_norm", "mix_norm", "gmlp_ln_g", "gmlp_ln_b", "gmlp_w_s", "gmlp_b_s",
                   "xattn_norm", "mem_norm", "ffn2_norm", "final_norm"]
    summed = small_all_reduce(_pack([small_grads[k] for k in small_names] + [d_cw, loss_part]))
    parts = _unpack(summed, [given[k].shape for k in small_names] + [(CONV_WIDTH, D), (1, LANES)])
    grads = dict(zip(small_names, parts[:len(small_names)]))
    grads["conv_w"] = lax.dynamic_slice(parts[-2], (jnp.int32(0), me * cshard), (CONV_WIDTH, cshard))[None]
    loss = parts[-1][0, 0]
    flat_names = small_names + ["conv_w"]
    flat = adamw_flat(*[_pack([src[k] for k in flat_names]) for src in
                        (grads, given, {k: given["m_" + k] for k in flat_names},
                         {k: given["v_" + k] for k in flat_names})])
    delta, new_m, new_v = [dict(zip(flat_names, _unpack(f, [given[k].shape for k in flat_names]))) for f in flat]

    for k in bf:
        w = given[k]
        L = w.shape[0]
        shard = w.shape[1:]
        view = lambda a: a.reshape((L,) + shard)
        g, dl, m2, v2 = adamw_sharded([partial[k, l] for l in range(L)], w, given["m_" + k], given["v_" + k])
        grads[k], delta[k], new_m[k], new_v[k] = view(g), view(dl), view(m2), view(v2)

    order = ["ffn1_norm", "ffn1_w13", "ffn1_w2", "mix_norm", "gmlp_w_in", "gmlp_ln_g", "gmlp_ln_b", "gmlp_w_s",
             "gmlp_b_s", "gmlp_w_out", "conv_w_in", "conv_w", "conv_w_out", "xattn_norm", "mem_norm", "xattn_wq",
             "xattn_wkv", "xattn_wo", "ffn2_norm", "ffn2_w13", "ffn2_w2", "final_norm"]
    return (loss, grad_x, *[grads[k] for k in order], *[delta[k] for k in order],
            *[new_m[k] for k in order], *[new_v[k] for k in order])
```

```python
import functools
import math

import jax
import jax.numpy as jnp
from jax import lax
from jax.experimental import pallas as pl
from jax.experimental.pallas import tpu as pltpu

F32 = jnp.float32
BF16 = jnp.bfloat16

N_DEV = 8
N_PEERS = N_DEV - 1
CHUNK = 128
GROUPS = 8
HEADS = 4
CONV_WIDTH = 3
RMS_EPS = 1e-6
LN_EPS = 1e-5
ADAM_LR = 0.001
ADAM_B1 = 0.9
ADAM_B2 = 0.999
ADAM_EPS = 1e-08
ADAM_WD = 0.01
ADAM_STEP = 10
LANES = 128
BF16_SUBLANES = 16
MXU_WIDTH = 256
VMEM_LIMIT_BYTES = 56 * 1024 * 1024

_NT = (((1,), (1,)), ((), ()))
_TN = (((0,), (0,)), ((), ()))
_SQRT_HALF = 0.7071067811865476
_INV_SQRT_2PI = 0.3989422804014327


def _div(n, pref, align):
    best = None
    for t in range(align, min(n, pref) + 1, align):
        if n % t == 0:
            best = t
    return n if best is None else best


def _chunks(n, width):
    return [(c0, min(width, n - c0)) for c0 in range(0, n, width)]


_ANY = pl.BlockSpec(memory_space=pl.ANY)


class Exchange:
    def __init__(self, arrays, out_shapes, sems, start, finish, forward=None):
        self.arrays, self.out_shapes, self.sems = list(arrays), list(out_shapes), list(sems)
        self.start, self.finish, self.forward = start, finish, forward
        self.results = None


def _call(body, name, grid, ins, outs, scratch=(), exchange=None):
    in_specs = [pl.BlockSpec(*spec[1:3], **({"pipeline_mode": pl.Buffered(1)} if len(spec) > 3 else {}))
                for spec in ins]
    ins = [spec[:3] for spec in ins]
    out_specs = [pl.BlockSpec(bs, im) for _, _, bs, im in outs]
    out_shape = [jax.ShapeDtypeStruct(s, d) for s, d, _, _ in outs]
    arrays = [a for a, _, _ in ins]
    scratch = list(scratch)
    kernel_fn = body
    if exchange is not None:
        n_in, n_out, n_scr = len(ins), len(outs), len(scratch)
        n_xin, n_xout = len(exchange.arrays), len(exchange.out_shapes)
        steps = math.prod(grid)
        forward_step = min(steps - 1, (15 * steps) // 16)

        def kernel_fn(*refs):
            refs = list(refs)
            own_in, x_in = refs[:n_in], refs[n_in:n_in + n_xin]
            refs = refs[n_in + n_xin:]
            own_out, x_out = refs[:n_out], refs[n_out:n_out + n_xout]
            refs = refs[n_out + n_xout:]
            own_scr, x_sems = refs[:n_scr], refs[n_scr:]
            step = 0
            for axis, size in enumerate(grid):
                step = step * size + pl.program_id(axis)

            @pl.when(step == 0)
            def _():
                exchange.start(x_in, x_out, x_sems)

            if exchange.forward is not None:
                @pl.when(step == forward_step)
                def _():
                    exchange.forward(x_in, x_out, x_sems)

            body(*own_in, *own_out, *own_scr)

            @pl.when(step == steps - 1)
            def _():
                exchange.finish(x_in, x_out, x_sems)

        in_specs += [_ANY] * n_xin
        out_specs += [_ANY] * n_xout
        out_shape += exchange.out_shapes
        arrays += exchange.arrays
        scratch += exchange.sems
    res = pl.pallas_call(
        kernel_fn,
        name=name,
        grid=grid,
        in_specs=in_specs,
        out_specs=out_specs,
        out_shape=out_shape,
        scratch_shapes=scratch,
        compiler_params=pltpu.CompilerParams(
            dimension_semantics=("arbitrary",) * len(grid), vmem_limit_bytes=VMEM_LIMIT_BYTES),
    )(*arrays)
    if exchange is not None:
        exchange.results = list(res[len(outs):])
        res = res[:len(outs)]
    return res


def _dot(a, b, dims=None):
    if dims is None:
        return jnp.dot(a, b, preferred_element_type=F32)
    return lax.dot_general(a, b, dims, preferred_element_type=F32)


def _sigmoid(v):
    return 1.0 / (1.0 + jnp.exp(-v))


def _normal_cdf(v):
    return 0.5 * (1.0 + lax.erf(v * _SQRT_HALF))


def _normal_pdf(v):
    return _INV_SQRT_2PI * jnp.exp(-0.5 * v * v)


def _accumulate_product(ref, first, product, skip=False):
    @pl.when(first)
    def _():
        ref[...] = product()

    @pl.when(jnp.logical_not(jnp.logical_or(first, skip)))
    def _():
        ref[...] += product()


def _accumulate(ref, part, first):
    @pl.when(first)
    def _():
        ref[...] = part

    @pl.when(jnp.logical_not(first))
    def _():
        ref[...] += part


def rms_fwd(x, g):
    T, D = x.shape
    tt = _div(T, 512, BF16_SUBLANES)

    def body(x_ref, g_ref, o_ref):
        xv = x_ref[...]
        r = lax.rsqrt(jnp.mean(xv * xv, axis=-1, keepdims=True) + RMS_EPS)
        o_ref[...] = ((xv * r) * g_ref[...]).astype(BF16)

    return _call(body, "rms_fwd", (T // tt,),
                 [(x, (tt, D), lambda i: (i, 0)), (g.reshape(1, D), (1, D), lambda i: (0, 0))],
                 [((T, D), BF16, (tt, D), lambda i: (i, 0))])[0]


def rms_bwd(x, g, dxn, d, scale):
    T, D = x.shape
    tt = _div(T, 256, BF16_SUBLANES)

    def body(x_ref, g_ref, dn_ref, d_ref, dx_ref, dxs_ref, dg_ref):
        xv = x_ref[...]
        r = lax.rsqrt(jnp.mean(xv * xv, axis=-1, keepdims=True) + RMS_EPS)
        xh = xv * r
        dn = dn_ref[...].astype(F32)
        dxh = dn * g_ref[...]
        dx = r * (dxh - xh * jnp.mean(dxh * xh, axis=-1, keepdims=True)) + d_ref[...]
        dx_ref[...] = dx
        dxs_ref[...] = (scale * dx).astype(BF16)
        _accumulate(dg_ref, jnp.sum(dn * xh, axis=0, keepdims=True), pl.program_id(0) == 0)

    row = lambda i: (i, 0)
    fix = lambda i: (0, 0)
    return _call(body, "rms_bwd", (T // tt,),
                 [(x, (tt, D), row), (g.reshape(1, D), (1, D), fix), (dxn, (tt, D), row), (d, (tt, D), row)],
                 [((T, D), F32, (tt, D), row), ((T, D), BF16, (tt, D), row), ((1, D), F32, (1, D), fix)])


def rms_gain_grad(x, dxn):
    T, D = x.shape
    tt = _div(T, 256, 8)

    def body(x_ref, dn_ref, dg_ref):
        xv = x_ref[...]
        r = lax.rsqrt(jnp.mean(xv * xv, axis=-1, keepdims=True) + RMS_EPS)
        _accumulate(dg_ref, jnp.sum(dn_ref[...] * (xv * r), axis=0, keepdims=True), pl.program_id(0) == 0)

    row = lambda i: (i, 0)
    return _call(body, "rms_gain_grad", (T // tt,), [(x, (tt, D), row), (dxn, (tt, D), row)],
                 [((1, D), F32, (1, D), lambda i: (0, 0))])[0]


def loss_head(x, g, target, scale):
    T, D = x.shape
    tt = _div(T, 256, BF16_SUBLANES)

    def body(x_ref, g_ref, t_ref, loss_ref, dx_ref, dxs_ref, dg_ref):
        first = pl.program_id(0) == 0
        xv = x_ref[...]
        gv = g_ref[...]
        r = lax.rsqrt(jnp.mean(xv * xv, axis=-1, keepdims=True) + RMS_EPS)
        xh = xv * r
        err = xh * gv - t_ref[...]
        part = 0.5 * jnp.sum(jnp.mean(err * err, axis=-1, keepdims=True), axis=0, keepdims=True)
        _accumulate(loss_ref, jnp.broadcast_to(part, (1, LANES)), first)
        dy = err * (1.0 / D)
        dxh = dy * gv
        dx = r * (dxh - xh * jnp.mean(dxh * xh, axis=-1, keepdims=True))
        dx_ref[...] = dx
        dxs_ref[...] = (scale * dx).astype(BF16)
        _accumulate(dg_ref, jnp.sum(dy * xh, axis=0, keepdims=True), first)

    row = lambda i: (i, 0)
    fix = lambda i: (0, 0)
    return _call(body, "loss_head", (T // tt,),
                 [(x, (tt, D), row), (g.reshape(1, D), (1, D), fix), (target, (tt, D), row)],
                 [((1, LANES), F32, (1, LANES), fix), ((T, D), F32, (tt, D), row),
                  ((T, D), BF16, (tt, D), row), ((1, D), F32, (1, D), fix)])


def mm_nn(a, w3, name, out_dtype=BF16, res=None, res_scale=1.0, tm_pref=1024, tn_pref=1024, group=1,
          exchange=None):
    M, K = a.shape
    nb, _, ns = w3.shape
    tn = ns if group > 1 else _div(ns, tn_pref, LANES)
    per = ns // tn
    tm = _div(M, tm_pref, BF16_SUBLANES)
    wide = group * tn
    ins = [(a, (tm, K), lambda j, m: (m, 0)), (w3, (group, K, tn), lambda j, m: (j // per, 0, j % per))]
    if res is not None:
        ins.append((res, (tm, wide), lambda j, m: (m, j)))

    def body(*refs):
        a_ref, w_ref = refs[0], refs[1]
        o_ref = refs[-1]
        for k in range(group):
            cols = slice(k * tn, (k + 1) * tn)
            acc = _dot(a_ref[...], w_ref[k])
            if res is not None:
                acc = refs[2][:, cols] + res_scale * acc
            o_ref[:, cols] = acc.astype(o_ref.dtype)

    return _call(body, name, (nb * per // group, M // tm), ins,
                 [((M, nb * ns), out_dtype, (tm, wide), lambda j, m: (m, j))], exchange=exchange)[0]


def mm_nt(a_in, w3, name, out_dtype, M, tm_pref=1024, to_pref=2048, group=1, exchange=None):
    nb, Ko, ns = w3.shape
    to = _div(Ko, to_pref, LANES)
    tm = _div(M, tm_pref, BF16_SUBLANES)
    if isinstance(a_in, tuple):
        a, a_bs, a_im = a_in
        a_bs = tuple(tm if s == "tm" else s for s in a_bs)
    else:
        a, a_bs, a_im = a_in, (tm, group * ns), lambda m, o, b: (m, b)
    steps = nb // group
    narrow_out = steps > 1 and out_dtype != F32

    def product(a_ref, w_ref):
        p = None
        for k in range(group):
            pk = _dot(a_ref[:, k * ns:(k + 1) * ns], w_ref[k], _NT)
            p = pk if p is None else p + pk
        return p

    def body(a_ref, w_ref, o_ref, *acc):
        b = pl.program_id(2)
        if steps == 1:
            o_ref[...] = product(a_ref, w_ref).astype(o_ref.dtype)
        elif not narrow_out:
            _accumulate_product(o_ref, b == 0, lambda: product(a_ref, w_ref))
        else:
            acc_ref, = acc
            _accumulate_product(acc_ref, b == 0, lambda: product(a_ref, w_ref), skip=b == steps - 1)

            @pl.when(b == steps - 1)
            def _():
                o_ref[...] = (acc_ref[...] + product(a_ref, w_ref)).astype(o_ref.dtype)

    return _call(body, name, (M // tm, Ko // to, steps),
                 [(a, a_bs, a_im), (w3, (group, to, ns), lambda m, o, b: (b, o, 0))],
                 [((M, Ko), out_dtype, (tm, to), lambda m, o, b: (m, o))],
                 scratch=[pltpu.VMEM((tm, to), F32)] if narrow_out else [], exchange=exchange)[0]


def mm_nt_rms_bwd(a, w3, name, x, g, d, scale, exchange=None):
    nb, D, ns = w3.shape
    M = x.shape[0]
    tm = _div(M, 512, BF16_SUBLANES)
    a_bs, a_im = (tm, ns), lambda m, b: (m, b)

    rc = _div(tm, 64, BF16_SUBLANES)

    def body(a_ref, w_ref, x_ref, g_ref, d_ref, dx_ref, dxs_ref, dg_ref, acc_ref):
        m, b = pl.program_id(0), pl.program_id(1)
        _accumulate_product(acc_ref, b == 0, lambda: _dot(a_ref[...], w_ref[...], _NT))

        @pl.when(b == nb - 1)
        def _():
            gv = g_ref[...]

            def piece(c, dg):
                rows = pl.ds(pl.multiple_of(c * rc, rc), rc)
                dn = acc_ref[rows, :]
                xv = x_ref[rows, :]
                r = lax.rsqrt(jnp.mean(xv * xv, axis=-1, keepdims=True) + RMS_EPS)
                xh = xv * r
                dxh = dn * gv
                dx = r * (dxh - xh * jnp.mean(dxh * xh, axis=-1, keepdims=True)) + d_ref[rows, :]
                dx_ref[rows, :] = dx
                dxs_ref[rows, :] = (scale * dx).astype(BF16)
                return dg + jnp.sum(dn * xh, axis=0, keepdims=True)

            _accumulate(dg_ref, lax.fori_loop(0, tm // rc, piece, jnp.zeros((1, D), F32)), m == 0)

    row = lambda m, b: (m, 0)
    fix = lambda m, b: (0, 0)
    w_spec = (w3, (None, D, ns), lambda m, b: (b, 0, 0)) + (("single",) if nb == 1 else ())
    return _call(body, name, (M // tm, nb),
                 [(a, a_bs, a_im), w_spec,
                  (x, (tm, D), row), (g.reshape(1, D), (1, D), fix), (d, (tm, D), row)],
                 [((M, D), F32, (tm, D), row), ((M, D), BF16, (tm, D), row), ((1, D), F32, (1, D), fix)],
                 scratch=[pltpu.VMEM((tm, D), F32)], exchange=exchange)


def mm_tn(a, b_in, name, nbo, ns, tka_pref=1024, tt_pref=2048, tn_pref=2048, group=1, exchange=None):
    T, Ka = a.shape
    tt = _div(T, tt_pref, BF16_SUBLANES)
    tka = _div(Ka, tka_pref, LANES)
    if isinstance(b_in, tuple):
        b, b_bs, b_im = b_in
        b_bs = tuple(tt if s == "tt" else s for s in b_bs)
        tn, per = ns, 1
    else:
        tn = ns if group > 1 else _div(ns, tn_pref, LANES)
        per = ns // tn
        b, b_bs, b_im = b_in, (tt, group * tn), lambda i, j, t: (t, j)
    nt = T // tt

    def body(a_ref, b_ref, o_ref, acc_ref):
        t = pl.program_id(2)
        product = lambda: _dot(a_ref[...], b_ref[...], _TN)

        def store(total):
            for k in range(group):
                o_ref[k] = total[:, k * tn:(k + 1) * tn].astype(BF16)

        if nt == 1:
            store(product())
        else:
            _accumulate_product(acc_ref, t == 0, product, skip=t == nt - 1)

            @pl.when(t == nt - 1)
            def _():
                store(acc_ref[...] + product())

    return _call(body, name, (Ka // tka, nbo * per // group, nt),
                 [(a, (tt, tka), lambda i, j, t: (t, i)), (b, b_bs, b_im)],
                 [((nbo, Ka, ns), BF16, (group, tka, tn), lambda i, j, t: (j // per, i, j % per))],
                 scratch=[pltpu.VMEM((tka, group * tn), F32)], exchange=exchange)[0]


def ffn_up(xn, w13, exchange=None):
    T, D = xn.shape
    nb, _, ns = w13.shape
    half = nb // 2
    F = half * ns
    tm = _div(T, 512, BF16_SUBLANES)
    pair = 2 if half % 2 == 0 else 1

    def columns(w_ref, c0, cw):
        k, off = divmod(c0, ns)
        if off + cw <= ns:
            return w_ref[k, :, off:off + cw]
        return jnp.concatenate([w_ref[k, :, off:ns], w_ref[k + 1, :, 0:off + cw - ns]], axis=1)

    def body(x_ref, wg_ref, wu_ref, fac_ref, act_ref):
        xv = x_ref[...]
        for c0, cw in _chunks(pair * ns, MXU_WIDTH):
            cols = slice(c0, c0 + cw)
            gate = _dot(xv, columns(wg_ref, c0, cw))
            up = _dot(xv, columns(wu_ref, c0, cw))
            s = _sigmoid(gate)
            silu = gate * s
            fac_ref[0, :, cols] = (up * (s * (1.0 + gate * (1.0 - s)))).astype(BF16)
            fac_ref[1, :, cols] = silu.astype(BF16)
            act_ref[:, cols] = (silu * up).astype(BF16)

    tn = pair * ns
    return _call(body, "ffn_up", (half // pair, T // tm),
                 [(xn, (tm, D), lambda j, m: (m, 0)),
                  (w13, (pair, D, ns), lambda j, m: (j, 0, 0), "single"),
                  (w13, (pair, D, ns), lambda j, m: (j + half // pair, 0, 0), "single")],
                 [((2, T, F), BF16, (2, tm, tn), lambda j, m: (0, m, j)),
                  ((T, F), BF16, (tm, tn), lambda j, m: (m, j))], exchange=exchange)


def ffn_dact(dy, w2, fac, exchange=None):
    T, D = dy.shape
    F = w2.shape[0]
    tm = _div(T, 512, BF16_SUBLANES)
    tn = _div(F, F // 2, MXU_WIDTH)

    def body(dy_ref, w_ref, fac_ref, dh_ref):
        dyv = dy_ref[...]
        for c0, cw in _chunks(tn, MXU_WIDTH):
            cols = slice(c0, c0 + cw)
            da = _dot(dyv, w_ref[cols, :], _NT)
            dh_ref[0, :, cols] = (da * fac_ref[0, :, cols].astype(F32)).astype(BF16)
            dh_ref[1, :, cols] = (da * fac_ref[1, :, cols].astype(F32)).astype(BF16)

    return _call(body, "ffn_dact", (F // tn, T // tm),
                 [(dy, (tm, D), lambda j, m: (m, 0)), (w2, (tn, D), lambda j, m: (j, 0), "single"),
                  (fac, (2, tm, tn), lambda j, m: (0, m, j))],
                 [((2, T, F), BF16, (2, tm, tn), lambda j, m: (0, m, j))], exchange=exchange)[0]


def ffn_fwd(x, norm_g, w13, get_w2, up_exchange=None, down_exchange=None):
    xn = rms_fwd(x, norm_g)
    fac, act = ffn_up(xn, w13, exchange=up_exchange)
    F = act.shape[1]
    y = mm_nn(act, get_w2().reshape(1, F, -1), "ffn_down", F32, res=x, res_scale=0.5, tm_pref=512,
              exchange=down_exchange)
    return y, (x, xn, fac, act)


def ffn_bwd(d, dys, saved, norm_g, w13, w2, scale_out):
    x, xn, fac, act = saved
    T, D = x.shape
    nb, _, ns = w13.shape
    half = nb // 2
    F = half * ns
    dh = ffn_dact(dys, w2.reshape(F, D), fac)
    dw2 = mm_tn(act, dys, "ffn_dw2", 1, D, tka_pref=ns, tn_pref=1024)
    send_w2 = scatter_exchange([dw2.reshape(N_DEV, F // N_DEV, D)])
    dw13 = mm_tn(xn, (dh, (None, "tt", ns), lambda i, j, t: (j // half, t, j % half)), "ffn_dw13", nb, ns,
                 exchange=send_w2)
    send_w13 = scatter_exchange([dw13])
    pair = 2 if half % 2 == 0 else 1
    per = half // pair
    dxn = mm_nt((dh, (None, "tm", pair * ns), lambda m, o, b: (b // per, m, b % per)), w13, "ffn_dxn", BF16, T,
                tm_pref=512, group=pair, exchange=send_w13)
    dx, dxs, dg = rms_bwd(x, norm_g, dxn, d, scale_out)
    return dx, dxs, dg, send_w13.results[0], send_w2.results[0]


def _gmlp_parts(p_ref, lng_ref, lnb_ref):
    E = lng_ref.shape[-1]
    pv = p_ref[...].astype(F32)
    cdf = _normal_cdf(pv)
    z = pv * cdf
    u = z[:, :E]
    vp = z[:, E:]
    mu = jnp.mean(vp, axis=-1, keepdims=True)
    xc = vp - mu
    rstd = lax.rsqrt(jnp.mean(xc * xc, axis=-1, keepdims=True) + LN_EPS)
    vh = xc * rstd
    v = vh * lng_ref[...] + lnb_ref[...]
    return u, vh, rstd, v, pv, cdf


def _causal_ws(ws_ref, g):
    keep = lax.broadcasted_iota(jnp.int32, (CHUNK, CHUNK), 0) >= lax.broadcasted_iota(jnp.int32, (CHUNK, CHUNK), 1)
    return jnp.where(keep, ws_ref[g], 0.0).astype(BF16), keep


def gmlp_mid_fwd(p, ln_g, ln_b, w_s, bias_full):
    T, E2 = p.shape
    E = E2 // 2
    gd = E // GROUPS
    tm = _div(T, 256, CHUNK)
    fix2 = lambda i: (0, 0)

    def body(p_ref, lng_ref, lnb_ref, ws_ref, bias_ref, o_ref):
        u, _, _, v, _, _ = _gmlp_parts(p_ref, lng_ref, lnb_ref)
        vb = v.astype(BF16)
        for g in range(GROUPS):
            wm, _ = _causal_ws(ws_ref, g)
            cols = slice(g * gd, (g + 1) * gd)
            for c in range(tm // CHUNK):
                rows = slice(c * CHUNK, (c + 1) * CHUNK)
                f = _dot(wm, vb[rows, cols]) + bias_ref[:, cols]
                o_ref[rows, cols] = (u[rows, cols] * f).astype(BF16)

    return _call(body, "gmlp_mid_fwd", (T // tm,),
                 [(p, (tm, E2), lambda i: (i, 0)), (ln_g, (1, E), fix2), (ln_b, (1, E), fix2),
                  (w_s, (GROUPS, CHUNK, CHUNK), lambda i: (0, 0, 0)), (bias_full, (CHUNK, E), fix2)],
                 [((T, E), BF16, (tm, E), lambda i: (i, 0))])[0]


def gmlp_mid_bwd(p, dgated, ln_g, ln_b, w_s, bias_full, exchange=None):
    T, E2 = p.shape
    E = E2 // 2
    gd = E // GROUPS
    tm = _div(T, 256, CHUNK)
    nsteps = T // tm
    fix2 = lambda i: (0, 0)
    fix3 = lambda i: (0, 0, 0)

    def body(p_ref, dg_ref, lng_ref, lnb_ref, ws_ref, bias_ref,
             dp_ref, dws_ref, dbs_ref, dlng_ref, dlnb_ref, f_sc, dv_sc, db_sc):
        i = pl.program_id(0)
        first = i == 0
        u, vh, rstd, v, pv, cdf = _gmlp_parts(p_ref, lng_ref, lnb_ref)
        vb = v.astype(BF16)
        dgt = dg_ref[...].astype(F32)
        df = dgt * u
        dfb = df.astype(BF16)
        for g in range(GROUPS):
            wm, keep = _causal_ws(ws_ref, g)
            cols = slice(g * gd, (g + 1) * gd)
            dw = None
            dbg = None
            for c in range(tm // CHUNK):
                rows = slice(c * CHUNK, (c + 1) * CHUNK)
                f_sc[rows, cols] = _dot(wm, vb[rows, cols]) + bias_ref[:, cols]
                dv_sc[rows, cols] = _dot(wm, dfb[rows, cols], _TN)
                part = _dot(dfb[rows, cols], vb[rows, cols], _NT)
                dw = part if dw is None else dw + part
                dbg = df[rows, cols] if dbg is None else dbg + df[rows, cols]
            dw = jnp.where(keep, dw, 0.0)

            @pl.when(first)
            def _():
                dws_ref[g] = dw
                db_sc[:, cols] = dbg

            @pl.when(jnp.logical_not(first))
            def _():
                dws_ref[g] += dw
                db_sc[:, cols] += dbg

        du = dgt * f_sc[...]
        dv = dv_sc[...]
        _accumulate(dlng_ref, jnp.sum(dv * vh, axis=0, keepdims=True), first)
        _accumulate(dlnb_ref, jnp.sum(dv, axis=0, keepdims=True), first)
        dvh = dv * lng_ref[...]
        dvp = rstd * (dvh - jnp.mean(dvh, axis=-1, keepdims=True)
                      - vh * jnp.mean(dvh * vh, axis=-1, keepdims=True))
        gp = cdf + pv * _normal_pdf(pv)
        dp_ref[:, :E] = (du * gp[:, :E]).astype(BF16)
        dp_ref[:, E:] = (dvp * gp[:, E:]).astype(BF16)

        @pl.when(i == nsteps - 1)
        def _():
            for g in range(GROUPS):
                tot = jnp.sum(db_sc[:, g * gd:(g + 1) * gd], axis=-1, keepdims=True)
                dbs_ref[g] = jnp.broadcast_to(tot, (CHUNK, LANES))

    return _call(body, "gmlp_mid_bwd", (nsteps,),
                 [(p, (tm, E2), lambda i: (i, 0)), (dgated, (tm, E), lambda i: (i, 0)),
                  (ln_g, (1, E), fix2), (ln_b, (1, E), fix2),
                  (w_s, (GROUPS, CHUNK, CHUNK), fix3), (bias_full, (CHUNK, E), fix2)],
                 [((T, E2), BF16, (tm, E2), lambda i: (i, 0)),
                  ((GROUPS, CHUNK, CHUNK), F32, (GROUPS, CHUNK, CHUNK), fix3),
                  ((GROUPS, CHUNK, LANES), F32, (GROUPS, CHUNK, LANES), fix3),
                  ((1, E), F32, (1, E), fix2), ((1, E), F32, (1, E), fix2)],
                 scratch=[pltpu.VMEM((tm, E), F32), pltpu.VMEM((tm, E), F32), pltpu.VMEM((CHUNK, E), F32)],
                 exchange=exchange)


HALO = 16


def _row_of(block, r):
    rows = lax.broadcasted_iota(jnp.int32, block.shape, 0)
    return jnp.sum(jnp.where(rows == r, block, 0.0), axis=0, keepdims=True)


def _shift_down(z, k, fill):
    out = pltpu.roll(z, k, 0)
    rows = lax.broadcasted_iota(jnp.int32, z.shape, 0)
    for t in range(k):
        out = jnp.where(rows == t, fill[t], out)
    return out


def _shift_up(z, k, fill):
    n = z.shape[0]
    out = pltpu.roll(z, n - k, 0)
    rows = lax.broadcasted_iota(jnp.int32, z.shape, 0)
    for j in range(k):
        out = jnp.where(rows == n - k + j, fill[j], out)
    return out


def _conv_parts(p_ref, prev_ref, cw_ref, is_first):
    D = cw_ref.shape[-1]
    pv = p_ref[...].astype(F32)
    bg, cg, val = pv[:, :D], pv[:, D:2 * D], pv[:, 2 * D:]
    z = cg * val
    pp = prev_ref[...].astype(F32)
    zp = jnp.where(is_first, 0.0, pp[:, D:2 * D] * pp[:, 2 * D:])
    zl1 = _row_of(zp, HALO - 1)
    zl2 = _row_of(zp, HALO - 2)
    z1 = _shift_down(z, 1, [zl1])
    z2 = _shift_down(z, 2, [zl2, zl1])
    conv = z2 * cw_ref[0:1, :] + z1 * cw_ref[1:2, :] + z * cw_ref[2:3, :]
    return bg, cg, val, z, z1, z2, conv


def conv_mid_fwd(p, cw):
    T, D3 = p.shape
    D = D3 // 3
    tm = _div(T, 256, HALO)
    per = tm // HALO

    def body(p_ref, prev_ref, cw_ref, o_ref):
        bg, _, _, _, _, _, conv = _conv_parts(p_ref, prev_ref, cw_ref, pl.program_id(0) == 0)
        o_ref[...] = (bg * conv).astype(BF16)

    return _call(body, "conv_mid_fwd", (T // tm,),
                 [(p, (tm, D3), lambda i: (i, 0)),
                  (p, (HALO, D3), lambda i: (jnp.maximum(i * per - 1, 0), 0)),
                  (cw, (CONV_WIDTH, D), lambda i: (0, 0))],
                 [((T, D), BF16, (tm, D), lambda i: (i, 0))])[0]


def conv_mid_bwd(p, dgated, cw, exchange=None):
    T, D3 = p.shape
    D = D3 // 3
    tm = _div(T, 256, HALO)
    per = tm // HALO
    nsteps = T // tm
    last_halo = T // HALO - 1
    nxt = lambda i: (jnp.minimum((i + 1) * per, last_halo), 0)

    def body(p_ref, prev_ref, next_ref, dg_ref, dgn_ref, cw_ref, dp_ref, dcw_ref):
        i = pl.program_id(0)
        bg, cg, val, z, z1, z2, conv = _conv_parts(p_ref, prev_ref, cw_ref, i == 0)
        dgt = dg_ref[...].astype(F32)
        dconv = dgt * bg
        dcn = jnp.where(i == nsteps - 1, 0.0, dgn_ref[...].astype(F32) * next_ref[:, :D].astype(F32))
        n0 = _row_of(dcn, 0)
        n1 = _row_of(dcn, 1)
        up1 = _shift_up(dconv, 1, [n0])
        up2 = _shift_up(dconv, 2, [n0, n1])
        dz = dconv * cw_ref[2:3, :] + up1 * cw_ref[1:2, :] + up2 * cw_ref[0:1, :]
        dp_ref[:, :D] = (dgt * conv).astype(BF16)
        dp_ref[:, D:2 * D] = (dz * val).astype(BF16)
        dp_ref[:, 2 * D:] = (dz * cg).astype(BF16)
        first = i == 0
        parts = (jnp.sum(dconv * z2, axis=0, keepdims=True), jnp.sum(dconv * z1, axis=0, keepdims=True),
                 jnp.sum(dconv * z, axis=0, keepdims=True))

        @pl.when(first)
        def _():
            for k in range(CONV_WIDTH):
                dcw_ref[k:k + 1, :] = parts[k]

        @pl.when(jnp.logical_not(first))
        def _():
            for k in range(CONV_WIDTH):
                dcw_ref[k:k + 1, :] += parts[k]

    return _call(body, "conv_mid_bwd", (nsteps,),
                 [(p, (tm, D3), lambda i: (i, 0)),
                  (p, (HALO, D3), lambda i: (jnp.maximum(i * per - 1, 0), 0)),
                  (p, (HALO, D3), nxt),
                  (dgated, (tm, D), lambda i: (i, 0)),
                  (dgated, (HALO, D), nxt),
                  (cw, (CONV_WIDTH, D), lambda i: (0, 0))],
                 [((T, D3), BF16, (tm, D3), lambda i: (i, 0)),
                  ((CONV_WIDTH, D), F32, (CONV_WIDTH, D), lambda i: (0, 0))], exchange=exchange)


def mixer_bwd_common(d, dys, saved, norm_g, w_in, w_out, mid_bwd, scale_out, dwkv):
    x, hn, p, gated = saved
    T, D = x.shape
    E = gated.shape[1]
    w_out3 = w_out.reshape(1, E, D)
    dgated = mm_nt(dys, w_out3, "mix_dgated", BF16, T)
    dw_out = mm_tn(gated, dys, "mix_dwout", 1, D)
    send_wout = scatter_exchange([dw_out.reshape(N_DEV, E // N_DEV, D)])
    dp, extra = mid_bwd(p, dgated, send_wout)
    nb, _, ns = w_in.shape
    group = max(g for g in (1, 2, 4, 8) if nb % g == 0 and g * ns <= D)
    send_wkv = scatter_exchange([dwkv])
    dw_in = mm_tn(hn, dp, "mix_dwin", nb, ns, group=group, exchange=send_wkv)
    send_win = scatter_exchange([dw_in])
    dhn = mm_nt(dp, w_in, "mix_dhn", BF16, T, tm_pref=512, group=group, exchange=send_win)
    dx, dxs, dg = rms_bwd(x, norm_g, dhn, d, scale_out)
    return dx, dxs, dg, send_win.results[0], send_wout.results[0], send_wkv.results[0], extra


def _softmax_rows(s):
    e = jnp.exp(s - jnp.max(s, axis=-1, keepdims=True))
    return e / jnp.sum(e, axis=-1, keepdims=True)


def attn_fwd(q, kv):
    T, D = q.shape
    M = kv.shape[0]
    hd = D // HEADS
    scale = hd ** -0.5
    tm = _div(T, 512, BF16_SUBLANES)

    def body(q_ref, kv_ref, o_ref):
        for h in range(HEADS):
            cols = slice(h * hd, (h + 1) * hd)
            s = _dot(q_ref[:, cols], kv_ref[:, cols], _NT) * scale
            pr = _softmax_rows(s).astype(BF16)
            o_ref[:, cols] = _dot(pr, kv_ref[:, D + h * hd:D + (h + 1) * hd]).astype(BF16)

    return _call(body, "attn_fwd", (T // tm,),
                 [(q, (tm, D), lambda i: (i, 0)), (kv, (M, 2 * D), lambda i: (0, 0))],
                 [((T, D), BF16, (tm, D), lambda i: (i, 0))])[0]


def attn_bwd(q, do, kv, exchange=None):
    T, D = q.shape
    M = kv.shape[0]
    hd = D // HEADS
    scale = hd ** -0.5
    tm = _div(T, 512, BF16_SUBLANES)
    nsteps = T // tm

    def body(q_ref, do_ref, kv_ref, dq_ref, dkv_ref, acc_ref):
        i = pl.program_id(0)
        for h in range(HEADS):
            cols = slice(h * hd, (h + 1) * hd)
            vcols = slice(D + h * hd, D + (h + 1) * hd)
            qh = q_ref[:, cols]
            kh = kv_ref[:, cols]
            doh = do_ref[:, cols]
            pr = _softmax_rows(_dot(qh, kh, _NT) * scale)
            dpr = _dot(doh, kv_ref[:, vcols], _NT)
            ds = (pr * (dpr - jnp.sum(dpr * pr, axis=-1, keepdims=True)) * scale).astype(BF16)
            dq_ref[:, cols] = _dot(ds, kh).astype(BF16)
            dk = _dot(ds, qh, _TN)
            dv = _dot(pr.astype(BF16), doh, _TN)

            @pl.when(i == 0)
            def _():
                acc_ref[:, cols] = dk
                acc_ref[:, vcols] = dv

            @pl.when(i > 0)
            def _():
                acc_ref[:, cols] += dk
                acc_ref[:, vcols] += dv

        @pl.when(i == nsteps - 1)
        def _():
            dkv_ref[...] = acc_ref[...].astype(BF16)

    return _call(body, "attn_bwd", (nsteps,),
                 [(q, (tm, D), lambda i: (i, 0)), (do, (tm, D), lambda i: (i, 0)),
                  (kv, (M, 2 * D), lambda i: (0, 0))],
                 [((T, D), BF16, (tm, D), lambda i: (i, 0)), ((M, 2 * D), BF16, (M, 2 * D), lambda i: (0, 0))],
                 scratch=[pltpu.VMEM((M, 2 * D), F32)], exchange=exchange)


def xattn_fwd(x, mem, xnorm_g, mnorm_g, wq, wkv, wo):
    D = x.shape[1]
    hq = rms_fwd(x, xnorm_g)
    mn = rms_fwd(mem, mnorm_g)
    q = mm_nn(hq, wq.reshape(1, D, D), "xattn_q")
    kv = mm_nn(mn, wkv, "xattn_kv")
    o = attn_fwd(q, kv)
    y = mm_nn(o, wo.reshape(1, D, D), "xattn_out", F32, res=x)
    return y, (x, hq, mn, q, kv, o)


def xattn_bwd(d, dys, saved, mem, xnorm_g, wq, wkv, wo, scale_out):
    x, hq, mn, q, kv, o = saved
    T, D = x.shape
    M = mem.shape[0]
    do = mm_nt(dys, wo.reshape(1, D, D), "xattn_do", BF16, T)
    rows = D // N_DEV
    dwo = mm_tn(o, dys, "xattn_dwo", 1, D)
    send_wo = scatter_exchange([dwo.reshape(N_DEV, rows, D)])
    dq, dkv = attn_bwd(q, do, kv, exchange=send_wo)
    nb, _, ns = wkv.shape
    dwkv = mm_tn(mn, dkv, "xattn_dwkv", nb, ns)
    dmn = mm_nt(dkv, wkv, "xattn_dmn", F32, M)
    dgm = rms_gain_grad(mem, dmn)
    dwq = mm_tn(hq, dq, "xattn_dwq", 1, D)
    send_wq = scatter_exchange([dwq.reshape(N_DEV, rows, D)])
    dx, dxs, dgx = mm_nt_rms_bwd(dq, wq.reshape(1, D, D), "xattn_dhq", x, xnorm_g, d, scale_out,
                                 exchange=send_wq)
    return dx, dxs, dgx, dgm, send_wq.results[0], dwkv, send_wo.results[0]


def _mesh_places():
    x, y, c = lax.axis_index("x"), lax.axis_index("y"), lax.axis_index("c")
    chips = [(1 - x, y), (x, 1 - y), (1 - x, 1 - y)]
    return (x, y, c), (x, y, 1 - c), chips


def _slot(place):
    return 4 * place[0] + 2 * place[1] + place[2]


def _exchange_sems(n):
    return [pltpu.SemaphoreType.DMA((n * N_PEERS,)), pltpu.SemaphoreType.DMA((n * N_PEERS,)),
            pltpu.SemaphoreType.DMA((n,))]


def gather_exchange(shards):
    n = len(shards)
    shapes = [a.shape if l is None else a.shape[1:] for a, l in shards]

    def parts(x_in, x_out, sems):
        ins = [r if l is None else r.at[l] for r, (_, l) in zip(x_in, shards)]
        send_sems, recv_sems, local_sems = sems
        me, sibling, chips = _mesh_places()

        def copy(a, k, block, to, src=None):
            dst = x_out[a].at[_slot(block)]
            return pltpu.make_async_remote_copy(
                src_ref=dst if src is None else src, dst_ref=dst,
                send_sem=send_sems.at[a * N_PEERS + k], recv_sem=recv_sems.at[a * N_PEERS + k],
                device_id=to, device_id_type=pl.DeviceIdType.MESH)

        mine = [pltpu.make_async_copy(ins[a], x_out[a].at[_slot(me)], local_sems.at[a]) for a in range(n)]
        first = []
        for a in range(n):
            first.append(copy(a, 0, me, sibling, src=ins[a]))
            first += [copy(a, 1 + j, me, (*chip, me[2]), src=ins[a]) for j, chip in enumerate(chips)]
        return me, sibling, chips, copy, mine, first

    def start(x_in, x_out, sems):
        _, _, _, _, mine, first = parts(x_in, x_out, sems)
        for cp in mine + first:
            cp.start()

    def forward(x_in, x_out, sems):
        me, sibling, chips, copy, _, _ = parts(x_in, x_out, sems)
        for j, chip in enumerate(chips):
            for a in range(n):
                copy(a, 1 + j, (*chip, me[2]), me).wait_recv()
                copy(a, 4 + j, (*chip, me[2]), sibling).start()

    def finish(x_in, x_out, sems):
        me, sibling, chips, copy, mine, first = parts(x_in, x_out, sems)
        for a in range(n):
            copy(a, 0, sibling, me).wait_recv()
        for j, chip in enumerate(chips):
            for a in range(n):
                copy(a, 4 + j, (*chip, 1 - me[2]), me).wait_recv()
        for cp in first:
            cp.wait_send()
        for j, chip in enumerate(chips):
            for a in range(n):
                copy(a, 4 + j, (*chip, me[2]), sibling).wait_send()
        for cp in mine:
            cp.wait()

    return Exchange([a for a, _ in shards],
                    [jax.ShapeDtypeStruct((N_DEV,) + tuple(s), a.dtype) for s, (a, _) in zip(shapes, shards)],
                    _exchange_sems(n), start, finish, forward)


def _all_peers(me, chips):
    c = me[2]
    return [(me[0], me[1], 1 - c)] + [(*chip, c) for chip in chips] + [(*chip, 1 - c) for chip in chips]


def scatter_exchange(grads):
    n = len(grads)

    def parts(x_in, x_out, sems):
        send_sems, recv_sems, local_sems = sems
        me, _, chips = _mesh_places()
        mine = [pltpu.make_async_copy(x_in[a].at[_slot(me)], x_out[a].at[_slot(me)], local_sems.at[a])
                for a in range(n)]
        sends, recvs = [], []
        for a in range(n):
            for k, peer in enumerate(_all_peers(me, chips)):
                sem = dict(send_sem=send_sems.at[a * N_PEERS + k], recv_sem=recv_sems.at[a * N_PEERS + k],
                           device_id=peer, device_id_type=pl.DeviceIdType.MESH)
                sends.append(pltpu.make_async_remote_copy(
                    src_ref=x_in[a].at[_slot(peer)], dst_ref=x_out[a].at[_slot(me)], **sem))
                recvs.append(pltpu.make_async_remote_copy(
                    src_ref=x_in[a].at[_slot(peer)], dst_ref=x_out[a].at[_slot(peer)], **sem))
        return mine, sends, recvs

    def start(x_in, x_out, sems):
        mine, sends, _ = parts(x_in, x_out, sems)
        for cp in mine + sends:
            cp.start()

    def finish(x_in, x_out, sems):
        mine, sends, recvs = parts(x_in, x_out, sems)
        for cp in recvs:
            cp.wait_recv()
        for cp in sends:
            cp.wait_send()
        for cp in mine:
            cp.wait()

    return Exchange(grads, [jax.ShapeDtypeStruct(g.shape, g.dtype) for g in grads],
                    _exchange_sems(n), start, finish)


def run_exchange(exchange, name):
    n_in, n_out = len(exchange.arrays), len(exchange.out_shapes)

    def body(*refs):
        x_in, x_out, sems = refs[:n_in], refs[n_in:n_in + n_out], refs[n_in + n_out:]
        exchange.start(x_in, x_out, sems)
        if exchange.forward is not None:
            exchange.forward(x_in, x_out, sems)
        exchange.finish(x_in, x_out, sems)

    exchange.results = list(pl.pallas_call(
        body, name=name, out_shape=exchange.out_shapes, in_specs=[_ANY] * n_in, out_specs=[_ANY] * n_out,
        scratch_shapes=exchange.sems)(*exchange.arrays))
    return exchange.results


def small_all_reduce(vec):
    R = vec.shape[0]

    def body(v_ref, o_ref, all_ref, send_sems, recv_sems):
        me, _, chips = _mesh_places()
        peers = _all_peers(me, chips)
        all_ref[_slot(me)] = v_ref[...]
        sends, recvs = [], []
        for k, peer in enumerate(peers):
            sem = dict(send_sem=send_sems.at[k], recv_sem=recv_sems.at[k],
                       device_id=peer, device_id_type=pl.DeviceIdType.MESH)
            sends.append(pltpu.make_async_remote_copy(src_ref=v_ref, dst_ref=all_ref.at[_slot(me)], **sem))
            recvs.append(pltpu.make_async_remote_copy(src_ref=v_ref, dst_ref=all_ref.at[_slot(peer)], **sem))
        for cp in sends:
            cp.start()
        for cp in recvs:
            cp.wait_recv()
        for cp in sends:
            cp.wait_send()
        acc = all_ref[0]
        for s in range(1, N_DEV):
            acc = acc + all_ref[s]
        o_ref[...] = acc

    return pl.pallas_call(
        body, name="small_all_reduce",
        out_shape=jax.ShapeDtypeStruct(vec.shape, F32),
        in_specs=[pl.BlockSpec(memory_space=pltpu.VMEM)], out_specs=pl.BlockSpec(memory_space=pltpu.VMEM),
        scratch_shapes=[pltpu.VMEM((N_DEV, R, LANES), F32), pltpu.SemaphoreType.DMA((N_PEERS,)),
                        pltpu.SemaphoreType.DMA((N_PEERS,))],
    )(vec)


def _adamw_math(w, g, m, v):
    m2 = ADAM_B1 * m + (1.0 - ADAM_B1) * g
    v2 = ADAM_B2 * v + (1.0 - ADAM_B2) * (g * g)
    m_hat = m2 / (1.0 - ADAM_B1 ** ADAM_STEP)
    v_hat = v2 / (1.0 - ADAM_B2 ** ADAM_STEP)
    delta = -ADAM_LR * (m_hat / (jnp.sqrt(v_hat) + ADAM_EPS) + ADAM_WD * w)
    return delta, m2, v2


def adamw_sharded(partials, w, m, v):
    L, r, c = w.shape
    row_bytes = 2 * c * (L * N_DEV * 2 + 7 * 4)
    tr = _div(r, max(BF16_SUBLANES, min(512, (VMEM_LIMIT_BYTES * 3 // 4) // row_bytes)), BF16_SUBLANES)
    nt = r // tr

    def part_map(l0):
        return lambda l, t: (0, jnp.where(l == l0, t, jnp.where(l < l0, 0, nt - 1)), 0)

    def body(*refs):
        parts = refs[:L]
        w_ref, m_ref, v_ref, g_out, d_out, m_out, v_out = refs[L:]
        layer = pl.program_id(0)
        for l0 in range(L):
            @pl.when(layer == l0)
            def _():
                g = parts[l0][0].astype(F32)
                for s in range(1, N_DEV):
                    g = g + parts[l0][s].astype(F32)
                delta, m2, v2 = _adamw_math(w_ref[...], g, m_ref[...], v_ref[...])
                g_out[...] = g
                d_out[...] = delta
                m_out[...] = m2
                v_out[...] = v2

    own = lambda l, t: (l, t, 0)
    return _call(body, "adamw_sharded", (L, nt),
                 [(p, (N_DEV, tr, c), part_map(l0)) for l0, p in enumerate(partials)]
                 + [(w, (None, tr, c), own), (m, (None, tr, c), own), (v, (None, tr, c), own)],
                 [((L, r, c), F32, (None, tr, c), own)] * 4)


def adamw_flat(g, w, m, v):
    shape = g.shape

    def body(g_ref, w_ref, m_ref, v_ref, d_out, m_out, v_out):
        delta, m2, v2 = _adamw_math(w_ref[...], g_ref[...], m_ref[...], v_ref[...])
        d_out[...] = delta
        m_out[...] = m2
        v_out[...] = v2

    whole = lambda: (0, 0)
    return _call(body, "adamw_flat", (), [(a, shape, whole) for a in (g, w, m, v)],
                 [(shape, F32, shape, whole)] * 3)


def _pack(parts):
    flat = jnp.concatenate([p.reshape(-1).astype(F32) for p in parts])
    rows = -(-flat.shape[0] // (8 * LANES)) * 8
    return jnp.pad(flat, (0, rows * LANES - flat.shape[0])).reshape(rows, LANES)


def _unpack(packed, shapes):
    flat = packed.reshape(-1)
    out, off = [], 0
    for s in shapes:
        size = math.prod(s)
        out.append(flat[off:off + size].reshape(s))
        off += size
    return out


def kernel(x, mem, ffn1_norm, ffn1_w13, ffn1_w2, mix_norm, gmlp_w_in, gmlp_ln_g, gmlp_ln_b, gmlp_w_s, gmlp_b_s, gmlp_w_out, conv_w_in, conv_w, conv_w_out, xattn_norm, mem_norm, xattn_wq, xattn_wkv, xattn_wo, ffn2_norm, ffn2_w13, ffn2_w2, final_norm, loss_target, m_ffn1_norm, m_ffn1_w13, m_ffn1_w2, m_mix_norm, m_gmlp_w_in, m_gmlp_ln_g, m_gmlp_ln_b, m_gmlp_w_s, m_gmlp_b_s, m_gmlp_w_out, m_conv_w_in, m_conv_w, m_conv_w_out, m_xattn_norm, m_mem_norm, m_xattn_wq, m_xattn_wkv, m_xattn_wo, m_ffn2_norm, m_ffn2_w13, m_ffn2_w2, m_final_norm, v_ffn1_norm, v_ffn1_w13, v_ffn1_w2, v_mix_norm, v_gmlp_w_in, v_gmlp_ln_g, v_gmlp_ln_b, v_gmlp_w_s, v_gmlp_b_s, v_gmlp_w_out, v_conv_w_in, v_conv_w, v_conv_w_out, v_xattn_norm, v_mem_norm, v_xattn_wq, v_xattn_wkv, v_xattn_wo, v_ffn2_norm, v_ffn2_w13, v_ffn2_w2, v_final_norm):
    given = dict(locals())
    T, D = x.shape[1], x.shape[2]
    depth = ffn1_norm.shape[0]
    xs = x.reshape(T, D)
    mems = mem.reshape(mem.shape[1], D)
    target = loss_target.reshape(T, D)
    me = 4 * lax.axis_index("x") + 2 * lax.axis_index("y") + lax.axis_index("c")
    E = gmlp_ln_g.shape[1]
    gd = E // GROUPS
    cshard = conv_w.shape[2]

    bf = {k: given[k].astype(BF16) for k in
          ("ffn1_w13", "ffn1_w2", "gmlp_w_in", "gmlp_w_out", "conv_w_in", "conv_w_out",
           "xattn_wq", "xattn_wkv", "xattn_wo", "ffn2_w13", "ffn2_w2")}

    W = {}

    def gather(names_layers):
        return names_layers, gather_exchange([(bf[k], l) for k, l in names_layers])

    def landed(tagged):
        names_layers, exchange = tagged
        W.update(zip(names_layers, exchange.results))

    bias_full = jnp.repeat(gmlp_b_s[0].T, gd, axis=1)

    W["ffn1_w13", 0], cw_shards = run_exchange(gather_exchange([(bf["ffn1_w13"], 0), (conv_w, 0)]), "gather_first")
    cw_full = jnp.transpose(cw_shards, (1, 0, 2)).reshape(CONV_WIDTH, D)
    saved = []
    h = xs
    for i in range(depth):
        j = i // 2
        is_gmlp = i % 2 == 0
        mix = ("gmlp_w_in", "gmlp_w_out") if is_gmlp else ("conv_w_in", "conv_w_out")
        on_up1 = gather(([("ffn1_w2", i)] if i == 0 else []) + [(mix[0], j), ("ffn2_w13", i)])
        on_down1 = gather([(mix[1], j), ("xattn_wq", i), ("xattn_wkv", i)])

        def w2_after_up(on_up=on_up1, i=i):
            landed(on_up)
            return W["ffn1_w2", i]

        h, sv1 = ffn_fwd(h, ffn1_norm[i], W["ffn1_w13", i], w2_after_up, on_up1[1], on_down1[1])
        landed(on_down1)
        hn = rms_fwd(h, mix_norm[i])
        on_mix_in = gather([("xattn_wo", i), ("ffn2_w2", i)])
        p = mm_nn(hn, W[mix[0], j], "mix_in", group=2, exchange=on_mix_in[1])
        landed(on_mix_in)
        if is_gmlp:
            gated = gmlp_mid_fwd(p, gmlp_ln_g[j:j + 1], gmlp_ln_b[j:j + 1], gmlp_w_s[j], bias_full)
        else:
            gated = conv_mid_fwd(p, cw_full)
        h_mix = mm_nn(gated, W[mix[1], j].reshape(1, gated.shape[1], D), "mix_out", F32, res=h)
        sv2 = (h, hn, p, gated)
        h, sv3 = xattn_fwd(h_mix, mems, xattn_norm[i], mem_norm[i],
                           W["xattn_wq", i], W["xattn_wkv", i], W["xattn_wo", i])
        on_up2 = gather([("ffn1_w13", i + 1), ("ffn1_w2", i + 1)]) if i + 1 < depth else None
        h, sv4 = ffn_fwd(h, ffn2_norm[i], W["ffn2_w13", i], lambda i=i: W["ffn2_w2", i],
                         None if on_up2 is None else on_up2[1])
        if on_up2 is not None:
            landed(on_up2)
        saved.append((sv1, sv2, sv3, sv4))

    loss_part, d, dys, d_final_norm = loss_head(h, final_norm, target, 0.5)

    small = {k: [None] * depth for k in ("ffn1_norm", "mix_norm", "xattn_norm", "mem_norm", "ffn2_norm")}
    partial = {}
    for i in reversed(range(depth)):
        j = i // 2
        is_gmlp = i % 2 == 0
        sv1, sv2, sv3, sv4 = saved[i]
        d, dys, small["ffn2_norm"][i], partial["ffn2_w13", i], partial["ffn2_w2", i] = ffn_bwd(
            d, dys, sv4, ffn2_norm[i], W["ffn2_w13", i], W["ffn2_w2", i], 1.0)
        (d, dys, small["xattn_norm"][i], small["mem_norm"][i], partial["xattn_wq", i],
         dwkv, partial["xattn_wo", i]) = xattn_bwd(
            d, dys, sv3, mems, xattn_norm[i], W["xattn_wq", i], W["xattn_wkv", i], W["xattn_wo", i], 1.0)
        if is_gmlp:
            mix = ("gmlp_w_in", "gmlp_w_out")
            mid = lambda p, dg, send: (lambda r: (r[0], r[1:]))(gmlp_mid_bwd(
                p, dg, gmlp_ln_g[j:j + 1], gmlp_ln_b[j:j + 1], gmlp_w_s[j], bias_full, exchange=send))
        else:
            mix = ("conv_w_in", "conv_w_out")
            mid = lambda p, dg, send: (lambda r: (r[0], r[1:]))(conv_mid_bwd(p, dg, cw_full, exchange=send))
        (d, dys, small["mix_norm"][i], partial[mix[0], j], partial[mix[1], j], partial["xattn_wkv", i],
         extra) = mixer_bwd_common(d, dys, sv2, mix_norm[i], W[mix[0], j], W[mix[1], j], mid, 0.5, dwkv)
        if is_gmlp:
            d_ws, d_bs_wide, d_lng, d_lnb = extra
        else:
            (d_cw,) = extra
        d, dys, small["ffn1_norm"][i], partial["ffn1_w13", i], partial["ffn1_w2", i] = ffn_bwd(
            d, dys, sv1, ffn1_norm[i], W["ffn1_w13", i], W["ffn1_w2", i], 0.5)
    grad_x = d.reshape(x.shape)

    small_grads = {k: jnp.concatenate(v, axis=0) for k, v in small.items()}
    small_grads["gmlp_ln_g"] = d_lng
    small_grads["gmlp_ln_b"] = d_lnb
    small_grads["gmlp_w_s"] = d_ws[None]
    small_grads["gmlp_b_s"] = d_bs_wide[None, :, :, 0]
    small_grads["final_norm"] = d_final_norm.reshape(-1)
    small_names = ["ffn1_norm", "mix_norm", "gmlp_ln_g", "gmlp_ln_b", "gmlp_w_s", "gmlp_b_s",
                   "xattn_norm", "mem_norm", "ffn2_norm", "final_norm"]
    summed = small_all_reduce(_pack([small_grads[k] for k in small_names] + [d_cw, loss_part]))
    parts = _unpack(summed, [given[k].shape for k in small_names] + [(CONV_WIDTH, D), (1, LANES)])
    grads = dict(zip(small_names, parts[:len(small_names)]))
    grads["conv_w"] = lax.dynamic_slice(parts[-2], (jnp.int32(0), me * cshard), (CONV_WIDTH, cshard))[None]
    loss = parts[-1][0, 0]
    flat_names = small_names + ["conv_w"]
    flat = adamw_flat(*[_pack([src[k] for k in flat_names]) for src in
                        (grads, given, {k: given["m_" + k] for k in flat_names},
                         {k: given["v_" + k] for k in flat_names})])
    delta, new_m, new_v = [dict(zip(flat_names, _unpack(f, [given[k].shape for k in flat_names]))) for f in flat]

    for k in bf:
        w = given[k]
        L = w.shape[0]
        shard = w.shape[1:]
        view = lambda a: a.reshape((L,) + shard)
        g, dl, m2, v2 = adamw_sharded([partial[k, l] for l in range(L)], w, given["m_" + k], given["v_" + k])
        grads[k], delta[k], new_m[k], new_v[k] = view(g), view(dl), view(m2), view(v2)

    order = ["ffn1_norm", "ffn1_w13", "ffn1_w2", "mix_norm", "gmlp_w_in", "gmlp_ln_g", "gmlp_ln_b", "gmlp_w_s",
             "gmlp_b_s", "gmlp_w_out", "conv_w_in", "conv_w", "conv_w_out", "xattn_norm", "mem_norm", "xattn_wq",
             "xattn_wkv", "xattn_wo", "ffn2_norm", "ffn2_w13", "ffn2_w2", "final_norm"]
    return (loss, grad_x, *[grads[k] for k in order], *[delta[k] for k in order],
            *[new_m[k] for k in order], *[new_v[k] for k in order])
```

```python
import math

import jax
import jax.numpy as jnp
from jax import lax
from jax.experimental import pallas as pl
from jax.experimental.pallas import tpu as pltpu

F32 = jnp.float32
BF16 = jnp.bfloat16

N_DEV = 8
N_PEERS = N_DEV - 1
CHUNK = 128
GROUPS = 8
HEADS = 4
CONV_WIDTH = 3
RMS_EPS = 1e-6
LN_EPS = 1e-5
ADAM_LR = 0.001
ADAM_B1 = 0.9
ADAM_B2 = 0.999
ADAM_EPS = 1e-08
ADAM_WD = 0.01
ADAM_STEP = 10
LANES = 128
BF16_SUBLANES = 16
MXU_WIDTH = 256
VMEM_LIMIT_BYTES = 56 * 1024 * 1024

_NT = (((1,), (1,)), ((), ()))
_TN = (((0,), (0,)), ((), ()))
_SQRT_HALF = 0.7071067811865476
_INV_SQRT_2PI = 0.3989422804014327


def _div(n, pref, align):
    best = None
    for t in range(align, min(n, pref) + 1, align):
        if n % t == 0:
            best = t
    return n if best is None else best


def _chunks(n, width):
    return [(c0, min(width, n - c0)) for c0 in range(0, n, width)]


_ANY = pl.BlockSpec(memory_space=pl.ANY)


class Exchange:
    def __init__(self, arrays, out_shapes, sems, start, finish, forward=None):
        self.arrays, self.out_shapes, self.sems = list(arrays), list(out_shapes), list(sems)
        self.start, self.finish, self.forward = start, finish, forward
        self.results = None


def _call(body, name, grid, ins, outs, scratch=(), exchange=None):
    in_specs = [pl.BlockSpec(*spec[1:3], **({"pipeline_mode": pl.Buffered(1)} if len(spec) > 3 else {}))
                for spec in ins]
    ins = [spec[:3] for spec in ins]
    out_specs = [pl.BlockSpec(bs, im) for _, _, bs, im in outs]
    out_shape = [jax.ShapeDtypeStruct(s, d) for s, d, _, _ in outs]
    arrays = [a for a, _, _ in ins]
    scratch = list(scratch)
    kernel_fn = body
    if exchange is not None:
        n_in, n_out, n_scr = len(ins), len(outs), len(scratch)
        n_xin, n_xout = len(exchange.arrays), len(exchange.out_shapes)
        steps = math.prod(grid)
        forward_step = min(steps - 1, (15 * steps) // 16)

        def kernel_fn(*refs):
            refs = list(refs)
            own_in, x_in = refs[:n_in], refs[n_in:n_in + n_xin]
            refs = refs[n_in + n_xin:]
            own_out, x_out = refs[:n_out], refs[n_out:n_out + n_xout]
            refs = refs[n_out + n_xout:]
            own_scr, x_sems = refs[:n_scr], refs[n_scr:]
            step = 0
            for axis, size in enumerate(grid):
                step = step * size + pl.program_id(axis)

            @pl.when(step == 0)
            def _():
                exchange.start(x_in, x_out, x_sems)

            if exchange.forward is not None:
                @pl.when(step == forward_step)
                def _():
                    exchange.forward(x_in, x_out, x_sems)

            body(*own_in, *own_out, *own_scr)

            @pl.when(step == steps - 1)
            def _():
                exchange.finish(x_in, x_out, x_sems)

        in_specs += [_ANY] * n_xin
        out_specs += [_ANY] * n_xout
        out_shape += exchange.out_shapes
        arrays += exchange.arrays
        scratch += exchange.sems
    res = pl.pallas_call(
        kernel_fn,
        name=name,
        grid=grid,
        in_specs=in_specs,
        out_specs=out_specs,
        out_shape=out_shape,
        scratch_shapes=scratch,
        compiler_params=pltpu.CompilerParams(
            dimension_semantics=("arbitrary",) * len(grid), vmem_limit_bytes=VMEM_LIMIT_BYTES),
    )(*arrays)
    if exchange is not None:
        exchange.results = list(res[len(outs):])
        res = res[:len(outs)]
    return res


def _dot(a, b, dims=None):
    if dims is None:
        return jnp.dot(a, b, preferred_element_type=F32)
    return lax.dot_general(a, b, dims, preferred_element_type=F32)


def _sigmoid(v):
    return 1.0 / (1.0 + jnp.exp(-v))


def _normal_cdf(v):
    return 0.5 * (1.0 + lax.erf(v * _SQRT_HALF))


def _normal_pdf(v):
    return _INV_SQRT_2PI * jnp.exp(-0.5 * v * v)


def _accumulate_product(ref, first, product, skip=False):
    @pl.when(first)
    def _():
        ref[...] = product()

    @pl.when(jnp.logical_not(jnp.logical_or(first, skip)))
    def _():
        ref[...] += product()


def _accumulate(ref, part, first):
    @pl.when(first)
    def _():
        ref[...] = part

    @pl.when(jnp.logical_not(first))
    def _():
        ref[...] += part


def rms_fwd(x, g, exchange=None):
    T, D = x.shape
    tt = _div(T, 512, BF16_SUBLANES)

    def body(x_ref, g_ref, o_ref):
        xv = x_ref[...]
        r = lax.rsqrt(jnp.mean(xv * xv, axis=-1, keepdims=True) + RMS_EPS)
        o_ref[...] = ((xv * r) * g_ref[...]).astype(BF16)

    return _call(body, "rms_fwd", (T // tt,),
                 [(x, (tt, D), lambda i: (i, 0)), (g.reshape(1, D), (1, D), lambda i: (0, 0))],
                 [((T, D), BF16, (tt, D), lambda i: (i, 0))], exchange=exchange)[0]


def rms_bwd(x, g, dxn, d, scale):
    T, D = x.shape
    tt = _div(T, 256, BF16_SUBLANES)

    def body(x_ref, g_ref, dn_ref, d_ref, dx_ref, dxs_ref, dg_ref):
        xv = x_ref[...]
        r = lax.rsqrt(jnp.mean(xv * xv, axis=-1, keepdims=True) + RMS_EPS)
        xh = xv * r
        dn = dn_ref[...].astype(F32)
        dxh = dn * g_ref[...]
        dx = r * (dxh - xh * jnp.mean(dxh * xh, axis=-1, keepdims=True)) + d_ref[...]
        dx_ref[...] = dx
        dxs_ref[...] = (scale * dx).astype(BF16)
        _accumulate(dg_ref, jnp.sum(dn * xh, axis=0, keepdims=True), pl.program_id(0) == 0)

    row = lambda i: (i, 0)
    fix = lambda i: (0, 0)
    return _call(body, "rms_bwd", (T // tt,),
                 [(x, (tt, D), row), (g.reshape(1, D), (1, D), fix), (dxn, (tt, D), row), (d, (tt, D), row)],
                 [((T, D), F32, (tt, D), row), ((T, D), BF16, (tt, D), row), ((1, D), F32, (1, D), fix)])


def rms_gain_grad(x, dxn):
    T, D = x.shape
    tt = _div(T, 256, 8)

    def body(x_ref, dn_ref, dg_ref):
        xv = x_ref[...]
        r = lax.rsqrt(jnp.mean(xv * xv, axis=-1, keepdims=True) + RMS_EPS)
        _accumulate(dg_ref, jnp.sum(dn_ref[...] * (xv * r), axis=0, keepdims=True), pl.program_id(0) == 0)

    row = lambda i: (i, 0)
    return _call(body, "rms_gain_grad", (T // tt,), [(x, (tt, D), row), (dxn, (tt, D), row)],
                 [((1, D), F32, (1, D), lambda i: (0, 0))])[0]


def loss_head(x, g, target, scale):
    T, D = x.shape
    tt = _div(T, 256, BF16_SUBLANES)

    def body(x_ref, g_ref, t_ref, loss_ref, dx_ref, dxs_ref, dg_ref):
        first = pl.program_id(0) == 0
        xv = x_ref[...]
        gv = g_ref[...]
        r = lax.rsqrt(jnp.mean(xv * xv, axis=-1, keepdims=True) + RMS_EPS)
        xh = xv * r
        err = xh * gv - t_ref[...]
        part = 0.5 * jnp.sum(jnp.mean(err * err, axis=-1, keepdims=True), axis=0, keepdims=True)
        _accumulate(loss_ref, jnp.broadcast_to(part, (1, LANES)), first)
        dy = err * (1.0 / D)
        dxh = dy * gv
        dx = r * (dxh - xh * jnp.mean(dxh * xh, axis=-1, keepdims=True))
        dx_ref[...] = dx
        dxs_ref[...] = (scale * dx).astype(BF16)
        _accumulate(dg_ref, jnp.sum(dy * xh, axis=0, keepdims=True), first)

    row = lambda i: (i, 0)
    fix = lambda i: (0, 0)
    return _call(body, "loss_head", (T // tt,),
                 [(x, (tt, D), row), (g.reshape(1, D), (1, D), fix), (target, (tt, D), row)],
                 [((1, LANES), F32, (1, LANES), fix), ((T, D), F32, (tt, D), row),
                  ((T, D), BF16, (tt, D), row), ((1, D), F32, (1, D), fix)])


def mm_nn(a, w3, name, out_dtype=BF16, res=None, res_scale=1.0, tm_pref=1024, tn_pref=1024, group=1,
          exchange=None):
    M, K = a.shape
    nb, _, ns = w3.shape
    tn = ns if group > 1 else _div(ns, tn_pref, LANES)
    per = ns // tn
    tm = _div(M, tm_pref, BF16_SUBLANES)
    wide = group * tn
    ins = [(a, (tm, K), lambda j, m: (m, 0)), (w3, (group, K, tn), lambda j, m: (j // per, 0, j % per))]
    if res is not None:
        ins.append((res, (tm, wide), lambda j, m: (m, j)))

    def body(*refs):
        a_ref, w_ref = refs[0], refs[1]
        o_ref = refs[-1]
        for k in range(group):
            cols = slice(k * tn, (k + 1) * tn)
            acc = _dot(a_ref[...], w_ref[k])
            if res is not None:
                acc = refs[2][:, cols] + res_scale * acc
            o_ref[:, cols] = acc.astype(o_ref.dtype)

    return _call(body, name, (nb * per // group, M // tm), ins,
                 [((M, nb * ns), out_dtype, (tm, wide), lambda j, m: (m, j))], exchange=exchange)[0]


def mm_nt(a_in, w3, name, out_dtype, M, tm_pref=1024, to_pref=2048, group=1, exchange=None):
    nb, Ko, ns = w3.shape
    to = _div(Ko, to_pref, LANES)
    tm = _div(M, tm_pref, BF16_SUBLANES)
    if isinstance(a_in, tuple):
        a, a_bs, a_im = a_in
        a_bs = tuple(tm if s == "tm" else s for s in a_bs)
    else:
        a, a_bs, a_im = a_in, (tm, group * ns), lambda m, o, b: (m, b)
    steps = nb // group
    narrow_out = steps > 1 and out_dtype != F32

    def product(a_ref, w_ref):
        p = None
        for k in range(group):
            pk = _dot(a_ref[:, k * ns:(k + 1) * ns], w_ref[k], _NT)
            p = pk if p is None else p + pk
        return p

    def body(a_ref, w_ref, o_ref, *acc):
        b = pl.program_id(2)
        if steps == 1:
            o_ref[...] = product(a_ref, w_ref).astype(o_ref.dtype)
        elif not narrow_out:
            _accumulate_product(o_ref, b == 0, lambda: product(a_ref, w_ref))
        else:
            acc_ref, = acc
            _accumulate_product(acc_ref, b == 0, lambda: product(a_ref, w_ref), skip=b == steps - 1)

            @pl.when(b == steps - 1)
            def _():
                o_ref[...] = (acc_ref[...] + product(a_ref, w_ref)).astype(o_ref.dtype)

    return _call(body, name, (M // tm, Ko // to, steps),
                 [(a, a_bs, a_im), (w3, (group, to, ns), lambda m, o, b: (b, o, 0))],
                 [((M, Ko), out_dtype, (tm, to), lambda m, o, b: (m, o))],
                 scratch=[pltpu.VMEM((tm, to), F32)] if narrow_out else [], exchange=exchange)[0]


def mm_nt_rms_bwd(a, w3, name, x, g, d, scale, exchange=None):
    nb, D, ns = w3.shape
    M = x.shape[0]
    tm = _div(M, 512, BF16_SUBLANES)
    a_bs, a_im = (tm, ns), lambda m, b: (m, b)

    rc = _div(tm, 64, BF16_SUBLANES)

    def body(a_ref, w_ref, x_ref, g_ref, d_ref, dx_ref, dxs_ref, dg_ref, acc_ref):
        m, b = pl.program_id(0), pl.program_id(1)
        _accumulate_product(acc_ref, b == 0, lambda: _dot(a_ref[...], w_ref[...], _NT))

        @pl.when(b == nb - 1)
        def _():
            gv = g_ref[...]

            def piece(c, dg):
                rows = pl.ds(pl.multiple_of(c * rc, rc), rc)
                dn = acc_ref[rows, :]
                xv = x_ref[rows, :]
                r = lax.rsqrt(jnp.mean(xv * xv, axis=-1, keepdims=True) + RMS_EPS)
                xh = xv * r
                dxh = dn * gv
                dx = r * (dxh - xh * jnp.mean(dxh * xh, axis=-1, keepdims=True)) + d_ref[rows, :]
                dx_ref[rows, :] = dx
                dxs_ref[rows, :] = (scale * dx).astype(BF16)
                return dg + jnp.sum(dn * xh, axis=0, keepdims=True)

            _accumulate(dg_ref, lax.fori_loop(0, tm // rc, piece, jnp.zeros((1, D), F32)), m == 0)

    row = lambda m, b: (m, 0)
    fix = lambda m, b: (0, 0)
    w_spec = (w3, (None, D, ns), lambda m, b: (b, 0, 0)) + (("single",) if nb == 1 else ())
    return _call(body, name, (M // tm, nb),
                 [(a, a_bs, a_im), w_spec,
                  (x, (tm, D), row), (g.reshape(1, D), (1, D), fix), (d, (tm, D), row)],
                 [((M, D), F32, (tm, D), row), ((M, D), BF16, (tm, D), row), ((1, D), F32, (1, D), fix)],
                 scratch=[pltpu.VMEM((tm, D), F32)], exchange=exchange)


def mm_tn(a, b_in, name, nbo, ns, tka_pref=1024, tt_pref=2048, tn_pref=2048, group=1, exchange=None):
    T, Ka = a.shape
    tt = _div(T, tt_pref, BF16_SUBLANES)
    tka = _div(Ka, tka_pref, LANES)
    if isinstance(b_in, tuple):
        b, b_bs, b_im = b_in
        b_bs = tuple(tt if s == "tt" else s for s in b_bs)
        tn, per = ns, 1
    else:
        tn = ns if group > 1 else _div(ns, tn_pref, LANES)
        per = ns // tn
        b, b_bs, b_im = b_in, (tt, group * tn), lambda i, j, t: (t, j)
    nt = T // tt

    def body(a_ref, b_ref, o_ref, acc_ref):
        t = pl.program_id(2)
        product = lambda: _dot(a_ref[...], b_ref[...], _TN)

        def store(total):
            for k in range(group):
                o_ref[k] = total[:, k * tn:(k + 1) * tn].astype(BF16)

        if nt == 1:
            store(product())
        else:
            _accumulate_product(acc_ref, t == 0, product, skip=t == nt - 1)

            @pl.when(t == nt - 1)
            def _():
                store(acc_ref[...] + product())

    return _call(body, name, (Ka // tka, nbo * per // group, nt),
                 [(a, (tt, tka), lambda i, j, t: (t, i)), (b, b_bs, b_im)],
                 [((nbo, Ka, ns), BF16, (group, tka, tn), lambda i, j, t: (j // per, i, j % per))],
                 scratch=[pltpu.VMEM((tka, group * tn), F32)], exchange=exchange)[0]


def ffn_up(xn, w13, exchange=None):
    T, D = xn.shape
    nb, _, ns = w13.shape
    half = nb // 2
    F = half * ns
    tm = _div(T, 512, BF16_SUBLANES)
    pair = 2 if half % 2 == 0 else 1

    def columns(w_ref, c0, cw):
        k, off = divmod(c0, ns)
        if off + cw <= ns:
            return w_ref[k, :, off:off + cw]
        return jnp.concatenate([w_ref[k, :, off:ns], w_ref[k + 1, :, 0:off + cw - ns]], axis=1)

    def body(x_ref, wg_ref, wu_ref, fac_ref, act_ref):
        xv = x_ref[...]
        for c0, cw in _chunks(pair * ns, MXU_WIDTH):
            cols = slice(c0, c0 + cw)
            gate = _dot(xv, columns(wg_ref, c0, cw))
            up = _dot(xv, columns(wu_ref, c0, cw))
            s = _sigmoid(gate)
            silu = gate * s
            fac_ref[0, :, cols] = (up * (s * (1.0 + gate * (1.0 - s)))).astype(BF16)
            fac_ref[1, :, cols] = silu.astype(BF16)
            act_ref[:, cols] = (silu * up).astype(BF16)

    tn = pair * ns
    return _call(body, "ffn_up", (half // pair, T // tm),
                 [(xn, (tm, D), lambda j, m: (m, 0)),
                  (w13, (pair, D, ns), lambda j, m: (j, 0, 0), "single"),
                  (w13, (pair, D, ns), lambda j, m: (j + half // pair, 0, 0), "single")],
                 [((2, T, F), BF16, (2, tm, tn), lambda j, m: (0, m, j)),
                  ((T, F), BF16, (tm, tn), lambda j, m: (m, j))], exchange=exchange)


def ffn_dact(dy, w2, fac, exchange=None):
    T, D = dy.shape
    F = w2.shape[0]
    tm = _div(T, 512, BF16_SUBLANES)
    tn = _div(F, F // 2, MXU_WIDTH)

    def body(dy_ref, w_ref, fac_ref, dh_ref):
        dyv = dy_ref[...]
        for c0, cw in _chunks(tn, MXU_WIDTH):
            cols = slice(c0, c0 + cw)
            da = _dot(dyv, w_ref[cols, :], _NT)
            dh_ref[0, :, cols] = (da * fac_ref[0, :, cols].astype(F32)).astype(BF16)
            dh_ref[1, :, cols] = (da * fac_ref[1, :, cols].astype(F32)).astype(BF16)

    return _call(body, "ffn_dact", (F // tn, T // tm),
                 [(dy, (tm, D), lambda j, m: (m, 0)), (w2, (tn, D), lambda j, m: (j, 0), "single"),
                  (fac, (2, tm, tn), lambda j, m: (0, m, j))],
                 [((2, T, F), BF16, (2, tm, tn), lambda j, m: (0, m, j))], exchange=exchange)[0]


def ffn_fwd(x, norm_g, get_w13, get_w2, norm_exchange=None, up_exchange=None, down_exchange=None):
    xn = rms_fwd(x, norm_g, exchange=norm_exchange)
    fac, act = ffn_up(xn, get_w13(), exchange=up_exchange)
    F = act.shape[1]
    y = mm_nn(act, get_w2().reshape(1, F, -1), "ffn_down", F32, res=x, res_scale=0.5, tm_pref=512,
              exchange=down_exchange)
    return y, (x, xn, fac, act)


def ffn_bwd(d, dys, saved, norm_g, w13, w2, scale_out):
    x, xn, fac, act = saved
    T, D = x.shape
    nb, _, ns = w13.shape
    half = nb // 2
    F = half * ns
    dh = ffn_dact(dys, w2.reshape(F, D), fac)
    dw2 = mm_tn(act, dys, "ffn_dw2", 1, D, tka_pref=ns, tn_pref=1024)
    send_w2 = scatter_exchange([dw2.reshape(N_DEV, F // N_DEV, D)])
    dw13 = mm_tn(xn, (dh, (None, "tt", ns), lambda i, j, t: (j // half, t, j % half)), "ffn_dw13", nb, ns,
                 exchange=send_w2)
    send_w13 = scatter_exchange([dw13])
    pair = 2 if half % 2 == 0 else 1
    per = half // pair
    dxn = mm_nt((dh, (None, "tm", pair * ns), lambda m, o, b: (b // per, m, b % per)), w13, "ffn_dxn", BF16, T,
                tm_pref=512, group=pair, exchange=send_w13)
    dx, dxs, dg = rms_bwd(x, norm_g, dxn, d, scale_out)
    return dx, dxs, dg, send_w13.results[0], send_w2.results[0]


def _gmlp_parts(p_ref, lng_ref, lnb_ref):
    E = lng_ref.shape[-1]
    pv = p_ref[...].astype(F32)
    cdf = _normal_cdf(pv)
    z = pv * cdf
    u = z[:, :E]
    vp = z[:, E:]
    mu = jnp.mean(vp, axis=-1, keepdims=True)
    xc = vp - mu
    rstd = lax.rsqrt(jnp.mean(xc * xc, axis=-1, keepdims=True) + LN_EPS)
    vh = xc * rstd
    v = vh * lng_ref[...] + lnb_ref[...]
    return u, vh, rstd, v, pv, cdf


def _causal_ws(ws_ref, g):
    keep = lax.broadcasted_iota(jnp.int32, (CHUNK, CHUNK), 0) >= lax.broadcasted_iota(jnp.int32, (CHUNK, CHUNK), 1)
    return jnp.where(keep, ws_ref[g], 0.0).astype(BF16), keep


def gmlp_mid_fwd(p, ln_g, ln_b, w_s, bias_full):
    T, E2 = p.shape
    E = E2 // 2
    gd = E // GROUPS
    tm = _div(T, 256, CHUNK)
    fix2 = lambda i: (0, 0)

    def body(p_ref, lng_ref, lnb_ref, ws_ref, bias_ref, o_ref):
        u, _, _, v, _, _ = _gmlp_parts(p_ref, lng_ref, lnb_ref)
        vb = v.astype(BF16)
        for g in range(GROUPS):
            wm, _ = _causal_ws(ws_ref, g)
            cols = slice(g * gd, (g + 1) * gd)
            for c in range(tm // CHUNK):
                rows = slice(c * CHUNK, (c + 1) * CHUNK)
                f = _dot(wm, vb[rows, cols]) + bias_ref[:, cols]
                o_ref[rows, cols] = (u[rows, cols] * f).astype(BF16)

    return _call(body, "gmlp_mid_fwd", (T // tm,),
                 [(p, (tm, E2), lambda i: (i, 0)), (ln_g, (1, E), fix2), (ln_b, (1, E), fix2),
                  (w_s, (GROUPS, CHUNK, CHUNK), lambda i: (0, 0, 0)), (bias_full, (CHUNK, E), fix2)],
                 [((T, E), BF16, (tm, E), lambda i: (i, 0))])[0]


def gmlp_mid_bwd(p, dgated, ln_g, ln_b, w_s, bias_full, exchange=None):
    T, E2 = p.shape
    E = E2 // 2
    gd = E // GROUPS
    tm = _div(T, 256, CHUNK)
    nsteps = T // tm
    fix2 = lambda i: (0, 0)
    fix3 = lambda i: (0, 0, 0)

    def body(p_ref, dg_ref, lng_ref, lnb_ref, ws_ref, bias_ref,
             dp_ref, dws_ref, dbs_ref, dlng_ref, dlnb_ref, f_sc, dv_sc, db_sc):
        i = pl.program_id(0)
        first = i == 0
        u, vh, rstd, v, pv, cdf = _gmlp_parts(p_ref, lng_ref, lnb_ref)
        vb = v.astype(BF16)
        dgt = dg_ref[...].astype(F32)
        df = dgt * u
        dfb = df.astype(BF16)
        for g in range(GROUPS):
            wm, keep = _causal_ws(ws_ref, g)
            cols = slice(g * gd, (g + 1) * gd)
            dw = None
            dbg = None
            for c in range(tm // CHUNK):
                rows = slice(c * CHUNK, (c + 1) * CHUNK)
                f_sc[rows, cols] = _dot(wm, vb[rows, cols]) + bias_ref[:, cols]
                dv_sc[rows, cols] = _dot(wm, dfb[rows, cols], _TN)
                part = _dot(dfb[rows, cols], vb[rows, cols], _NT)
                dw = part if dw is None else dw + part
                dbg = df[rows, cols] if dbg is None else dbg + df[rows, cols]
            dw = jnp.where(keep, dw, 0.0)

            @pl.when(first)
            def _():
                dws_ref[g] = dw
                db_sc[:, cols] = dbg

            @pl.when(jnp.logical_not(first))
            def _():
                dws_ref[g] += dw
                db_sc[:, cols] += dbg

        du = dgt * f_sc[...]
        dv = dv_sc[...]
        _accumulate(dlng_ref, jnp.sum(dv * vh, axis=0, keepdims=True), first)
        _accumulate(dlnb_ref, jnp.sum(dv, axis=0, keepdims=True), first)
        dvh = dv * lng_ref[...]
        dvp = rstd * (dvh - jnp.mean(dvh, axis=-1, keepdims=True)
                      - vh * jnp.mean(dvh * vh, axis=-1, keepdims=True))
        gp = cdf + pv * _normal_pdf(pv)
        dp_ref[:, :E] = (du * gp[:, :E]).astype(BF16)
        dp_ref[:, E:] = (dvp * gp[:, E:]).astype(BF16)

        @pl.when(i == nsteps - 1)
        def _():
            for g in range(GROUPS):
                tot = jnp.sum(db_sc[:, g * gd:(g + 1) * gd], axis=-1, keepdims=True)
                dbs_ref[g] = jnp.broadcast_to(tot, (CHUNK, LANES))

    return _call(body, "gmlp_mid_bwd", (nsteps,),
                 [(p, (tm, E2), lambda i: (i, 0)), (dgated, (tm, E), lambda i: (i, 0)),
                  (ln_g, (1, E), fix2), (ln_b, (1, E), fix2),
                  (w_s, (GROUPS, CHUNK, CHUNK), fix3), (bias_full, (CHUNK, E), fix2)],
                 [((T, E2), BF16, (tm, E2), lambda i: (i, 0)),
                  ((GROUPS, CHUNK, CHUNK), F32, (GROUPS, CHUNK, CHUNK), fix3),
                  ((GROUPS, CHUNK, LANES), F32, (GROUPS, CHUNK, LANES), fix3),
                  ((1, E), F32, (1, E), fix2), ((1, E), F32, (1, E), fix2)],
                 scratch=[pltpu.VMEM((tm, E), F32), pltpu.VMEM((tm, E), F32), pltpu.VMEM((CHUNK, E), F32)],
                 exchange=exchange)


HALO = 16


def _row_of(block, r):
    rows = lax.broadcasted_iota(jnp.int32, block.shape, 0)
    return jnp.sum(jnp.where(rows == r, block, 0.0), axis=0, keepdims=True)


def _shift_down(z, k, fill):
    out = pltpu.roll(z, k, 0)
    rows = lax.broadcasted_iota(jnp.int32, z.shape, 0)
    for t in range(k):
        out = jnp.where(rows == t, fill[t], out)
    return out


def _shift_up(z, k, fill):
    n = z.shape[0]
    out = pltpu.roll(z, n - k, 0)
    rows = lax.broadcasted_iota(jnp.int32, z.shape, 0)
    for j in range(k):
        out = jnp.where(rows == n - k + j, fill[j], out)
    return out


def _conv_parts(p_ref, prev_ref, cw_ref, is_first):
    D = cw_ref.shape[-1]
    pv = p_ref[...].astype(F32)
    bg, cg, val = pv[:, :D], pv[:, D:2 * D], pv[:, 2 * D:]
    z = cg * val
    pp = prev_ref[...].astype(F32)
    zp = jnp.where(is_first, 0.0, pp[:, D:2 * D] * pp[:, 2 * D:])
    zl1 = _row_of(zp, HALO - 1)
    zl2 = _row_of(zp, HALO - 2)
    z1 = _shift_down(z, 1, [zl1])
    z2 = _shift_down(z, 2, [zl2, zl1])
    conv = z2 * cw_ref[0:1, :] + z1 * cw_ref[1:2, :] + z * cw_ref[2:3, :]
    return bg, cg, val, z, z1, z2, conv


def conv_mid_fwd(p, cw):
    T, D3 = p.shape
    D = D3 // 3
    tm = _div(T, 256, HALO)
    per = tm // HALO

    def body(p_ref, prev_ref, cw_ref, o_ref):
        bg, _, _, _, _, _, conv = _conv_parts(p_ref, prev_ref, cw_ref, pl.program_id(0) == 0)
        o_ref[...] = (bg * conv).astype(BF16)

    return _call(body, "conv_mid_fwd", (T // tm,),
                 [(p, (tm, D3), lambda i: (i, 0)),
                  (p, (HALO, D3), lambda i: (jnp.maximum(i * per - 1, 0), 0)),
                  (cw, (CONV_WIDTH, D), lambda i: (0, 0))],
                 [((T, D), BF16, (tm, D), lambda i: (i, 0))])[0]


def conv_mid_bwd(p, dgated, cw, exchange=None):
    T, D3 = p.shape
    D = D3 // 3
    tm = _div(T, 256, HALO)
    per = tm // HALO
    nsteps = T // tm
    last_halo = T // HALO - 1
    nxt = lambda i: (jnp.minimum((i + 1) * per, last_halo), 0)

    def body(p_ref, prev_ref, next_ref, dg_ref, dgn_ref, cw_ref, dp_ref, dcw_ref):
        i = pl.program_id(0)
        bg, cg, val, z, z1, z2, conv = _conv_parts(p_ref, prev_ref, cw_ref, i == 0)
        dgt = dg_ref[...].astype(F32)
        dconv = dgt * bg
        dcn = jnp.where(i == nsteps - 1, 0.0, dgn_ref[...].astype(F32) * next_ref[:, :D].astype(F32))
        n0 = _row_of(dcn, 0)
        n1 = _row_of(dcn, 1)
        up1 = _shift_up(dconv, 1, [n0])
        up2 = _shift_up(dconv, 2, [n0, n1])
        dz = dconv * cw_ref[2:3, :] + up1 * cw_ref[1:2, :] + up2 * cw_ref[0:1, :]
        dp_ref[:, :D] = (dgt * conv).astype(BF16)
        dp_ref[:, D:2 * D] = (dz * val).astype(BF16)
        dp_ref[:, 2 * D:] = (dz * cg).astype(BF16)
        first = i == 0
        parts = (jnp.sum(dconv * z2, axis=0, keepdims=True), jnp.sum(dconv * z1, axis=0, keepdims=True),
                 jnp.sum(dconv * z, axis=0, keepdims=True))

        @pl.when(first)
        def _():
            for k in range(CONV_WIDTH):
                dcw_ref[k:k + 1, :] = parts[k]

        @pl.when(jnp.logical_not(first))
        def _():
            for k in range(CONV_WIDTH):
                dcw_ref[k:k + 1, :] += parts[k]

    return _call(body, "conv_mid_bwd", (nsteps,),
                 [(p, (tm, D3), lambda i: (i, 0)),
                  (p, (HALO, D3), lambda i: (jnp.maximum(i * per - 1, 0), 0)),
                  (p, (HALO, D3), nxt),
                  (dgated, (tm, D), lambda i: (i, 0)),
                  (dgated, (HALO, D), nxt),
                  (cw, (CONV_WIDTH, D), lambda i: (0, 0))],
                 [((T, D3), BF16, (tm, D3), lambda i: (i, 0)),
                  ((CONV_WIDTH, D), F32, (CONV_WIDTH, D), lambda i: (0, 0))], exchange=exchange)


def mixer_bwd_common(d, dys, saved, norm_g, w_in, w_out, mid_bwd, scale_out, dwkv):
    x, hn, p, gated = saved
    T, D = x.shape
    E = gated.shape[1]
    w_out3 = w_out.reshape(1, E, D)
    dgated = mm_nt(dys, w_out3, "mix_dgated", BF16, T)
    dw_out = mm_tn(gated, dys, "mix_dwout", 1, D)
    send_wout = scatter_exchange([dw_out.reshape(N_DEV, E // N_DEV, D)])
    dp, extra = mid_bwd(p, dgated, send_wout)
    nb, _, ns = w_in.shape
    group = max(g for g in (1, 2, 4, 8) if nb % g == 0 and g * ns <= D)
    send_wkv = scatter_exchange([dwkv])
    dw_in = mm_tn(hn, dp, "mix_dwin", nb, ns, group=group, exchange=send_wkv)
    send_win = scatter_exchange([dw_in])
    dhn = mm_nt(dp, w_in, "mix_dhn", BF16, T, tm_pref=512, group=group, exchange=send_win)
    dx, dxs, dg = rms_bwd(x, norm_g, dhn, d, scale_out)
    return dx, dxs, dg, send_win.results[0], send_wout.results[0], send_wkv.results[0], extra


def _softmax_rows(s):
    e = jnp.exp(s - jnp.max(s, axis=-1, keepdims=True))
    return e / jnp.sum(e, axis=-1, keepdims=True)


def attn_fwd(q, kv):
    T, D = q.shape
    M = kv.shape[0]
    hd = D // HEADS
    scale = hd ** -0.5
    tm = _div(T, 512, BF16_SUBLANES)

    def body(q_ref, kv_ref, o_ref):
        for h in range(HEADS):
            cols = slice(h * hd, (h + 1) * hd)
            s = _dot(q_ref[:, cols], kv_ref[:, cols], _NT) * scale
            pr = _softmax_rows(s).astype(BF16)
            o_ref[:, cols] = _dot(pr, kv_ref[:, D + h * hd:D + (h + 1) * hd]).astype(BF16)

    return _call(body, "attn_fwd", (T // tm,),
                 [(q, (tm, D), lambda i: (i, 0)), (kv, (M, 2 * D), lambda i: (0, 0))],
                 [((T, D), BF16, (tm, D), lambda i: (i, 0))])[0]


def attn_bwd(q, do, kv, exchange=None):
    T, D = q.shape
    M = kv.shape[0]
    hd = D // HEADS
    scale = hd ** -0.5
    tm = _div(T, 512, BF16_SUBLANES)
    nsteps = T // tm

    def body(q_ref, do_ref, kv_ref, dq_ref, dkv_ref, acc_ref):
        i = pl.program_id(0)
        for h in range(HEADS):
            cols = slice(h * hd, (h + 1) * hd)
            vcols = slice(D + h * hd, D + (h + 1) * hd)
            qh = q_ref[:, cols]
            kh = kv_ref[:, cols]
            doh = do_ref[:, cols]
            pr = _softmax_rows(_dot(qh, kh, _NT) * scale)
            dpr = _dot(doh, kv_ref[:, vcols], _NT)
            ds = (pr * (dpr - jnp.sum(dpr * pr, axis=-1, keepdims=True)) * scale).astype(BF16)
            dq_ref[:, cols] = _dot(ds, kh).astype(BF16)
            dk = _dot(ds, qh, _TN)
            dv = _dot(pr.astype(BF16), doh, _TN)

            @pl.when(i == 0)
            def _():
                acc_ref[:, cols] = dk
                acc_ref[:, vcols] = dv

            @pl.when(i > 0)
            def _():
                acc_ref[:, cols] += dk
                acc_ref[:, vcols] += dv

        @pl.when(i == nsteps - 1)
        def _():
            dkv_ref[...] = acc_ref[...].astype(BF16)

    return _call(body, "attn_bwd", (nsteps,),
                 [(q, (tm, D), lambda i: (i, 0)), (do, (tm, D), lambda i: (i, 0)),
                  (kv, (M, 2 * D), lambda i: (0, 0))],
                 [((T, D), BF16, (tm, D), lambda i: (i, 0)), ((M, 2 * D), BF16, (M, 2 * D), lambda i: (0, 0))],
                 scratch=[pltpu.VMEM((M, 2 * D), F32)], exchange=exchange)


def xattn_fwd(x, mem, xnorm_g, mnorm_g, wq, wkv, wo):
    D = x.shape[1]
    hq = rms_fwd(x, xnorm_g)
    mn = rms_fwd(mem, mnorm_g)
    q = mm_nn(hq, wq.reshape(1, D, D), "xattn_q")
    kv = mm_nn(mn, wkv, "xattn_kv")
    o = attn_fwd(q, kv)
    y = mm_nn(o, wo.reshape(1, D, D), "xattn_out", F32, res=x)
    return y, (x, hq, mn, q, kv, o)


def xattn_bwd(d, dys, saved, mem, xnorm_g, wq, wkv, wo, scale_out):
    x, hq, mn, q, kv, o = saved
    T, D = x.shape
    M = mem.shape[0]
    do = mm_nt(dys, wo.reshape(1, D, D), "xattn_do", BF16, T)
    rows = D // N_DEV
    dwo = mm_tn(o, dys, "xattn_dwo", 1, D)
    send_wo = scatter_exchange([dwo.reshape(N_DEV, rows, D)])
    dq, dkv = attn_bwd(q, do, kv, exchange=send_wo)
    nb, _, ns = wkv.shape
    dwkv = mm_tn(mn, dkv, "xattn_dwkv", nb, ns)
    dmn = mm_nt(dkv, wkv, "xattn_dmn", F32, M)
    dgm = rms_gain_grad(mem, dmn)
    dwq = mm_tn(hq, dq, "xattn_dwq", 1, D)
    send_wq = scatter_exchange([dwq.reshape(N_DEV, rows, D)])
    dx, dxs, dgx = mm_nt_rms_bwd(dq, wq.reshape(1, D, D), "xattn_dhq", x, xnorm_g, d, scale_out,
                                 exchange=send_wq)
    return dx, dxs, dgx, dgm, send_wq.results[0], dwkv, send_wo.results[0]


def _mesh_places():
    x, y, c = lax.axis_index("x"), lax.axis_index("y"), lax.axis_index("c")
    chips = [(1 - x, y), (x, 1 - y), (1 - x, 1 - y)]
    return (x, y, c), (x, y, 1 - c), chips


def _slot(place):
    return 4 * place[0] + 2 * place[1] + place[2]


def _exchange_sems(n):
    return [pltpu.SemaphoreType.DMA((n * N_PEERS,)), pltpu.SemaphoreType.DMA((n * N_PEERS,)),
            pltpu.SemaphoreType.DMA((n,))]


def gather_exchange(shards):
    n = len(shards)
    shapes = [a.shape if l is None else a.shape[1:] for a, l in shards]

    def parts(x_in, x_out, sems):
        ins = [r if l is None else r.at[l] for r, (_, l) in zip(x_in, shards)]
        send_sems, recv_sems, local_sems = sems
        me, sibling, chips = _mesh_places()

        def copy(a, k, block, to, src=None):
            dst = x_out[a].at[_slot(block)]
            return pltpu.make_async_remote_copy(
                src_ref=dst if src is None else src, dst_ref=dst,
                send_sem=send_sems.at[a * N_PEERS + k], recv_sem=recv_sems.at[a * N_PEERS + k],
                device_id=to, device_id_type=pl.DeviceIdType.MESH)

        mine = [pltpu.make_async_copy(ins[a], x_out[a].at[_slot(me)], local_sems.at[a]) for a in range(n)]
        first = []
        for a in range(n):
            first.append(copy(a, 0, me, sibling, src=ins[a]))
            first += [copy(a, 1 + j, me, (*chip, me[2]), src=ins[a]) for j, chip in enumerate(chips)]
        return me, sibling, chips, copy, mine, first

    def start(x_in, x_out, sems):
        _, _, _, _, mine, first = parts(x_in, x_out, sems)
        for cp in mine + first:
            cp.start()

    def forward(x_in, x_out, sems):
        me, sibling, chips, copy, _, _ = parts(x_in, x_out, sems)
        for j, chip in enumerate(chips):
            for a in range(n):
                copy(a, 1 + j, (*chip, me[2]), me).wait_recv()
                copy(a, 4 + j, (*chip, me[2]), sibling).start()

    def finish(x_in, x_out, sems):
        me, sibling, chips, copy, mine, first = parts(x_in, x_out, sems)
        for a in range(n):
            copy(a, 0, sibling, me).wait_recv()
        for j, chip in enumerate(chips):
            for a in range(n):
                copy(a, 4 + j, (*chip, 1 - me[2]), me).wait_recv()
        for cp in first:
            cp.wait_send()
        for j, chip in enumerate(chips):
            for a in range(n):
                copy(a, 4 + j, (*chip, me[2]), sibling).wait_send()
        for cp in mine:
            cp.wait()

    return Exchange([a for a, _ in shards],
                    [jax.ShapeDtypeStruct((N_DEV,) + tuple(s), a.dtype) for s, (a, _) in zip(shapes, shards)],
                    _exchange_sems(n), start, finish, forward)


def _all_peers(me, chips):
    c = me[2]
    return [(me[0], me[1], 1 - c)] + [(*chip, c) for chip in chips] + [(*chip, 1 - c) for chip in chips]


def scatter_exchange(grads):
    n = len(grads)

    def parts(x_in, x_out, sems):
        send_sems, recv_sems, local_sems = sems
        me, _, chips = _mesh_places()
        mine = [pltpu.make_async_copy(x_in[a].at[_slot(me)], x_out[a].at[_slot(me)], local_sems.at[a])
                for a in range(n)]
        sends, recvs = [], []
        for a in range(n):
            for k, peer in enumerate(_all_peers(me, chips)):
                sem = dict(send_sem=send_sems.at[a * N_PEERS + k], recv_sem=recv_sems.at[a * N_PEERS + k],
                           device_id=peer, device_id_type=pl.DeviceIdType.MESH)
                sends.append(pltpu.make_async_remote_copy(
                    src_ref=x_in[a].at[_slot(peer)], dst_ref=x_out[a].at[_slot(me)], **sem))
                recvs.append(pltpu.make_async_remote_copy(
                    src_ref=x_in[a].at[_slot(peer)], dst_ref=x_out[a].at[_slot(peer)], **sem))
        return mine, sends, recvs

    def start(x_in, x_out, sems):
        mine, sends, _ = parts(x_in, x_out, sems)
        for cp in mine + sends:
            cp.start()

    def finish(x_in, x_out, sems):
        mine, sends, recvs = parts(x_in, x_out, sems)
        for cp in recvs:
            cp.wait_recv()
        for cp in sends:
            cp.wait_send()
        for cp in mine:
            cp.wait()

    return Exchange(grads, [jax.ShapeDtypeStruct(g.shape, g.dtype) for g in grads],
                    _exchange_sems(n), start, finish)


def small_all_reduce(vec):
    R = vec.shape[0]

    def body(v_ref, o_ref, all_ref, send_sems, recv_sems):
        me, _, chips = _mesh_places()
        peers = _all_peers(me, chips)
        all_ref[_slot(me)] = v_ref[...]
        sends, recvs = [], []
        for k, peer in enumerate(peers):
            sem = dict(send_sem=send_sems.at[k], recv_sem=recv_sems.at[k],
                       device_id=peer, device_id_type=pl.DeviceIdType.MESH)
            sends.append(pltpu.make_async_remote_copy(src_ref=v_ref, dst_ref=all_ref.at[_slot(me)], **sem))
            recvs.append(pltpu.make_async_remote_copy(src_ref=v_ref, dst_ref=all_ref.at[_slot(peer)], **sem))
        for cp in sends:
            cp.start()
        for cp in recvs:
            cp.wait_recv()
        for cp in sends:
            cp.wait_send()
        acc = all_ref[0]
        for s in range(1, N_DEV):
            acc = acc + all_ref[s]
        o_ref[...] = acc

    return pl.pallas_call(
        body, name="small_all_reduce",
        out_shape=jax.ShapeDtypeStruct(vec.shape, F32),
        in_specs=[pl.BlockSpec(memory_space=pltpu.VMEM)], out_specs=pl.BlockSpec(memory_space=pltpu.VMEM),
        scratch_shapes=[pltpu.VMEM((N_DEV, R, LANES), F32), pltpu.SemaphoreType.DMA((N_PEERS,)),
                        pltpu.SemaphoreType.DMA((N_PEERS,))],
    )(vec)


def _adamw_math(w, g, m, v):
    m2 = ADAM_B1 * m + (1.0 - ADAM_B1) * g
    v2 = ADAM_B2 * v + (1.0 - ADAM_B2) * (g * g)
    m_hat = m2 / (1.0 - ADAM_B1 ** ADAM_STEP)
    v_hat = v2 / (1.0 - ADAM_B2 ** ADAM_STEP)
    delta = -ADAM_LR * (m_hat / (jnp.sqrt(v_hat) + ADAM_EPS) + ADAM_WD * w)
    return delta, m2, v2


def adamw_sharded(partials, w, m, v):
    L, r, c = w.shape
    row_bytes = 2 * c * (L * N_DEV * 2 + 7 * 4)
    tr = _div(r, max(BF16_SUBLANES, min(512, (VMEM_LIMIT_BYTES * 3 // 4) // row_bytes)), BF16_SUBLANES)
    nt = r // tr

    def part_map(l0):
        return lambda l, t: (0, jnp.where(l == l0, t, jnp.where(l < l0, 0, nt - 1)), 0)

    def body(*refs):
        parts = refs[:L]
        w_ref, m_ref, v_ref, g_out, d_out, m_out, v_out = refs[L:]
        layer = pl.program_id(0)
        for l0 in range(L):
            @pl.when(layer == l0)
            def _():
                g = parts[l0][0].astype(F32)
                for s in range(1, N_DEV):
                    g = g + parts[l0][s].astype(F32)
                delta, m2, v2 = _adamw_math(w_ref[...], g, m_ref[...], v_ref[...])
                g_out[...] = g
                d_out[...] = delta
                m_out[...] = m2
                v_out[...] = v2

    own = lambda l, t: (l, t, 0)
    return _call(body, "adamw_sharded", (L, nt),
                 [(p, (N_DEV, tr, c), part_map(l0)) for l0, p in enumerate(partials)]
                 + [(w, (None, tr, c), own), (m, (None, tr, c), own), (v, (None, tr, c), own)],
                 [((L, r, c), F32, (None, tr, c), own)] * 4)


def adamw_flat(g, w, m, v):
    shape = g.shape

    def body(g_ref, w_ref, m_ref, v_ref, d_out, m_out, v_out):
        delta, m2, v2 = _adamw_math(w_ref[...], g_ref[...], m_ref[...], v_ref[...])
        d_out[...] = delta
        m_out[...] = m2
        v_out[...] = v2

    whole = lambda: (0, 0)
    return _call(body, "adamw_flat", (), [(a, shape, whole) for a in (g, w, m, v)],
                 [(shape, F32, shape, whole)] * 3)


def _pack(parts):
    flat = jnp.concatenate([p.reshape(-1).astype(F32) for p in parts])
    rows = -(-flat.shape[0] // (8 * LANES)) * 8
    return jnp.pad(flat, (0, rows * LANES - flat.shape[0])).reshape(rows, LANES)


def _unpack(packed, shapes):
    flat = packed.reshape(-1)
    out, off = [], 0
    for s in shapes:
        size = math.prod(s)
        out.append(flat[off:off + size].reshape(s))
        off += size
    return out


def kernel(x, mem, ffn1_norm, ffn1_w13, ffn1_w2, mix_norm, gmlp_w_in, gmlp_ln_g, gmlp_ln_b, gmlp_w_s, gmlp_b_s, gmlp_w_out, conv_w_in, conv_w, conv_w_out, xattn_norm, mem_norm, xattn_wq, xattn_wkv, xattn_wo, ffn2_norm, ffn2_w13, ffn2_w2, final_norm, loss_target, m_ffn1_norm, m_ffn1_w13, m_ffn1_w2, m_mix_norm, m_gmlp_w_in, m_gmlp_ln_g, m_gmlp_ln_b, m_gmlp_w_s, m_gmlp_b_s, m_gmlp_w_out, m_conv_w_in, m_conv_w, m_conv_w_out, m_xattn_norm, m_mem_norm, m_xattn_wq, m_xattn_wkv, m_xattn_wo, m_ffn2_norm, m_ffn2_w13, m_ffn2_w2, m_final_norm, v_ffn1_norm, v_ffn1_w13, v_ffn1_w2, v_mix_norm, v_gmlp_w_in, v_gmlp_ln_g, v_gmlp_ln_b, v_gmlp_w_s, v_gmlp_b_s, v_gmlp_w_out, v_conv_w_in, v_conv_w, v_conv_w_out, v_xattn_norm, v_mem_norm, v_xattn_wq, v_xattn_wkv, v_xattn_wo, v_ffn2_norm, v_ffn2_w13, v_ffn2_w2, v_final_norm):
    given = dict(locals())
    T, D = x.shape[1], x.shape[2]
    depth = ffn1_norm.shape[0]
    xs = x.reshape(T, D)
    mems = mem.reshape(mem.shape[1], D)
    target = loss_target.reshape(T, D)
    me = 4 * lax.axis_index("x") + 2 * lax.axis_index("y") + lax.axis_index("c")
    E = gmlp_ln_g.shape[1]
    gd = E // GROUPS
    cshard = conv_w.shape[2]

    bf = {k: given[k].astype(BF16) for k in
          ("ffn1_w13", "ffn1_w2", "gmlp_w_in", "gmlp_w_out", "conv_w_in", "conv_w_out",
           "xattn_wq", "xattn_wkv", "xattn_wo", "ffn2_w13", "ffn2_w2")}

    W = {}

    def gather(names_layers):
        return names_layers, gather_exchange([(bf[k], l) for k, l in names_layers])

    def landed(tagged):
        names_layers, exchange = tagged
        W.update(zip(names_layers, exchange.results))

    bias_full = jnp.repeat(gmlp_b_s[0].T, gd, axis=1)

    first = gather_exchange([(bf["ffn1_w13"], 0), (conv_w, 0)])
    taps = {}

    def first_w13():
        W["ffn1_w13", 0], cw_shards = first.results
        taps["conv"] = jnp.transpose(cw_shards, (1, 0, 2)).reshape(CONV_WIDTH, D)
        return W["ffn1_w13", 0]

    saved = []
    h = xs
    for i in range(depth):
        j = i // 2
        is_gmlp = i % 2 == 0
        mix = ("gmlp_w_in", "gmlp_w_out") if is_gmlp else ("conv_w_in", "conv_w_out")
        on_up1 = gather(([("ffn1_w2", i)] if i == 0 else []) + [(mix[0], j), ("ffn2_w13", i)])
        on_down1 = gather([(mix[1], j), ("xattn_wq", i), ("xattn_wkv", i)])

        def w2_after_up(on_up=on_up1, i=i):
            landed(on_up)
            return W["ffn1_w2", i]

        h, sv1 = ffn_fwd(h, ffn1_norm[i], first_w13 if i == 0 else (lambda i=i: W["ffn1_w13", i]), w2_after_up,
                         first if i == 0 else None, on_up1[1], on_down1[1])
        landed(on_down1)
        hn = rms_fwd(h, mix_norm[i])
        on_mix_in = gather([("xattn_wo", i), ("ffn2_w2", i)])
        p = mm_nn(hn, W[mix[0], j], "mix_in", group=2, exchange=on_mix_in[1])
        landed(on_mix_in)
        if is_gmlp:
            gated = gmlp_mid_fwd(p, gmlp_ln_g[j:j + 1], gmlp_ln_b[j:j + 1], gmlp_w_s[j], bias_full)
        else:
            gated = conv_mid_fwd(p, taps["conv"])
        h_mix = mm_nn(gated, W[mix[1], j].reshape(1, gated.shape[1], D), "mix_out", F32, res=h)
        sv2 = (h, hn, p, gated)
        h, sv3 = xattn_fwd(h_mix, mems, xattn_norm[i], mem_norm[i],
                           W["xattn_wq", i], W["xattn_wkv", i], W["xattn_wo", i])
        on_up2 = gather([("ffn1_w13", i + 1), ("ffn1_w2", i + 1)]) if i + 1 < depth else None
        h, sv4 = ffn_fwd(h, ffn2_norm[i], lambda i=i: W["ffn2_w13", i], lambda i=i: W["ffn2_w2", i],
                         up_exchange=None if on_up2 is None else on_up2[1])
        if on_up2 is not None:
            landed(on_up2)
        saved.append((sv1, sv2, sv3, sv4))

    loss_part, d, dys, d_final_norm = loss_head(h, final_norm, target, 0.5)

    small = {k: [None] * depth for k in ("ffn1_norm", "mix_norm", "xattn_norm", "mem_norm", "ffn2_norm")}
    partial = {}
    for i in reversed(range(depth)):
        j = i // 2
        is_gmlp = i % 2 == 0
        sv1, sv2, sv3, sv4 = saved[i]
        d, dys, small["ffn2_norm"][i], partial["ffn2_w13", i], partial["ffn2_w2", i] = ffn_bwd(
            d, dys, sv4, ffn2_norm[i], W["ffn2_w13", i], W["ffn2_w2", i], 1.0)
        (d, dys, small["xattn_norm"][i], small["mem_norm"][i], partial["xattn_wq", i],
         dwkv, partial["xattn_wo", i]) = xattn_bwd(
            d, dys, sv3, mems, xattn_norm[i], W["xattn_wq", i], W["xattn_wkv", i], W["xattn_wo", i], 1.0)
        if is_gmlp:
            mix = ("gmlp_w_in", "gmlp_w_out")
            mid = lambda p, dg, send: (lambda r: (r[0], r[1:]))(gmlp_mid_bwd(
                p, dg, gmlp_ln_g[j:j + 1], gmlp_ln_b[j:j + 1], gmlp_w_s[j], bias_full, exchange=send))
        else:
            mix = ("conv_w_in", "conv_w_out")
            mid = lambda p, dg, send: (lambda r: (r[0], r[1:]))(conv_mid_bwd(p, dg, taps["conv"], exchange=send))
        (d, dys, small["mix_norm"][i], partial[mix[0], j], partial[mix[1], j], partial["xattn_wkv", i],
         extra) = mixer_bwd_common(d, dys, sv2, mix_norm[i], W[mix[0], j], W[mix[1], j], mid, 0.5, dwkv)
        if is_gmlp:
            d_ws, d_bs_wide, d_lng, d_lnb = extra
        else:
            (d_cw,) = extra
        d, dys, small["ffn1_norm"][i], partial["ffn1_w13", i], partial["ffn1_w2", i] = ffn_bwd(
            d, dys, sv1, ffn1_norm[i], W["ffn1_w13", i], W["ffn1_w2", i], 0.5)
    grad_x = d.reshape(x.shape)

    small_grads = {k: jnp.concatenate(v, axis=0) for k, v in small.items()}
    small_grads["gmlp_ln_g"] = d_lng
    small_grads["gmlp_ln_b"] = d_lnb
    small_grads["gmlp_w_s"] = d_ws[None]
    small_grads["gmlp_b_s"] = d_bs_wide[None, :, :, 0]
    small_grads["final_norm"] = d_final_norm.reshape(-1)
    small_names = ["ffn1_norm", "mix_norm", "gmlp_ln_g", "gmlp_ln_b", "gmlp_w_s", "gmlp_b_s",
                   "xattn_norm", "mem_norm", "ffn2_norm", "final_norm"]
    summed = small_all_reduce(_pack([small_grads[k] for k in small_names] + [d_cw, loss_part]))
    parts = _unpack(summed, [given[k].shape for k in small_names] + [(CONV_WIDTH, D), (1, LANES)])
    grads = dict(zip(small_names, parts[:len(small_names)]))
    grads["conv_w"] = lax.dynamic_slice(parts[-2], (jnp.int32(0), me * cshard), (CONV_WIDTH, cshard))[None]
    loss = parts[-1][0, 0]
    flat_names = small_names + ["conv_w"]
    flat = adamw_flat(*[_pack([src[k] for k in flat_names]) for src in
                        (grads, given, {k: given["m_" + k] for k in flat_names},
                         {k: given["v_" + k] for k in flat_names})])
    delta, new_m, new_v = [dict(zip(flat_names, _unpack(f, [given[k].shape for k in flat_names]))) for f in flat]

    for k in bf:
        w = given[k]
        L = w.shape[0]
        shard = w.shape[1:]
        view = lambda a: a.reshape((L,) + shard)
        g, dl, m2, v2 = adamw_sharded([partial[k, l] for l in range(L)], w, given["m_" + k], given["v_" + k])
        grads[k], delta[k], new_m[k], new_v[k] = view(g), view(dl), view(m2), view(v2)

    order = ["ffn1_norm", "ffn1_w13", "ffn1_w2", "mix_norm", "gmlp_w_in", "gmlp_ln_g", "gmlp_ln_b", "gmlp_w_s",
             "gmlp_b_s", "gmlp_w_out", "conv_w_in", "conv_w", "conv_w_out", "xattn_norm", "mem_norm", "xattn_wq",
             "xattn_wkv", "xattn_wo", "ffn2_norm", "ffn2_w13", "ffn2_w2", "final_norm"]
    return (loss, grad_x, *[grads[k] for k in order], *[delta[k] for k in order],
            *[new_m[k] for k in order], *[new_v[k] for k in order])
```

```python
import math

import jax
import jax.numpy as jnp
from jax import lax
from jax.experimental import pallas as pl
from jax.experimental.pallas import tpu as pltpu

F32 = jnp.float32
BF16 = jnp.bfloat16

N_DEV = 8
N_PEERS = N_DEV - 1
CHUNK = 128
GROUPS = 8
HEADS = 4
CONV_WIDTH = 3
RMS_EPS = 1e-6
LN_EPS = 1e-5
ADAM_LR = 0.001
ADAM_B1 = 0.9
ADAM_B2 = 0.999
ADAM_EPS = 1e-08
ADAM_WD = 0.01
ADAM_STEP = 10
LANES = 128
BF16_SUBLANES = 16
MXU_WIDTH = 256
VMEM_LIMIT_BYTES = 56 * 1024 * 1024

_NT = (((1,), (1,)), ((), ()))
_TN = (((0,), (0,)), ((), ()))
_SQRT_HALF = 0.7071067811865476
_INV_SQRT_2PI = 0.3989422804014327


def _div(n, pref, align):
    best = None
    for t in range(align, min(n, pref) + 1, align):
        if n % t == 0:
            best = t
    return n if best is None else best


def _chunks(n, width):
    return [(c0, min(width, n - c0)) for c0 in range(0, n, width)]


_ANY = pl.BlockSpec(memory_space=pl.ANY)


class Exchange:
    def __init__(self, arrays, out_shapes, sems, start, finish, forward=None):
        self.arrays, self.out_shapes, self.sems = list(arrays), list(out_shapes), list(sems)
        self.start, self.finish, self.forward = start, finish, forward
        self.results = None


def _call(body, name, grid, ins, outs, scratch=(), exchange=None):
    in_specs = [pl.BlockSpec(*spec[1:3], **({"pipeline_mode": pl.Buffered(1)} if len(spec) > 3 else {}))
                for spec in ins]
    ins = [spec[:3] for spec in ins]
    out_specs = [pl.BlockSpec(bs, im) for _, _, bs, im in outs]
    out_shape = [jax.ShapeDtypeStruct(s, d) for s, d, _, _ in outs]
    arrays = [a for a, _, _ in ins]
    scratch = list(scratch)
    kernel_fn = body
    if exchange is not None:
        n_in, n_out, n_scr = len(ins), len(outs), len(scratch)
        n_xin, n_xout = len(exchange.arrays), len(exchange.out_shapes)
        steps = math.prod(grid)
        forward_step = min(steps - 1, (15 * steps) // 16)

        def kernel_fn(*refs):
            refs = list(refs)
            own_in, x_in = refs[:n_in], refs[n_in:n_in + n_xin]
            refs = refs[n_in + n_xin:]
            own_out, x_out = refs[:n_out], refs[n_out:n_out + n_xout]
            refs = refs[n_out + n_xout:]
            own_scr, x_sems = refs[:n_scr], refs[n_scr:]
            step = 0
            for axis, size in enumerate(grid):
                step = step * size + pl.program_id(axis)

            @pl.when(step == 0)
            def _():
                exchange.start(x_in, x_out, x_sems)

            if exchange.forward is not None:
                @pl.when(step == forward_step)
                def _():
                    exchange.forward(x_in, x_out, x_sems)

            body(*own_in, *own_out, *own_scr)

            @pl.when(step == steps - 1)
            def _():
                exchange.finish(x_in, x_out, x_sems)

        in_specs += [_ANY] * n_xin
        out_specs += [_ANY] * n_xout
        out_shape += exchange.out_shapes
        arrays += exchange.arrays
        scratch += exchange.sems
    res = pl.pallas_call(
        kernel_fn,
        name=name,
        grid=grid,
        in_specs=in_specs,
        out_specs=out_specs,
        out_shape=out_shape,
        scratch_shapes=scratch,
        compiler_params=pltpu.CompilerParams(
            dimension_semantics=("arbitrary",) * len(grid), vmem_limit_bytes=VMEM_LIMIT_BYTES),
    )(*arrays)
    if exchange is not None:
        exchange.results = list(res[len(outs):])
        res = res[:len(outs)]
    return res


def _dot(a, b, dims=None):
    if dims is None:
        return jnp.dot(a, b, preferred_element_type=F32)
    return lax.dot_general(a, b, dims, preferred_element_type=F32)


def _sigmoid(v):
    return 1.0 / (1.0 + jnp.exp(-v))


def _normal_cdf(v):
    return 0.5 * (1.0 + lax.erf(v * _SQRT_HALF))


def _normal_pdf(v):
    return _INV_SQRT_2PI * jnp.exp(-0.5 * v * v)


def _accumulate_product(ref, first, product, skip=False):
    @pl.when(first)
    def _():
        ref[...] = product()

    @pl.when(jnp.logical_not(jnp.logical_or(first, skip)))
    def _():
        ref[...] += product()


def _accumulate(ref, part, first):
    @pl.when(first)
    def _():
        ref[...] = part

    @pl.when(jnp.logical_not(first))
    def _():
        ref[...] += part


def rms_fwd(x, g, exchange=None):
    T, D = x.shape
    tt = _div(T, 512, BF16_SUBLANES)

    def body(x_ref, g_ref, o_ref):
        xv = x_ref[...]
        r = lax.rsqrt(jnp.mean(xv * xv, axis=-1, keepdims=True) + RMS_EPS)
        o_ref[...] = ((xv * r) * g_ref[...]).astype(BF16)

    return _call(body, "rms_fwd", (T // tt,),
                 [(x, (tt, D), lambda i: (i, 0)), (g.reshape(1, D), (1, D), lambda i: (0, 0))],
                 [((T, D), BF16, (tt, D), lambda i: (i, 0))], exchange=exchange)[0]


def rms_bwd(x, g, dxn, d, scale):
    T, D = x.shape
    tt = _div(T, 512, BF16_SUBLANES)

    def body(x_ref, g_ref, dn_ref, d_ref, dx_ref, dxs_ref, dg_ref):
        xv = x_ref[...]
        r = lax.rsqrt(jnp.mean(xv * xv, axis=-1, keepdims=True) + RMS_EPS)
        xh = xv * r
        dn = dn_ref[...].astype(F32)
        dxh = dn * g_ref[...]
        dx = r * (dxh - xh * jnp.mean(dxh * xh, axis=-1, keepdims=True)) + d_ref[...]
        dx_ref[...] = dx
        dxs_ref[...] = (scale * dx).astype(BF16)
        _accumulate(dg_ref, jnp.sum(dn * xh, axis=0, keepdims=True), pl.program_id(0) == 0)

    row = lambda i: (i, 0)
    fix = lambda i: (0, 0)
    return _call(body, "rms_bwd", (T // tt,),
                 [(x, (tt, D), row), (g.reshape(1, D), (1, D), fix), (dxn, (tt, D), row), (d, (tt, D), row)],
                 [((T, D), F32, (tt, D), row), ((T, D), BF16, (tt, D), row), ((1, D), F32, (1, D), fix)])


def rms_gain_grad(x, dxn):
    T, D = x.shape
    tt = _div(T, 256, 8)

    def body(x_ref, dn_ref, dg_ref):
        xv = x_ref[...]
        r = lax.rsqrt(jnp.mean(xv * xv, axis=-1, keepdims=True) + RMS_EPS)
        _accumulate(dg_ref, jnp.sum(dn_ref[...] * (xv * r), axis=0, keepdims=True), pl.program_id(0) == 0)

    row = lambda i: (i, 0)
    return _call(body, "rms_gain_grad", (T // tt,), [(x, (tt, D), row), (dxn, (tt, D), row)],
                 [((1, D), F32, (1, D), lambda i: (0, 0))])[0]


def loss_head(x, g, target, scale):
    T, D = x.shape
    tt = _div(T, 256, BF16_SUBLANES)

    def body(x_ref, g_ref, t_ref, loss_ref, dx_ref, dxs_ref, dg_ref):
        first = pl.program_id(0) == 0
        xv = x_ref[...]
        gv = g_ref[...]
        r = lax.rsqrt(jnp.mean(xv * xv, axis=-1, keepdims=True) + RMS_EPS)
        xh = xv * r
        err = xh * gv - t_ref[...]
        part = 0.5 * jnp.sum(jnp.mean(err * err, axis=-1, keepdims=True), axis=0, keepdims=True)
        _accumulate(loss_ref, jnp.broadcast_to(part, (1, LANES)), first)
        dy = err * (1.0 / D)
        dxh = dy * gv
        dx = r * (dxh - xh * jnp.mean(dxh * xh, axis=-1, keepdims=True))
        dx_ref[...] = dx
        dxs_ref[...] = (scale * dx).astype(BF16)
        _accumulate(dg_ref, jnp.sum(dy * xh, axis=0, keepdims=True), first)

    row = lambda i: (i, 0)
    fix = lambda i: (0, 0)
    return _call(body, "loss_head", (T // tt,),
                 [(x, (tt, D), row), (g.reshape(1, D), (1, D), fix), (target, (tt, D), row)],
                 [((1, LANES), F32, (1, LANES), fix), ((T, D), F32, (tt, D), row),
                  ((T, D), BF16, (tt, D), row), ((1, D), F32, (1, D), fix)])


def mm_nn(a, w3, name, out_dtype=BF16, res=None, res_scale=1.0, tm_pref=1024, tn_pref=1024, group=1,
          exchange=None):
    M, K = a.shape
    nb, _, ns = w3.shape
    tn = ns if group > 1 else _div(ns, tn_pref, LANES)
    per = ns // tn
    tm = _div(M, tm_pref, BF16_SUBLANES)
    wide = group * tn
    ins = [(a, (tm, K), lambda j, m: (m, 0)), (w3, (group, K, tn), lambda j, m: (j // per, 0, j % per))]
    if res is not None:
        ins.append((res, (tm, wide), lambda j, m: (m, j)))

    def body(*refs):
        a_ref, w_ref = refs[0], refs[1]
        o_ref = refs[-1]
        for k in range(group):
            cols = slice(k * tn, (k + 1) * tn)
            acc = _dot(a_ref[...], w_ref[k])
            if res is not None:
                acc = refs[2][:, cols] + res_scale * acc
            o_ref[:, cols] = acc.astype(o_ref.dtype)

    return _call(body, name, (nb * per // group, M // tm), ins,
                 [((M, nb * ns), out_dtype, (tm, wide), lambda j, m: (m, j))], exchange=exchange)[0]


def mm_nt(a_in, w3, name, out_dtype, M, tm_pref=1024, to_pref=2048, group=1, exchange=None):
    nb, Ko, ns = w3.shape
    to = _div(Ko, to_pref, LANES)
    tm = _div(M, tm_pref, BF16_SUBLANES)
    if isinstance(a_in, tuple):
        a, a_bs, a_im = a_in
        a_bs = tuple(tm if s == "tm" else s for s in a_bs)
    else:
        a, a_bs, a_im = a_in, (tm, group * ns), lambda m, o, b: (m, b)
    steps = nb // group
    narrow_out = steps > 1 and out_dtype != F32

    def product(a_ref, w_ref):
        p = None
        for k in range(group):
            pk = _dot(a_ref[:, k * ns:(k + 1) * ns], w_ref[k], _NT)
            p = pk if p is None else p + pk
        return p

    def body(a_ref, w_ref, o_ref, *acc):
        b = pl.program_id(2)
        if steps == 1:
            o_ref[...] = product(a_ref, w_ref).astype(o_ref.dtype)
        elif not narrow_out:
            _accumulate_product(o_ref, b == 0, lambda: product(a_ref, w_ref))
        else:
            acc_ref, = acc
            _accumulate_product(acc_ref, b == 0, lambda: product(a_ref, w_ref), skip=b == steps - 1)

            @pl.when(b == steps - 1)
            def _():
                o_ref[...] = (acc_ref[...] + product(a_ref, w_ref)).astype(o_ref.dtype)

    return _call(body, name, (M // tm, Ko // to, steps),
                 [(a, a_bs, a_im), (w3, (group, to, ns), lambda m, o, b: (b, o, 0))],
                 [((M, Ko), out_dtype, (tm, to), lambda m, o, b: (m, o))],
                 scratch=[pltpu.VMEM((tm, to), F32)] if narrow_out else [], exchange=exchange)[0]


def mm_nt_rms_bwd(a, w3, name, x, g, d, scale, exchange=None):
    nb, D, ns = w3.shape
    M = x.shape[0]
    tm = _div(M, 512, BF16_SUBLANES)
    a_bs, a_im = (tm, ns), lambda m, b: (m, b)

    rc = _div(tm, 64, BF16_SUBLANES)

    def body(a_ref, w_ref, x_ref, g_ref, d_ref, dx_ref, dxs_ref, dg_ref, acc_ref):
        m, b = pl.program_id(0), pl.program_id(1)
        _accumulate_product(acc_ref, b == 0, lambda: _dot(a_ref[...], w_ref[...], _NT))

        @pl.when(b == nb - 1)
        def _():
            gv = g_ref[...]

            def piece(c, dg):
                rows = pl.ds(pl.multiple_of(c * rc, rc), rc)
                dn = acc_ref[rows, :]
                xv = x_ref[rows, :]
                r = lax.rsqrt(jnp.mean(xv * xv, axis=-1, keepdims=True) + RMS_EPS)
                xh = xv * r
                dxh = dn * gv
                dx = r * (dxh - xh * jnp.mean(dxh * xh, axis=-1, keepdims=True)) + d_ref[rows, :]
                dx_ref[rows, :] = dx
                dxs_ref[rows, :] = (scale * dx).astype(BF16)
                return dg + jnp.sum(dn * xh, axis=0, keepdims=True)

            _accumulate(dg_ref, lax.fori_loop(0, tm // rc, piece, jnp.zeros((1, D), F32)), m == 0)

    row = lambda m, b: (m, 0)
    fix = lambda m, b: (0, 0)
    w_spec = (w3, (None, D, ns), lambda m, b: (b, 0, 0)) + (("single",) if nb == 1 else ())
    return _call(body, name, (M // tm, nb),
                 [(a, a_bs, a_im), w_spec,
                  (x, (tm, D), row), (g.reshape(1, D), (1, D), fix), (d, (tm, D), row)],
                 [((M, D), F32, (tm, D), row), ((M, D), BF16, (tm, D), row), ((1, D), F32, (1, D), fix)],
                 scratch=[pltpu.VMEM((tm, D), F32)], exchange=exchange)


def mm_tn(a, b_in, name, nbo, ns, tka_pref=1024, tt_pref=2048, tn_pref=2048, group=1, exchange=None):
    T, Ka = a.shape
    tt = _div(T, tt_pref, BF16_SUBLANES)
    tka = _div(Ka, tka_pref, LANES)
    if isinstance(b_in, tuple):
        b, b_bs, b_im = b_in
        b_bs = tuple(tt if s == "tt" else s for s in b_bs)
        tn, per = ns, 1
    else:
        tn = ns if group > 1 else _div(ns, tn_pref, LANES)
        per = ns // tn
        b, b_bs, b_im = b_in, (tt, group * tn), lambda i, j, t: (t, j)
    nt = T // tt

    def body(a_ref, b_ref, o_ref, acc_ref):
        t = pl.program_id(2)
        product = lambda: _dot(a_ref[...], b_ref[...], _TN)

        def store(total):
            for k in range(group):
                o_ref[k] = total[:, k * tn:(k + 1) * tn].astype(BF16)

        if nt == 1:
            store(product())
        else:
            _accumulate_product(acc_ref, t == 0, product, skip=t == nt - 1)

            @pl.when(t == nt - 1)
            def _():
                store(acc_ref[...] + product())

    return _call(body, name, (Ka // tka, nbo * per // group, nt),
                 [(a, (tt, tka), lambda i, j, t: (t, i)), (b, b_bs, b_im)],
                 [((nbo, Ka, ns), BF16, (group, tka, tn), lambda i, j, t: (j // per, i, j % per))],
                 scratch=[pltpu.VMEM((tka, group * tn), F32)], exchange=exchange)[0]


def ffn_up(xn, w13, exchange=None):
    T, D = xn.shape
    nb, _, ns = w13.shape
    half = nb // 2
    F = half * ns
    tm = _div(T, 512, BF16_SUBLANES)
    pair = 2 if half % 2 == 0 else 1

    def columns(w_ref, c0, cw):
        k, off = divmod(c0, ns)
        if off + cw <= ns:
            return w_ref[k, :, off:off + cw]
        return jnp.concatenate([w_ref[k, :, off:ns], w_ref[k + 1, :, 0:off + cw - ns]], axis=1)

    def body(x_ref, wg_ref, wu_ref, fac_ref, act_ref):
        xv = x_ref[...]
        for c0, cw in _chunks(pair * ns, MXU_WIDTH):
            cols = slice(c0, c0 + cw)
            gate = _dot(xv, columns(wg_ref, c0, cw))
            up = _dot(xv, columns(wu_ref, c0, cw))
            s = _sigmoid(gate)
            silu = gate * s
            fac_ref[0, :, cols] = (up * (s * (1.0 + gate * (1.0 - s)))).astype(BF16)
            fac_ref[1, :, cols] = silu.astype(BF16)
            act_ref[:, cols] = (silu * up).astype(BF16)

    tn = pair * ns
    return _call(body, "ffn_up", (half // pair, T // tm),
                 [(xn, (tm, D), lambda j, m: (m, 0)),
                  (w13, (pair, D, ns), lambda j, m: (j, 0, 0), "single"),
                  (w13, (pair, D, ns), lambda j, m: (j + half // pair, 0, 0), "single")],
                 [((2, T, F), BF16, (2, tm, tn), lambda j, m: (0, m, j)),
                  ((T, F), BF16, (tm, tn), lambda j, m: (m, j))], exchange=exchange)


def ffn_dact(dy, w2, fac, exchange=None):
    T, D = dy.shape
    F = w2.shape[0]
    tm = _div(T, 512, BF16_SUBLANES)
    tn = _div(F, F // 2, MXU_WIDTH)

    def body(dy_ref, w_ref, fac_ref, dh_ref):
        dyv = dy_ref[...]
        for c0, cw in _chunks(tn, MXU_WIDTH):
            cols = slice(c0, c0 + cw)
            da = _dot(dyv, w_ref[cols, :], _NT)
            dh_ref[0, :, cols] = (da * fac_ref[0, :, cols].astype(F32)).astype(BF16)
            dh_ref[1, :, cols] = (da * fac_ref[1, :, cols].astype(F32)).astype(BF16)

    return _call(body, "ffn_dact", (F // tn, T // tm),
                 [(dy, (tm, D), lambda j, m: (m, 0)), (w2, (tn, D), lambda j, m: (j, 0), "single"),
                  (fac, (2, tm, tn), lambda j, m: (0, m, j))],
                 [((2, T, F), BF16, (2, tm, tn), lambda j, m: (0, m, j))], exchange=exchange)[0]


def ffn_fwd(x, norm_g, get_w13, get_w2, norm_exchange=None, up_exchange=None, down_exchange=None):
    xn = rms_fwd(x, norm_g, exchange=norm_exchange)
    fac, act = ffn_up(xn, get_w13(), exchange=up_exchange)
    F = act.shape[1]
    y = mm_nn(act, get_w2().reshape(1, F, -1), "ffn_down", F32, res=x, res_scale=0.5, tm_pref=512,
              exchange=down_exchange)
    return y, (x, xn, fac, act)


def ffn_bwd(d, dys, saved, norm_g, w13, w2, scale_out):
    x, xn, fac, act = saved
    T, D = x.shape
    nb, _, ns = w13.shape
    half = nb // 2
    F = half * ns
    dh = ffn_dact(dys, w2.reshape(F, D), fac)
    dw2 = mm_tn(act, dys, "ffn_dw2", 1, D, tka_pref=ns, tn_pref=1024)
    send_w2 = scatter_exchange([dw2.reshape(N_DEV, F // N_DEV, D)])
    dw13 = mm_tn(xn, (dh, (None, "tt", ns), lambda i, j, t: (j // half, t, j % half)), "ffn_dw13", nb, ns,
                 exchange=send_w2)
    send_w13 = scatter_exchange([dw13])
    pair = 2 if half % 2 == 0 else 1
    per = half // pair
    dxn = mm_nt((dh, (None, "tm", pair * ns), lambda m, o, b: (b // per, m, b % per)), w13, "ffn_dxn", BF16, T,
                tm_pref=512, group=pair, exchange=send_w13)
    dx, dxs, dg = rms_bwd(x, norm_g, dxn, d, scale_out)
    return dx, dxs, dg, send_w13.results[0], send_w2.results[0]


def _gmlp_parts(p_ref, lng_ref, lnb_ref):
    E = lng_ref.shape[-1]
    pv = p_ref[...].astype(F32)
    cdf = _normal_cdf(pv)
    z = pv * cdf
    u = z[:, :E]
    vp = z[:, E:]
    mu = jnp.mean(vp, axis=-1, keepdims=True)
    xc = vp - mu
    rstd = lax.rsqrt(jnp.mean(xc * xc, axis=-1, keepdims=True) + LN_EPS)
    vh = xc * rstd
    v = vh * lng_ref[...] + lnb_ref[...]
    return u, vh, rstd, v, pv, cdf


def _causal_ws(ws_ref, g):
    keep = lax.broadcasted_iota(jnp.int32, (CHUNK, CHUNK), 0) >= lax.broadcasted_iota(jnp.int32, (CHUNK, CHUNK), 1)
    return jnp.where(keep, ws_ref[g], 0.0).astype(BF16), keep


def gmlp_mid_fwd(p, ln_g, ln_b, w_s, bias_full):
    T, E2 = p.shape
    E = E2 // 2
    gd = E // GROUPS
    tm = _div(T, 256, CHUNK)
    fix2 = lambda i: (0, 0)

    def body(p_ref, lng_ref, lnb_ref, ws_ref, bias_ref, o_ref):
        u, _, _, v, _, _ = _gmlp_parts(p_ref, lng_ref, lnb_ref)
        vb = v.astype(BF16)
        for g in range(GROUPS):
            wm, _ = _causal_ws(ws_ref, g)
            cols = slice(g * gd, (g + 1) * gd)
            for c in range(tm // CHUNK):
                rows = slice(c * CHUNK, (c + 1) * CHUNK)
                f = _dot(wm, vb[rows, cols]) + bias_ref[:, cols]
                o_ref[rows, cols] = (u[rows, cols] * f).astype(BF16)

    return _call(body, "gmlp_mid_fwd", (T // tm,),
                 [(p, (tm, E2), lambda i: (i, 0)), (ln_g, (1, E), fix2), (ln_b, (1, E), fix2),
                  (w_s, (GROUPS, CHUNK, CHUNK), lambda i: (0, 0, 0)), (bias_full, (CHUNK, E), fix2)],
                 [((T, E), BF16, (tm, E), lambda i: (i, 0))])[0]


def gmlp_mid_bwd(p, dgated, ln_g, ln_b, w_s, bias_full, exchange=None):
    T, E2 = p.shape
    E = E2 // 2
    gd = E // GROUPS
    tm = _div(T, 256, CHUNK)
    nsteps = T // tm
    fix2 = lambda i: (0, 0)
    fix3 = lambda i: (0, 0, 0)

    def body(p_ref, dg_ref, lng_ref, lnb_ref, ws_ref, bias_ref,
             dp_ref, dws_ref, dbs_ref, dlng_ref, dlnb_ref, f_sc, dv_sc, db_sc):
        i = pl.program_id(0)
        first = i == 0
        u, vh, rstd, v, pv, cdf = _gmlp_parts(p_ref, lng_ref, lnb_ref)
        vb = v.astype(BF16)
        dgt = dg_ref[...].astype(F32)
        df = dgt * u
        dfb = df.astype(BF16)
        for g in range(GROUPS):
            wm, keep = _causal_ws(ws_ref, g)
            cols = slice(g * gd, (g + 1) * gd)
            dw = None
            dbg = None
            for c in range(tm // CHUNK):
                rows = slice(c * CHUNK, (c + 1) * CHUNK)
                f_sc[rows, cols] = _dot(wm, vb[rows, cols]) + bias_ref[:, cols]
                dv_sc[rows, cols] = _dot(wm, dfb[rows, cols], _TN)
                part = _dot(dfb[rows, cols], vb[rows, cols], _NT)
                dw = part if dw is None else dw + part
                dbg = df[rows, cols] if dbg is None else dbg + df[rows, cols]
            dw = jnp.where(keep, dw, 0.0)

            @pl.when(first)
            def _():
                dws_ref[g] = dw
                db_sc[:, cols] = dbg

            @pl.when(jnp.logical_not(first))
            def _():
                dws_ref[g] += dw
                db_sc[:, cols] += dbg

        du = dgt * f_sc[...]
        dv = dv_sc[...]
        _accumulate(dlng_ref, jnp.sum(dv * vh, axis=0, keepdims=True), first)
        _accumulate(dlnb_ref, jnp.sum(dv, axis=0, keepdims=True), first)
        dvh = dv * lng_ref[...]
        dvp = rstd * (dvh - jnp.mean(dvh, axis=-1, keepdims=True)
                      - vh * jnp.mean(dvh * vh, axis=-1, keepdims=True))
        gp = cdf + pv * _normal_pdf(pv)
        dp_ref[:, :E] = (du * gp[:, :E]).astype(BF16)
        dp_ref[:, E:] = (dvp * gp[:, E:]).astype(BF16)

        @pl.when(i == nsteps - 1)
        def _():
            for g in range(GROUPS):
                tot = jnp.sum(db_sc[:, g * gd:(g + 1) * gd], axis=-1, keepdims=True)
                dbs_ref[g] = jnp.broadcast_to(tot, (CHUNK, LANES))

    return _call(body, "gmlp_mid_bwd", (nsteps,),
                 [(p, (tm, E2), lambda i: (i, 0)), (dgated, (tm, E), lambda i: (i, 0)),
                  (ln_g, (1, E), fix2), (ln_b, (1, E), fix2),
                  (w_s, (GROUPS, CHUNK, CHUNK), fix3), (bias_full, (CHUNK, E), fix2)],
                 [((T, E2), BF16, (tm, E2), lambda i: (i, 0)),
                  ((GROUPS, CHUNK, CHUNK), F32, (GROUPS, CHUNK, CHUNK), fix3),
                  ((GROUPS, CHUNK, LANES), F32, (GROUPS, CHUNK, LANES), fix3),
                  ((1, E), F32, (1, E), fix2), ((1, E), F32, (1, E), fix2)],
                 scratch=[pltpu.VMEM((tm, E), F32), pltpu.VMEM((tm, E), F32), pltpu.VMEM((CHUNK, E), F32)],
                 exchange=exchange)


HALO = 16


def _row_of(block, r):
    rows = lax.broadcasted_iota(jnp.int32, block.shape, 0)
    return jnp.sum(jnp.where(rows == r, block, 0.0), axis=0, keepdims=True)


def _shift_down(z, k, fill):
    out = pltpu.roll(z, k, 0)
    rows = lax.broadcasted_iota(jnp.int32, z.shape, 0)
    for t in range(k):
        out = jnp.where(rows == t, fill[t], out)
    return out


def _shift_up(z, k, fill):
    n = z.shape[0]
    out = pltpu.roll(z, n - k, 0)
    rows = lax.broadcasted_iota(jnp.int32, z.shape, 0)
    for j in range(k):
        out = jnp.where(rows == n - k + j, fill[j], out)
    return out


def _conv_parts(p_ref, prev_ref, cw_ref, is_first):
    D = cw_ref.shape[-1]
    pv = p_ref[...].astype(F32)
    bg, cg, val = pv[:, :D], pv[:, D:2 * D], pv[:, 2 * D:]
    z = cg * val
    pp = prev_ref[...].astype(F32)
    zp = jnp.where(is_first, 0.0, pp[:, D:2 * D] * pp[:, 2 * D:])
    zl1 = _row_of(zp, HALO - 1)
    zl2 = _row_of(zp, HALO - 2)
    z1 = _shift_down(z, 1, [zl1])
    z2 = _shift_down(z, 2, [zl2, zl1])
    conv = z2 * cw_ref[0:1, :] + z1 * cw_ref[1:2, :] + z * cw_ref[2:3, :]
    return bg, cg, val, z, z1, z2, conv


def conv_mid_fwd(p, cw):
    T, D3 = p.shape
    D = D3 // 3
    tm = _div(T, 256, HALO)
    per = tm // HALO

    def body(p_ref, prev_ref, cw_ref, o_ref):
        bg, _, _, _, _, _, conv = _conv_parts(p_ref, prev_ref, cw_ref, pl.program_id(0) == 0)
        o_ref[...] = (bg * conv).astype(BF16)

    return _call(body, "conv_mid_fwd", (T // tm,),
                 [(p, (tm, D3), lambda i: (i, 0)),
                  (p, (HALO, D3), lambda i: (jnp.maximum(i * per - 1, 0), 0)),
                  (cw, (CONV_WIDTH, D), lambda i: (0, 0))],
                 [((T, D), BF16, (tm, D), lambda i: (i, 0))])[0]


def conv_mid_bwd(p, dgated, cw, exchange=None):
    T, D3 = p.shape
    D = D3 // 3
    tm = _div(T, 256, HALO)
    per = tm // HALO
    nsteps = T // tm
    last_halo = T // HALO - 1
    nxt = lambda i: (jnp.minimum((i + 1) * per, last_halo), 0)

    def body(p_ref, prev_ref, next_ref, dg_ref, dgn_ref, cw_ref, dp_ref, dcw_ref):
        i = pl.program_id(0)
        bg, cg, val, z, z1, z2, conv = _conv_parts(p_ref, prev_ref, cw_ref, i == 0)
        dgt = dg_ref[...].astype(F32)
        dconv = dgt * bg
        dcn = jnp.where(i == nsteps - 1, 0.0, dgn_ref[...].astype(F32) * next_ref[:, :D].astype(F32))
        n0 = _row_of(dcn, 0)
        n1 = _row_of(dcn, 1)
        up1 = _shift_up(dconv, 1, [n0])
        up2 = _shift_up(dconv, 2, [n0, n1])
        dz = dconv * cw_ref[2:3, :] + up1 * cw_ref[1:2, :] + up2 * cw_ref[0:1, :]
        dp_ref[:, :D] = (dgt * conv).astype(BF16)
        dp_ref[:, D:2 * D] = (dz * val).astype(BF16)
        dp_ref[:, 2 * D:] = (dz * cg).astype(BF16)
        first = i == 0
        parts = (jnp.sum(dconv * z2, axis=0, keepdims=True), jnp.sum(dconv * z1, axis=0, keepdims=True),
                 jnp.sum(dconv * z, axis=0, keepdims=True))

        @pl.when(first)
        def _():
            for k in range(CONV_WIDTH):
                dcw_ref[k:k + 1, :] = parts[k]

        @pl.when(jnp.logical_not(first))
        def _():
            for k in range(CONV_WIDTH):
                dcw_ref[k:k + 1, :] += parts[k]

    return _call(body, "conv_mid_bwd", (nsteps,),
                 [(p, (tm, D3), lambda i: (i, 0)),
                  (p, (HALO, D3), lambda i: (jnp.maximum(i * per - 1, 0), 0)),
                  (p, (HALO, D3), nxt),
                  (dgated, (tm, D), lambda i: (i, 0)),
                  (dgated, (HALO, D), nxt),
                  (cw, (CONV_WIDTH, D), lambda i: (0, 0))],
                 [((T, D3), BF16, (tm, D3), lambda i: (i, 0)),
                  ((CONV_WIDTH, D), F32, (CONV_WIDTH, D), lambda i: (0, 0))], exchange=exchange)


def mixer_bwd_common(d, dys, saved, norm_g, w_in, w_out, mid_bwd, scale_out, dwkv):
    x, hn, p, gated = saved
    T, D = x.shape
    E = gated.shape[1]
    w_out3 = w_out.reshape(1, E, D)
    dgated = mm_nt(dys, w_out3, "mix_dgated", BF16, T)
    dw_out = mm_tn(gated, dys, "mix_dwout", 1, D)
    send_wout = scatter_exchange([dw_out.reshape(N_DEV, E // N_DEV, D)])
    dp, extra = mid_bwd(p, dgated, send_wout)
    nb, _, ns = w_in.shape
    group = max(g for g in (1, 2, 4, 8) if nb % g == 0 and g * ns <= D)
    send_wkv = scatter_exchange([dwkv])
    dw_in = mm_tn(hn, dp, "mix_dwin", nb, ns, group=group, exchange=send_wkv)
    send_win = scatter_exchange([dw_in])
    dhn = mm_nt(dp, w_in, "mix_dhn", BF16, T, tm_pref=512, group=group, exchange=send_win)
    dx, dxs, dg = rms_bwd(x, norm_g, dhn, d, scale_out)
    return dx, dxs, dg, send_win.results[0], send_wout.results[0], send_wkv.results[0], extra


def _softmax_rows(s):
    e = jnp.exp(s - jnp.max(s, axis=-1, keepdims=True))
    return e / jnp.sum(e, axis=-1, keepdims=True)


def attn_fwd(q, kv):
    T, D = q.shape
    M = kv.shape[0]
    hd = D // HEADS
    scale = hd ** -0.5
    tm = _div(T, 512, BF16_SUBLANES)

    def body(q_ref, kv_ref, o_ref):
        for h in range(HEADS):
            cols = slice(h * hd, (h + 1) * hd)
            s = _dot(q_ref[:, cols], kv_ref[:, cols], _NT) * scale
            pr = _softmax_rows(s).astype(BF16)
            o_ref[:, cols] = _dot(pr, kv_ref[:, D + h * hd:D + (h + 1) * hd]).astype(BF16)

    return _call(body, "attn_fwd", (T // tm,),
                 [(q, (tm, D), lambda i: (i, 0)), (kv, (M, 2 * D), lambda i: (0, 0))],
                 [((T, D), BF16, (tm, D), lambda i: (i, 0))])[0]


def attn_bwd(q, do, kv, exchange=None):
    T, D = q.shape
    M = kv.shape[0]
    hd = D // HEADS
    scale = hd ** -0.5
    tm = _div(T, 512, BF16_SUBLANES)
    nsteps = T // tm

    def body(q_ref, do_ref, kv_ref, dq_ref, dkv_ref, acc_ref):
        i = pl.program_id(0)
        for h in range(HEADS):
            cols = slice(h * hd, (h + 1) * hd)
            vcols = slice(D + h * hd, D + (h + 1) * hd)
            qh = q_ref[:, cols]
            kh = kv_ref[:, cols]
            doh = do_ref[:, cols]
            pr = _softmax_rows(_dot(qh, kh, _NT) * scale)
            dpr = _dot(doh, kv_ref[:, vcols], _NT)
            ds = (pr * (dpr - jnp.sum(dpr * pr, axis=-1, keepdims=True)) * scale).astype(BF16)
            dq_ref[:, cols] = _dot(ds, kh).astype(BF16)
            dk = _dot(ds, qh, _TN)
            dv = _dot(pr.astype(BF16), doh, _TN)

            @pl.when(i == 0)
            def _():
                acc_ref[:, cols] = dk
                acc_ref[:, vcols] = dv

            @pl.when(i > 0)
            def _():
                acc_ref[:, cols] += dk
                acc_ref[:, vcols] += dv

        @pl.when(i == nsteps - 1)
        def _():
            dkv_ref[...] = acc_ref[...].astype(BF16)

    return _call(body, "attn_bwd", (nsteps,),
                 [(q, (tm, D), lambda i: (i, 0)), (do, (tm, D), lambda i: (i, 0)),
                  (kv, (M, 2 * D), lambda i: (0, 0))],
                 [((T, D), BF16, (tm, D), lambda i: (i, 0)), ((M, 2 * D), BF16, (M, 2 * D), lambda i: (0, 0))],
                 scratch=[pltpu.VMEM((M, 2 * D), F32)], exchange=exchange)


def xattn_fwd(x, mem, xnorm_g, mnorm_g, wq, wkv, wo):
    D = x.shape[1]
    hq = rms_fwd(x, xnorm_g)
    mn = rms_fwd(mem, mnorm_g)
    q = mm_nn(hq, wq.reshape(1, D, D), "xattn_q")
    kv = mm_nn(mn, wkv, "xattn_kv")
    o = attn_fwd(q, kv)
    y = mm_nn(o, wo.reshape(1, D, D), "xattn_out", F32, res=x)
    return y, (x, hq, mn, q, kv, o)


def xattn_bwd(d, dys, saved, mem, xnorm_g, wq, wkv, wo, scale_out):
    x, hq, mn, q, kv, o = saved
    T, D = x.shape
    M = mem.shape[0]
    do = mm_nt(dys, wo.reshape(1, D, D), "xattn_do", BF16, T)
    rows = D // N_DEV
    dwo = mm_tn(o, dys, "xattn_dwo", 1, D)
    send_wo = scatter_exchange([dwo.reshape(N_DEV, rows, D)])
    dq, dkv = attn_bwd(q, do, kv, exchange=send_wo)
    nb, _, ns = wkv.shape
    dwkv = mm_tn(mn, dkv, "xattn_dwkv", nb, ns)
    dmn = mm_nt(dkv, wkv, "xattn_dmn", F32, M)
    dgm = rms_gain_grad(mem, dmn)
    dwq = mm_tn(hq, dq, "xattn_dwq", 1, D)
    send_wq = scatter_exchange([dwq.reshape(N_DEV, rows, D)])
    dx, dxs, dgx = mm_nt_rms_bwd(dq, wq.reshape(1, D, D), "xattn_dhq", x, xnorm_g, d, scale_out,
                                 exchange=send_wq)
    return dx, dxs, dgx, dgm, send_wq.results[0], dwkv, send_wo.results[0]


def _mesh_places():
    x, y, c = lax.axis_index("x"), lax.axis_index("y"), lax.axis_index("c")
    chips = [(1 - x, y), (x, 1 - y), (1 - x, 1 - y)]
    return (x, y, c), (x, y, 1 - c), chips


def _slot(place):
    return 4 * place[0] + 2 * place[1] + place[2]


def _exchange_sems(n):
    return [pltpu.SemaphoreType.DMA((n * N_PEERS,)), pltpu.SemaphoreType.DMA((n * N_PEERS,)),
            pltpu.SemaphoreType.DMA((n,))]


def gather_exchange(shards):
    n = len(shards)
    shapes = [a.shape if l is None else a.shape[1:] for a, l in shards]

    def parts(x_in, x_out, sems):
        ins = [r if l is None else r.at[l] for r, (_, l) in zip(x_in, shards)]
        send_sems, recv_sems, local_sems = sems
        me, sibling, chips = _mesh_places()

        def copy(a, k, block, to, src=None):
            dst = x_out[a].at[_slot(block)]
            return pltpu.make_async_remote_copy(
                src_ref=dst if src is None else src, dst_ref=dst,
                send_sem=send_sems.at[a * N_PEERS + k], recv_sem=recv_sems.at[a * N_PEERS + k],
                device_id=to, device_id_type=pl.DeviceIdType.MESH)

        mine = [pltpu.make_async_copy(ins[a], x_out[a].at[_slot(me)], local_sems.at[a]) for a in range(n)]
        first = []
        for a in range(n):
            first.append(copy(a, 0, me, sibling, src=ins[a]))
            first += [copy(a, 1 + j, me, (*chip, me[2]), src=ins[a]) for j, chip in enumerate(chips)]
        return me, sibling, chips, copy, mine, first

    def start(x_in, x_out, sems):
        _, _, _, _, mine, first = parts(x_in, x_out, sems)
        for cp in mine + first:
            cp.start()

    def forward(x_in, x_out, sems):
        me, sibling, chips, copy, _, _ = parts(x_in, x_out, sems)
        for j, chip in enumerate(chips):
            for a in range(n):
                copy(a, 1 + j, (*chip, me[2]), me).wait_recv()
                copy(a, 4 + j, (*chip, me[2]), sibling).start()

    def finish(x_in, x_out, sems):
        me, sibling, chips, copy, mine, first = parts(x_in, x_out, sems)
        for a in range(n):
            copy(a, 0, sibling, me).wait_recv()
        for j, chip in enumerate(chips):
            for a in range(n):
                copy(a, 4 + j, (*chip, 1 - me[2]), me).wait_recv()
        for cp in first:
            cp.wait_send()
        for j, chip in enumerate(chips):
            for a in range(n):
                copy(a, 4 + j, (*chip, me[2]), sibling).wait_send()
        for cp in mine:
            cp.wait()

    return Exchange([a for a, _ in shards],
                    [jax.ShapeDtypeStruct((N_DEV,) + tuple(s), a.dtype) for s, (a, _) in zip(shapes, shards)],
                    _exchange_sems(n), start, finish, forward)


def _all_peers(me, chips):
    c = me[2]
    return [(me[0], me[1], 1 - c)] + [(*chip, c) for chip in chips] + [(*chip, 1 - c) for chip in chips]


def scatter_exchange(grads):
    n = len(grads)

    def parts(x_in, x_out, sems):
        send_sems, recv_sems, local_sems = sems
        me, _, chips = _mesh_places()
        mine = [pltpu.make_async_copy(x_in[a].at[_slot(me)], x_out[a].at[_slot(me)], local_sems.at[a])
                for a in range(n)]
        sends, recvs = [], []
        for a in range(n):
            for k, peer in enumerate(_all_peers(me, chips)):
                sem = dict(send_sem=send_sems.at[a * N_PEERS + k], recv_sem=recv_sems.at[a * N_PEERS + k],
                           device_id=peer, device_id_type=pl.DeviceIdType.MESH)
                sends.append(pltpu.make_async_remote_copy(
                    src_ref=x_in[a].at[_slot(peer)], dst_ref=x_out[a].at[_slot(me)], **sem))
                recvs.append(pltpu.make_async_remote_copy(
                    src_ref=x_in[a].at[_slot(peer)], dst_ref=x_out[a].at[_slot(peer)], **sem))
        return mine, sends, recvs

    def start(x_in, x_out, sems):
        mine, sends, _ = parts(x_in, x_out, sems)
        for cp in mine + sends:
            cp.start()

    def finish(x_in, x_out, sems):
        mine, sends, recvs = parts(x_in, x_out, sems)
        for cp in recvs:
            cp.wait_recv()
        for cp in sends:
            cp.wait_send()
        for cp in mine:
            cp.wait()

    return Exchange(grads, [jax.ShapeDtypeStruct(g.shape, g.dtype) for g in grads],
                    _exchange_sems(n), start, finish)


def small_all_reduce(vec):
    R = vec.shape[0]

    def body(v_ref, o_ref, all_ref, send_sems, recv_sems):
        me, _, chips = _mesh_places()
        peers = _all_peers(me, chips)
        all_ref[_slot(me)] = v_ref[...]
        sends, recvs = [], []
        for k, peer in enumerate(peers):
            sem = dict(send_sem=send_sems.at[k], recv_sem=recv_sems.at[k],
                       device_id=peer, device_id_type=pl.DeviceIdType.MESH)
            sends.append(pltpu.make_async_remote_copy(src_ref=v_ref, dst_ref=all_ref.at[_slot(me)], **sem))
            recvs.append(pltpu.make_async_remote_copy(src_ref=v_ref, dst_ref=all_ref.at[_slot(peer)], **sem))
        for cp in sends:
            cp.start()
        for cp in recvs:
            cp.wait_recv()
        for cp in sends:
            cp.wait_send()
        acc = all_ref[0]
        for s in range(1, N_DEV):
            acc = acc + all_ref[s]
        o_ref[...] = acc

    return pl.pallas_call(
        body, name="small_all_reduce",
        out_shape=jax.ShapeDtypeStruct(vec.shape, F32),
        in_specs=[pl.BlockSpec(memory_space=pltpu.VMEM)], out_specs=pl.BlockSpec(memory_space=pltpu.VMEM),
        scratch_shapes=[pltpu.VMEM((N_DEV, R, LANES), F32), pltpu.SemaphoreType.DMA((N_PEERS,)),
                        pltpu.SemaphoreType.DMA((N_PEERS,))],
    )(vec)


def _adamw_math(w, g, m, v):
    m2 = ADAM_B1 * m + (1.0 - ADAM_B1) * g
    v2 = ADAM_B2 * v + (1.0 - ADAM_B2) * (g * g)
    m_hat = m2 / (1.0 - ADAM_B1 ** ADAM_STEP)
    v_hat = v2 / (1.0 - ADAM_B2 ** ADAM_STEP)
    delta = -ADAM_LR * (m_hat / (jnp.sqrt(v_hat) + ADAM_EPS) + ADAM_WD * w)
    return delta, m2, v2


def adamw_sharded(partials, w, m, v):
    L, r, c = w.shape
    row_bytes = 2 * c * (L * N_DEV * 2 + 7 * 4)
    tr = _div(r, max(BF16_SUBLANES, min(512, (VMEM_LIMIT_BYTES * 3 // 4) // row_bytes)), BF16_SUBLANES)
    nt = r // tr

    def part_map(l0):
        return lambda l, t: (0, jnp.where(l == l0, t, jnp.where(l < l0, 0, nt - 1)), 0)

    def body(*refs):
        parts = refs[:L]
        w_ref, m_ref, v_ref, g_out, d_out, m_out, v_out = refs[L:]
        layer = pl.program_id(0)
        for l0 in range(L):
            @pl.when(layer == l0)
            def _():
                g = parts[l0][0].astype(F32)
                for s in range(1, N_DEV):
                    g = g + parts[l0][s].astype(F32)
                delta, m2, v2 = _adamw_math(w_ref[...], g, m_ref[...], v_ref[...])
                g_out[...] = g
                d_out[...] = delta
                m_out[...] = m2
                v_out[...] = v2

    own = lambda l, t: (l, t, 0)
    return _call(body, "adamw_sharded", (L, nt),
                 [(p, (N_DEV, tr, c), part_map(l0)) for l0, p in enumerate(partials)]
                 + [(w, (None, tr, c), own), (m, (None, tr, c), own), (v, (None, tr, c), own)],
                 [((L, r, c), F32, (None, tr, c), own)] * 4)


def adamw_flat(g, w, m, v):
    shape = g.shape

    def body(g_ref, w_ref, m_ref, v_ref, d_out, m_out, v_out):
        delta, m2, v2 = _adamw_math(w_ref[...], g_ref[...], m_ref[...], v_ref[...])
        d_out[...] = delta
        m_out[...] = m2
        v_out[...] = v2

    whole = lambda: (0, 0)
    return _call(body, "adamw_flat", (), [(a, shape, whole) for a in (g, w, m, v)],
                 [(shape, F32, shape, whole)] * 3)


def _pack(parts):
    flat = jnp.concatenate([p.reshape(-1).astype(F32) for p in parts])
    rows = -(-flat.shape[0] // (8 * LANES)) * 8
    return jnp.pad(flat, (0, rows * LANES - flat.shape[0])).reshape(rows, LANES)


def _unpack(packed, shapes):
    flat = packed.reshape(-1)
    out, off = [], 0
    for s in shapes:
        size = math.prod(s)
        out.append(flat[off:off + size].reshape(s))
        off += size
    return out


def kernel(x, mem, ffn1_norm, ffn1_w13, ffn1_w2, mix_norm, gmlp_w_in, gmlp_ln_g, gmlp_ln_b, gmlp_w_s, gmlp_b_s, gmlp_w_out, conv_w_in, conv_w, conv_w_out, xattn_norm, mem_norm, xattn_wq, xattn_wkv, xattn_wo, ffn2_norm, ffn2_w13, ffn2_w2, final_norm, loss_target, m_ffn1_norm, m_ffn1_w13, m_ffn1_w2, m_mix_norm, m_gmlp_w_in, m_gmlp_ln_g, m_gmlp_ln_b, m_gmlp_w_s, m_gmlp_b_s, m_gmlp_w_out, m_conv_w_in, m_conv_w, m_conv_w_out, m_xattn_norm, m_mem_norm, m_xattn_wq, m_xattn_wkv, m_xattn_wo, m_ffn2_norm, m_ffn2_w13, m_ffn2_w2, m_final_norm, v_ffn1_norm, v_ffn1_w13, v_ffn1_w2, v_mix_norm, v_gmlp_w_in, v_gmlp_ln_g, v_gmlp_ln_b, v_gmlp_w_s, v_gmlp_b_s, v_gmlp_w_out, v_conv_w_in, v_conv_w, v_conv_w_out, v_xattn_norm, v_mem_norm, v_xattn_wq, v_xattn_wkv, v_xattn_wo, v_ffn2_norm, v_ffn2_w13, v_ffn2_w2, v_final_norm):
    given = dict(locals())
    T, D = x.shape[1], x.shape[2]
    depth = ffn1_norm.shape[0]
    xs = x.reshape(T, D)
    mems = mem.reshape(mem.shape[1], D)
    target = loss_target.reshape(T, D)
    me = 4 * lax.axis_index("x") + 2 * lax.axis_index("y") + lax.axis_index("c")
    E = gmlp_ln_g.shape[1]
    gd = E // GROUPS
    cshard = conv_w.shape[2]

    bf = {k: given[k].astype(BF16) for k in
          ("ffn1_w13", "ffn1_w2", "gmlp_w_in", "gmlp_w_out", "conv_w_in", "conv_w_out",
           "xattn_wq", "xattn_wkv", "xattn_wo", "ffn2_w13", "ffn2_w2")}

    W = {}

    def gather(names_layers):
        return names_layers, gather_exchange([(bf[k], l) for k, l in names_layers])

    def landed(tagged):
        names_layers, exchange = tagged
        W.update(zip(names_layers, exchange.results))

    bias_full = jnp.repeat(gmlp_b_s[0].T, gd, axis=1)

    first = gather_exchange([(bf["ffn1_w13"], 0), (conv_w, 0)])
    taps = {}

    def first_w13():
        W["ffn1_w13", 0], cw_shards = first.results
        taps["conv"] = jnp.transpose(cw_shards, (1, 0, 2)).reshape(CONV_WIDTH, D)
        return W["ffn1_w13", 0]

    saved = []
    h = xs
    for i in range(depth):
        j = i // 2
        is_gmlp = i % 2 == 0
        mix = ("gmlp_w_in", "gmlp_w_out") if is_gmlp else ("conv_w_in", "conv_w_out")
        on_up1 = gather(([("ffn1_w2", i)] if i == 0 else []) + [(mix[0], j), ("ffn2_w13", i)])
        on_down1 = gather([(mix[1], j), ("xattn_wq", i), ("xattn_wkv", i)])

        def w2_after_up(on_up=on_up1, i=i):
            landed(on_up)
            return W["ffn1_w2", i]

        h, sv1 = ffn_fwd(h, ffn1_norm[i], first_w13 if i == 0 else (lambda i=i: W["ffn1_w13", i]), w2_after_up,
                         first if i == 0 else None, on_up1[1], on_down1[1])
        landed(on_down1)
        hn = rms_fwd(h, mix_norm[i])
        on_mix_in = gather([("xattn_wo", i), ("ffn2_w2", i)])
        p = mm_nn(hn, W[mix[0], j], "mix_in", group=2, exchange=on_mix_in[1])
        landed(on_mix_in)
        if is_gmlp:
            gated = gmlp_mid_fwd(p, gmlp_ln_g[j:j + 1], gmlp_ln_b[j:j + 1], gmlp_w_s[j], bias_full)
        else:
            gated = conv_mid_fwd(p, taps["conv"])
        h_mix = mm_nn(gated, W[mix[1], j].reshape(1, gated.shape[1], D), "mix_out", F32, res=h)
        sv2 = (h, hn, p, gated)
        h, sv3 = xattn_fwd(h_mix, mems, xattn_norm[i], mem_norm[i],
                           W["xattn_wq", i], W["xattn_wkv", i], W["xattn_wo", i])
        on_up2 = gather([("ffn1_w13", i + 1), ("ffn1_w2", i + 1)]) if i + 1 < depth else None
        h, sv4 = ffn_fwd(h, ffn2_norm[i], lambda i=i: W["ffn2_w13", i], lambda i=i: W["ffn2_w2", i],
                         up_exchange=None if on_up2 is None else on_up2[1])
        if on_up2 is not None:
            landed(on_up2)
        saved.append((sv1, sv2, sv3, sv4))

    loss_part, d, dys, d_final_norm = loss_head(h, final_norm, target, 0.5)

    small = {k: [None] * depth for k in ("ffn1_norm", "mix_norm", "xattn_norm", "mem_norm", "ffn2_norm")}
    partial = {}
    for i in reversed(range(depth)):
        j = i // 2
        is_gmlp = i % 2 == 0
        sv1, sv2, sv3, sv4 = saved[i]
        d, dys, small["ffn2_norm"][i], partial["ffn2_w13", i], partial["ffn2_w2", i] = ffn_bwd(
            d, dys, sv4, ffn2_norm[i], W["ffn2_w13", i], W["ffn2_w2", i], 1.0)
        (d, dys, small["xattn_norm"][i], small["mem_norm"][i], partial["xattn_wq", i],
         dwkv, partial["xattn_wo", i]) = xattn_bwd(
            d, dys, sv3, mems, xattn_norm[i], W["xattn_wq", i], W["xattn_wkv", i], W["xattn_wo", i], 1.0)
        if is_gmlp:
            mix = ("gmlp_w_in", "gmlp_w_out")
            mid = lambda p, dg, send: (lambda r: (r[0], r[1:]))(gmlp_mid_bwd(
                p, dg, gmlp_ln_g[j:j + 1], gmlp_ln_b[j:j + 1], gmlp_w_s[j], bias_full, exchange=send))
        else:
            mix = ("conv_w_in", "conv_w_out")
            mid = lambda p, dg, send: (lambda r: (r[0], r[1:]))(conv_mid_bwd(p, dg, taps["conv"], exchange=send))
        (d, dys, small["mix_norm"][i], partial[mix[0], j], partial[mix[1], j], partial["xattn_wkv", i],
         extra) = mixer_bwd_common(d, dys, sv2, mix_norm[i], W[mix[0], j], W[mix[1], j], mid, 0.5, dwkv)
        if is_gmlp:
            d_ws, d_bs_wide, d_lng, d_lnb = extra
        else:
            (d_cw,) = extra
        d, dys, small["ffn1_norm"][i], partial["ffn1_w13", i], partial["ffn1_w2", i] = ffn_bwd(
            d, dys, sv1, ffn1_norm[i], W["ffn1_w13", i], W["ffn1_w2", i], 0.5)
    grad_x = d.reshape(x.shape)

    small_grads = {k: jnp.concatenate(v, axis=0) for k, v in small.items()}
    small_grads["gmlp_ln_g"] = d_lng
    small_grads["gmlp_ln_b"] = d_lnb
    small_grads["gmlp_w_s"] = d_ws[None]
    small_grads["gmlp_b_s"] = d_bs_wide[None, :, :, 0]
    small_grads["final_norm"] = d_final_norm.reshape(-1)
    small_names = ["ffn1_norm", "mix_norm", "gmlp_ln_g", "gmlp_ln_b", "gmlp_w_s", "gmlp_b_s",
                   "xattn_norm", "mem_norm", "ffn2_norm", "final_norm"]
    summed = small_all_reduce(_pack([small_grads[k] for k in small_names] + [d_cw, loss_part]))
    parts = _unpack(summed, [given[k].shape for k in small_names] + [(CONV_WIDTH, D), (1, LANES)])
    grads = dict(zip(small_names, parts[:len(small_names)]))
    grads["conv_w"] = lax.dynamic_slice(parts[-2], (jnp.int32(0), me * cshard), (CONV_WIDTH, cshard))[None]
    loss = parts[-1][0, 0]
    flat_names = small_names + ["conv_w"]
    flat = adamw_flat(*[_pack([src[k] for k in flat_names]) for src in
                        (grads, given, {k: given["m_" + k] for k in flat_names},
                         {k: given["v_" + k] for k in flat_names})])
    delta, new_m, new_v = [dict(zip(flat_names, _unpack(f, [given[k].shape for k in flat_names]))) for f in flat]

    for k in bf:
        w = given[k]
        L = w.shape[0]
        shard = w.shape[1:]
        view = lambda a: a.reshape((L,) + shard)
        g, dl, m2, v2 = adamw_sharded([partial[k, l] for l in range(L)], w, given["m_" + k], given["v_" + k])
        grads[k], delta[k], new_m[k], new_v[k] = view(g), view(dl), view(m2), view(v2)

    order = ["ffn1_norm", "ffn1_w13", "ffn1_w2", "mix_norm", "gmlp_w_in", "gmlp_ln_g", "gmlp_ln_b", "gmlp_w_s",
             "gmlp_b_s", "gmlp_w_out", "conv_w_in", "conv_w", "conv_w_out", "xattn_norm", "mem_norm", "xattn_wq",
             "xattn_wkv", "xattn_wo", "ffn2_norm", "ffn2_w13", "ffn2_w2", "final_norm"]
    return (loss, grad_x, *[grads[k] for k in order], *[delta[k] for k in order],
            *[new_m[k] for k in order], *[new_v[k] for k in order])
```

```python
import math

import jax
import jax.numpy as jnp
from jax import lax
from jax.experimental import pallas as pl
from jax.experimental.pallas import tpu as pltpu

F32 = jnp.float32
BF16 = jnp.bfloat16

N_DEV = 8
N_PEERS = N_DEV - 1
CHUNK = 128
GROUPS = 8
HEADS = 4
CONV_WIDTH = 3
RMS_EPS = 1e-6
LN_EPS = 1e-5
ADAM_LR = 0.001
ADAM_B1 = 0.9
ADAM_B2 = 0.999
ADAM_EPS = 1e-08
ADAM_WD = 0.01
ADAM_STEP = 10
LANES = 128
BF16_SUBLANES = 16
MXU_WIDTH = 256
VMEM_LIMIT_BYTES = 56 * 1024 * 1024

_NT = (((1,), (1,)), ((), ()))
_TN = (((0,), (0,)), ((), ()))
_SQRT_HALF = 0.7071067811865476
_INV_SQRT_2PI = 0.3989422804014327


def _div(n, pref, align):
    best = None
    for t in range(align, min(n, pref) + 1, align):
        if n % t == 0:
            best = t
    return n if best is None else best


def _chunks(n, width):
    return [(c0, min(width, n - c0)) for c0 in range(0, n, width)]


_ANY = pl.BlockSpec(memory_space=pl.ANY)


class Exchange:
    def __init__(self, arrays, out_shapes, sems, start, finish, forward=None):
        self.arrays, self.out_shapes, self.sems = list(arrays), list(out_shapes), list(sems)
        self.start, self.finish, self.forward = start, finish, forward
        self.results = None


def _call(body, name, grid, ins, outs, scratch=(), exchange=None):
    in_specs = [pl.BlockSpec(*spec[1:3], **({"pipeline_mode": pl.Buffered(1)} if len(spec) > 3 else {}))
                for spec in ins]
    ins = [spec[:3] for spec in ins]
    out_specs = [pl.BlockSpec(bs, im) for _, _, bs, im in outs]
    out_shape = [jax.ShapeDtypeStruct(s, d) for s, d, _, _ in outs]
    arrays = [a for a, _, _ in ins]
    scratch = list(scratch)
    kernel_fn = body
    if exchange is not None:
        n_in, n_out, n_scr = len(ins), len(outs), len(scratch)
        n_xin, n_xout = len(exchange.arrays), len(exchange.out_shapes)
        steps = math.prod(grid)
        forward_step = min(steps - 1, (15 * steps) // 16)

        def kernel_fn(*refs):
            refs = list(refs)
            own_in, x_in = refs[:n_in], refs[n_in:n_in + n_xin]
            refs = refs[n_in + n_xin:]
            own_out, x_out = refs[:n_out], refs[n_out:n_out + n_xout]
            refs = refs[n_out + n_xout:]
            own_scr, x_sems = refs[:n_scr], refs[n_scr:]
            step = 0
            for axis, size in enumerate(grid):
                step = step * size + pl.program_id(axis)

            @pl.when(step == 0)
            def _():
                exchange.start(x_in, x_out, x_sems)

            if exchange.forward is not None:
                @pl.when(step == forward_step)
                def _():
                    exchange.forward(x_in, x_out, x_sems)

            body(*own_in, *own_out, *own_scr)

            @pl.when(step == steps - 1)
            def _():
                exchange.finish(x_in, x_out, x_sems)

        in_specs += [_ANY] * n_xin
        out_specs += [_ANY] * n_xout
        out_shape += exchange.out_shapes
        arrays += exchange.arrays
        scratch += exchange.sems
    res = pl.pallas_call(
        kernel_fn,
        name=name,
        grid=grid,
        in_specs=in_specs,
        out_specs=out_specs,
        out_shape=out_shape,
        scratch_shapes=scratch,
        compiler_params=pltpu.CompilerParams(
            dimension_semantics=("arbitrary",) * len(grid), vmem_limit_bytes=VMEM_LIMIT_BYTES),
    )(*arrays)
    if exchange is not None:
        exchange.results = list(res[len(outs):])
        res = res[:len(outs)]
    return res


def _dot(a, b, dims=None):
    if dims is None:
        return jnp.dot(a, b, preferred_element_type=F32)
    return lax.dot_general(a, b, dims, preferred_element_type=F32)


def _sigmoid(v):
    return 1.0 / (1.0 + jnp.exp(-v))


def _normal_cdf(v):
    return 0.5 * (1.0 + lax.erf(v * _SQRT_HALF))


def _normal_pdf(v):
    return _INV_SQRT_2PI * jnp.exp(-0.5 * v * v)


def _accumulate_product(ref, first, product, skip=False):
    @pl.when(first)
    def _():
        ref[...] = product()

    @pl.when(jnp.logical_not(jnp.logical_or(first, skip)))
    def _():
        ref[...] += product()


def _accumulate(ref, part, first):
    @pl.when(first)
    def _():
        ref[...] = part

    @pl.when(jnp.logical_not(first))
    def _():
        ref[...] += part


def rms_fwd(x, g, exchange=None):
    T, D = x.shape
    tt = _div(T, 512, BF16_SUBLANES)

    def body(x_ref, g_ref, o_ref):
        xv = x_ref[...]
        r = lax.rsqrt(jnp.mean(xv * xv, axis=-1, keepdims=True) + RMS_EPS)
        o_ref[...] = ((xv * r) * g_ref[...]).astype(BF16)

    return _call(body, "rms_fwd", (T // tt,),
                 [(x, (tt, D), lambda i: (i, 0)), (g.reshape(1, D), (1, D), lambda i: (0, 0))],
                 [((T, D), BF16, (tt, D), lambda i: (i, 0))], exchange=exchange)[0]


def rms_bwd(x, g, dxn, d, scale):
    T, D = x.shape
    tt = _div(T, 512, BF16_SUBLANES)

    def body(x_ref, g_ref, dn_ref, d_ref, dx_ref, dxs_ref, dg_ref):
        xv = x_ref[...]
        r = lax.rsqrt(jnp.mean(xv * xv, axis=-1, keepdims=True) + RMS_EPS)
        xh = xv * r
        dn = dn_ref[...].astype(F32)
        dxh = dn * g_ref[...]
        dx = r * (dxh - xh * jnp.mean(dxh * xh, axis=-1, keepdims=True)) + d_ref[...]
        dx_ref[...] = dx
        dxs_ref[...] = (scale * dx).astype(BF16)
        _accumulate(dg_ref, jnp.sum(dn * xh, axis=0, keepdims=True), pl.program_id(0) == 0)

    row = lambda i: (i, 0)
    fix = lambda i: (0, 0)
    return _call(body, "rms_bwd", (T // tt,),
                 [(x, (tt, D), row), (g.reshape(1, D), (1, D), fix), (dxn, (tt, D), row), (d, (tt, D), row)],
                 [((T, D), F32, (tt, D), row), ((T, D), BF16, (tt, D), row), ((1, D), F32, (1, D), fix)])


def rms_gain_grad(x, dxn):
    T, D = x.shape
    tt = _div(T, 256, 8)

    def body(x_ref, dn_ref, dg_ref):
        xv = x_ref[...]
        r = lax.rsqrt(jnp.mean(xv * xv, axis=-1, keepdims=True) + RMS_EPS)
        _accumulate(dg_ref, jnp.sum(dn_ref[...] * (xv * r), axis=0, keepdims=True), pl.program_id(0) == 0)

    row = lambda i: (i, 0)
    return _call(body, "rms_gain_grad", (T // tt,), [(x, (tt, D), row), (dxn, (tt, D), row)],
                 [((1, D), F32, (1, D), lambda i: (0, 0))])[0]


def loss_head(x, g, target, scale):
    T, D = x.shape
    tt = _div(T, 256, BF16_SUBLANES)

    def body(x_ref, g_ref, t_ref, loss_ref, dx_ref, dxs_ref, dg_ref):
        first = pl.program_id(0) == 0
        xv = x_ref[...]
        gv = g_ref[...]
        r = lax.rsqrt(jnp.mean(xv * xv, axis=-1, keepdims=True) + RMS_EPS)
        xh = xv * r
        err = xh * gv - t_ref[...]
        part = 0.5 * jnp.sum(jnp.mean(err * err, axis=-1, keepdims=True), axis=0, keepdims=True)
        _accumulate(loss_ref, jnp.broadcast_to(part, (1, LANES)), first)
        dy = err * (1.0 / D)
        dxh = dy * gv
        dx = r * (dxh - xh * jnp.mean(dxh * xh, axis=-1, keepdims=True))
        dx_ref[...] = dx
        dxs_ref[...] = (scale * dx).astype(BF16)
        _accumulate(dg_ref, jnp.sum(dy * xh, axis=0, keepdims=True), first)

    row = lambda i: (i, 0)
    fix = lambda i: (0, 0)
    return _call(body, "loss_head", (T // tt,),
                 [(x, (tt, D), row), (g.reshape(1, D), (1, D), fix), (target, (tt, D), row)],
                 [((1, LANES), F32, (1, LANES), fix), ((T, D), F32, (tt, D), row),
                  ((T, D), BF16, (tt, D), row), ((1, D), F32, (1, D), fix)])


def mm_nn(a, w3, name, out_dtype=BF16, res=None, res_scale=1.0, tm_pref=1024, tn_pref=1024, group=1,
          exchange=None):
    M, K = a.shape
    nb, _, ns = w3.shape
    tn = ns if group > 1 else _div(ns, tn_pref, LANES)
    per = ns // tn
    tm = _div(M, tm_pref, BF16_SUBLANES)
    wide = group * tn
    ins = [(a, (tm, K), lambda j, m: (m, 0)), (w3, (group, K, tn), lambda j, m: (j // per, 0, j % per))]
    if res is not None:
        ins.append((res, (tm, wide), lambda j, m: (m, j)))

    def body(*refs):
        a_ref, w_ref = refs[0], refs[1]
        o_ref = refs[-1]
        for k in range(group):
            cols = slice(k * tn, (k + 1) * tn)
            acc = _dot(a_ref[...], w_ref[k])
            if res is not None:
                acc = refs[2][:, cols] + res_scale * acc
            o_ref[:, cols] = acc.astype(o_ref.dtype)

    return _call(body, name, (nb * per // group, M // tm), ins,
                 [((M, nb * ns), out_dtype, (tm, wide), lambda j, m: (m, j))], exchange=exchange)[0]


def mm_nt(a_in, w3, name, out_dtype, M, tm_pref=1024, to_pref=2048, group=1, exchange=None):
    nb, Ko, ns = w3.shape
    to = _div(Ko, to_pref, LANES)
    tm = _div(M, tm_pref, BF16_SUBLANES)
    if isinstance(a_in, tuple):
        a, a_bs, a_im = a_in
        a_bs = tuple(tm if s == "tm" else s for s in a_bs)
    else:
        a, a_bs, a_im = a_in, (tm, group * ns), lambda m, o, b: (m, b)
    steps = nb // group
    narrow_out = steps > 1 and out_dtype != F32

    def product(a_ref, w_ref):
        p = None
        for k in range(group):
            pk = _dot(a_ref[:, k * ns:(k + 1) * ns], w_ref[k], _NT)
            p = pk if p is None else p + pk
        return p

    def body(a_ref, w_ref, o_ref, *acc):
        b = pl.program_id(2)
        if steps == 1:
            o_ref[...] = product(a_ref, w_ref).astype(o_ref.dtype)
        elif not narrow_out:
            _accumulate_product(o_ref, b == 0, lambda: product(a_ref, w_ref))
        else:
            acc_ref, = acc
            _accumulate_product(acc_ref, b == 0, lambda: product(a_ref, w_ref), skip=b == steps - 1)

            @pl.when(b == steps - 1)
            def _():
                o_ref[...] = (acc_ref[...] + product(a_ref, w_ref)).astype(o_ref.dtype)

    return _call(body, name, (M // tm, Ko // to, steps),
                 [(a, a_bs, a_im), (w3, (group, to, ns), lambda m, o, b: (b, o, 0))],
                 [((M, Ko), out_dtype, (tm, to), lambda m, o, b: (m, o))],
                 scratch=[pltpu.VMEM((tm, to), F32)] if narrow_out else [], exchange=exchange)[0]


def mm_nt_rms_bwd(a, w3, name, x, g, d, scale, exchange=None):
    nb, D, ns = w3.shape
    M = x.shape[0]
    tm = _div(M, 512, BF16_SUBLANES)
    a_bs, a_im = (tm, ns), lambda m, b: (m, b)

    rc = _div(tm, 64, BF16_SUBLANES)

    def body(a_ref, w_ref, x_ref, g_ref, d_ref, dx_ref, dxs_ref, dg_ref, acc_ref):
        m, b = pl.program_id(0), pl.program_id(1)
        _accumulate_product(acc_ref, b == 0, lambda: _dot(a_ref[...], w_ref[...], _NT))

        @pl.when(b == nb - 1)
        def _():
            gv = g_ref[...]

            def piece(c, dg):
                rows = pl.ds(pl.multiple_of(c * rc, rc), rc)
                dn = acc_ref[rows, :]
                xv = x_ref[rows, :]
                r = lax.rsqrt(jnp.mean(xv * xv, axis=-1, keepdims=True) + RMS_EPS)
                xh = xv * r
                dxh = dn * gv
                dx = r * (dxh - xh * jnp.mean(dxh * xh, axis=-1, keepdims=True)) + d_ref[rows, :]
                dx_ref[rows, :] = dx
                dxs_ref[rows, :] = (scale * dx).astype(BF16)
                return dg + jnp.sum(dn * xh, axis=0, keepdims=True)

            _accumulate(dg_ref, lax.fori_loop(0, tm // rc, piece, jnp.zeros((1, D), F32)), m == 0)

    row = lambda m, b: (m, 0)
    fix = lambda m, b: (0, 0)
    w_spec = (w3, (None, D, ns), lambda m, b: (b, 0, 0)) + (("single",) if nb == 1 else ())
    return _call(body, name, (M // tm, nb),
                 [(a, a_bs, a_im), w_spec,
                  (x, (tm, D), row), (g.reshape(1, D), (1, D), fix), (d, (tm, D), row)],
                 [((M, D), F32, (tm, D), row), ((M, D), BF16, (tm, D), row), ((1, D), F32, (1, D), fix)],
                 scratch=[pltpu.VMEM((tm, D), F32)], exchange=exchange)


def mm_tn(a, b_in, name, nbo, ns, tka_pref=1024, tt_pref=2048, tn_pref=2048, group=1, exchange=None):
    T, Ka = a.shape
    tt = _div(T, tt_pref, BF16_SUBLANES)
    tka = _div(Ka, tka_pref, LANES)
    if isinstance(b_in, tuple):
        b, b_bs, b_im = b_in
        b_bs = tuple(tt if s == "tt" else s for s in b_bs)
        tn, per = ns, 1
    else:
        tn = ns if group > 1 else _div(ns, tn_pref, LANES)
        per = ns // tn
        b, b_bs, b_im = b_in, (tt, group * tn), lambda i, j, t: (t, j)
    nt = T // tt

    def body(a_ref, b_ref, o_ref, acc_ref):
        t = pl.program_id(2)
        product = lambda: _dot(a_ref[...], b_ref[...], _TN)

        def store(total):
            for k in range(group):
                o_ref[k] = total[:, k * tn:(k + 1) * tn].astype(BF16)

        if nt == 1:
            store(product())
        else:
            _accumulate_product(acc_ref, t == 0, product, skip=t == nt - 1)

            @pl.when(t == nt - 1)
            def _():
                store(acc_ref[...] + product())

    return _call(body, name, (Ka // tka, nbo * per // group, nt),
                 [(a, (tt, tka), lambda i, j, t: (t, i)), (b, b_bs, b_im)],
                 [((nbo, Ka, ns), BF16, (group, tka, tn), lambda i, j, t: (j // per, i, j % per))],
                 scratch=[pltpu.VMEM((tka, group * tn), F32)], exchange=exchange)[0]


def ffn_up(xn, w13, exchange=None):
    T, D = xn.shape
    nb, _, ns = w13.shape
    half = nb // 2
    F = half * ns
    tm = _div(T, 512, BF16_SUBLANES)
    pair = 2 if half % 2 == 0 else 1

    def columns(w_ref, c0, cw):
        k, off = divmod(c0, ns)
        if off + cw <= ns:
            return w_ref[k, :, off:off + cw]
        return jnp.concatenate([w_ref[k, :, off:ns], w_ref[k + 1, :, 0:off + cw - ns]], axis=1)

    def body(x_ref, wg_ref, wu_ref, fac_ref, act_ref):
        xv = x_ref[...]
        for c0, cw in _chunks(pair * ns, MXU_WIDTH):
            cols = slice(c0, c0 + cw)
            gate = _dot(xv, columns(wg_ref, c0, cw))
            up = _dot(xv, columns(wu_ref, c0, cw))
            s = _sigmoid(gate)
            silu = gate * s
            fac_ref[0, :, cols] = (up * (s * (1.0 + gate * (1.0 - s)))).astype(BF16)
            fac_ref[1, :, cols] = silu.astype(BF16)
            act_ref[:, cols] = (silu * up).astype(BF16)

    tn = pair * ns
    return _call(body, "ffn_up", (half // pair, T // tm),
                 [(xn, (tm, D), lambda j, m: (m, 0)),
                  (w13, (pair, D, ns), lambda j, m: (j, 0, 0), "single"),
                  (w13, (pair, D, ns), lambda j, m: (j + half // pair, 0, 0), "single")],
                 [((2, T, F), BF16, (2, tm, tn), lambda j, m: (0, m, j)),
                  ((T, F), BF16, (tm, tn), lambda j, m: (m, j))], exchange=exchange)


def ffn_dact(dy, w2, fac, exchange=None):
    T, D = dy.shape
    F = w2.shape[0]
    tm = _div(T, 512, BF16_SUBLANES)
    tn = _div(F, F // 2, MXU_WIDTH)

    def body(dy_ref, w_ref, fac_ref, dh_ref):
        dyv = dy_ref[...]
        for c0, cw in _chunks(tn, MXU_WIDTH):
            cols = slice(c0, c0 + cw)
            da = _dot(dyv, w_ref[cols, :], _NT)
            dh_ref[0, :, cols] = (da * fac_ref[0, :, cols].astype(F32)).astype(BF16)
            dh_ref[1, :, cols] = (da * fac_ref[1, :, cols].astype(F32)).astype(BF16)

    return _call(body, "ffn_dact", (F // tn, T // tm),
                 [(dy, (tm, D), lambda j, m: (m, 0)), (w2, (tn, D), lambda j, m: (j, 0), "single"),
                  (fac, (2, tm, tn), lambda j, m: (0, m, j))],
                 [((2, T, F), BF16, (2, tm, tn), lambda j, m: (0, m, j))], exchange=exchange)[0]


def ffn_fwd(x, norm_g, get_w13, get_w2, norm_exchange=None, up_exchange=None, down_exchange=None):
    xn = rms_fwd(x, norm_g, exchange=norm_exchange)
    fac, act = ffn_up(xn, get_w13(), exchange=up_exchange)
    F = act.shape[1]
    y = mm_nn(act, get_w2().reshape(1, F, -1), "ffn_down", F32, res=x, res_scale=0.5, tm_pref=512,
              exchange=down_exchange)
    return y, (x, xn, fac, act)


def ffn_bwd(d, dys, saved, norm_g, w13, w2, scale_out):
    x, xn, fac, act = saved
    T, D = x.shape
    nb, _, ns = w13.shape
    half = nb // 2
    F = half * ns
    dh = ffn_dact(dys, w2.reshape(F, D), fac)
    dw2 = mm_tn(act, dys, "ffn_dw2", 1, D, tka_pref=ns, tn_pref=1024)
    send_w2 = scatter_exchange([dw2.reshape(N_DEV, F // N_DEV, D)])
    pair = 2 if half % 2 == 0 else 1
    per = half // pair
    dw13 = mm_tn(xn, (dh, (None, "tt", pair * ns), lambda i, j, t: (j // per, t, j % per)), "ffn_dw13", nb, ns,
                 tka_pref=512, group=pair, exchange=send_w2)
    send_w13 = scatter_exchange([dw13])
    dxn = mm_nt((dh, (None, "tm", pair * ns), lambda m, o, b: (b // per, m, b % per)), w13, "ffn_dxn", BF16, T,
                tm_pref=512, group=pair, exchange=send_w13)
    dx, dxs, dg = rms_bwd(x, norm_g, dxn, d, scale_out)
    return dx, dxs, dg, send_w13.results[0], send_w2.results[0]


def _gmlp_parts(p_ref, lng_ref, lnb_ref):
    E = lng_ref.shape[-1]
    pv = p_ref[...].astype(F32)
    cdf = _normal_cdf(pv)
    z = pv * cdf
    u = z[:, :E]
    vp = z[:, E:]
    mu = jnp.mean(vp, axis=-1, keepdims=True)
    xc = vp - mu
    rstd = lax.rsqrt(jnp.mean(xc * xc, axis=-1, keepdims=True) + LN_EPS)
    vh = xc * rstd
    v = vh * lng_ref[...] + lnb_ref[...]
    return u, vh, rstd, v, pv, cdf


def _causal_ws(ws_ref, g):
    keep = lax.broadcasted_iota(jnp.int32, (CHUNK, CHUNK), 0) >= lax.broadcasted_iota(jnp.int32, (CHUNK, CHUNK), 1)
    return jnp.where(keep, ws_ref[g], 0.0).astype(BF16), keep


def gmlp_mid_fwd(p, ln_g, ln_b, w_s, bias_full):
    T, E2 = p.shape
    E = E2 // 2
    gd = E // GROUPS
    tm = _div(T, 256, CHUNK)
    fix2 = lambda i: (0, 0)

    def body(p_ref, lng_ref, lnb_ref, ws_ref, bias_ref, o_ref):
        u, _, _, v, _, _ = _gmlp_parts(p_ref, lng_ref, lnb_ref)
        vb = v.astype(BF16)
        for g in range(GROUPS):
            wm, _ = _causal_ws(ws_ref, g)
            cols = slice(g * gd, (g + 1) * gd)
            for c in range(tm // CHUNK):
                rows = slice(c * CHUNK, (c + 1) * CHUNK)
                f = _dot(wm, vb[rows, cols]) + bias_ref[:, cols]
                o_ref[rows, cols] = (u[rows, cols] * f).astype(BF16)

    return _call(body, "gmlp_mid_fwd", (T // tm,),
                 [(p, (tm, E2), lambda i: (i, 0)), (ln_g, (1, E), fix2), (ln_b, (1, E), fix2),
                  (w_s, (GROUPS, CHUNK, CHUNK), lambda i: (0, 0, 0)), (bias_full, (CHUNK, E), fix2)],
                 [((T, E), BF16, (tm, E), lambda i: (i, 0))])[0]


def gmlp_mid_bwd(p, dgated, ln_g, ln_b, w_s, bias_full, exchange=None):
    T, E2 = p.shape
    E = E2 // 2
    gd = E // GROUPS
    tm = _div(T, 256, CHUNK)
    nsteps = T // tm
    fix2 = lambda i: (0, 0)
    fix3 = lambda i: (0, 0, 0)

    def body(p_ref, dg_ref, lng_ref, lnb_ref, ws_ref, bias_ref,
             dp_ref, dws_ref, dbs_ref, dlng_ref, dlnb_ref, f_sc, dv_sc, db_sc):
        i = pl.program_id(0)
        first = i == 0
        u, vh, rstd, v, pv, cdf = _gmlp_parts(p_ref, lng_ref, lnb_ref)
        vb = v.astype(BF16)
        dgt = dg_ref[...].astype(F32)
        df = dgt * u
        dfb = df.astype(BF16)
        for g in range(GROUPS):
            wm, keep = _causal_ws(ws_ref, g)
            cols = slice(g * gd, (g + 1) * gd)
            dw = None
            dbg = None
            for c in range(tm // CHUNK):
                rows = slice(c * CHUNK, (c + 1) * CHUNK)
                f_sc[rows, cols] = _dot(wm, vb[rows, cols]) + bias_ref[:, cols]
                dv_sc[rows, cols] = _dot(wm, dfb[rows, cols], _TN)
                part = _dot(dfb[rows, cols], vb[rows, cols], _NT)
                dw = part if dw is None else dw + part
                dbg = df[rows, cols] if dbg is None else dbg + df[rows, cols]
            dw = jnp.where(keep, dw, 0.0)

            @pl.when(first)
            def _():
                dws_ref[g] = dw
                db_sc[:, cols] = dbg

            @pl.when(jnp.logical_not(first))
            def _():
                dws_ref[g] += dw
                db_sc[:, cols] += dbg

        du = dgt * f_sc[...]
        dv = dv_sc[...]
        _accumulate(dlng_ref, jnp.sum(dv * vh, axis=0, keepdims=True), first)
        _accumulate(dlnb_ref, jnp.sum(dv, axis=0, keepdims=True), first)
        dvh = dv * lng_ref[...]
        dvp = rstd * (dvh - jnp.mean(dvh, axis=-1, keepdims=True)
                      - vh * jnp.mean(dvh * vh, axis=-1, keepdims=True))
        gp = cdf + pv * _normal_pdf(pv)
        dp_ref[:, :E] = (du * gp[:, :E]).astype(BF16)
        dp_ref[:, E:] = (dvp * gp[:, E:]).astype(BF16)

        @pl.when(i == nsteps - 1)
        def _():
            for g in range(GROUPS):
                tot = jnp.sum(db_sc[:, g * gd:(g + 1) * gd], axis=-1, keepdims=True)
                dbs_ref[g] = jnp.broadcast_to(tot, (CHUNK, LANES))

    return _call(body, "gmlp_mid_bwd", (nsteps,),
                 [(p, (tm, E2), lambda i: (i, 0)), (dgated, (tm, E), lambda i: (i, 0)),
                  (ln_g, (1, E), fix2), (ln_b, (1, E), fix2),
                  (w_s, (GROUPS, CHUNK, CHUNK), fix3), (bias_full, (CHUNK, E), fix2)],
                 [((T, E2), BF16, (tm, E2), lambda i: (i, 0)),
                  ((GROUPS, CHUNK, CHUNK), F32, (GROUPS, CHUNK, CHUNK), fix3),
                  ((GROUPS, CHUNK, LANES), F32, (GROUPS, CHUNK, LANES), fix3),
                  ((1, E), F32, (1, E), fix2), ((1, E), F32, (1, E), fix2)],
                 scratch=[pltpu.VMEM((tm, E), F32), pltpu.VMEM((tm, E), F32), pltpu.VMEM((CHUNK, E), F32)],
                 exchange=exchange)


HALO = 16


def _row_of(block, r):
    rows = lax.broadcasted_iota(jnp.int32, block.shape, 0)
    return jnp.sum(jnp.where(rows == r, block, 0.0), axis=0, keepdims=True)


def _shift_down(z, k, fill):
    out = pltpu.roll(z, k, 0)
    rows = lax.broadcasted_iota(jnp.int32, z.shape, 0)
    for t in range(k):
        out = jnp.where(rows == t, fill[t], out)
    return out


def _shift_up(z, k, fill):
    n = z.shape[0]
    out = pltpu.roll(z, n - k, 0)
    rows = lax.broadcasted_iota(jnp.int32, z.shape, 0)
    for j in range(k):
        out = jnp.where(rows == n - k + j, fill[j], out)
    return out


def _conv_parts(p_ref, prev_ref, cw_ref, is_first):
    D = cw_ref.shape[-1]
    pv = p_ref[...].astype(F32)
    bg, cg, val = pv[:, :D], pv[:, D:2 * D], pv[:, 2 * D:]
    z = cg * val
    pp = prev_ref[...].astype(F32)
    zp = jnp.where(is_first, 0.0, pp[:, D:2 * D] * pp[:, 2 * D:])
    zl1 = _row_of(zp, HALO - 1)
    zl2 = _row_of(zp, HALO - 2)
    z1 = _shift_down(z, 1, [zl1])
    z2 = _shift_down(z, 2, [zl2, zl1])
    conv = z2 * cw_ref[0:1, :] + z1 * cw_ref[1:2, :] + z * cw_ref[2:3, :]
    return bg, cg, val, z, z1, z2, conv


def conv_mid_fwd(p, cw):
    T, D3 = p.shape
    D = D3 // 3
    tm = _div(T, 256, HALO)
    per = tm // HALO

    def body(p_ref, prev_ref, cw_ref, o_ref):
        bg, _, _, _, _, _, conv = _conv_parts(p_ref, prev_ref, cw_ref, pl.program_id(0) == 0)
        o_ref[...] = (bg * conv).astype(BF16)

    return _call(body, "conv_mid_fwd", (T // tm,),
                 [(p, (tm, D3), lambda i: (i, 0)),
                  (p, (HALO, D3), lambda i: (jnp.maximum(i * per - 1, 0), 0)),
                  (cw, (CONV_WIDTH, D), lambda i: (0, 0))],
                 [((T, D), BF16, (tm, D), lambda i: (i, 0))])[0]


def conv_mid_bwd(p, dgated, cw, exchange=None):
    T, D3 = p.shape
    D = D3 // 3
    tm = _div(T, 256, HALO)
    per = tm // HALO
    nsteps = T // tm
    last_halo = T // HALO - 1
    nxt = lambda i: (jnp.minimum((i + 1) * per, last_halo), 0)

    def body(p_ref, prev_ref, next_ref, dg_ref, dgn_ref, cw_ref, dp_ref, dcw_ref):
        i = pl.program_id(0)
        bg, cg, val, z, z1, z2, conv = _conv_parts(p_ref, prev_ref, cw_ref, i == 0)
        dgt = dg_ref[...].astype(F32)
        dconv = dgt * bg
        dcn = jnp.where(i == nsteps - 1, 0.0, dgn_ref[...].astype(F32) * next_ref[:, :D].astype(F32))
        n0 = _row_of(dcn, 0)
        n1 = _row_of(dcn, 1)
        up1 = _shift_up(dconv, 1, [n0])
        up2 = _shift_up(dconv, 2, [n0, n1])
        dz = dconv * cw_ref[2:3, :] + up1 * cw_ref[1:2, :] + up2 * cw_ref[0:1, :]
        dp_ref[:, :D] = (dgt * conv).astype(BF16)
        dp_ref[:, D:2 * D] = (dz * val).astype(BF16)
        dp_ref[:, 2 * D:] = (dz * cg).astype(BF16)
        first = i == 0
        parts = (jnp.sum(dconv * z2, axis=0, keepdims=True), jnp.sum(dconv * z1, axis=0, keepdims=True),
                 jnp.sum(dconv * z, axis=0, keepdims=True))

        @pl.when(first)
        def _():
            for k in range(CONV_WIDTH):
                dcw_ref[k:k + 1, :] = parts[k]

        @pl.when(jnp.logical_not(first))
        def _():
            for k in range(CONV_WIDTH):
                dcw_ref[k:k + 1, :] += parts[k]

    return _call(body, "conv_mid_bwd", (nsteps,),
                 [(p, (tm, D3), lambda i: (i, 0)),
                  (p, (HALO, D3), lambda i: (jnp.maximum(i * per - 1, 0), 0)),
                  (p, (HALO, D3), nxt),
                  (dgated, (tm, D), lambda i: (i, 0)),
                  (dgated, (HALO, D), nxt),
                  (cw, (CONV_WIDTH, D), lambda i: (0, 0))],
                 [((T, D3), BF16, (tm, D3), lambda i: (i, 0)),
                  ((CONV_WIDTH, D), F32, (CONV_WIDTH, D), lambda i: (0, 0))], exchange=exchange)


def mixer_bwd_common(d, dys, saved, norm_g, w_in, w_out, mid_bwd, scale_out, dwkv):
    x, hn, p, gated = saved
    T, D = x.shape
    E = gated.shape[1]
    w_out3 = w_out.reshape(1, E, D)
    dgated = mm_nt(dys, w_out3, "mix_dgated", BF16, T)
    dw_out = mm_tn(gated, dys, "mix_dwout", 1, D)
    send_wout = scatter_exchange([dw_out.reshape(N_DEV, E // N_DEV, D)])
    dp, extra = mid_bwd(p, dgated, send_wout)
    nb, _, ns = w_in.shape
    group = max(g for g in (1, 2, 4, 8) if nb % g == 0 and g * ns <= D)
    send_wkv = scatter_exchange([dwkv])
    dw_in = mm_tn(hn, dp, "mix_dwin", nb, ns, group=group, exchange=send_wkv)
    send_win = scatter_exchange([dw_in])
    dhn = mm_nt(dp, w_in, "mix_dhn", BF16, T, tm_pref=512, group=group, exchange=send_win)
    dx, dxs, dg = rms_bwd(x, norm_g, dhn, d, scale_out)
    return dx, dxs, dg, send_win.results[0], send_wout.results[0], send_wkv.results[0], extra


def _softmax_rows(s):
    e = jnp.exp(s - jnp.max(s, axis=-1, keepdims=True))
    return e / jnp.sum(e, axis=-1, keepdims=True)


def attn_fwd(q, kv):
    T, D = q.shape
    M = kv.shape[0]
    hd = D // HEADS
    scale = hd ** -0.5
    tm = _div(T, 512, BF16_SUBLANES)

    def body(q_ref, kv_ref, o_ref):
        for h in range(HEADS):
            cols = slice(h * hd, (h + 1) * hd)
            s = _dot(q_ref[:, cols], kv_ref[:, cols], _NT) * scale
            pr = _softmax_rows(s).astype(BF16)
            o_ref[:, cols] = _dot(pr, kv_ref[:, D + h * hd:D + (h + 1) * hd]).astype(BF16)

    return _call(body, "attn_fwd", (T // tm,),
                 [(q, (tm, D), lambda i: (i, 0)), (kv, (M, 2 * D), lambda i: (0, 0))],
                 [((T, D), BF16, (tm, D), lambda i: (i, 0))])[0]


def attn_bwd(q, do, kv, exchange=None):
    T, D = q.shape
    M = kv.shape[0]
    hd = D // HEADS
    scale = hd ** -0.5
    tm = _div(T, 512, BF16_SUBLANES)
    nsteps = T // tm

    def body(q_ref, do_ref, kv_ref, dq_ref, dkv_ref, acc_ref):
        i = pl.program_id(0)
        for h in range(HEADS):
            cols = slice(h * hd, (h + 1) * hd)
            vcols = slice(D + h * hd, D + (h + 1) * hd)
            qh = q_ref[:, cols]
            kh = kv_ref[:, cols]
            doh = do_ref[:, cols]
            pr = _softmax_rows(_dot(qh, kh, _NT) * scale)
            dpr = _dot(doh, kv_ref[:, vcols], _NT)
            ds = (pr * (dpr - jnp.sum(dpr * pr, axis=-1, keepdims=True)) * scale).astype(BF16)
            dq_ref[:, cols] = _dot(ds, kh).astype(BF16)
            dk = _dot(ds, qh, _TN)
            dv = _dot(pr.astype(BF16), doh, _TN)

            @pl.when(i == 0)
            def _():
                acc_ref[:, cols] = dk
                acc_ref[:, vcols] = dv

            @pl.when(i > 0)
            def _():
                acc_ref[:, cols] += dk
                acc_ref[:, vcols] += dv

        @pl.when(i == nsteps - 1)
        def _():
            dkv_ref[...] = acc_ref[...].astype(BF16)

    return _call(body, "attn_bwd", (nsteps,),
                 [(q, (tm, D), lambda i: (i, 0)), (do, (tm, D), lambda i: (i, 0)),
                  (kv, (M, 2 * D), lambda i: (0, 0))],
                 [((T, D), BF16, (tm, D), lambda i: (i, 0)), ((M, 2 * D), BF16, (M, 2 * D), lambda i: (0, 0))],
                 scratch=[pltpu.VMEM((M, 2 * D), F32)], exchange=exchange)


def xattn_fwd(x, mem, xnorm_g, mnorm_g, wq, wkv, wo):
    D = x.shape[1]
    hq = rms_fwd(x, xnorm_g)
    mn = rms_fwd(mem, mnorm_g)
    q = mm_nn(hq, wq.reshape(1, D, D), "xattn_q")
    kv = mm_nn(mn, wkv, "xattn_kv")
    o = attn_fwd(q, kv)
    y = mm_nn(o, wo.reshape(1, D, D), "xattn_out", F32, res=x)
    return y, (x, hq, mn, q, kv, o)


def xattn_bwd(d, dys, saved, mem, xnorm_g, wq, wkv, wo, scale_out):
    x, hq, mn, q, kv, o = saved
    T, D = x.shape
    M = mem.shape[0]
    do = mm_nt(dys, wo.reshape(1, D, D), "xattn_do", BF16, T)
    rows = D // N_DEV
    dwo = mm_tn(o, dys, "xattn_dwo", 1, D)
    send_wo = scatter_exchange([dwo.reshape(N_DEV, rows, D)])
    dq, dkv = attn_bwd(q, do, kv, exchange=send_wo)
    nb, _, ns = wkv.shape
    dwkv = mm_tn(mn, dkv, "xattn_dwkv", nb, ns)
    dmn = mm_nt(dkv, wkv, "xattn_dmn", F32, M)
    dgm = rms_gain_grad(mem, dmn)
    dwq = mm_tn(hq, dq, "xattn_dwq", 1, D)
    send_wq = scatter_exchange([dwq.reshape(N_DEV, rows, D)])
    dx, dxs, dgx = mm_nt_rms_bwd(dq, wq.reshape(1, D, D), "xattn_dhq", x, xnorm_g, d, scale_out,
                                 exchange=send_wq)
    return dx, dxs, dgx, dgm, send_wq.results[0], dwkv, send_wo.results[0]


def _mesh_places():
    x, y, c = lax.axis_index("x"), lax.axis_index("y"), lax.axis_index("c")
    chips = [(1 - x, y), (x, 1 - y), (1 - x, 1 - y)]
    return (x, y, c), (x, y, 1 - c), chips


def _slot(place):
    return 4 * place[0] + 2 * place[1] + place[2]


def _exchange_sems(n):
    return [pltpu.SemaphoreType.DMA((n * N_PEERS,)), pltpu.SemaphoreType.DMA((n * N_PEERS,)),
            pltpu.SemaphoreType.DMA((n,))]


def gather_exchange(shards):
    n = len(shards)
    shapes = [a.shape if l is None else a.shape[1:] for a, l in shards]

    def parts(x_in, x_out, sems):
        ins = [r if l is None else r.at[l] for r, (_, l) in zip(x_in, shards)]
        send_sems, recv_sems, local_sems = sems
        me, sibling, chips = _mesh_places()

        def copy(a, k, block, to, src=None):
            dst = x_out[a].at[_slot(block)]
            return pltpu.make_async_remote_copy(
                src_ref=dst if src is None else src, dst_ref=dst,
                send_sem=send_sems.at[a * N_PEERS + k], recv_sem=recv_sems.at[a * N_PEERS + k],
                device_id=to, device_id_type=pl.DeviceIdType.MESH)

        mine = [pltpu.make_async_copy(ins[a], x_out[a].at[_slot(me)], local_sems.at[a]) for a in range(n)]
        first = []
        for a in range(n):
            first.append(copy(a, 0, me, sibling, src=ins[a]))
            first += [copy(a, 1 + j, me, (*chip, me[2]), src=ins[a]) for j, chip in enumerate(chips)]
        return me, sibling, chips, copy, mine, first

    def start(x_in, x_out, sems):
        _, _, _, _, mine, first = parts(x_in, x_out, sems)
        for cp in mine + first:
            cp.start()

    def forward(x_in, x_out, sems):
        me, sibling, chips, copy, _, _ = parts(x_in, x_out, sems)
        for j, chip in enumerate(chips):
            for a in range(n):
                copy(a, 1 + j, (*chip, me[2]), me).wait_recv()
                copy(a, 4 + j, (*chip, me[2]), sibling).start()

    def finish(x_in, x_out, sems):
        me, sibling, chips, copy, mine, first = parts(x_in, x_out, sems)
        for a in range(n):
            copy(a, 0, sibling, me).wait_recv()
        for j, chip in enumerate(chips):
            for a in range(n):
                copy(a, 4 + j, (*chip, 1 - me[2]), me).wait_recv()
        for cp in first:
            cp.wait_send()
        for j, chip in enumerate(chips):
            for a in range(n):
                copy(a, 4 + j, (*chip, me[2]), sibling).wait_send()
        for cp in mine:
            cp.wait()

    return Exchange([a for a, _ in shards],
                    [jax.ShapeDtypeStruct((N_DEV,) + tuple(s), a.dtype) for s, (a, _) in zip(shapes, shards)],
                    _exchange_sems(n), start, finish, forward)


def _all_peers(me, chips):
    c = me[2]
    return [(me[0], me[1], 1 - c)] + [(*chip, c) for chip in chips] + [(*chip, 1 - c) for chip in chips]


def scatter_exchange(grads):
    n = len(grads)

    def parts(x_in, x_out, sems):
        send_sems, recv_sems, local_sems = sems
        me, _, chips = _mesh_places()
        mine = [pltpu.make_async_copy(x_in[a].at[_slot(me)], x_out[a].at[_slot(me)], local_sems.at[a])
                for a in range(n)]
        sends, recvs = [], []
        for a in range(n):
            for k, peer in enumerate(_all_peers(me, chips)):
                sem = dict(send_sem=send_sems.at[a * N_PEERS + k], recv_sem=recv_sems.at[a * N_PEERS + k],
                           device_id=peer, device_id_type=pl.DeviceIdType.MESH)
                sends.append(pltpu.make_async_remote_copy(
                    src_ref=x_in[a].at[_slot(peer)], dst_ref=x_out[a].at[_slot(me)], **sem))
                recvs.append(pltpu.make_async_remote_copy(
                    src_ref=x_in[a].at[_slot(peer)], dst_ref=x_out[a].at[_slot(peer)], **sem))
        return mine, sends, recvs

    def start(x_in, x_out, sems):
        mine, sends, _ = parts(x_in, x_out, sems)
        for cp in mine + sends:
            cp.start()

    def finish(x_in, x_out, sems):
        mine, sends, recvs = parts(x_in, x_out, sems)
        for cp in recvs:
            cp.wait_recv()
        for cp in sends:
            cp.wait_send()
        for cp in mine:
            cp.wait()

    return Exchange(grads, [jax.ShapeDtypeStruct(g.shape, g.dtype) for g in grads],
                    _exchange_sems(n), start, finish)


def small_all_reduce(vec):
    R = vec.shape[0]

    def body(v_ref, o_ref, all_ref, send_sems, recv_sems):
        me, _, chips = _mesh_places()
        peers = _all_peers(me, chips)
        all_ref[_slot(me)] = v_ref[...]
        sends, recvs = [], []
        for k, peer in enumerate(peers):
            sem = dict(send_sem=send_sems.at[k], recv_sem=recv_sems.at[k],
                       device_id=peer, device_id_type=pl.DeviceIdType.MESH)
            sends.append(pltpu.make_async_remote_copy(src_ref=v_ref, dst_ref=all_ref.at[_slot(me)], **sem))
            recvs.append(pltpu.make_async_remote_copy(src_ref=v_ref, dst_ref=all_ref.at[_slot(peer)], **sem))
        for cp in sends:
            cp.start()
        for cp in recvs:
            cp.wait_recv()
        for cp in sends:
            cp.wait_send()
        acc = all_ref[0]
        for s in range(1, N_DEV):
            acc = acc + all_ref[s]
        o_ref[...] = acc

    return pl.pallas_call(
        body, name="small_all_reduce",
        out_shape=jax.ShapeDtypeStruct(vec.shape, F32),
        in_specs=[pl.BlockSpec(memory_space=pltpu.VMEM)], out_specs=pl.BlockSpec(memory_space=pltpu.VMEM),
        scratch_shapes=[pltpu.VMEM((N_DEV, R, LANES), F32), pltpu.SemaphoreType.DMA((N_PEERS,)),
                        pltpu.SemaphoreType.DMA((N_PEERS,))],
    )(vec)


def _adamw_math(w, g, m, v):
    m2 = ADAM_B1 * m + (1.0 - ADAM_B1) * g
    v2 = ADAM_B2 * v + (1.0 - ADAM_B2) * (g * g)
    m_hat = m2 / (1.0 - ADAM_B1 ** ADAM_STEP)
    v_hat = v2 / (1.0 - ADAM_B2 ** ADAM_STEP)
    delta = -ADAM_LR * (m_hat / (jnp.sqrt(v_hat) + ADAM_EPS) + ADAM_WD * w)
    return delta, m2, v2


def adamw_sharded(partials, w, m, v):
    L, r, c = w.shape
    row_bytes = 2 * c * (L * N_DEV * 2 + 7 * 4)
    tr = _div(r, max(BF16_SUBLANES, min(512, (VMEM_LIMIT_BYTES * 3 // 4) // row_bytes)), BF16_SUBLANES)
    nt = r // tr

    def part_map(l0):
        return lambda l, t: (0, jnp.where(l == l0, t, jnp.where(l < l0, 0, nt - 1)), 0)

    def body(*refs):
        parts = refs[:L]
        w_ref, m_ref, v_ref, g_out, d_out, m_out, v_out = refs[L:]
        layer = pl.program_id(0)
        for l0 in range(L):
            @pl.when(layer == l0)
            def _():
                g = parts[l0][0].astype(F32)
                for s in range(1, N_DEV):
                    g = g + parts[l0][s].astype(F32)
                delta, m2, v2 = _adamw_math(w_ref[...], g, m_ref[...], v_ref[...])
                g_out[...] = g
                d_out[...] = delta
                m_out[...] = m2
                v_out[...] = v2

    own = lambda l, t: (l, t, 0)
    return _call(body, "adamw_sharded", (L, nt),
                 [(p, (N_DEV, tr, c), part_map(l0)) for l0, p in enumerate(partials)]
                 + [(w, (None, tr, c), own), (m, (None, tr, c), own), (v, (None, tr, c), own)],
                 [((L, r, c), F32, (None, tr, c), own)] * 4)


def adamw_flat(g, w, m, v):
    shape = g.shape

    def body(g_ref, w_ref, m_ref, v_ref, d_out, m_out, v_out):
        delta, m2, v2 = _adamw_math(w_ref[...], g_ref[...], m_ref[...], v_ref[...])
        d_out[...] = delta
        m_out[...] = m2
        v_out[...] = v2

    whole = lambda: (0, 0)
    return _call(body, "adamw_flat", (), [(a, shape, whole) for a in (g, w, m, v)],
                 [(shape, F32, shape, whole)] * 3)


def _pack(parts):
    flat = jnp.concatenate([p.reshape(-1).astype(F32) for p in parts])
    rows = -(-flat.shape[0] // (8 * LANES)) * 8
    return jnp.pad(flat, (0, rows * LANES - flat.shape[0])).reshape(rows, LANES)


def _unpack(packed, shapes):
    flat = packed.reshape(-1)
    out, off = [], 0
    for s in shapes:
        size = math.prod(s)
        out.append(flat[off:off + size].reshape(s))
        off += size
    return out


def kernel(x, mem, ffn1_norm, ffn1_w13, ffn1_w2, mix_norm, gmlp_w_in, gmlp_ln_g, gmlp_ln_b, gmlp_w_s, gmlp_b_s, gmlp_w_out, conv_w_in, conv_w, conv_w_out, xattn_norm, mem_norm, xattn_wq, xattn_wkv, xattn_wo, ffn2_norm, ffn2_w13, ffn2_w2, final_norm, loss_target, m_ffn1_norm, m_ffn1_w13, m_ffn1_w2, m_mix_norm, m_gmlp_w_in, m_gmlp_ln_g, m_gmlp_ln_b, m_gmlp_w_s, m_gmlp_b_s, m_gmlp_w_out, m_conv_w_in, m_conv_w, m_conv_w_out, m_xattn_norm, m_mem_norm, m_xattn_wq, m_xattn_wkv, m_xattn_wo, m_ffn2_norm, m_ffn2_w13, m_ffn2_w2, m_final_norm, v_ffn1_norm, v_ffn1_w13, v_ffn1_w2, v_mix_norm, v_gmlp_w_in, v_gmlp_ln_g, v_gmlp_ln_b, v_gmlp_w_s, v_gmlp_b_s, v_gmlp_w_out, v_conv_w_in, v_conv_w, v_conv_w_out, v_xattn_norm, v_mem_norm, v_xattn_wq, v_xattn_wkv, v_xattn_wo, v_ffn2_norm, v_ffn2_w13, v_ffn2_w2, v_final_norm):
    given = dict(locals())
    T, D = x.shape[1], x.shape[2]
    depth = ffn1_norm.shape[0]
    xs = x.reshape(T, D)
    mems = mem.reshape(mem.shape[1], D)
    target = loss_target.reshape(T, D)
    me = 4 * lax.axis_index("x") + 2 * lax.axis_index("y") + lax.axis_index("c")
    E = gmlp_ln_g.shape[1]
    gd = E // GROUPS
    cshard = conv_w.shape[2]

    bf = {k: given[k].astype(BF16) for k in
          ("ffn1_w13", "ffn1_w2", "gmlp_w_in", "gmlp_w_out", "conv_w_in", "conv_w_out",
           "xattn_wq", "xattn_wkv", "xattn_wo", "ffn2_w13", "ffn2_w2")}

    W = {}

    def gather(names_layers):
        return names_layers, gather_exchange([(bf[k], l) for k, l in names_layers])

    def landed(tagged):
        names_layers, exchange = tagged
        W.update(zip(names_layers, exchange.results))

    bias_full = jnp.repeat(gmlp_b_s[0].T, gd, axis=1)

    first = gather_exchange([(bf["ffn1_w13"], 0), (conv_w, 0)])
    taps = {}

    def first_w13():
        W["ffn1_w13", 0], cw_shards = first.results
        taps["conv"] = jnp.transpose(cw_shards, (1, 0, 2)).reshape(CONV_WIDTH, D)
        return W["ffn1_w13", 0]

    saved = []
    h = xs
    for i in range(depth):
        j = i // 2
        is_gmlp = i % 2 == 0
        mix = ("gmlp_w_in", "gmlp_w_out") if is_gmlp else ("conv_w_in", "conv_w_out")
        on_up1 = gather(([("ffn1_w2", i)] if i == 0 else []) + [(mix[0], j), ("ffn2_w13", i)])
        on_down1 = gather([(mix[1], j), ("xattn_wq", i), ("xattn_wkv", i)])

        def w2_after_up(on_up=on_up1, i=i):
            landed(on_up)
            return W["ffn1_w2", i]

        h, sv1 = ffn_fwd(h, ffn1_norm[i], first_w13 if i == 0 else (lambda i=i: W["ffn1_w13", i]), w2_after_up,
                         first if i == 0 else None, on_up1[1], on_down1[1])
        landed(on_down1)
        hn = rms_fwd(h, mix_norm[i])
        on_mix_in = gather([("xattn_wo", i), ("ffn2_w2", i)])
        p = mm_nn(hn, W[mix[0], j], "mix_in", group=2, exchange=on_mix_in[1])
        landed(on_mix_in)
        if is_gmlp:
            gated = gmlp_mid_fwd(p, gmlp_ln_g[j:j + 1], gmlp_ln_b[j:j + 1], gmlp_w_s[j], bias_full)
        else:
            gated = conv_mid_fwd(p, taps["conv"])
        h_mix = mm_nn(gated, W[mix[1], j].reshape(1, gated.shape[1], D), "mix_out", F32, res=h)
        sv2 = (h, hn, p, gated)
        h, sv3 = xattn_fwd(h_mix, mems, xattn_norm[i], mem_norm[i],
                           W["xattn_wq", i], W["xattn_wkv", i], W["xattn_wo", i])
        on_up2 = gather([("ffn1_w13", i + 1), ("ffn1_w2", i + 1)]) if i + 1 < depth else None
        h, sv4 = ffn_fwd(h, ffn2_norm[i], lambda i=i: W["ffn2_w13", i], lambda i=i: W["ffn2_w2", i],
                         up_exchange=None if on_up2 is None else on_up2[1])
        if on_up2 is not None:
            landed(on_up2)
        saved.append((sv1, sv2, sv3, sv4))

    loss_part, d, dys, d_final_norm = loss_head(h, final_norm, target, 0.5)

    small = {k: [None] * depth for k in ("ffn1_norm", "mix_norm", "xattn_norm", "mem_norm", "ffn2_norm")}
    partial = {}
    for i in reversed(range(depth)):
        j = i // 2
        is_gmlp = i % 2 == 0
        sv1, sv2, sv3, sv4 = saved[i]
        d, dys, small["ffn2_norm"][i], partial["ffn2_w13", i], partial["ffn2_w2", i] = ffn_bwd(
            d, dys, sv4, ffn2_norm[i], W["ffn2_w13", i], W["ffn2_w2", i], 1.0)
        (d, dys, small["xattn_norm"][i], small["mem_norm"][i], partial["xattn_wq", i],
         dwkv, partial["xattn_wo", i]) = xattn_bwd(
            d, dys, sv3, mems, xattn_norm[i], W["xattn_wq", i], W["xattn_wkv", i], W["xattn_wo", i], 1.0)
        if is_gmlp:
            mix = ("gmlp_w_in", "gmlp_w_out")
            mid = lambda p, dg, send: (lambda r: (r[0], r[1:]))(gmlp_mid_bwd(
                p, dg, gmlp_ln_g[j:j + 1], gmlp_ln_b[j:j + 1], gmlp_w_s[j], bias_full, exchange=send))
        else:
            mix = ("conv_w_in", "conv_w_out")
            mid = lambda p, dg, send: (lambda r: (r[0], r[1:]))(conv_mid_bwd(p, dg, taps["conv"], exchange=send))
        (d, dys, small["mix_norm"][i], partial[mix[0], j], partial[mix[1], j], partial["xattn_wkv", i],
         extra) = mixer_bwd_common(d, dys, sv2, mix_norm[i], W[mix[0], j], W[mix[1], j], mid, 0.5, dwkv)
        if is_gmlp:
            d_ws, d_bs_wide, d_lng, d_lnb = extra
        else:
            (d_cw,) = extra
        d, dys, small["ffn1_norm"][i], partial["ffn1_w13", i], partial["ffn1_w2", i] = ffn_bwd(
            d, dys, sv1, ffn1_norm[i], W["ffn1_w13", i], W["ffn1_w2", i], 0.5)
    grad_x = d.reshape(x.shape)

    small_grads = {k: jnp.concatenate(v, axis=0) for k, v in small.items()}
    small_grads["gmlp_ln_g"] = d_lng
    small_grads["gmlp_ln_b"] = d_lnb
    small_grads["gmlp_w_s"] = d_ws[None]
    small_grads["gmlp_b_s"] = d_bs_wide[None, :, :, 0]
    small_grads["final_norm"] = d_final_norm.reshape(-1)
    small_names = ["ffn1_norm", "mix_norm", "gmlp_ln_g", "gmlp_ln_b", "gmlp_w_s", "gmlp_b_s",
                   "xattn_norm", "mem_norm", "ffn2_norm", "final_norm"]
    summed = small_all_reduce(_pack([small_grads[k] for k in small_names] + [d_cw, loss_part]))
    parts = _unpack(summed, [given[k].shape for k in small_names] + [(CONV_WIDTH, D), (1, LANES)])
    grads = dict(zip(small_names, parts[:len(small_names)]))
    grads["conv_w"] = lax.dynamic_slice(parts[-2], (jnp.int32(0), me * cshard), (CONV_WIDTH, cshard))[None]
    loss = parts[-1][0, 0]
    flat_names = small_names + ["conv_w"]
    flat = adamw_flat(*[_pack([src[k] for k in flat_names]) for src in
                        (grads, given, {k: given["m_" + k] for k in flat_names},
                         {k: given["v_" + k] for k in flat_names})])
    delta, new_m, new_v = [dict(zip(flat_names, _unpack(f, [given[k].shape for k in flat_names]))) for f in flat]

    for k in bf:
        w = given[k]
        L = w.shape[0]
        shard = w.shape[1:]
        view = lambda a: a.reshape((L,) + shard)
        g, dl, m2, v2 = adamw_sharded([partial[k, l] for l in range(L)], w, given["m_" + k], given["v_" + k])
        grads[k], delta[k], new_m[k], new_v[k] = view(g), view(dl), view(m2), view(v2)

    order = ["ffn1_norm", "ffn1_w13", "ffn1_w2", "mix_norm", "gmlp_w_in", "gmlp_ln_g", "gmlp_ln_b", "gmlp_w_s",
             "gmlp_b_s", "gmlp_w_out", "conv_w_in", "conv_w", "conv_w_out", "xattn_norm", "mem_norm", "xattn_wq",
             "xattn_wkv", "xattn_wo", "ffn2_norm", "ffn2_w13", "ffn2_w2", "final_norm"]
    return (loss, grad_x, *[grads[k] for k in order], *[delta[k] for k in order],
            *[new_m[k] for k in order], *[new_v[k] for k in order])
```

```python
import math

import jax
import jax.numpy as jnp
from jax import lax
from jax.experimental import pallas as pl
from jax.experimental.pallas import tpu as pltpu

F32 = jnp.float32
BF16 = jnp.bfloat16

N_DEV = 8
N_PEERS = N_DEV - 1
CHUNK = 128
GROUPS = 8
HEADS = 4
CONV_WIDTH = 3
RMS_EPS = 1e-6
LN_EPS = 1e-5
ADAM_LR = 0.001
ADAM_B1 = 0.9
ADAM_B2 = 0.999
ADAM_EPS = 1e-08
ADAM_WD = 0.01
ADAM_STEP = 10
LANES = 128
BF16_SUBLANES = 16
MXU_WIDTH = 256
VMEM_LIMIT_BYTES = 56 * 1024 * 1024

_NT = (((1,), (1,)), ((), ()))
_TN = (((0,), (0,)), ((), ()))
_SQRT_HALF = 0.7071067811865476
_INV_SQRT_2PI = 0.3989422804014327


def _div(n, pref, align):
    best = None
    for t in range(align, min(n, pref) + 1, align):
        if n % t == 0:
            best = t
    return n if best is None else best


def _chunks(n, width):
    return [(c0, min(width, n - c0)) for c0 in range(0, n, width)]


_ANY = pl.BlockSpec(memory_space=pl.ANY)


class Exchange:
    def __init__(self, arrays, out_shapes, sems, start, finish, forward=None):
        self.arrays, self.out_shapes, self.sems = list(arrays), list(out_shapes), list(sems)
        self.start, self.finish, self.forward = start, finish, forward
        self.results = None


def _call(body, name, grid, ins, outs, scratch=(), exchange=None):
    in_specs = [pl.BlockSpec(*spec[1:3], **({"pipeline_mode": pl.Buffered(1)} if len(spec) > 3 else {}))
                for spec in ins]
    ins = [spec[:3] for spec in ins]
    out_specs = [pl.BlockSpec(bs, im) for _, _, bs, im in outs]
    out_shape = [jax.ShapeDtypeStruct(s, d) for s, d, _, _ in outs]
    arrays = [a for a, _, _ in ins]
    scratch = list(scratch)
    kernel_fn = body
    if exchange is not None:
        n_in, n_out, n_scr = len(ins), len(outs), len(scratch)
        n_xin, n_xout = len(exchange.arrays), len(exchange.out_shapes)
        steps = math.prod(grid)
        forward_step = min(steps - 1, (15 * steps) // 16)

        def kernel_fn(*refs):
            refs = list(refs)
            own_in, x_in = refs[:n_in], refs[n_in:n_in + n_xin]
            refs = refs[n_in + n_xin:]
            own_out, x_out = refs[:n_out], refs[n_out:n_out + n_xout]
            refs = refs[n_out + n_xout:]
            own_scr, x_sems = refs[:n_scr], refs[n_scr:]
            step = 0
            for axis, size in enumerate(grid):
                step = step * size + pl.program_id(axis)

            @pl.when(step == 0)
            def _():
                exchange.start(x_in, x_out, x_sems)

            if exchange.forward is not None:
                @pl.when(step == forward_step)
                def _():
                    exchange.forward(x_in, x_out, x_sems)

            body(*own_in, *own_out, *own_scr)

            @pl.when(step == steps - 1)
            def _():
                exchange.finish(x_in, x_out, x_sems)

        in_specs += [_ANY] * n_xin
        out_specs += [_ANY] * n_xout
        out_shape += exchange.out_shapes
        arrays += exchange.arrays
        scratch += exchange.sems
    res = pl.pallas_call(
        kernel_fn,
        name=name,
        grid=grid,
        in_specs=in_specs,
        out_specs=out_specs,
        out_shape=out_shape,
        scratch_shapes=scratch,
        compiler_params=pltpu.CompilerParams(
            dimension_semantics=("arbitrary",) * len(grid), vmem_limit_bytes=VMEM_LIMIT_BYTES),
    )(*arrays)
    if exchange is not None:
        exchange.results = list(res[len(outs):])
        res = res[:len(outs)]
    return res


def _dot(a, b, dims=None):
    if dims is None:
        return jnp.dot(a, b, preferred_element_type=F32)
    return lax.dot_general(a, b, dims, preferred_element_type=F32)


def _sigmoid(v):
    return 1.0 / (1.0 + jnp.exp(-v))


def _normal_cdf(v):
    return 0.5 * (1.0 + lax.erf(v * _SQRT_HALF))


def _normal_pdf(v):
    return _INV_SQRT_2PI * jnp.exp(-0.5 * v * v)


def _accumulate_product(ref, first, product, skip=False):
    @pl.when(first)
    def _():
        ref[...] = product()

    @pl.when(jnp.logical_not(jnp.logical_or(first, skip)))
    def _():
        ref[...] += product()


def _accumulate(ref, part, first):
    @pl.when(first)
    def _():
        ref[...] = part

    @pl.when(jnp.logical_not(first))
    def _():
        ref[...] += part


def rms_fwd(x, g, exchange=None):
    T, D = x.shape
    tt = _div(T, 512, BF16_SUBLANES)

    def body(x_ref, g_ref, o_ref):
        xv = x_ref[...]
        r = lax.rsqrt(jnp.mean(xv * xv, axis=-1, keepdims=True) + RMS_EPS)
        o_ref[...] = ((xv * r) * g_ref[...]).astype(BF16)

    return _call(body, "rms_fwd", (T // tt,),
                 [(x, (tt, D), lambda i: (i, 0)), (g.reshape(1, D), (1, D), lambda i: (0, 0))],
                 [((T, D), BF16, (tt, D), lambda i: (i, 0))], exchange=exchange)[0]


def rms_bwd(x, g, dxn, d, scale):
    T, D = x.shape
    tt = _div(T, 512, BF16_SUBLANES)

    def body(x_ref, g_ref, dn_ref, d_ref, dx_ref, dxs_ref, dg_ref):
        xv = x_ref[...]
        r = lax.rsqrt(jnp.mean(xv * xv, axis=-1, keepdims=True) + RMS_EPS)
        xh = xv * r
        dn = dn_ref[...].astype(F32)
        dxh = dn * g_ref[...]
        dx = r * (dxh - xh * jnp.mean(dxh * xh, axis=-1, keepdims=True)) + d_ref[...]
        dx_ref[...] = dx
        dxs_ref[...] = (scale * dx).astype(BF16)
        _accumulate(dg_ref, jnp.sum(dn * xh, axis=0, keepdims=True), pl.program_id(0) == 0)

    row = lambda i: (i, 0)
    fix = lambda i: (0, 0)
    return _call(body, "rms_bwd", (T // tt,),
                 [(x, (tt, D), row), (g.reshape(1, D), (1, D), fix), (dxn, (tt, D), row), (d, (tt, D), row)],
                 [((T, D), F32, (tt, D), row), ((T, D), BF16, (tt, D), row), ((1, D), F32, (1, D), fix)])


def rms_gain_grad(x, dxn):
    T, D = x.shape
    tt = _div(T, 256, 8)

    def body(x_ref, dn_ref, dg_ref):
        xv = x_ref[...]
        r = lax.rsqrt(jnp.mean(xv * xv, axis=-1, keepdims=True) + RMS_EPS)
        _accumulate(dg_ref, jnp.sum(dn_ref[...] * (xv * r), axis=0, keepdims=True), pl.program_id(0) == 0)

    row = lambda i: (i, 0)
    return _call(body, "rms_gain_grad", (T // tt,), [(x, (tt, D), row), (dxn, (tt, D), row)],
                 [((1, D), F32, (1, D), lambda i: (0, 0))])[0]


def loss_head(x, g, target, scale):
    T, D = x.shape
    tt = _div(T, 256, BF16_SUBLANES)

    def body(x_ref, g_ref, t_ref, loss_ref, dx_ref, dxs_ref, dg_ref):
        first = pl.program_id(0) == 0
        xv = x_ref[...]
        gv = g_ref[...]
        r = lax.rsqrt(jnp.mean(xv * xv, axis=-1, keepdims=True) + RMS_EPS)
        xh = xv * r
        err = xh * gv - t_ref[...]
        part = 0.5 * jnp.sum(jnp.mean(err * err, axis=-1, keepdims=True), axis=0, keepdims=True)
        _accumulate(loss_ref, jnp.broadcast_to(part, (1, LANES)), first)
        dy = err * (1.0 / D)
        dxh = dy * gv
        dx = r * (dxh - xh * jnp.mean(dxh * xh, axis=-1, keepdims=True))
        dx_ref[...] = dx
        dxs_ref[...] = (scale * dx).astype(BF16)
        _accumulate(dg_ref, jnp.sum(dy * xh, axis=0, keepdims=True), first)

    row = lambda i: (i, 0)
    fix = lambda i: (0, 0)
    return _call(body, "loss_head", (T // tt,),
                 [(x, (tt, D), row), (g.reshape(1, D), (1, D), fix), (target, (tt, D), row)],
                 [((1, LANES), F32, (1, LANES), fix), ((T, D), F32, (tt, D), row),
                  ((T, D), BF16, (tt, D), row), ((1, D), F32, (1, D), fix)])


def mm_nn(a, w3, name, out_dtype=BF16, res=None, res_scale=1.0, tm_pref=1024, tn_pref=1024, group=1,
          exchange=None):
    M, K = a.shape
    nb, _, ns = w3.shape
    tn = ns if group > 1 else _div(ns, tn_pref, LANES)
    per = ns // tn
    tm = _div(M, tm_pref, BF16_SUBLANES)
    wide = group * tn
    ins = [(a, (tm, K), lambda j, m: (m, 0)), (w3, (group, K, tn), lambda j, m: (j // per, 0, j % per))]
    if res is not None:
        ins.append((res, (tm, wide), lambda j, m: (m, j)))

    def body(*refs):
        a_ref, w_ref = refs[0], refs[1]
        o_ref = refs[-1]
        for k in range(group):
            cols = slice(k * tn, (k + 1) * tn)
            acc = _dot(a_ref[...], w_ref[k])
            if res is not None:
                acc = refs[2][:, cols] + res_scale * acc
            o_ref[:, cols] = acc.astype(o_ref.dtype)

    return _call(body, name, (nb * per // group, M // tm), ins,
                 [((M, nb * ns), out_dtype, (tm, wide), lambda j, m: (m, j))], exchange=exchange)[0]


def mm_nt(a_in, w3, name, out_dtype, M, tm_pref=1024, to_pref=2048, group=1, exchange=None):
    nb, Ko, ns = w3.shape
    to = _div(Ko, to_pref, LANES)
    tm = _div(M, tm_pref, BF16_SUBLANES)
    if isinstance(a_in, tuple):
        a, a_bs, a_im = a_in
        a_bs = tuple(tm if s == "tm" else s for s in a_bs)
    else:
        a, a_bs, a_im = a_in, (tm, group * ns), lambda m, o, b: (m, b)
    steps = nb // group
    narrow_out = steps > 1 and out_dtype != F32

    def product(a_ref, w_ref):
        p = None
        for k in range(group):
            pk = _dot(a_ref[:, k * ns:(k + 1) * ns], w_ref[k], _NT)
            p = pk if p is None else p + pk
        return p

    def body(a_ref, w_ref, o_ref, *acc):
        b = pl.program_id(2)
        if steps == 1:
            o_ref[...] = product(a_ref, w_ref).astype(o_ref.dtype)
        elif not narrow_out:
            _accumulate_product(o_ref, b == 0, lambda: product(a_ref, w_ref))
        else:
            acc_ref, = acc
            _accumulate_product(acc_ref, b == 0, lambda: product(a_ref, w_ref), skip=b == steps - 1)

            @pl.when(b == steps - 1)
            def _():
                o_ref[...] = (acc_ref[...] + product(a_ref, w_ref)).astype(o_ref.dtype)

    return _call(body, name, (M // tm, Ko // to, steps),
                 [(a, a_bs, a_im), (w3, (group, to, ns), lambda m, o, b: (b, o, 0))],
                 [((M, Ko), out_dtype, (tm, to), lambda m, o, b: (m, o))],
                 scratch=[pltpu.VMEM((tm, to), F32)] if narrow_out else [], exchange=exchange)[0]


def mm_nt_rms_bwd(a, w3, name, x, g, d, scale, exchange=None):
    nb, D, ns = w3.shape
    M = x.shape[0]
    tm = _div(M, 512, BF16_SUBLANES)
    a_bs, a_im = (tm, ns), lambda m, b: (m, b)

    rc = _div(tm, 64, BF16_SUBLANES)

    def body(a_ref, w_ref, x_ref, g_ref, d_ref, dx_ref, dxs_ref, dg_ref, acc_ref):
        m, b = pl.program_id(0), pl.program_id(1)
        _accumulate_product(acc_ref, b == 0, lambda: _dot(a_ref[...], w_ref[...], _NT))

        @pl.when(b == nb - 1)
        def _():
            gv = g_ref[...]

            def piece(c, dg):
                rows = pl.ds(pl.multiple_of(c * rc, rc), rc)
                dn = acc_ref[rows, :]
                xv = x_ref[rows, :]
                r = lax.rsqrt(jnp.mean(xv * xv, axis=-1, keepdims=True) + RMS_EPS)
                xh = xv * r
                dxh = dn * gv
                dx = r * (dxh - xh * jnp.mean(dxh * xh, axis=-1, keepdims=True)) + d_ref[rows, :]
                dx_ref[rows, :] = dx
                dxs_ref[rows, :] = (scale * dx).astype(BF16)
                return dg + jnp.sum(dn * xh, axis=0, keepdims=True)

            _accumulate(dg_ref, lax.fori_loop(0, tm // rc, piece, jnp.zeros((1, D), F32)), m == 0)

    row = lambda m, b: (m, 0)
    fix = lambda m, b: (0, 0)
    w_spec = (w3, (None, D, ns), lambda m, b: (b, 0, 0)) + (("single",) if nb == 1 else ())
    return _call(body, name, (M // tm, nb),
                 [(a, a_bs, a_im), w_spec,
                  (x, (tm, D), row), (g.reshape(1, D), (1, D), fix), (d, (tm, D), row)],
                 [((M, D), F32, (tm, D), row), ((M, D), BF16, (tm, D), row), ((1, D), F32, (1, D), fix)],
                 scratch=[pltpu.VMEM((tm, D), F32)], exchange=exchange)


def mm_tn(a, b_in, name, nbo, ns, tka_pref=1024, tt_pref=2048, tn_pref=2048, group=1, exchange=None):
    T, Ka = a.shape
    tt = _div(T, tt_pref, BF16_SUBLANES)
    tka = _div(Ka, tka_pref, LANES)
    if isinstance(b_in, tuple):
        b, b_bs, b_im = b_in
        b_bs = tuple(tt if s == "tt" else s for s in b_bs)
        tn, per = ns, 1
    else:
        tn = ns if group > 1 else _div(ns, tn_pref, LANES)
        per = ns // tn
        b, b_bs, b_im = b_in, (tt, group * tn), lambda i, j, t: (t, j)
    nt = T // tt

    def body(a_ref, b_ref, o_ref, acc_ref):
        t = pl.program_id(2)
        product = lambda: _dot(a_ref[...], b_ref[...], _TN)

        def store(total):
            for k in range(group):
                o_ref[k] = total[:, k * tn:(k + 1) * tn].astype(BF16)

        if nt == 1:
            store(product())
        else:
            _accumulate_product(acc_ref, t == 0, product, skip=t == nt - 1)

            @pl.when(t == nt - 1)
            def _():
                store(acc_ref[...] + product())

    return _call(body, name, (Ka // tka, nbo * per // group, nt),
                 [(a, (tt, tka), lambda i, j, t: (t, i)), (b, b_bs, b_im)],
                 [((nbo, Ka, ns), BF16, (group, tka, tn), lambda i, j, t: (j // per, i, j % per))],
                 scratch=[pltpu.VMEM((tka, group * tn), F32)], exchange=exchange)[0]


def ffn_up(xn, w13, exchange=None):
    T, D = xn.shape
    nb, _, ns = w13.shape
    half = nb // 2
    F = half * ns
    tm = _div(T, 512, BF16_SUBLANES)
    pair = 2 if half % 2 == 0 else 1

    def columns(w_ref, c0, cw):
        k, off = divmod(c0, ns)
        if off + cw <= ns:
            return w_ref[k, :, off:off + cw]
        return jnp.concatenate([w_ref[k, :, off:ns], w_ref[k + 1, :, 0:off + cw - ns]], axis=1)

    def body(x_ref, wg_ref, wu_ref, fac_ref, act_ref):
        xv = x_ref[...]
        for c0, cw in _chunks(pair * ns, MXU_WIDTH):
            cols = slice(c0, c0 + cw)
            gate = _dot(xv, columns(wg_ref, c0, cw))
            up = _dot(xv, columns(wu_ref, c0, cw))
            s = _sigmoid(gate)
            silu = gate * s
            fac_ref[0, :, cols] = (up * (s * (1.0 + gate * (1.0 - s)))).astype(BF16)
            fac_ref[1, :, cols] = silu.astype(BF16)
            act_ref[:, cols] = (silu * up).astype(BF16)

    tn = pair * ns
    return _call(body, "ffn_up", (half // pair, T // tm),
                 [(xn, (tm, D), lambda j, m: (m, 0)),
                  (w13, (pair, D, ns), lambda j, m: (j, 0, 0), "single"),
                  (w13, (pair, D, ns), lambda j, m: (j + half // pair, 0, 0), "single")],
                 [((2, T, F), BF16, (2, tm, tn), lambda j, m: (0, m, j)),
                  ((T, F), BF16, (tm, tn), lambda j, m: (m, j))], exchange=exchange)


def ffn_dact(dy, w2, fac, exchange=None):
    T, D = dy.shape
    F = w2.shape[0]
    tm = _div(T, 512, BF16_SUBLANES)
    tn = _div(F, F // 2, MXU_WIDTH)

    def body(dy_ref, w_ref, fac_ref, dh_ref):
        dyv = dy_ref[...]
        for c0, cw in _chunks(tn, MXU_WIDTH):
            cols = slice(c0, c0 + cw)
            da = _dot(dyv, w_ref[cols, :], _NT)
            dh_ref[0, :, cols] = (da * fac_ref[0, :, cols].astype(F32)).astype(BF16)
            dh_ref[1, :, cols] = (da * fac_ref[1, :, cols].astype(F32)).astype(BF16)

    return _call(body, "ffn_dact", (F // tn, T // tm),
                 [(dy, (tm, D), lambda j, m: (m, 0)), (w2, (tn, D), lambda j, m: (j, 0), "single"),
                  (fac, (2, tm, tn), lambda j, m: (0, m, j))],
                 [((2, T, F), BF16, (2, tm, tn), lambda j, m: (0, m, j))], exchange=exchange)[0]


def ffn_fwd(x, norm_g, get_w13, get_w2, norm_exchange=None, up_exchange=None, down_exchange=None):
    xn = rms_fwd(x, norm_g, exchange=norm_exchange)
    fac, act = ffn_up(xn, get_w13(), exchange=up_exchange)
    F = act.shape[1]
    y = mm_nn(act, get_w2().reshape(1, F, -1), "ffn_down", F32, res=x, res_scale=0.5, tm_pref=512,
              exchange=down_exchange)
    return y, (x, xn, fac, act)


def ffn_bwd(d, dys, saved, norm_g, w13, w2, scale_out):
    x, xn, fac, act = saved
    T, D = x.shape
    nb, _, ns = w13.shape
    half = nb // 2
    F = half * ns
    dh = ffn_dact(dys, w2.reshape(F, D), fac)
    dw2 = mm_tn(act, dys, "ffn_dw2", 1, D, tka_pref=ns, tn_pref=1024)
    send_w2 = scatter_exchange([dw2.reshape(N_DEV, F // N_DEV, D)])
    pair = 2 if half % 2 == 0 else 1
    per = half // pair
    dw13 = mm_tn(xn, (dh, (None, "tt", pair * ns), lambda i, j, t: (j // per, t, j % per)), "ffn_dw13", nb, ns,
                 tka_pref=512, group=pair, exchange=send_w2)
    send_w13 = scatter_exchange([dw13])
    dxn = mm_nt((dh, (None, "tm", pair * ns), lambda m, o, b: (b // per, m, b % per)), w13, "ffn_dxn", BF16, T,
                tm_pref=512, group=pair, exchange=send_w13)
    dx, dxs, dg = rms_bwd(x, norm_g, dxn, d, scale_out)
    return dx, dxs, dg, send_w13.results[0], send_w2.results[0]


def _gmlp_parts(p_ref, lng_ref, lnb_ref):
    E = lng_ref.shape[-1]
    pv = p_ref[...].astype(F32)
    cdf = _normal_cdf(pv)
    z = pv * cdf
    u = z[:, :E]
    vp = z[:, E:]
    mu = jnp.mean(vp, axis=-1, keepdims=True)
    xc = vp - mu
    rstd = lax.rsqrt(jnp.mean(xc * xc, axis=-1, keepdims=True) + LN_EPS)
    vh = xc * rstd
    v = vh * lng_ref[...] + lnb_ref[...]
    return u, vh, rstd, v, pv, cdf


def _causal_ws(ws_ref, g):
    keep = lax.broadcasted_iota(jnp.int32, (CHUNK, CHUNK), 0) >= lax.broadcasted_iota(jnp.int32, (CHUNK, CHUNK), 1)
    return jnp.where(keep, ws_ref[g], 0.0).astype(BF16), keep


def gmlp_mid_fwd(p, ln_g, ln_b, w_s, bias_full):
    T, E2 = p.shape
    E = E2 // 2
    gd = E // GROUPS
    tm = _div(T, 256, CHUNK)
    fix2 = lambda i: (0, 0)

    def body(p_ref, lng_ref, lnb_ref, ws_ref, bias_ref, o_ref):
        u, _, _, v, _, _ = _gmlp_parts(p_ref, lng_ref, lnb_ref)
        vb = v.astype(BF16)
        for g in range(GROUPS):
            wm, _ = _causal_ws(ws_ref, g)
            cols = slice(g * gd, (g + 1) * gd)
            for c in range(tm // CHUNK):
                rows = slice(c * CHUNK, (c + 1) * CHUNK)
                f = _dot(wm, vb[rows, cols]) + bias_ref[:, cols]
                o_ref[rows, cols] = (u[rows, cols] * f).astype(BF16)

    return _call(body, "gmlp_mid_fwd", (T // tm,),
                 [(p, (tm, E2), lambda i: (i, 0)), (ln_g, (1, E), fix2), (ln_b, (1, E), fix2),
                  (w_s, (GROUPS, CHUNK, CHUNK), lambda i: (0, 0, 0)), (bias_full, (CHUNK, E), fix2)],
                 [((T, E), BF16, (tm, E), lambda i: (i, 0))])[0]


def gmlp_mid_bwd(p, dgated, ln_g, ln_b, w_s, bias_full, exchange=None):
    T, E2 = p.shape
    E = E2 // 2
    gd = E // GROUPS
    tm = _div(T, 256, CHUNK)
    nsteps = T // tm
    fix2 = lambda i: (0, 0)
    fix3 = lambda i: (0, 0, 0)

    def body(p_ref, dg_ref, lng_ref, lnb_ref, ws_ref, bias_ref,
             dp_ref, dws_ref, dbs_ref, dlng_ref, dlnb_ref, f_sc, dv_sc, db_sc):
        i = pl.program_id(0)
        first = i == 0
        u, vh, rstd, v, pv, cdf = _gmlp_parts(p_ref, lng_ref, lnb_ref)
        vb = v.astype(BF16)
        dgt = dg_ref[...].astype(F32)
        df = dgt * u
        dfb = df.astype(BF16)
        for g in range(GROUPS):
            wm, keep = _causal_ws(ws_ref, g)
            cols = slice(g * gd, (g + 1) * gd)
            dw = None
            dbg = None
            for c in range(tm // CHUNK):
                rows = slice(c * CHUNK, (c + 1) * CHUNK)
                f_sc[rows, cols] = _dot(wm, vb[rows, cols]) + bias_ref[:, cols]
                dv_sc[rows, cols] = _dot(wm, dfb[rows, cols], _TN)
                part = _dot(dfb[rows, cols], vb[rows, cols], _NT)
                dw = part if dw is None else dw + part
                dbg = df[rows, cols] if dbg is None else dbg + df[rows, cols]
            dw = jnp.where(keep, dw, 0.0)

            @pl.when(first)
            def _():
                dws_ref[g] = dw
                db_sc[:, cols] = dbg

            @pl.when(jnp.logical_not(first))
            def _():
                dws_ref[g] += dw
                db_sc[:, cols] += dbg

        du = dgt * f_sc[...]
        dv = dv_sc[...]
        _accumulate(dlng_ref, jnp.sum(dv * vh, axis=0, keepdims=True), first)
        _accumulate(dlnb_ref, jnp.sum(dv, axis=0, keepdims=True), first)
        dvh = dv * lng_ref[...]
        dvp = rstd * (dvh - jnp.mean(dvh, axis=-1, keepdims=True)
                      - vh * jnp.mean(dvh * vh, axis=-1, keepdims=True))
        gp = cdf + pv * _normal_pdf(pv)
        dp_ref[:, :E] = (du * gp[:, :E]).astype(BF16)
        dp_ref[:, E:] = (dvp * gp[:, E:]).astype(BF16)

        @pl.when(i == nsteps - 1)
        def _():
            for g in range(GROUPS):
                tot = jnp.sum(db_sc[:, g * gd:(g + 1) * gd], axis=-1, keepdims=True)
                dbs_ref[g] = jnp.broadcast_to(tot, (CHUNK, LANES))

    return _call(body, "gmlp_mid_bwd", (nsteps,),
                 [(p, (tm, E2), lambda i: (i, 0)), (dgated, (tm, E), lambda i: (i, 0)),
                  (ln_g, (1, E), fix2), (ln_b, (1, E), fix2),
                  (w_s, (GROUPS, CHUNK, CHUNK), fix3), (bias_full, (CHUNK, E), fix2)],
                 [((T, E2), BF16, (tm, E2), lambda i: (i, 0)),
                  ((GROUPS, CHUNK, CHUNK), F32, (GROUPS, CHUNK, CHUNK), fix3),
                  ((GROUPS, CHUNK, LANES), F32, (GROUPS, CHUNK, LANES), fix3),
                  ((1, E), F32, (1, E), fix2), ((1, E), F32, (1, E), fix2)],
                 scratch=[pltpu.VMEM((tm, E), F32), pltpu.VMEM((tm, E), F32), pltpu.VMEM((CHUNK, E), F32)],
                 exchange=exchange)


HALO = 16


def _row_of(block, r):
    rows = lax.broadcasted_iota(jnp.int32, block.shape, 0)
    return jnp.sum(jnp.where(rows == r, block, 0.0), axis=0, keepdims=True)


def _shift_down(z, k, fill):
    out = pltpu.roll(z, k, 0)
    rows = lax.broadcasted_iota(jnp.int32, z.shape, 0)
    for t in range(k):
        out = jnp.where(rows == t, fill[t], out)
    return out


def _shift_up(z, k, fill):
    n = z.shape[0]
    out = pltpu.roll(z, n - k, 0)
    rows = lax.broadcasted_iota(jnp.int32, z.shape, 0)
    for j in range(k):
        out = jnp.where(rows == n - k + j, fill[j], out)
    return out


def _conv_parts(p_ref, prev_ref, cw_ref, is_first):
    D = cw_ref.shape[-1]
    pv = p_ref[...].astype(F32)
    bg, cg, val = pv[:, :D], pv[:, D:2 * D], pv[:, 2 * D:]
    z = cg * val
    pp = prev_ref[...].astype(F32)
    zp = jnp.where(is_first, 0.0, pp[:, D:2 * D] * pp[:, 2 * D:])
    zl1 = _row_of(zp, HALO - 1)
    zl2 = _row_of(zp, HALO - 2)
    z1 = _shift_down(z, 1, [zl1])
    z2 = _shift_down(z, 2, [zl2, zl1])
    conv = z2 * cw_ref[0:1, :] + z1 * cw_ref[1:2, :] + z * cw_ref[2:3, :]
    return bg, cg, val, z, z1, z2, conv


def conv_mid_fwd(p, cw):
    T, D3 = p.shape
    D = D3 // 3
    tm = _div(T, 256, HALO)
    per = tm // HALO

    def body(p_ref, prev_ref, cw_ref, o_ref):
        bg, _, _, _, _, _, conv = _conv_parts(p_ref, prev_ref, cw_ref, pl.program_id(0) == 0)
        o_ref[...] = (bg * conv).astype(BF16)

    return _call(body, "conv_mid_fwd", (T // tm,),
                 [(p, (tm, D3), lambda i: (i, 0)),
                  (p, (HALO, D3), lambda i: (jnp.maximum(i * per - 1, 0), 0)),
                  (cw, (CONV_WIDTH, D), lambda i: (0, 0))],
                 [((T, D), BF16, (tm, D), lambda i: (i, 0))])[0]


def conv_mid_bwd(p, dgated, cw, exchange=None):
    T, D3 = p.shape
    D = D3 // 3
    tm = _div(T, 256, HALO)
    per = tm // HALO
    nsteps = T // tm
    last_halo = T // HALO - 1
    nxt = lambda i: (jnp.minimum((i + 1) * per, last_halo), 0)

    def body(p_ref, prev_ref, next_ref, dg_ref, dgn_ref, cw_ref, dp_ref, dcw_ref):
        i = pl.program_id(0)
        bg, cg, val, z, z1, z2, conv = _conv_parts(p_ref, prev_ref, cw_ref, i == 0)
        dgt = dg_ref[...].astype(F32)
        dconv = dgt * bg
        dcn = jnp.where(i == nsteps - 1, 0.0, dgn_ref[...].astype(F32) * next_ref[:, :D].astype(F32))
        n0 = _row_of(dcn, 0)
        n1 = _row_of(dcn, 1)
        up1 = _shift_up(dconv, 1, [n0])
        up2 = _shift_up(dconv, 2, [n0, n1])
        dz = dconv * cw_ref[2:3, :] + up1 * cw_ref[1:2, :] + up2 * cw_ref[0:1, :]
        dp_ref[:, :D] = (dgt * conv).astype(BF16)
        dp_ref[:, D:2 * D] = (dz * val).astype(BF16)
        dp_ref[:, 2 * D:] = (dz * cg).astype(BF16)
        first = i == 0
        parts = (jnp.sum(dconv * z2, axis=0, keepdims=True), jnp.sum(dconv * z1, axis=0, keepdims=True),
                 jnp.sum(dconv * z, axis=0, keepdims=True))

        @pl.when(first)
        def _():
            for k in range(CONV_WIDTH):
                dcw_ref[k:k + 1, :] = parts[k]

        @pl.when(jnp.logical_not(first))
        def _():
            for k in range(CONV_WIDTH):
                dcw_ref[k:k + 1, :] += parts[k]

    return _call(body, "conv_mid_bwd", (nsteps,),
                 [(p, (tm, D3), lambda i: (i, 0)),
                  (p, (HALO, D3), lambda i: (jnp.maximum(i * per - 1, 0), 0)),
                  (p, (HALO, D3), nxt),
                  (dgated, (tm, D), lambda i: (i, 0)),
                  (dgated, (HALO, D), nxt),
                  (cw, (CONV_WIDTH, D), lambda i: (0, 0))],
                 [((T, D3), BF16, (tm, D3), lambda i: (i, 0)),
                  ((CONV_WIDTH, D), F32, (CONV_WIDTH, D), lambda i: (0, 0))], exchange=exchange)


def mixer_bwd_common(d, dys, saved, norm_g, w_in, w_out, mid_bwd, scale_out, dwkv):
    x, hn, p, gated = saved
    T, D = x.shape
    E = gated.shape[1]
    w_out3 = w_out.reshape(1, E, D)
    dgated = mm_nt(dys, w_out3, "mix_dgated", BF16, T)
    dw_out = mm_tn(gated, dys, "mix_dwout", 1, D)
    send_wout = scatter_exchange([dw_out.reshape(N_DEV, E // N_DEV, D)])
    dp, extra = mid_bwd(p, dgated, send_wout)
    nb, _, ns = w_in.shape
    group = max(g for g in (1, 2, 4, 8) if nb % g == 0 and g * ns <= D)
    send_wkv = scatter_exchange([dwkv])
    dw_in = mm_tn(hn, dp, "mix_dwin", nb, ns, group=group, exchange=send_wkv)
    send_win = scatter_exchange([dw_in])
    dhn = mm_nt(dp, w_in, "mix_dhn", BF16, T, tm_pref=512, group=group, exchange=send_win)
    dx, dxs, dg = rms_bwd(x, norm_g, dhn, d, scale_out)
    return dx, dxs, dg, send_win.results[0], send_wout.results[0], send_wkv.results[0], extra


def _softmax_rows(s):
    e = jnp.exp(s - jnp.max(s, axis=-1, keepdims=True))
    return e / jnp.sum(e, axis=-1, keepdims=True)


def attn_fwd(q, kv):
    T, D = q.shape
    M = kv.shape[0]
    hd = D // HEADS
    scale = hd ** -0.5
    tm = _div(T, 512, BF16_SUBLANES)

    def body(q_ref, kv_ref, o_ref):
        for h in range(HEADS):
            cols = slice(h * hd, (h + 1) * hd)
            s = _dot(q_ref[:, cols], kv_ref[:, cols], _NT) * scale
            pr = _softmax_rows(s).astype(BF16)
            o_ref[:, cols] = _dot(pr, kv_ref[:, D + h * hd:D + (h + 1) * hd]).astype(BF16)

    return _call(body, "attn_fwd", (T // tm,),
                 [(q, (tm, D), lambda i: (i, 0)), (kv, (M, 2 * D), lambda i: (0, 0))],
                 [((T, D), BF16, (tm, D), lambda i: (i, 0))])[0]


def attn_bwd(q, do, kv, exchange=None):
    T, D = q.shape
    M = kv.shape[0]
    hd = D // HEADS
    scale = hd ** -0.5
    tm = _div(T, 512, BF16_SUBLANES)
    nsteps = T // tm

    def body(q_ref, do_ref, kv_ref, dq_ref, dkv_ref, acc_ref):
        i = pl.program_id(0)

        @pl.when(i == 0)
        def _():
            acc_ref[...] = jnp.zeros(acc_ref.shape, F32)

        for h in range(HEADS):
            cols = slice(h * hd, (h + 1) * hd)
            vcols = slice(D + h * hd, D + (h + 1) * hd)
            qh = q_ref[:, cols]
            kh = kv_ref[:, cols]
            doh = do_ref[:, cols]
            pr = _softmax_rows(_dot(qh, kh, _NT) * scale)
            dpr = _dot(doh, kv_ref[:, vcols], _NT)
            ds = (pr * (dpr - jnp.sum(dpr * pr, axis=-1, keepdims=True)) * scale).astype(BF16)
            dq_ref[:, cols] = _dot(ds, kh).astype(BF16)
            acc_ref[:, cols] += _dot(ds, qh, _TN)
            acc_ref[:, vcols] += _dot(pr.astype(BF16), doh, _TN)

        @pl.when(i == nsteps - 1)
        def _():
            dkv_ref[...] = acc_ref[...].astype(BF16)

    return _call(body, "attn_bwd", (nsteps,),
                 [(q, (tm, D), lambda i: (i, 0)), (do, (tm, D), lambda i: (i, 0)),
                  (kv, (M, 2 * D), lambda i: (0, 0))],
                 [((T, D), BF16, (tm, D), lambda i: (i, 0)), ((M, 2 * D), BF16, (M, 2 * D), lambda i: (0, 0))],
                 scratch=[pltpu.VMEM((M, 2 * D), F32)], exchange=exchange)


def xattn_fwd(x, mem, xnorm_g, mnorm_g, wq, wkv, wo):
    D = x.shape[1]
    hq = rms_fwd(x, xnorm_g)
    mn = rms_fwd(mem, mnorm_g)
    q = mm_nn(hq, wq.reshape(1, D, D), "xattn_q")
    kv = mm_nn(mn, wkv, "xattn_kv")
    o = attn_fwd(q, kv)
    y = mm_nn(o, wo.reshape(1, D, D), "xattn_out", F32, res=x)
    return y, (x, hq, mn, q, kv, o)


def xattn_bwd(d, dys, saved, mem, xnorm_g, wq, wkv, wo, scale_out):
    x, hq, mn, q, kv, o = saved
    T, D = x.shape
    M = mem.shape[0]
    do = mm_nt(dys, wo.reshape(1, D, D), "xattn_do", BF16, T)
    rows = D // N_DEV
    dwo = mm_tn(o, dys, "xattn_dwo", 1, D)
    send_wo = scatter_exchange([dwo.reshape(N_DEV, rows, D)])
    dq, dkv = attn_bwd(q, do, kv, exchange=send_wo)
    nb, _, ns = wkv.shape
    dwkv = mm_tn(mn, dkv, "xattn_dwkv", nb, ns)
    dmn = mm_nt(dkv, wkv, "xattn_dmn", F32, M)
    dgm = rms_gain_grad(mem, dmn)
    dwq = mm_tn(hq, dq, "xattn_dwq", 1, D)
    send_wq = scatter_exchange([dwq.reshape(N_DEV, rows, D)])
    dx, dxs, dgx = mm_nt_rms_bwd(dq, wq.reshape(1, D, D), "xattn_dhq", x, xnorm_g, d, scale_out,
                                 exchange=send_wq)
    return dx, dxs, dgx, dgm, send_wq.results[0], dwkv, send_wo.results[0]


def _mesh_places():
    x, y, c = lax.axis_index("x"), lax.axis_index("y"), lax.axis_index("c")
    chips = [(1 - x, y), (x, 1 - y), (1 - x, 1 - y)]
    return (x, y, c), (x, y, 1 - c), chips


def _slot(place):
    return 4 * place[0] + 2 * place[1] + place[2]


def _exchange_sems(n):
    return [pltpu.SemaphoreType.DMA((n * N_PEERS,)), pltpu.SemaphoreType.DMA((n * N_PEERS,)),
            pltpu.SemaphoreType.DMA((n,))]


def gather_exchange(shards):
    n = len(shards)
    shapes = [a.shape if l is None else a.shape[1:] for a, l in shards]

    def parts(x_in, x_out, sems):
        ins = [r if l is None else r.at[l] for r, (_, l) in zip(x_in, shards)]
        send_sems, recv_sems, local_sems = sems
        me, sibling, chips = _mesh_places()

        def copy(a, k, block, to, src=None):
            dst = x_out[a].at[_slot(block)]
            return pltpu.make_async_remote_copy(
                src_ref=dst if src is None else src, dst_ref=dst,
                send_sem=send_sems.at[a * N_PEERS + k], recv_sem=recv_sems.at[a * N_PEERS + k],
                device_id=to, device_id_type=pl.DeviceIdType.MESH)

        mine = [pltpu.make_async_copy(ins[a], x_out[a].at[_slot(me)], local_sems.at[a]) for a in range(n)]
        first = []
        for a in range(n):
            first.append(copy(a, 0, me, sibling, src=ins[a]))
            first += [copy(a, 1 + j, me, (*chip, me[2]), src=ins[a]) for j, chip in enumerate(chips)]
        return me, sibling, chips, copy, mine, first

    def start(x_in, x_out, sems):
        _, _, _, _, mine, first = parts(x_in, x_out, sems)
        for cp in mine + first:
            cp.start()

    def forward(x_in, x_out, sems):
        me, sibling, chips, copy, _, _ = parts(x_in, x_out, sems)
        for j, chip in enumerate(chips):
            for a in range(n):
                copy(a, 1 + j, (*chip, me[2]), me).wait_recv()
                copy(a, 4 + j, (*chip, me[2]), sibling).start()

    def finish(x_in, x_out, sems):
        me, sibling, chips, copy, mine, first = parts(x_in, x_out, sems)
        for a in range(n):
            copy(a, 0, sibling, me).wait_recv()
        for j, chip in enumerate(chips):
            for a in range(n):
                copy(a, 4 + j, (*chip, 1 - me[2]), me).wait_recv()
        for cp in first:
            cp.wait_send()
        for j, chip in enumerate(chips):
            for a in range(n):
                copy(a, 4 + j, (*chip, me[2]), sibling).wait_send()
        for cp in mine:
            cp.wait()

    return Exchange([a for a, _ in shards],
                    [jax.ShapeDtypeStruct((N_DEV,) + tuple(s), a.dtype) for s, (a, _) in zip(shapes, shards)],
                    _exchange_sems(n), start, finish, forward)


def _all_peers(me, chips):
    c = me[2]
    return [(me[0], me[1], 1 - c)] + [(*chip, c) for chip in chips] + [(*chip, 1 - c) for chip in chips]


def scatter_exchange(grads):
    n = len(grads)

    def parts(x_in, x_out, sems):
        send_sems, recv_sems, local_sems = sems
        me, _, chips = _mesh_places()
        mine = [pltpu.make_async_copy(x_in[a].at[_slot(me)], x_out[a].at[_slot(me)], local_sems.at[a])
                for a in range(n)]
        sends, recvs = [], []
        for a in range(n):
            for k, peer in enumerate(_all_peers(me, chips)):
                sem = dict(send_sem=send_sems.at[a * N_PEERS + k], recv_sem=recv_sems.at[a * N_PEERS + k],
                           device_id=peer, device_id_type=pl.DeviceIdType.MESH)
                sends.append(pltpu.make_async_remote_copy(
                    src_ref=x_in[a].at[_slot(peer)], dst_ref=x_out[a].at[_slot(me)], **sem))
                recvs.append(pltpu.make_async_remote_copy(
                    src_ref=x_in[a].at[_slot(peer)], dst_ref=x_out[a].at[_slot(peer)], **sem))
        return mine, sends, recvs

    def start(x_in, x_out, sems):
        mine, sends, _ = parts(x_in, x_out, sems)
        for cp in mine + sends:
            cp.start()

    def finish(x_in, x_out, sems):
        mine, sends, recvs = parts(x_in, x_out, sems)
        for cp in recvs:
            cp.wait_recv()
        for cp in sends:
            cp.wait_send()
        for cp in mine:
            cp.wait()

    return Exchange(grads, [jax.ShapeDtypeStruct(g.shape, g.dtype) for g in grads],
                    _exchange_sems(n), start, finish)


def small_all_reduce(vec):
    R = vec.shape[0]

    def body(v_ref, o_ref, all_ref, send_sems, recv_sems):
        me, _, chips = _mesh_places()
        peers = _all_peers(me, chips)
        all_ref[_slot(me)] = v_ref[...]
        sends, recvs = [], []
        for k, peer in enumerate(peers):
            sem = dict(send_sem=send_sems.at[k], recv_sem=recv_sems.at[k],
                       device_id=peer, device_id_type=pl.DeviceIdType.MESH)
            sends.append(pltpu.make_async_remote_copy(src_ref=v_ref, dst_ref=all_ref.at[_slot(me)], **sem))
            recvs.append(pltpu.make_async_remote_copy(src_ref=v_ref, dst_ref=all_ref.at[_slot(peer)], **sem))
        for cp in sends:
            cp.start()
        for cp in recvs:
            cp.wait_recv()
        for cp in sends:
            cp.wait_send()
        acc = all_ref[0]
        for s in range(1, N_DEV):
            acc = acc + all_ref[s]
        o_ref[...] = acc

    return pl.pallas_call(
        body, name="small_all_reduce",
        out_shape=jax.ShapeDtypeStruct(vec.shape, F32),
        in_specs=[pl.BlockSpec(memory_space=pltpu.VMEM)], out_specs=pl.BlockSpec(memory_space=pltpu.VMEM),
        scratch_shapes=[pltpu.VMEM((N_DEV, R, LANES), F32), pltpu.SemaphoreType.DMA((N_PEERS,)),
                        pltpu.SemaphoreType.DMA((N_PEERS,))],
    )(vec)


def _adamw_math(w, g, m, v):
    m2 = ADAM_B1 * m + (1.0 - ADAM_B1) * g
    v2 = ADAM_B2 * v + (1.0 - ADAM_B2) * (g * g)
    m_hat = m2 / (1.0 - ADAM_B1 ** ADAM_STEP)
    v_hat = v2 / (1.0 - ADAM_B2 ** ADAM_STEP)
    delta = -ADAM_LR * (m_hat / (jnp.sqrt(v_hat) + ADAM_EPS) + ADAM_WD * w)
    return delta, m2, v2


def adamw_sharded(partials, w, m, v):
    L, r, c = w.shape
    row_bytes = 2 * c * (L * N_DEV * 2 + 7 * 4)
    tr = _div(r, max(BF16_SUBLANES, min(512, (VMEM_LIMIT_BYTES * 3 // 4) // row_bytes)), BF16_SUBLANES)
    nt = r // tr

    def part_map(l0):
        return lambda l, t: (0, jnp.where(l == l0, t, jnp.where(l < l0, 0, nt - 1)), 0)

    def body(*refs):
        parts = refs[:L]
        w_ref, m_ref, v_ref, g_out, d_out, m_out, v_out = refs[L:]
        layer = pl.program_id(0)
        for l0 in range(L):
            @pl.when(layer == l0)
            def _():
                g = parts[l0][0].astype(F32)
                for s in range(1, N_DEV):
                    g = g + parts[l0][s].astype(F32)
                delta, m2, v2 = _adamw_math(w_ref[...], g, m_ref[...], v_ref[...])
                g_out[...] = g
                d_out[...] = delta
                m_out[...] = m2
                v_out[...] = v2

    own = lambda l, t: (l, t, 0)
    return _call(body, "adamw_sharded", (L, nt),
                 [(p, (N_DEV, tr, c), part_map(l0)) for l0, p in enumerate(partials)]
                 + [(w, (None, tr, c), own), (m, (None, tr, c), own), (v, (None, tr, c), own)],
                 [((L, r, c), F32, (None, tr, c), own)] * 4)


def adamw_flat(g, w, m, v):
    shape = g.shape

    def body(g_ref, w_ref, m_ref, v_ref, d_out, m_out, v_out):
        delta, m2, v2 = _adamw_math(w_ref[...], g_ref[...], m_ref[...], v_ref[...])
        d_out[...] = delta
        m_out[...] = m2
        v_out[...] = v2

    whole = lambda: (0, 0)
    return _call(body, "adamw_flat", (), [(a, shape, whole) for a in (g, w, m, v)],
                 [(shape, F32, shape, whole)] * 3)


def _pack(parts):
    flat = jnp.concatenate([p.reshape(-1).astype(F32) for p in parts])
    rows = -(-flat.shape[0] // (8 * LANES)) * 8
    return jnp.pad(flat, (0, rows * LANES - flat.shape[0])).reshape(rows, LANES)


def _unpack(packed, shapes):
    flat = packed.reshape(-1)
    out, off = [], 0
    for s in shapes:
        size = math.prod(s)
        out.append(flat[off:off + size].reshape(s))
        off += size
    return out


def kernel(x, mem, ffn1_norm, ffn1_w13, ffn1_w2, mix_norm, gmlp_w_in, gmlp_ln_g, gmlp_ln_b, gmlp_w_s, gmlp_b_s, gmlp_w_out, conv_w_in, conv_w, conv_w_out, xattn_norm, mem_norm, xattn_wq, xattn_wkv, xattn_wo, ffn2_norm, ffn2_w13, ffn2_w2, final_norm, loss_target, m_ffn1_norm, m_ffn1_w13, m_ffn1_w2, m_mix_norm, m_gmlp_w_in, m_gmlp_ln_g, m_gmlp_ln_b, m_gmlp_w_s, m_gmlp_b_s, m_gmlp_w_out, m_conv_w_in, m_conv_w, m_conv_w_out, m_xattn_norm, m_mem_norm, m_xattn_wq, m_xattn_wkv, m_xattn_wo, m_ffn2_norm, m_ffn2_w13, m_ffn2_w2, m_final_norm, v_ffn1_norm, v_ffn1_w13, v_ffn1_w2, v_mix_norm, v_gmlp_w_in, v_gmlp_ln_g, v_gmlp_ln_b, v_gmlp_w_s, v_gmlp_b_s, v_gmlp_w_out, v_conv_w_in, v_conv_w, v_conv_w_out, v_xattn_norm, v_mem_norm, v_xattn_wq, v_xattn_wkv, v_xattn_wo, v_ffn2_norm, v_ffn2_w13, v_ffn2_w2, v_final_norm):
    given = dict(locals())
    T, D = x.shape[1], x.shape[2]
    depth = ffn1_norm.shape[0]
    xs = x.reshape(T, D)
    mems = mem.reshape(mem.shape[1], D)
    target = loss_target.reshape(T, D)
    me = 4 * lax.axis_index("x") + 2 * lax.axis_index("y") + lax.axis_index("c")
    E = gmlp_ln_g.shape[1]
    gd = E // GROUPS
    cshard = conv_w.shape[2]

    bf = {k: given[k].astype(BF16) for k in
          ("ffn1_w13", "ffn1_w2", "gmlp_w_in", "gmlp_w_out", "conv_w_in", "conv_w_out",
           "xattn_wq", "xattn_wkv", "xattn_wo", "ffn2_w13", "ffn2_w2")}

    W = {}

    def gather(names_layers):
        return names_layers, gather_exchange([(bf[k], l) for k, l in names_layers])

    def landed(tagged):
        names_layers, exchange = tagged
        W.update(zip(names_layers, exchange.results))

    bias_full = jnp.repeat(gmlp_b_s[0].T, gd, axis=1)

    first = gather_exchange([(bf["ffn1_w13"], 0), (conv_w, 0)])
    taps = {}

    def first_w13():
        W["ffn1_w13", 0], cw_shards = first.results
        taps["conv"] = jnp.transpose(cw_shards, (1, 0, 2)).reshape(CONV_WIDTH, D)
        return W["ffn1_w13", 0]

    saved = []
    h = xs
    for i in range(depth):
        j = i // 2
        is_gmlp = i % 2 == 0
        mix = ("gmlp_w_in", "gmlp_w_out") if is_gmlp else ("conv_w_in", "conv_w_out")
        on_up1 = gather(([("ffn1_w2", i)] if i == 0 else []) + [(mix[0], j), ("ffn2_w13", i)])
        on_down1 = gather([(mix[1], j), ("xattn_wq", i), ("xattn_wkv", i)])

        def w2_after_up(on_up=on_up1, i=i):
            landed(on_up)
            return W["ffn1_w2", i]

        h, sv1 = ffn_fwd(h, ffn1_norm[i], first_w13 if i == 0 else (lambda i=i: W["ffn1_w13", i]), w2_after_up,
                         first if i == 0 else None, on_up1[1], on_down1[1])
        landed(on_down1)
        hn = rms_fwd(h, mix_norm[i])
        on_mix_in = gather([("xattn_wo", i), ("ffn2_w2", i)])
        p = mm_nn(hn, W[mix[0], j], "mix_in", group=2, exchange=on_mix_in[1])
        landed(on_mix_in)
        if is_gmlp:
            gated = gmlp_mid_fwd(p, gmlp_ln_g[j:j + 1], gmlp_ln_b[j:j + 1], gmlp_w_s[j], bias_full)
        else:
            gated = conv_mid_fwd(p, taps["conv"])
        h_mix = mm_nn(gated, W[mix[1], j].reshape(1, gated.shape[1], D), "mix_out", F32, res=h)
        sv2 = (h, hn, p, gated)
        h, sv3 = xattn_fwd(h_mix, mems, xattn_norm[i], mem_norm[i],
                           W["xattn_wq", i], W["xattn_wkv", i], W["xattn_wo", i])
        on_up2 = gather([("ffn1_w13", i + 1), ("ffn1_w2", i + 1)]) if i + 1 < depth else None
        h, sv4 = ffn_fwd(h, ffn2_norm[i], lambda i=i: W["ffn2_w13", i], lambda i=i: W["ffn2_w2", i],
                         up_exchange=None if on_up2 is None else on_up2[1])
        if on_up2 is not None:
            landed(on_up2)
        saved.append((sv1, sv2, sv3, sv4))

    loss_part, d, dys, d_final_norm = loss_head(h, final_norm, target, 0.5)

    small = {k: [None] * depth for k in ("ffn1_norm", "mix_norm", "xattn_norm", "mem_norm", "ffn2_norm")}
    partial = {}
    for i in reversed(range(depth)):
        j = i // 2
        is_gmlp = i % 2 == 0
        sv1, sv2, sv3, sv4 = saved[i]
        d, dys, small["ffn2_norm"][i], partial["ffn2_w13", i], partial["ffn2_w2", i] = ffn_bwd(
            d, dys, sv4, ffn2_norm[i], W["ffn2_w13", i], W["ffn2_w2", i], 1.0)
        (d, dys, small["xattn_norm"][i], small["mem_norm"][i], partial["xattn_wq", i],
         dwkv, partial["xattn_wo", i]) = xattn_bwd(
            d, dys, sv3, mems, xattn_norm[i], W["xattn_wq", i], W["xattn_wkv", i], W["xattn_wo", i], 1.0)
        if is_gmlp:
            mix = ("gmlp_w_in", "gmlp_w_out")
            mid = lambda p, dg, send: (lambda r: (r[0], r[1:]))(gmlp_mid_bwd(
                p, dg, gmlp_ln_g[j:j + 1], gmlp_ln_b[j:j + 1], gmlp_w_s[j], bias_full, exchange=send))
        else:
            mix = ("conv_w_in", "conv_w_out")
            mid = lambda p, dg, send: (lambda r: (r[0], r[1:]))(conv_mid_bwd(p, dg, taps["conv"], exchange=send))
        (d, dys, small["mix_norm"][i], partial[mix[0], j], partial[mix[1], j], partial["xattn_wkv", i],
         extra) = mixer_bwd_common(d, dys, sv2, mix_norm[i], W[mix[0], j], W[mix[1], j], mid, 0.5, dwkv)
        if is_gmlp:
            d_ws, d_bs_wide, d_lng, d_lnb = extra
        else:
            (d_cw,) = extra
        d, dys, small["ffn1_norm"][i], partial["ffn1_w13", i], partial["ffn1_w2", i] = ffn_bwd(
            d, dys, sv1, ffn1_norm[i], W["ffn1_w13", i], W["ffn1_w2", i], 0.5)
    grad_x = d.reshape(x.shape)

    small_grads = {k: jnp.concatenate(v, axis=0) for k, v in small.items()}
    small_grads["gmlp_ln_g"] = d_lng
    small_grads["gmlp_ln_b"] = d_lnb
    small_grads["gmlp_w_s"] = d_ws[None]
    small_grads["gmlp_b_s"] = d_bs_wide[None, :, :, 0]
    small_grads["final_norm"] = d_final_norm.reshape(-1)
    small_names = ["ffn1_norm", "mix_norm", "gmlp_ln_g", "gmlp_ln_b", "gmlp_w_s", "gmlp_b_s",
                   "xattn_norm", "mem_norm", "ffn2_norm", "final_norm"]
    summed = small_all_reduce(_pack([small_grads[k] for k in small_names] + [d_cw, loss_part]))
    parts = _unpack(summed, [given[k].shape for k in small_names] + [(CONV_WIDTH, D), (1, LANES)])
    grads = dict(zip(small_names, parts[:len(small_names)]))
    grads["conv_w"] = lax.dynamic_slice(parts[-2], (jnp.int32(0), me * cshard), (CONV_WIDTH, cshard))[None]
    loss = parts[-1][0, 0]
    flat_names = small_names + ["conv_w"]
    flat = adamw_flat(*[_pack([src[k] for k in flat_names]) for src in
                        (grads, given, {k: given["m_" + k] for k in flat_names},
                         {k: given["v_" + k] for k in flat_names})])
    delta, new_m, new_v = [dict(zip(flat_names, _unpack(f, [given[k].shape for k in flat_names]))) for f in flat]

    for k in bf:
        w = given[k]
        L = w.shape[0]
        shard = w.shape[1:]
        view = lambda a: a.reshape((L,) + shard)
        g, dl, m2, v2 = adamw_sharded([partial[k, l] for l in range(L)], w, given["m_" + k], given["v_" + k])
        grads[k], delta[k], new_m[k], new_v[k] = view(g), view(dl), view(m2), view(v2)

    order = ["ffn1_norm", "ffn1_w13", "ffn1_w2", "mix_norm", "gmlp_w_in", "gmlp_ln_g", "gmlp_ln_b", "gmlp_w_s",
             "gmlp_b_s", "gmlp_w_out", "conv_w_in", "conv_w", "conv_w_out", "xattn_norm", "mem_norm", "xattn_wq",
             "xattn_wkv", "xattn_wo", "ffn2_norm", "ffn2_w13", "ffn2_w2", "final_norm"]
    return (loss, grad_x, *[grads[k] for k in order], *[delta[k] for k in order],
            *[new_m[k] for k in order], *[new_v[k] for k in order])
```

```python
import math

import jax
import jax.numpy as jnp
from jax import lax
from jax.experimental import pallas as pl
from jax.experimental.pallas import tpu as pltpu

F32 = jnp.float32
BF16 = jnp.bfloat16

N_DEV = 8
N_PEERS = N_DEV - 1
CHUNK = 128
GROUPS = 8
HEADS = 4
CONV_WIDTH = 3
RMS_EPS = 1e-6
LN_EPS = 1e-5
ADAM_LR = 0.001
ADAM_B1 = 0.9
ADAM_B2 = 0.999
ADAM_EPS = 1e-08
ADAM_WD = 0.01
ADAM_STEP = 10
LANES = 128
BF16_SUBLANES = 16
MXU_WIDTH = 256
VMEM_LIMIT_BYTES = 56 * 1024 * 1024

_NT = (((1,), (1,)), ((), ()))
_TN = (((0,), (0,)), ((), ()))
_SQRT_HALF = 0.7071067811865476
_INV_SQRT_2PI = 0.3989422804014327


def _div(n, pref, align):
    best = None
    for t in range(align, min(n, pref) + 1, align):
        if n % t == 0:
            best = t
    return n if best is None else best


def _chunks(n, width):
    return [(c0, min(width, n - c0)) for c0 in range(0, n, width)]


_ANY = pl.BlockSpec(memory_space=pl.ANY)


class Exchange:
    def __init__(self, arrays, out_shapes, sems, start, finish, forward=None):
        self.arrays, self.out_shapes, self.sems = list(arrays), list(out_shapes), list(sems)
        self.start, self.finish, self.forward = start, finish, forward
        self.results = None


def _call(body, name, grid, ins, outs, scratch=(), exchange=None):
    in_specs = [pl.BlockSpec(*spec[1:3], **({"pipeline_mode": pl.Buffered(1)} if len(spec) > 3 else {}))
                for spec in ins]
    ins = [spec[:3] for spec in ins]
    out_specs = [pl.BlockSpec(bs, im) for _, _, bs, im in outs]
    out_shape = [jax.ShapeDtypeStruct(s, d) for s, d, _, _ in outs]
    arrays = [a for a, _, _ in ins]
    scratch = list(scratch)
    kernel_fn = body
    if exchange is not None:
        n_in, n_out, n_scr = len(ins), len(outs), len(scratch)
        n_xin, n_xout = len(exchange.arrays), len(exchange.out_shapes)
        steps = math.prod(grid)
        forward_step = min(steps - 1, (15 * steps) // 16)

        def kernel_fn(*refs):
            refs = list(refs)
            own_in, x_in = refs[:n_in], refs[n_in:n_in + n_xin]
            refs = refs[n_in + n_xin:]
            own_out, x_out = refs[:n_out], refs[n_out:n_out + n_xout]
            refs = refs[n_out + n_xout:]
            own_scr, x_sems = refs[:n_scr], refs[n_scr:]
            step = 0
            for axis, size in enumerate(grid):
                step = step * size + pl.program_id(axis)

            @pl.when(step == 0)
            def _():
                exchange.start(x_in, x_out, x_sems)

            if exchange.forward is not None:
                @pl.when(step == forward_step)
                def _():
                    exchange.forward(x_in, x_out, x_sems)

            body(*own_in, *own_out, *own_scr)

            @pl.when(step == steps - 1)
            def _():
                exchange.finish(x_in, x_out, x_sems)

        in_specs += [_ANY] * n_xin
        out_specs += [_ANY] * n_xout
        out_shape += exchange.out_shapes
        arrays += exchange.arrays
        scratch += exchange.sems
    res = pl.pallas_call(
        kernel_fn,
        name=name,
        grid=grid,
        in_specs=in_specs,
        out_specs=out_specs,
        out_shape=out_shape,
        scratch_shapes=scratch,
        compiler_params=pltpu.CompilerParams(
            dimension_semantics=("arbitrary",) * len(grid), vmem_limit_bytes=VMEM_LIMIT_BYTES),
    )(*arrays)
    if exchange is not None:
        exchange.results = list(res[len(outs):])
        res = res[:len(outs)]
    return res


def _dot(a, b, dims=None):
    if dims is None:
        return jnp.dot(a, b, preferred_element_type=F32)
    return lax.dot_general(a, b, dims, preferred_element_type=F32)


def _sigmoid(v):
    return 1.0 / (1.0 + jnp.exp(-v))


def _normal_cdf(v):
    return 0.5 * (1.0 + lax.erf(v * _SQRT_HALF))


def _normal_pdf(v):
    return _INV_SQRT_2PI * jnp.exp(-0.5 * v * v)


def _accumulate_product(ref, first, product, skip=False):
    @pl.when(first)
    def _():
        ref[...] = product()

    @pl.when(jnp.logical_not(jnp.logical_or(first, skip)))
    def _():
        ref[...] += product()


def _accumulate(ref, part, first):
    @pl.when(first)
    def _():
        ref[...] = part

    @pl.when(jnp.logical_not(first))
    def _():
        ref[...] += part


def rms_fwd(x, g, exchange=None):
    T, D = x.shape
    tt = _div(T, 512, BF16_SUBLANES)

    def body(x_ref, g_ref, o_ref):
        xv = x_ref[...]
        r = lax.rsqrt(jnp.mean(xv * xv, axis=-1, keepdims=True) + RMS_EPS)
        o_ref[...] = ((xv * r) * g_ref[...]).astype(BF16)

    return _call(body, "rms_fwd", (T // tt,),
                 [(x, (tt, D), lambda i: (i, 0)), (g.reshape(1, D), (1, D), lambda i: (0, 0))],
                 [((T, D), BF16, (tt, D), lambda i: (i, 0))], exchange=exchange)[0]


def rms_bwd(x, g, dxn, d, scale):
    T, D = x.shape
    tt = _div(T, 512, BF16_SUBLANES)

    def body(x_ref, g_ref, dn_ref, d_ref, dx_ref, dxs_ref, dg_ref):
        xv = x_ref[...]
        r = lax.rsqrt(jnp.mean(xv * xv, axis=-1, keepdims=True) + RMS_EPS)
        xh = xv * r
        dn = dn_ref[...].astype(F32)
        dxh = dn * g_ref[...]
        dx = r * (dxh - xh * jnp.mean(dxh * xh, axis=-1, keepdims=True)) + d_ref[...]
        dx_ref[...] = dx
        dxs_ref[...] = (scale * dx).astype(BF16)
        _accumulate(dg_ref, jnp.sum(dn * xh, axis=0, keepdims=True), pl.program_id(0) == 0)

    row = lambda i: (i, 0)
    fix = lambda i: (0, 0)
    return _call(body, "rms_bwd", (T // tt,),
                 [(x, (tt, D), row), (g.reshape(1, D), (1, D), fix), (dxn, (tt, D), row), (d, (tt, D), row)],
                 [((T, D), F32, (tt, D), row), ((T, D), BF16, (tt, D), row), ((1, D), F32, (1, D), fix)])


def rms_gain_grad(x, dxn):
    T, D = x.shape
    tt = _div(T, 256, 8)

    def body(x_ref, dn_ref, dg_ref):
        xv = x_ref[...]
        r = lax.rsqrt(jnp.mean(xv * xv, axis=-1, keepdims=True) + RMS_EPS)
        _accumulate(dg_ref, jnp.sum(dn_ref[...] * (xv * r), axis=0, keepdims=True), pl.program_id(0) == 0)

    row = lambda i: (i, 0)
    return _call(body, "rms_gain_grad", (T // tt,), [(x, (tt, D), row), (dxn, (tt, D), row)],
                 [((1, D), F32, (1, D), lambda i: (0, 0))])[0]


def loss_head(x, g, target, scale):
    T, D = x.shape
    tt = _div(T, 256, BF16_SUBLANES)

    def body(x_ref, g_ref, t_ref, loss_ref, dx_ref, dxs_ref, dg_ref):
        first = pl.program_id(0) == 0
        xv = x_ref[...]
        gv = g_ref[...]
        r = lax.rsqrt(jnp.mean(xv * xv, axis=-1, keepdims=True) + RMS_EPS)
        xh = xv * r
        err = xh * gv - t_ref[...]
        part = 0.5 * jnp.sum(jnp.mean(err * err, axis=-1, keepdims=True), axis=0, keepdims=True)
        _accumulate(loss_ref, jnp.broadcast_to(part, (1, LANES)), first)
        dy = err * (1.0 / D)
        dxh = dy * gv
        dx = r * (dxh - xh * jnp.mean(dxh * xh, axis=-1, keepdims=True))
        dx_ref[...] = dx
        dxs_ref[...] = (scale * dx).astype(BF16)
        _accumulate(dg_ref, jnp.sum(dy * xh, axis=0, keepdims=True), first)

    row = lambda i: (i, 0)
    fix = lambda i: (0, 0)
    return _call(body, "loss_head", (T // tt,),
                 [(x, (tt, D), row), (g.reshape(1, D), (1, D), fix), (target, (tt, D), row)],
                 [((1, LANES), F32, (1, LANES), fix), ((T, D), F32, (tt, D), row),
                  ((T, D), BF16, (tt, D), row), ((1, D), F32, (1, D), fix)])


def mm_nn(a, w3, name, out_dtype=BF16, res=None, res_scale=1.0, tm_pref=1024, tn_pref=1024, group=1,
          exchange=None):
    M, K = a.shape
    nb, _, ns = w3.shape
    tn = ns if group > 1 else _div(ns, tn_pref, LANES)
    per = ns // tn
    tm = _div(M, tm_pref, BF16_SUBLANES)
    wide = group * tn
    ins = [(a, (tm, K), lambda j, m: (m, 0)), (w3, (group, K, tn), lambda j, m: (j // per, 0, j % per))]
    if res is not None:
        ins.append((res, (tm, wide), lambda j, m: (m, j)))

    def body(*refs):
        a_ref, w_ref = refs[0], refs[1]
        o_ref = refs[-1]
        for k in range(group):
            cols = slice(k * tn, (k + 1) * tn)
            acc = _dot(a_ref[...], w_ref[k])
            if res is not None:
                acc = refs[2][:, cols] + res_scale * acc
            o_ref[:, cols] = acc.astype(o_ref.dtype)

    return _call(body, name, (nb * per // group, M // tm), ins,
                 [((M, nb * ns), out_dtype, (tm, wide), lambda j, m: (m, j))], exchange=exchange)[0]


def mm_nt(a_in, w3, name, out_dtype, M, tm_pref=1024, to_pref=2048, group=1, exchange=None):
    nb, Ko, ns = w3.shape
    to = _div(Ko, to_pref, LANES)
    tm = _div(M, tm_pref, BF16_SUBLANES)
    if isinstance(a_in, tuple):
        a, a_bs, a_im = a_in
        a_bs = tuple(tm if s == "tm" else s for s in a_bs)
    else:
        a, a_bs, a_im = a_in, (tm, group * ns), lambda m, o, b: (m, b)
    steps = nb // group
    narrow_out = steps > 1 and out_dtype != F32

    def product(a_ref, w_ref):
        p = None
        for k in range(group):
            pk = _dot(a_ref[:, k * ns:(k + 1) * ns], w_ref[k], _NT)
            p = pk if p is None else p + pk
        return p

    def body(a_ref, w_ref, o_ref, *acc):
        b = pl.program_id(2)
        if steps == 1:
            o_ref[...] = product(a_ref, w_ref).astype(o_ref.dtype)
        elif not narrow_out:
            _accumulate_product(o_ref, b == 0, lambda: product(a_ref, w_ref))
        else:
            acc_ref, = acc
            _accumulate_product(acc_ref, b == 0, lambda: product(a_ref, w_ref), skip=b == steps - 1)

            @pl.when(b == steps - 1)
            def _():
                o_ref[...] = (acc_ref[...] + product(a_ref, w_ref)).astype(o_ref.dtype)

    return _call(body, name, (M // tm, Ko // to, steps),
                 [(a, a_bs, a_im), (w3, (group, to, ns), lambda m, o, b: (b, o, 0))],
                 [((M, Ko), out_dtype, (tm, to), lambda m, o, b: (m, o))],
                 scratch=[pltpu.VMEM((tm, to), F32)] if narrow_out else [], exchange=exchange)[0]


def mm_nt_rms_bwd(a, w3, name, x, g, d, scale, exchange=None):
    nb, D, ns = w3.shape
    M = x.shape[0]
    tm = _div(M, 512, BF16_SUBLANES)
    a_bs, a_im = (tm, ns), lambda m, b: (m, b)

    rc = _div(tm, 64, BF16_SUBLANES)

    def body(a_ref, w_ref, x_ref, g_ref, d_ref, dx_ref, dxs_ref, dg_ref, acc_ref):
        m, b = pl.program_id(0), pl.program_id(1)
        _accumulate_product(acc_ref, b == 0, lambda: _dot(a_ref[...], w_ref[...], _NT))

        @pl.when(b == nb - 1)
        def _():
            gv = g_ref[...]

            def piece(c, dg):
                rows = pl.ds(pl.multiple_of(c * rc, rc), rc)
                dn = acc_ref[rows, :]
                xv = x_ref[rows, :]
                r = lax.rsqrt(jnp.mean(xv * xv, axis=-1, keepdims=True) + RMS_EPS)
                xh = xv * r
                dxh = dn * gv
                dx = r * (dxh - xh * jnp.mean(dxh * xh, axis=-1, keepdims=True)) + d_ref[rows, :]
                dx_ref[rows, :] = dx
                dxs_ref[rows, :] = (scale * dx).astype(BF16)
                return dg + jnp.sum(dn * xh, axis=0, keepdims=True)

            _accumulate(dg_ref, lax.fori_loop(0, tm // rc, piece, jnp.zeros((1, D), F32)), m == 0)

    row = lambda m, b: (m, 0)
    fix = lambda m, b: (0, 0)
    w_spec = (w3, (None, D, ns), lambda m, b: (b, 0, 0)) + (("single",) if nb == 1 else ())
    return _call(body, name, (M // tm, nb),
                 [(a, a_bs, a_im), w_spec,
                  (x, (tm, D), row), (g.reshape(1, D), (1, D), fix), (d, (tm, D), row)],
                 [((M, D), F32, (tm, D), row), ((M, D), BF16, (tm, D), row), ((1, D), F32, (1, D), fix)],
                 scratch=[pltpu.VMEM((tm, D), F32)], exchange=exchange)


def mm_tn(a, b_in, name, nbo, ns, tka_pref=1024, tt_pref=2048, tn_pref=2048, group=1, exchange=None):
    T, Ka = a.shape
    tt = _div(T, tt_pref, BF16_SUBLANES)
    tka = _div(Ka, tka_pref, LANES)
    if isinstance(b_in, tuple):
        b, b_bs, b_im = b_in
        b_bs = tuple(tt if s == "tt" else s for s in b_bs)
        tn, per = ns, 1
    else:
        tn = ns if group > 1 else _div(ns, tn_pref, LANES)
        per = ns // tn
        b, b_bs, b_im = b_in, (tt, group * tn), lambda i, j, t: (t, j)
    nt = T // tt

    def body(a_ref, b_ref, o_ref, acc_ref):
        t = pl.program_id(2)
        product = lambda: _dot(a_ref[...], b_ref[...], _TN)

        def store(total):
            for k in range(group):
                o_ref[k] = total[:, k * tn:(k + 1) * tn].astype(BF16)

        if nt == 1:
            store(product())
        else:
            _accumulate_product(acc_ref, t == 0, product, skip=t == nt - 1)

            @pl.when(t == nt - 1)
            def _():
                store(acc_ref[...] + product())

    return _call(body, name, (Ka // tka, nbo * per // group, nt),
                 [(a, (tt, tka), lambda i, j, t: (t, i)), (b, b_bs, b_im)],
                 [((nbo, Ka, ns), BF16, (group, tka, tn), lambda i, j, t: (j // per, i, j % per))],
                 scratch=[pltpu.VMEM((tka, group * tn), F32)], exchange=exchange)[0]


def ffn_up(xn, w13, exchange=None):
    T, D = xn.shape
    nb, _, ns = w13.shape
    half = nb // 2
    F = half * ns
    tm = _div(T, 512, BF16_SUBLANES)
    pair = 2 if half % 2 == 0 else 1

    def columns(w_ref, c0, cw):
        k, off = divmod(c0, ns)
        if off + cw <= ns:
            return w_ref[k, :, off:off + cw]
        return jnp.concatenate([w_ref[k, :, off:ns], w_ref[k + 1, :, 0:off + cw - ns]], axis=1)

    def body(x_ref, wg_ref, wu_ref, fac_ref, act_ref):
        xv = x_ref[...]
        for c0, cw in _chunks(pair * ns, MXU_WIDTH):
            cols = slice(c0, c0 + cw)
            gate = _dot(xv, columns(wg_ref, c0, cw))
            up = _dot(xv, columns(wu_ref, c0, cw))
            s = _sigmoid(gate)
            silu = gate * s
            fac_ref[0, :, cols] = (up * (s * (1.0 + gate * (1.0 - s)))).astype(BF16)
            fac_ref[1, :, cols] = silu.astype(BF16)
            act_ref[:, cols] = (silu * up).astype(BF16)

    tn = pair * ns
    return _call(body, "ffn_up", (half // pair, T // tm),
                 [(xn, (tm, D), lambda j, m: (m, 0)),
                  (w13, (pair, D, ns), lambda j, m: (j, 0, 0), "single"),
                  (w13, (pair, D, ns), lambda j, m: (j + half // pair, 0, 0), "single")],
                 [((2, T, F), BF16, (2, tm, tn), lambda j, m: (0, m, j)),
                  ((T, F), BF16, (tm, tn), lambda j, m: (m, j))], exchange=exchange)


def ffn_dact(dy, w2, fac, exchange=None):
    T, D = dy.shape
    F = w2.shape[0]
    tm = _div(T, 512, BF16_SUBLANES)
    tn = _div(F, F // 2, MXU_WIDTH)

    def body(dy_ref, w_ref, fac_ref, dh_ref):
        dyv = dy_ref[...]
        for c0, cw in _chunks(tn, MXU_WIDTH):
            cols = slice(c0, c0 + cw)
            da = _dot(dyv, w_ref[cols, :], _NT)
            dh_ref[0, :, cols] = (da * fac_ref[0, :, cols].astype(F32)).astype(BF16)
            dh_ref[1, :, cols] = (da * fac_ref[1, :, cols].astype(F32)).astype(BF16)

    return _call(body, "ffn_dact", (F // tn, T // tm),
                 [(dy, (tm, D), lambda j, m: (m, 0)), (w2, (tn, D), lambda j, m: (j, 0), "single"),
                  (fac, (2, tm, tn), lambda j, m: (0, m, j))],
                 [((2, T, F), BF16, (2, tm, tn), lambda j, m: (0, m, j))], exchange=exchange)[0]


def ffn_fwd(x, norm_g, get_w13, get_w2, norm_exchange=None, up_exchange=None, down_exchange=None):
    xn = rms_fwd(x, norm_g, exchange=norm_exchange)
    fac, act = ffn_up(xn, get_w13(), exchange=up_exchange)
    F = act.shape[1]
    y = mm_nn(act, get_w2().reshape(1, F, -1), "ffn_down", F32, res=x, res_scale=0.5, tm_pref=512,
              exchange=down_exchange)
    return y, (x, xn, fac, act)


def ffn_bwd(d, dys, saved, norm_g, w13, w2, scale_out):
    x, xn, fac, act = saved
    T, D = x.shape
    nb, _, ns = w13.shape
    half = nb // 2
    F = half * ns
    dh = ffn_dact(dys, w2.reshape(F, D), fac)
    dw2 = mm_tn(act, dys, "ffn_dw2", 1, D, tka_pref=ns, tn_pref=1024)
    send_w2 = scatter_exchange([dw2.reshape(N_DEV, F // N_DEV, D)])
    pair = 2 if half % 2 == 0 else 1
    per = half // pair
    dw13 = mm_tn(xn, (dh, (None, "tt", pair * ns), lambda i, j, t: (j // per, t, j % per)), "ffn_dw13", nb, ns,
                 tka_pref=512, group=pair, exchange=send_w2)
    send_w13 = scatter_exchange([dw13])
    dxn = mm_nt((dh, (None, "tm", pair * ns), lambda m, o, b: (b // per, m, b % per)), w13, "ffn_dxn", BF16, T,
                tm_pref=512, group=pair, exchange=send_w13)
    dx, dxs, dg = rms_bwd(x, norm_g, dxn, d, scale_out)
    return dx, dxs, dg, send_w13.results[0], send_w2.results[0]


def _gmlp_parts(p_ref, lng_ref, lnb_ref):
    E = lng_ref.shape[-1]
    pv = p_ref[...].astype(F32)
    cdf = _normal_cdf(pv)
    z = pv * cdf
    u = z[:, :E]
    vp = z[:, E:]
    mu = jnp.mean(vp, axis=-1, keepdims=True)
    xc = vp - mu
    rstd = lax.rsqrt(jnp.mean(xc * xc, axis=-1, keepdims=True) + LN_EPS)
    vh = xc * rstd
    v = vh * lng_ref[...] + lnb_ref[...]
    return u, vh, rstd, v, pv, cdf


def _causal_ws(ws_ref, g):
    keep = lax.broadcasted_iota(jnp.int32, (CHUNK, CHUNK), 0) >= lax.broadcasted_iota(jnp.int32, (CHUNK, CHUNK), 1)
    return jnp.where(keep, ws_ref[g], 0.0).astype(BF16), keep


def gmlp_mid_fwd(p, ln_g, ln_b, w_s, bias_full):
    T, E2 = p.shape
    E = E2 // 2
    gd = E // GROUPS
    tm = _div(T, 256, CHUNK)
    fix2 = lambda i: (0, 0)

    def body(p_ref, lng_ref, lnb_ref, ws_ref, bias_ref, o_ref):
        u, _, _, v, _, _ = _gmlp_parts(p_ref, lng_ref, lnb_ref)
        vb = v.astype(BF16)
        for g in range(GROUPS):
            wm, _ = _causal_ws(ws_ref, g)
            cols = slice(g * gd, (g + 1) * gd)
            for c in range(tm // CHUNK):
                rows = slice(c * CHUNK, (c + 1) * CHUNK)
                f = _dot(wm, vb[rows, cols]) + bias_ref[:, cols]
                o_ref[rows, cols] = (u[rows, cols] * f).astype(BF16)

    return _call(body, "gmlp_mid_fwd", (T // tm,),
                 [(p, (tm, E2), lambda i: (i, 0)), (ln_g, (1, E), fix2), (ln_b, (1, E), fix2),
                  (w_s, (GROUPS, CHUNK, CHUNK), lambda i: (0, 0, 0)), (bias_full, (CHUNK, E), fix2)],
                 [((T, E), BF16, (tm, E), lambda i: (i, 0))])[0]


def gmlp_mid_bwd(p, dgated, ln_g, ln_b, w_s, bias_full, exchange=None):
    T, E2 = p.shape
    E = E2 // 2
    gd = E // GROUPS
    tm = _div(T, 256, CHUNK)
    nsteps = T // tm
    fix2 = lambda i: (0, 0)
    fix3 = lambda i: (0, 0, 0)

    def body(p_ref, dg_ref, lng_ref, lnb_ref, ws_ref, bias_ref,
             dp_ref, dws_ref, dbs_ref, dlng_ref, dlnb_ref, f_sc, dv_sc, db_sc):
        i = pl.program_id(0)
        first = i == 0

        @pl.when(first)
        def _():
            dws_ref[...] = jnp.zeros(dws_ref.shape, F32)
            db_sc[...] = jnp.zeros(db_sc.shape, F32)

        u, vh, rstd, v, pv, cdf = _gmlp_parts(p_ref, lng_ref, lnb_ref)
        vb = v.astype(BF16)
        dgt = dg_ref[...].astype(F32)
        df = dgt * u
        dfb = df.astype(BF16)
        for g in range(GROUPS):
            wm, keep = _causal_ws(ws_ref, g)
            cols = slice(g * gd, (g + 1) * gd)
            dw = None
            dbg = None
            for c in range(tm // CHUNK):
                rows = slice(c * CHUNK, (c + 1) * CHUNK)
                f_sc[rows, cols] = _dot(wm, vb[rows, cols]) + bias_ref[:, cols]
                dv_sc[rows, cols] = _dot(wm, dfb[rows, cols], _TN)
                part = _dot(dfb[rows, cols], vb[rows, cols], _NT)
                dw = part if dw is None else dw + part
                dbg = df[rows, cols] if dbg is None else dbg + df[rows, cols]
            dws_ref[g] += jnp.where(keep, dw, 0.0)
            db_sc[:, cols] += dbg

        du = dgt * f_sc[...]
        dv = dv_sc[...]
        _accumulate(dlng_ref, jnp.sum(dv * vh, axis=0, keepdims=True), first)
        _accumulate(dlnb_ref, jnp.sum(dv, axis=0, keepdims=True), first)
        dvh = dv * lng_ref[...]
        dvp = rstd * (dvh - jnp.mean(dvh, axis=-1, keepdims=True)
                      - vh * jnp.mean(dvh * vh, axis=-1, keepdims=True))
        gp = cdf + pv * _normal_pdf(pv)
        dp_ref[:, :E] = (du * gp[:, :E]).astype(BF16)
        dp_ref[:, E:] = (dvp * gp[:, E:]).astype(BF16)

        @pl.when(i == nsteps - 1)
        def _():
            for g in range(GROUPS):
                tot = jnp.sum(db_sc[:, g * gd:(g + 1) * gd], axis=-1, keepdims=True)
                dbs_ref[g] = jnp.broadcast_to(tot, (CHUNK, LANES))

    return _call(body, "gmlp_mid_bwd", (nsteps,),
                 [(p, (tm, E2), lambda i: (i, 0)), (dgated, (tm, E), lambda i: (i, 0)),
                  (ln_g, (1, E), fix2), (ln_b, (1, E), fix2),
                  (w_s, (GROUPS, CHUNK, CHUNK), fix3), (bias_full, (CHUNK, E), fix2)],
                 [((T, E2), BF16, (tm, E2), lambda i: (i, 0)),
                  ((GROUPS, CHUNK, CHUNK), F32, (GROUPS, CHUNK, CHUNK), fix3),
                  ((GROUPS, CHUNK, LANES), F32, (GROUPS, CHUNK, LANES), fix3),
                  ((1, E), F32, (1, E), fix2), ((1, E), F32, (1, E), fix2)],
                 scratch=[pltpu.VMEM((tm, E), F32), pltpu.VMEM((tm, E), F32), pltpu.VMEM((CHUNK, E), F32)],
                 exchange=exchange)


HALO = 16


def _row_of(block, r):
    rows = lax.broadcasted_iota(jnp.int32, block.shape, 0)
    return jnp.sum(jnp.where(rows == r, block, 0.0), axis=0, keepdims=True)


def _shift_down(z, k, fill):
    out = pltpu.roll(z, k, 0)
    rows = lax.broadcasted_iota(jnp.int32, z.shape, 0)
    for t in range(k):
        out = jnp.where(rows == t, fill[t], out)
    return out


def _shift_up(z, k, fill):
    n = z.shape[0]
    out = pltpu.roll(z, n - k, 0)
    rows = lax.broadcasted_iota(jnp.int32, z.shape, 0)
    for j in range(k):
        out = jnp.where(rows == n - k + j, fill[j], out)
    return out


def _conv_parts(p_ref, prev_ref, cw_ref, is_first):
    D = cw_ref.shape[-1]
    pv = p_ref[...].astype(F32)
    bg, cg, val = pv[:, :D], pv[:, D:2 * D], pv[:, 2 * D:]
    z = cg * val
    pp = prev_ref[...].astype(F32)
    zp = jnp.where(is_first, 0.0, pp[:, D:2 * D] * pp[:, 2 * D:])
    zl1 = _row_of(zp, HALO - 1)
    zl2 = _row_of(zp, HALO - 2)
    z1 = _shift_down(z, 1, [zl1])
    z2 = _shift_down(z, 2, [zl2, zl1])
    conv = z2 * cw_ref[0:1, :] + z1 * cw_ref[1:2, :] + z * cw_ref[2:3, :]
    return bg, cg, val, z, z1, z2, conv


def conv_mid_fwd(p, cw):
    T, D3 = p.shape
    D = D3 // 3
    tm = _div(T, 256, HALO)
    per = tm // HALO

    def body(p_ref, prev_ref, cw_ref, o_ref):
        bg, _, _, _, _, _, conv = _conv_parts(p_ref, prev_ref, cw_ref, pl.program_id(0) == 0)
        o_ref[...] = (bg * conv).astype(BF16)

    return _call(body, "conv_mid_fwd", (T // tm,),
                 [(p, (tm, D3), lambda i: (i, 0)),
                  (p, (HALO, D3), lambda i: (jnp.maximum(i * per - 1, 0), 0)),
                  (cw, (CONV_WIDTH, D), lambda i: (0, 0))],
                 [((T, D), BF16, (tm, D), lambda i: (i, 0))])[0]


def conv_mid_bwd(p, dgated, cw, exchange=None):
    T, D3 = p.shape
    D = D3 // 3
    tm = _div(T, 256, HALO)
    per = tm // HALO
    nsteps = T // tm
    last_halo = T // HALO - 1
    nxt = lambda i: (jnp.minimum((i + 1) * per, last_halo), 0)

    def body(p_ref, prev_ref, next_ref, dg_ref, dgn_ref, cw_ref, dp_ref, dcw_ref):
        i = pl.program_id(0)
        bg, cg, val, z, z1, z2, conv = _conv_parts(p_ref, prev_ref, cw_ref, i == 0)
        dgt = dg_ref[...].astype(F32)
        dconv = dgt * bg
        dcn = jnp.where(i == nsteps - 1, 0.0, dgn_ref[...].astype(F32) * next_ref[:, :D].astype(F32))
        n0 = _row_of(dcn, 0)
        n1 = _row_of(dcn, 1)
        up1 = _shift_up(dconv, 1, [n0])
        up2 = _shift_up(dconv, 2, [n0, n1])
        dz = dconv * cw_ref[2:3, :] + up1 * cw_ref[1:2, :] + up2 * cw_ref[0:1, :]
        dp_ref[:, :D] = (dgt * conv).astype(BF16)
        dp_ref[:, D:2 * D] = (dz * val).astype(BF16)
        dp_ref[:, 2 * D:] = (dz * cg).astype(BF16)
        first = i == 0
        parts = (jnp.sum(dconv * z2, axis=0, keepdims=True), jnp.sum(dconv * z1, axis=0, keepdims=True),
                 jnp.sum(dconv * z, axis=0, keepdims=True))

        @pl.when(first)
        def _():
            for k in range(CONV_WIDTH):
                dcw_ref[k:k + 1, :] = parts[k]

        @pl.when(jnp.logical_not(first))
        def _():
            for k in range(CONV_WIDTH):
                dcw_ref[k:k + 1, :] += parts[k]

    return _call(body, "conv_mid_bwd", (nsteps,),
                 [(p, (tm, D3), lambda i: (i, 0)),
                  (p, (HALO, D3), lambda i: (jnp.maximum(i * per - 1, 0), 0)),
                  (p, (HALO, D3), nxt),
                  (dgated, (tm, D), lambda i: (i, 0)),
                  (dgated, (HALO, D), nxt),
                  (cw, (CONV_WIDTH, D), lambda i: (0, 0))],
                 [((T, D3), BF16, (tm, D3), lambda i: (i, 0)),
                  ((CONV_WIDTH, D), F32, (CONV_WIDTH, D), lambda i: (0, 0))], exchange=exchange)


def mixer_bwd_common(d, dys, saved, norm_g, w_in, w_out, mid_bwd, scale_out, dwkv):
    x, hn, p, gated = saved
    T, D = x.shape
    E = gated.shape[1]
    w_out3 = w_out.reshape(1, E, D)
    dgated = mm_nt(dys, w_out3, "mix_dgated", BF16, T)
    dw_out = mm_tn(gated, dys, "mix_dwout", 1, D)
    send_wout = scatter_exchange([dw_out.reshape(N_DEV, E // N_DEV, D)])
    dp, extra = mid_bwd(p, dgated, send_wout)
    nb, _, ns = w_in.shape
    group = max(g for g in (1, 2, 4, 8) if nb % g == 0 and g * ns <= D)
    send_wkv = scatter_exchange([dwkv])
    dw_in = mm_tn(hn, dp, "mix_dwin", nb, ns, group=group, exchange=send_wkv)
    send_win = scatter_exchange([dw_in])
    dhn = mm_nt(dp, w_in, "mix_dhn", BF16, T, tm_pref=512, group=group, exchange=send_win)
    dx, dxs, dg = rms_bwd(x, norm_g, dhn, d, scale_out)
    return dx, dxs, dg, send_win.results[0], send_wout.results[0], send_wkv.results[0], extra


def _softmax_rows(s):
    e = jnp.exp(s - jnp.max(s, axis=-1, keepdims=True))
    return e / jnp.sum(e, axis=-1, keepdims=True)


def attn_fwd(q, kv):
    T, D = q.shape
    M = kv.shape[0]
    hd = D // HEADS
    scale = hd ** -0.5
    tm = _div(T, 512, BF16_SUBLANES)

    def body(q_ref, kv_ref, o_ref):
        for h in range(HEADS):
            cols = slice(h * hd, (h + 1) * hd)
            s = _dot(q_ref[:, cols], kv_ref[:, cols], _NT) * scale
            pr = _softmax_rows(s).astype(BF16)
            o_ref[:, cols] = _dot(pr, kv_ref[:, D + h * hd:D + (h + 1) * hd]).astype(BF16)

    return _call(body, "attn_fwd", (T // tm,),
                 [(q, (tm, D), lambda i: (i, 0)), (kv, (M, 2 * D), lambda i: (0, 0))],
                 [((T, D), BF16, (tm, D), lambda i: (i, 0))])[0]


def attn_bwd(q, do, kv, exchange=None):
    T, D = q.shape
    M = kv.shape[0]
    hd = D // HEADS
    scale = hd ** -0.5
    tm = _div(T, 512, BF16_SUBLANES)
    nsteps = T // tm

    def body(q_ref, do_ref, kv_ref, dq_ref, dkv_ref, acc_ref):
        i = pl.program_id(0)

        @pl.when(i == 0)
        def _():
            acc_ref[...] = jnp.zeros(acc_ref.shape, F32)

        for h in range(HEADS):
            cols = slice(h * hd, (h + 1) * hd)
            vcols = slice(D + h * hd, D + (h + 1) * hd)
            qh = q_ref[:, cols]
            kh = kv_ref[:, cols]
            doh = do_ref[:, cols]
            pr = _softmax_rows(_dot(qh, kh, _NT) * scale)
            dpr = _dot(doh, kv_ref[:, vcols], _NT)
            ds = (pr * (dpr - jnp.sum(dpr * pr, axis=-1, keepdims=True)) * scale).astype(BF16)
            dq_ref[:, cols] = _dot(ds, kh).astype(BF16)
            acc_ref[:, cols] += _dot(ds, qh, _TN)
            acc_ref[:, vcols] += _dot(pr.astype(BF16), doh, _TN)

        @pl.when(i == nsteps - 1)
        def _():
            dkv_ref[...] = acc_ref[...].astype(BF16)

    return _call(body, "attn_bwd", (nsteps,),
                 [(q, (tm, D), lambda i: (i, 0)), (do, (tm, D), lambda i: (i, 0)),
                  (kv, (M, 2 * D), lambda i: (0, 0))],
                 [((T, D), BF16, (tm, D), lambda i: (i, 0)), ((M, 2 * D), BF16, (M, 2 * D), lambda i: (0, 0))],
                 scratch=[pltpu.VMEM((M, 2 * D), F32)], exchange=exchange)


def xattn_fwd(x, mem, xnorm_g, mnorm_g, wq, wkv, wo):
    D = x.shape[1]
    hq = rms_fwd(x, xnorm_g)
    mn = rms_fwd(mem, mnorm_g)
    q = mm_nn(hq, wq.reshape(1, D, D), "xattn_q")
    kv = mm_nn(mn, wkv, "xattn_kv")
    o = attn_fwd(q, kv)
    y = mm_nn(o, wo.reshape(1, D, D), "xattn_out", F32, res=x)
    return y, (x, hq, mn, q, kv, o)


def xattn_bwd(d, dys, saved, mem, xnorm_g, wq, wkv, wo, scale_out):
    x, hq, mn, q, kv, o = saved
    T, D = x.shape
    M = mem.shape[0]
    do = mm_nt(dys, wo.reshape(1, D, D), "xattn_do", BF16, T)
    rows = D // N_DEV
    dwo = mm_tn(o, dys, "xattn_dwo", 1, D)
    send_wo = scatter_exchange([dwo.reshape(N_DEV, rows, D)])
    dq, dkv = attn_bwd(q, do, kv, exchange=send_wo)
    nb, _, ns = wkv.shape
    dwkv = mm_tn(mn, dkv, "xattn_dwkv", nb, ns)
    dmn = mm_nt(dkv, wkv, "xattn_dmn", F32, M)
    dgm = rms_gain_grad(mem, dmn)
    dwq = mm_tn(hq, dq, "xattn_dwq", 1, D)
    send_wq = scatter_exchange([dwq.reshape(N_DEV, rows, D)])
    dx, dxs, dgx = mm_nt_rms_bwd(dq, wq.reshape(1, D, D), "xattn_dhq", x, xnorm_g, d, scale_out,
                                 exchange=send_wq)
    return dx, dxs, dgx, dgm, send_wq.results[0], dwkv, send_wo.results[0]


def _mesh_places():
    x, y, c = lax.axis_index("x"), lax.axis_index("y"), lax.axis_index("c")
    chips = [(1 - x, y), (x, 1 - y), (1 - x, 1 - y)]
    return (x, y, c), (x, y, 1 - c), chips


def _slot(place):
    return 4 * place[0] + 2 * place[1] + place[2]


def _exchange_sems(n):
    return [pltpu.SemaphoreType.DMA((n * N_PEERS,)), pltpu.SemaphoreType.DMA((n * N_PEERS,)),
            pltpu.SemaphoreType.DMA((n,))]


def gather_exchange(shards):
    n = len(shards)
    shapes = [a.shape if l is None else a.shape[1:] for a, l in shards]

    def parts(x_in, x_out, sems):
        ins = [r if l is None else r.at[l] for r, (_, l) in zip(x_in, shards)]
        send_sems, recv_sems, local_sems = sems
        me, sibling, chips = _mesh_places()

        def copy(a, k, block, to, src=None):
            dst = x_out[a].at[_slot(block)]
            return pltpu.make_async_remote_copy(
                src_ref=dst if src is None else src, dst_ref=dst,
                send_sem=send_sems.at[a * N_PEERS + k], recv_sem=recv_sems.at[a * N_PEERS + k],
                device_id=to, device_id_type=pl.DeviceIdType.MESH)

        mine = [pltpu.make_async_copy(ins[a], x_out[a].at[_slot(me)], local_sems.at[a]) for a in range(n)]
        first = []
        for a in range(n):
            first.append(copy(a, 0, me, sibling, src=ins[a]))
            first += [copy(a, 1 + j, me, (*chip, me[2]), src=ins[a]) for j, chip in enumerate(chips)]
        return me, sibling, chips, copy, mine, first

    def start(x_in, x_out, sems):
        _, _, _, _, mine, first = parts(x_in, x_out, sems)
        for cp in mine + first:
            cp.start()

    def forward(x_in, x_out, sems):
        me, sibling, chips, copy, _, _ = parts(x_in, x_out, sems)
        for j, chip in enumerate(chips):
            for a in range(n):
                copy(a, 1 + j, (*chip, me[2]), me).wait_recv()
                copy(a, 4 + j, (*chip, me[2]), sibling).start()

    def finish(x_in, x_out, sems):
        me, sibling, chips, copy, mine, first = parts(x_in, x_out, sems)
        for a in range(n):
            copy(a, 0, sibling, me).wait_recv()
        for j, chip in enumerate(chips):
            for a in range(n):
                copy(a, 4 + j, (*chip, 1 - me[2]), me).wait_recv()
        for cp in first:
            cp.wait_send()
        for j, chip in enumerate(chips):
            for a in range(n):
                copy(a, 4 + j, (*chip, me[2]), sibling).wait_send()
        for cp in mine:
            cp.wait()

    return Exchange([a for a, _ in shards],
                    [jax.ShapeDtypeStruct((N_DEV,) + tuple(s), a.dtype) for s, (a, _) in zip(shapes, shards)],
                    _exchange_sems(n), start, finish, forward)


def _all_peers(me, chips):
    c = me[2]
    return [(me[0], me[1], 1 - c)] + [(*chip, c) for chip in chips] + [(*chip, 1 - c) for chip in chips]


def scatter_exchange(grads):
    n = len(grads)

    def parts(x_in, x_out, sems):
        send_sems, recv_sems, local_sems = sems
        me, _, chips = _mesh_places()
        mine = [pltpu.make_async_copy(x_in[a].at[_slot(me)], x_out[a].at[_slot(me)], local_sems.at[a])
                for a in range(n)]
        sends, recvs = [], []
        for a in range(n):
            for k, peer in enumerate(_all_peers(me, chips)):
                sem = dict(send_sem=send_sems.at[a * N_PEERS + k], recv_sem=recv_sems.at[a * N_PEERS + k],
                           device_id=peer, device_id_type=pl.DeviceIdType.MESH)
                sends.append(pltpu.make_async_remote_copy(
                    src_ref=x_in[a].at[_slot(peer)], dst_ref=x_out[a].at[_slot(me)], **sem))
                recvs.append(pltpu.make_async_remote_copy(
                    src_ref=x_in[a].at[_slot(peer)], dst_ref=x_out[a].at[_slot(peer)], **sem))
        return mine, sends, recvs

    def start(x_in, x_out, sems):
        mine, sends, _ = parts(x_in, x_out, sems)
        for cp in mine + sends:
            cp.start()

    def finish(x_in, x_out, sems):
        mine, sends, recvs = parts(x_in, x_out, sems)
        for cp in recvs:
            cp.wait_recv()
        for cp in sends:
            cp.wait_send()
        for cp in mine:
            cp.wait()

    return Exchange(grads, [jax.ShapeDtypeStruct(g.shape, g.dtype) for g in grads],
                    _exchange_sems(n), start, finish)


def small_all_reduce(vec):
    R = vec.shape[0]

    def body(v_ref, o_ref, all_ref, send_sems, recv_sems):
        me, _, chips = _mesh_places()
        peers = _all_peers(me, chips)
        all_ref[_slot(me)] = v_ref[...]
        sends, recvs = [], []
        for k, peer in enumerate(peers):
            sem = dict(send_sem=send_sems.at[k], recv_sem=recv_sems.at[k],
                       device_id=peer, device_id_type=pl.DeviceIdType.MESH)
            sends.append(pltpu.make_async_remote_copy(src_ref=v_ref, dst_ref=all_ref.at[_slot(me)], **sem))
            recvs.append(pltpu.make_async_remote_copy(src_ref=v_ref, dst_ref=all_ref.at[_slot(peer)], **sem))
        for cp in sends:
            cp.start()
        for cp in recvs:
            cp.wait_recv()
        for cp in sends:
            cp.wait_send()
        acc = all_ref[0]
        for s in range(1, N_DEV):
            acc = acc + all_ref[s]
        o_ref[...] = acc

    return pl.pallas_call(
        body, name="small_all_reduce",
        out_shape=jax.ShapeDtypeStruct(vec.shape, F32),
        in_specs=[pl.BlockSpec(memory_space=pltpu.VMEM)], out_specs=pl.BlockSpec(memory_space=pltpu.VMEM),
        scratch_shapes=[pltpu.VMEM((N_DEV, R, LANES), F32), pltpu.SemaphoreType.DMA((N_PEERS,)),
                        pltpu.SemaphoreType.DMA((N_PEERS,))],
    )(vec)


def _adamw_math(w, g, m, v):
    m2 = ADAM_B1 * m + (1.0 - ADAM_B1) * g
    v2 = ADAM_B2 * v + (1.0 - ADAM_B2) * (g * g)
    m_hat = m2 / (1.0 - ADAM_B1 ** ADAM_STEP)
    v_hat = v2 / (1.0 - ADAM_B2 ** ADAM_STEP)
    delta = -ADAM_LR * (m_hat / (jnp.sqrt(v_hat) + ADAM_EPS) + ADAM_WD * w)
    return delta, m2, v2


def adamw_sharded(partials, w, m, v):
    L, r, c = w.shape
    row_bytes = 2 * c * (L * N_DEV * 2 + 7 * 4)
    tr = _div(r, max(BF16_SUBLANES, min(512, (VMEM_LIMIT_BYTES * 3 // 4) // row_bytes)), BF16_SUBLANES)
    nt = r // tr

    def part_map(l0):
        return lambda l, t: (0, jnp.where(l == l0, t, jnp.where(l < l0, 0, nt - 1)), 0)

    def body(*refs):
        parts = refs[:L]
        w_ref, m_ref, v_ref, g_out, d_out, m_out, v_out = refs[L:]
        layer = pl.program_id(0)
        for l0 in range(L):
            @pl.when(layer == l0)
            def _():
                g = parts[l0][0].astype(F32)
                for s in range(1, N_DEV):
                    g = g + parts[l0][s].astype(F32)
                delta, m2, v2 = _adamw_math(w_ref[...], g, m_ref[...], v_ref[...])
                g_out[...] = g
                d_out[...] = delta
                m_out[...] = m2
                v_out[...] = v2

    own = lambda l, t: (l, t, 0)
    return _call(body, "adamw_sharded", (L, nt),
                 [(p, (N_DEV, tr, c), part_map(l0)) for l0, p in enumerate(partials)]
                 + [(w, (None, tr, c), own), (m, (None, tr, c), own), (v, (None, tr, c), own)],
                 [((L, r, c), F32, (None, tr, c), own)] * 4)


def adamw_flat(g, w, m, v):
    shape = g.shape

    def body(g_ref, w_ref, m_ref, v_ref, d_out, m_out, v_out):
        delta, m2, v2 = _adamw_math(w_ref[...], g_ref[...], m_ref[...], v_ref[...])
        d_out[...] = delta
        m_out[...] = m2
        v_out[...] = v2

    whole = lambda: (0, 0)
    return _call(body, "adamw_flat", (), [(a, shape, whole) for a in (g, w, m, v)],
                 [(shape, F32, shape, whole)] * 3)


def _pack(parts):
    flat = jnp.concatenate([p.reshape(-1).astype(F32) for p in parts])
    rows = -(-flat.shape[0] // (8 * LANES)) * 8
    return jnp.pad(flat, (0, rows * LANES - flat.shape[0])).reshape(rows, LANES)


def _unpack(packed, shapes):
    flat = packed.reshape(-1)
    out, off = [], 0
    for s in shapes:
        size = math.prod(s)
        out.append(flat[off:off + size].reshape(s))
        off += size
    return out


def kernel(x, mem, ffn1_norm, ffn1_w13, ffn1_w2, mix_norm, gmlp_w_in, gmlp_ln_g, gmlp_ln_b, gmlp_w_s, gmlp_b_s, gmlp_w_out, conv_w_in, conv_w, conv_w_out, xattn_norm, mem_norm, xattn_wq, xattn_wkv, xattn_wo, ffn2_norm, ffn2_w13, ffn2_w2, final_norm, loss_target, m_ffn1_norm, m_ffn1_w13, m_ffn1_w2, m_mix_norm, m_gmlp_w_in, m_gmlp_ln_g, m_gmlp_ln_b, m_gmlp_w_s, m_gmlp_b_s, m_gmlp_w_out, m_conv_w_in, m_conv_w, m_conv_w_out, m_xattn_norm, m_mem_norm, m_xattn_wq, m_xattn_wkv, m_xattn_wo, m_ffn2_norm, m_ffn2_w13, m_ffn2_w2, m_final_norm, v_ffn1_norm, v_ffn1_w13, v_ffn1_w2, v_mix_norm, v_gmlp_w_in, v_gmlp_ln_g, v_gmlp_ln_b, v_gmlp_w_s, v_gmlp_b_s, v_gmlp_w_out, v_conv_w_in, v_conv_w, v_conv_w_out, v_xattn_norm, v_mem_norm, v_xattn_wq, v_xattn_wkv, v_xattn_wo, v_ffn2_norm, v_ffn2_w13, v_ffn2_w2, v_final_norm):
    given = dict(locals())
    T, D = x.shape[1], x.shape[2]
    depth = ffn1_norm.shape[0]
    xs = x.reshape(T, D)
    mems = mem.reshape(mem.shape[1], D)
    target = loss_target.reshape(T, D)
    me = 4 * lax.axis_index("x") + 2 * lax.axis_index("y") + lax.axis_index("c")
    E = gmlp_ln_g.shape[1]
    gd = E // GROUPS
    cshard = conv_w.shape[2]

    bf = {k: given[k].astype(BF16) for k in
          ("ffn1_w13", "ffn1_w2", "gmlp_w_in", "gmlp_w_out", "conv_w_in", "conv_w_out",
           "xattn_wq", "xattn_wkv", "xattn_wo", "ffn2_w13", "ffn2_w2")}

    W = {}

    def gather(names_layers):
        return names_layers, gather_exchange([(bf[k], l) for k, l in names_layers])

    def landed(tagged):
        names_layers, exchange = tagged
        W.update(zip(names_layers, exchange.results))

    bias_full = jnp.repeat(gmlp_b_s[0].T, gd, axis=1)

    first = gather_exchange([(bf["ffn1_w13"], 0), (conv_w, 0)])
    taps = {}

    def first_w13():
        W["ffn1_w13", 0], cw_shards = first.results
        taps["conv"] = jnp.transpose(cw_shards, (1, 0, 2)).reshape(CONV_WIDTH, D)
        return W["ffn1_w13", 0]

    saved = []
    h = xs
    for i in range(depth):
        j = i // 2
        is_gmlp = i % 2 == 0
        mix = ("gmlp_w_in", "gmlp_w_out") if is_gmlp else ("conv_w_in", "conv_w_out")
        on_up1 = gather(([("ffn1_w2", i)] if i == 0 else []) + [(mix[0], j), ("ffn2_w13", i)])
        on_down1 = gather([(mix[1], j), ("xattn_wq", i), ("xattn_wkv", i)])

        def w2_after_up(on_up=on_up1, i=i):
            landed(on_up)
            return W["ffn1_w2", i]

        h, sv1 = ffn_fwd(h, ffn1_norm[i], first_w13 if i == 0 else (lambda i=i: W["ffn1_w13", i]), w2_after_up,
                         first if i == 0 else None, on_up1[1], on_down1[1])
        landed(on_down1)
        hn = rms_fwd(h, mix_norm[i])
        on_mix_in = gather([("xattn_wo", i), ("ffn2_w2", i)])
        p = mm_nn(hn, W[mix[0], j], "mix_in", group=2, exchange=on_mix_in[1])
        landed(on_mix_in)
        if is_gmlp:
            gated = gmlp_mid_fwd(p, gmlp_ln_g[j:j + 1], gmlp_ln_b[j:j + 1], gmlp_w_s[j], bias_full)
        else:
            gated = conv_mid_fwd(p, taps["conv"])
        h_mix = mm_nn(gated, W[mix[1], j].reshape(1, gated.shape[1], D), "mix_out", F32, res=h)
        sv2 = (h, hn, p, gated)
        h, sv3 = xattn_fwd(h_mix, mems, xattn_norm[i], mem_norm[i],
                           W["xattn_wq", i], W["xattn_wkv", i], W["xattn_wo", i])
        on_up2 = gather([("ffn1_w13", i + 1), ("ffn1_w2", i + 1)]) if i + 1 < depth else None
        h, sv4 = ffn_fwd(h, ffn2_norm[i], lambda i=i: W["ffn2_w13", i], lambda i=i: W["ffn2_w2", i],
                         up_exchange=None if on_up2 is None else on_up2[1])
        if on_up2 is not None:
            landed(on_up2)
        saved.append((sv1, sv2, sv3, sv4))

    loss_part, d, dys, d_final_norm = loss_head(h, final_norm, target, 0.5)

    small = {k: [None] * depth for k in ("ffn1_norm", "mix_norm", "xattn_norm", "mem_norm", "ffn2_norm")}
    partial = {}
    for i in reversed(range(depth)):
        j = i // 2
        is_gmlp = i % 2 == 0
        sv1, sv2, sv3, sv4 = saved[i]
        d, dys, small["ffn2_norm"][i], partial["ffn2_w13", i], partial["ffn2_w2", i] = ffn_bwd(
            d, dys, sv4, ffn2_norm[i], W["ffn2_w13", i], W["ffn2_w2", i], 1.0)
        (d, dys, small["xattn_norm"][i], small["mem_norm"][i], partial["xattn_wq", i],
         dwkv, partial["xattn_wo", i]) = xattn_bwd(
            d, dys, sv3, mems, xattn_norm[i], W["xattn_wq", i], W["xattn_wkv", i], W["xattn_wo", i], 1.0)
        if is_gmlp:
            mix = ("gmlp_w_in", "gmlp_w_out")
            mid = lambda p, dg, send: (lambda r: (r[0], r[1:]))(gmlp_mid_bwd(
                p, dg, gmlp_ln_g[j:j + 1], gmlp_ln_b[j:j + 1], gmlp_w_s[j], bias_full, exchange=send))
        else:
            mix = ("conv_w_in", "conv_w_out")
            mid = lambda p, dg, send: (lambda r: (r[0], r[1:]))(conv_mid_bwd(p, dg, taps["conv"], exchange=send))
        (d, dys, small["mix_norm"][i], partial[mix[0], j], partial[mix[1], j], partial["xattn_wkv", i],
         extra) = mixer_bwd_common(d, dys, sv2, mix_norm[i], W[mix[0], j], W[mix[1], j], mid, 0.5, dwkv)
        if is_gmlp:
            d_ws, d_bs_wide, d_lng, d_lnb = extra
        else:
            (d_cw,) = extra
        d, dys, small["ffn1_norm"][i], partial["ffn1_w13", i], partial["ffn1_w2", i] = ffn_bwd(
            d, dys, sv1, ffn1_norm[i], W["ffn1_w13", i], W["ffn1_w2", i], 0.5)
    grad_x = d.reshape(x.shape)

    small_grads = {k: jnp.concatenate(v, axis=0) for k, v in small.items()}
    small_grads["gmlp_ln_g"] = d_lng
    small_grads["gmlp_ln_b"] = d_lnb
    small_grads["gmlp_w_s"] = d_ws[None]
    small_grads["gmlp_b_s"] = d_bs_wide[None, :, :, 0]
    small_grads["final_norm"] = d_final_norm.reshape(-1)
    small_names = ["ffn1_norm", "mix_norm", "gmlp_ln_g", "gmlp_ln_b", "gmlp_w_s", "gmlp_b_s",
                   "xattn_norm", "mem_norm", "ffn2_norm", "final_norm"]
    summed = small_all_reduce(_pack([small_grads[k] for k in small_names] + [d_cw, loss_part]))
    parts = _unpack(summed, [given[k].shape for k in small_names] + [(CONV_WIDTH, D), (1, LANES)])
    grads = dict(zip(small_names, parts[:len(small_names)]))
    grads["conv_w"] = lax.dynamic_slice(parts[-2], (jnp.int32(0), me * cshard), (CONV_WIDTH, cshard))[None]
    loss = parts[-1][0, 0]
    flat_names = small_names + ["conv_w"]
    flat = adamw_flat(*[_pack([src[k] for k in flat_names]) for src in
                        (grads, given, {k: given["m_" + k] for k in flat_names},
                         {k: given["v_" + k] for k in flat_names})])
    delta, new_m, new_v = [dict(zip(flat_names, _unpack(f, [given[k].shape for k in flat_names]))) for f in flat]

    for k in bf:
        w = given[k]
        L = w.shape[0]
        shard = w.shape[1:]
        view = lambda a: a.reshape((L,) + shard)
        g, dl, m2, v2 = adamw_sharded([partial[k, l] for l in range(L)], w, given["m_" + k], given["v_" + k])
        grads[k], delta[k], new_m[k], new_v[k] = view(g), view(dl), view(m2), view(v2)

    order = ["ffn1_norm", "ffn1_w13", "ffn1_w2", "mix_norm", "gmlp_w_in", "gmlp_ln_g", "gmlp_ln_b", "gmlp_w_s",
             "gmlp_b_s", "gmlp_w_out", "conv_w_in", "conv_w", "conv_w_out", "xattn_norm", "mem_norm", "xattn_wq",
             "xattn_wkv", "xattn_wo", "ffn2_norm", "ffn2_w13", "ffn2_w2", "final_norm"]
    return (loss, grad_x, *[grads[k] for k in order], *[delta[k] for k in order],
            *[new_m[k] for k in order], *[new_v[k] for k in order])
```
